```python
import jax, jax.numpy as jnp
from jax import lax
import numpy as np

D_MODEL = 1024
BATCH = 8
SEQ = 8192
DEPTH = 1

GRID_W = 64
CTX_LEN = 256
D_RNN = 1024
RNN_BLOCKS = 8
RNN_BW = D_RNN // RNN_BLOCKS
CONV_W = 4
CONV_LEFT = 2
LRU_C = 8.0
RET_HEADS = 8
RET_DK = 128
RET_DV = 256
RET_CHUNK = 128
D_QK = RET_HEADS * RET_DK
D_V = RET_HEADS * RET_DV
ROPE_BASE = 10000.0
IN_SPLITS = (D_RNN, D_RNN, D_QK, D_QK, D_V, D_V, D_MODEL, D_MODEL)
D_IN = sum(IN_SPLITS)
N_EXPERTS = 32
TOP_K = 4
D_EXPERT = 1024
SWIGLU_ALPHA = 1.702
SWIGLU_LIMIT = 7.0
MOE_BLOCK = 512
EPS = 1e-6

kernel_name = "hybrid_rglru_retention_moe_prefix_dit_layer"


def rmsnorm(x, g):
    xf = x.astype(jnp.float32)
    y = xf * lax.rsqrt(jnp.mean(xf * xf, axis=-1, keepdims=True) + EPS)
    return (y * g.astype(jnp.float32)).astype(x.dtype)


def modulate(h, shift, scale):
    return h * (1.0 + scale) + shift


def split_columns(z):
    points = [int(p) for p in np.cumsum(IN_SPLITS)[:-1]]
    return jnp.split(z, points, axis=-1)


def centred_depthwise_conv(x, w, b):
    T = x.shape[1]
    xp = jnp.pad(x, ((0, 0), (CONV_LEFT, CONV_W - 1 - CONV_LEFT), (0, 0)))
    y = b
    for k in range(CONV_W):
        y = y + xp[:, k:k + T] * w[k]
    return y


def _linear_combine(e1, e2):
    a1, b1 = e1
    a2, b2 = e2
    return a1 * a2, a2 * b1 + b2


def rglru_scan(xc, wa, ba, wx, bx, lam, h0, reverse):
    B, T, _ = xc.shape
    xb = xc.reshape(B, T, RNN_BLOCKS, RNN_BW)
    r = jax.nn.sigmoid(jnp.einsum('btnd,nde->btne', xb, wa).reshape(B, T, D_RNN) + ba)
    i = jax.nn.sigmoid(jnp.einsum('btnd,nde->btne', xb, wx).reshape(B, T, D_RNN) + bx)
    log_a = -LRU_C * r * jax.nn.softplus(-lam)
    a = jnp.exp(log_a)
    u = jnp.sqrt(-jnp.expm1(2.0 * log_a)) * (i * xc)
    a_cum, h = lax.associative_scan(_linear_combine, (a, u), axis=1, reverse=reverse)
    return h + a_cum * h0[:, None, :]


def axial_rope(t, rows, cols):
    n_freq = RET_DK // 4
    inv = ROPE_BASE ** (-jnp.arange(n_freq, dtype=jnp.float32) / n_freq)
    ang = jnp.concatenate([rows[:, None] * inv, cols[:, None] * inv], axis=-1)
    cos, sin = jnp.cos(ang), jnp.sin(ang)
    t1, t2 = t[..., 0::2], t[..., 1::2]
    return jnp.stack([t1 * cos - t2 * sin, t1 * sin + t2 * cos], axis=-1).reshape(t.shape)


def retention_context(q, k, v, log_g):
    L = q.shape[2]
    pos = jnp.arange(L, dtype=jnp.float32)
    decay = jnp.exp(jnp.abs(pos[:, None] - pos[None, :])[None] * log_g[:, None, None])
    scores = jnp.einsum('bhid,bhjd->bhij', q, k) * decay
    o = jnp.einsum('bhij,bhjv->bhiv', scores, v)
    k_f = k * jnp.exp((L - 1.0 - pos)[None, :] * log_g[:, None])[..., None]
    k_b = k * jnp.exp(pos[None, :] * log_g[:, None])[..., None]
    s_fwd = jnp.einsum('bhjd,bhjv->bhdv', k_f, v)
    s_bwd = jnp.einsum('bhjd,bhjv->bhdv', k_b, v)
    return o, s_fwd, s_bwd


def retention_chunked(q, k, v, s0, log_g, strict):
    B, H, S, dk = q.shape
    dv = v.shape[-1]
    C = RET_CHUNK
    n = S // C
    idx = jnp.arange(C, dtype=jnp.float32)
    diff = idx[:, None] - idx[None, :]
    mask = (diff > 0) if strict else (diff >= 0)
    intra = jnp.where(mask[None], jnp.exp(jnp.where(mask, diff, 0.0)[None] * log_g[:, None, None]), 0.0)
    q_dec = jnp.exp((idx + 1.0)[None, :] * log_g[:, None])[..., None]
    k_dec = jnp.exp((C - 1.0 - idx)[None, :] * log_g[:, None])[..., None]
    c_dec = jnp.exp(C * log_g)[:, None, None]

    def to_chunks(t):
        return jnp.moveaxis(t.reshape(B, H, n, C, t.shape[-1]), 2, 0)

    def step(state, blk):
        qb, kb, vb = blk
        s = jnp.einsum('bhid,bhjd->bhij', qb, kb) * intra
        o = jnp.einsum('bhij,bhjv->bhiv', s, vb) + jnp.einsum('bhid,bhdv->bhiv', qb * q_dec, state)
        state = state * c_dec + jnp.einsum('bhjd,bhjv->bhdv', kb * k_dec, vb)
        return state, o

    _, o = lax.scan(step, s0, (to_chunks(q), to_chunks(k), to_chunks(v)))
    return jnp.moveaxis(o, 0, 2).reshape(B, H, S, dv)


def head_groupnorm(o):
    mu = jnp.mean(o, axis=-1, keepdims=True)
    var = jnp.mean(jnp.square(o - mu), axis=-1, keepdims=True)
    return (o - mu) * lax.rsqrt(var + EPS)


def token_mixing(h_lat, h_ctx, rows, cols, w_in, conv_w, conv_b, lru_wa, lru_ba, lru_wx, lru_bx,
                 lru_lambda, w_rnn_proj, w_ret_proj, w_out, with_ctx_out):
    f32 = jnp.float32
    B, S, _ = h_lat.shape
    L = h_ctx.shape[1]
    xr_l, gr_l, q_l, k_l, v_l, gs_l, ga_l, gb_l = split_columns(h_lat @ w_in)
    xr_c, gr_c, q_c, k_c, v_c, gs_c, ga_c, gb_c = split_columns(h_ctx @ w_in)

    xr_c = centred_depthwise_conv(xr_c.astype(f32), conv_w, conv_b)
    xr_l = centred_depthwise_conv(xr_l.astype(f32), conv_w, conv_b)
    zeros = jnp.zeros((B, D_RNN), f32)
    rnn_l = 0.0
    rnn_c = 0.0
    for d in range(2):
        rev = d == 1
        hc = rglru_scan(xr_c, lru_wa[d], lru_ba[d], lru_wx[d], lru_bx[d], lru_lambda[d], zeros, rev)
        h0 = hc[:, 0] if rev else hc[:, -1]
        rnn_l = rnn_l + rglru_scan(xr_l, lru_wa[d], lru_ba[d], lru_wx[d], lru_bx[d], lru_lambda[d], h0, rev)
        rnn_c = rnn_c + hc

    log_g = jnp.log1p(-jnp.exp2(-5.0 - jnp.arange(RET_HEADS, dtype=f32)))
    k_scale = RET_DK ** -0.5

    def heads(t, dh):
        return jnp.swapaxes(t.astype(f32).reshape(B, t.shape[1], RET_HEADS, dh), 1, 2)

    qh_l = axial_rope(heads(q_l, RET_DK), rows, cols)
    kh_l = axial_rope(heads(k_l, RET_DK), rows, cols) * k_scale
    vh_l = heads(v_l, RET_DV)
    o_c, s_fwd, s_bwd = retention_context(heads(q_c, RET_DK), heads(k_c, RET_DK) * k_scale,
                                          heads(v_c, RET_DV), log_g)
    o_l = retention_chunked(qh_l, kh_l, vh_l, s_fwd, log_g, False) + jnp.flip(
        retention_chunked(jnp.flip(qh_l, 2), jnp.flip(kh_l, 2), jnp.flip(vh_l, 2), s_bwd, log_g, True), 2)

    def finish(rnn, gr, o, gs, ga, gb):
        T = o.shape[2]
        y_rnn = (rnn * jax.nn.gelu(gr.astype(f32))) @ w_rnn_proj
        ret = jnp.swapaxes(head_groupnorm(o), 1, 2).reshape(B, T, D_V)
        y_ret = (jax.nn.silu(gs.astype(f32)) * ret) @ w_ret_proj
        merged = jax.nn.sigmoid(ga.astype(f32)) * y_rnn + jax.nn.sigmoid(gb.astype(f32)) * y_ret
        return merged @ w_out

    out_l = finish(rnn_l, gr_l, o_l, gs_l, ga_l, gb_l)
    out_c = finish(rnn_c, gr_c, o_c, gs_c, ga_c, gb_c) if with_ctx_out else None
    return out_l, out_c


def moe_ffn(h, router_w, router_b, w1, b1, w2, b2):
    N, D = h.shape
    logits = (h @ router_w + router_b).astype(jnp.float32)
    top_val, top_idx = lax.top_k(logits, TOP_K)
    gates = jax.nn.softmax(top_val, axis=-1)
    flat_e = top_idx.reshape(-1).astype(jnp.int32)
    NK = flat_e.shape[0]
    order = jnp.argsort(flat_e).astype(jnp.int32)
    sorted_e = flat_e[order]
    counts = jnp.zeros((N_EXPERTS,), jnp.int32).at[flat_e].add(1)
    padded = (counts + MOE_BLOCK - 1) // MOE_BLOCK * MOE_BLOCK
    start = jnp.cumsum(counts) - counts
    pad_end = jnp.cumsum(padded)
    pad_start = pad_end - padded
    dest = pad_start[sorted_e] + jnp.arange(NK, dtype=jnp.int32) - start[sorted_e]
    n_blocks = -(-NK // MOE_BLOCK) + N_EXPERTS
    P = n_blocks * MOE_BLOCK
    row_token = jnp.full((P,), N, jnp.int32).at[dest].set(order // TOP_K)
    h_pad = jnp.concatenate([h, jnp.zeros((1, D), h.dtype)], axis=0)[row_token]
    blk_start = jnp.arange(n_blocks, dtype=jnp.int32) * MOE_BLOCK
    blk_expert = jnp.minimum(jnp.searchsorted(pad_end, blk_start, side='right'), N_EXPERTS - 1).astype(jnp.int32)

    def expert_block(args):
        hb, e = args
        gu = hb @ w1[e] + b1[e]
        glu, lin = gu[:, 0::2], gu[:, 1::2]
        glu = jnp.minimum(glu, SWIGLU_LIMIT)
        lin = jnp.clip(lin, -SWIGLU_LIMIT, SWIGLU_LIMIT)
        return (glu * jax.nn.sigmoid(SWIGLU_ALPHA * glu) * (lin + 1.0)) @ w2[e] + b2[e]

    y_pad = lax.map(expert_block, (h_pad.reshape(n_blocks, MOE_BLOCK, D), blk_expert)).reshape(P, D)
    row_of_assign = jnp.zeros((NK,), jnp.int32).at[order].set(dest)
    y = y_pad[row_of_assign].reshape(N, TOP_K, D)
    return jnp.einsum('nk,nkd->nd', gates.astype(y.dtype), y)


def setup_inputs(seed: int = 0) -> dict:
    key = jax.random.key(seed)
    ks = jax.random.split(key, 32)
    f32 = jnp.float32

    def nrm(k, shape, scale):
        return jax.random.normal(k, shape, f32) * scale

    u = jax.random.uniform(ks[14], (DEPTH, 2, D_RNN), f32, 0.9, 0.999)
    a0 = u ** (1.0 / LRU_C)
    lru_lambda = jnp.log(a0) - jnp.log1p(-a0)
    return {
        "x": nrm(ks[0], (BATCH, SEQ, D_MODEL), 1.0),
        "c": nrm(ks[1], (BATCH, D_MODEL), 1.0),
        "ctx": nrm(ks[2], (BATCH, CTX_LEN, D_MODEL), 1.0),
        "c_ctx": nrm(ks[3], (D_MODEL,), 1.0),
        "ada_w": nrm(ks[4], (DEPTH, D_MODEL, 6 * D_MODEL), 0.5 * D_MODEL ** -0.5),
        "ada_b": nrm(ks[5], (DEPTH, 6 * D_MODEL), 0.02),
        "norm1_g": 1.0 + nrm(ks[6], (DEPTH, D_MODEL), 0.05),
        "w_in": nrm(ks[7], (DEPTH, D_MODEL, D_IN), D_MODEL ** -0.5),
        "conv_w": nrm(ks[8], (DEPTH, CONV_W, D_RNN), CONV_W ** -0.5),
        "conv_b": nrm(ks[9], (DEPTH, D_RNN), 0.01),
        "lru_wa": nrm(ks[10], (DEPTH, 2, RNN_BLOCKS, RNN_BW, RNN_BW), RNN_BW ** -0.5),
        "lru_ba": nrm(ks[11], (DEPTH, 2, D_RNN), 0.01),
        "lru_wx": nrm(ks[12], (DEPTH, 2, RNN_BLOCKS, RNN_BW, RNN_BW), RNN_BW ** -0.5),
        "lru_bx": nrm(ks[13], (DEPTH, 2, D_RNN), 0.01),
        "lru_lambda": lru_lambda,
        "w_rnn_proj": nrm(ks[15], (DEPTH, D_RNN, D_MODEL), D_RNN ** -0.5),
        "w_ret_proj": nrm(ks[16], (DEPTH, D_V, D_MODEL), D_V ** -0.5),
        "w_out": nrm(ks[17], (DEPTH, D_MODEL, D_MODEL), D_MODEL ** -0.5),
        "norm2_g": 1.0 + nrm(ks[18], (DEPTH, D_MODEL), 0.05),
        "router_w": nrm(ks[19], (DEPTH, D_MODEL, N_EXPERTS), D_MODEL ** -0.5),
        "router_b": nrm(ks[20], (DEPTH, N_EXPERTS), 0.01),
        "moe_w1": nrm(ks[21], (DEPTH, N_EXPERTS, D_MODEL, 2 * D_EXPERT), D_MODEL ** -0.5),
        "moe_b1": nrm(ks[22], (DEPTH, N_EXPERTS, 2 * D_EXPERT), 0.01),
        "moe_w2": nrm(ks[23], (DEPTH, N_EXPERTS, D_EXPERT, D_MODEL), D_EXPERT ** -0.5),
        "moe_b2": nrm(ks[24], (DEPTH, N_EXPERTS, D_MODEL), 0.01),
        "final_g": 1.0 + nrm(ks[25], (D_MODEL,), 0.05),
    }


def reference(x, c, ctx, c_ctx, ada_w, ada_b, norm1_g, w_in, conv_w, conv_b, lru_wa, lru_ba, lru_wx,
              lru_bx, lru_lambda, w_rnn_proj, w_ret_proj, w_out, norm2_g, router_w, router_b,
              moe_w1, moe_b1, moe_w2, moe_b2, final_g):
    B, S, D = x.shape
    ROWS = S // GRID_W
    rows = jnp.repeat(jnp.arange(ROWS, dtype=jnp.float32), GRID_W)
    cols = jnp.tile(jnp.arange(GRID_W, dtype=jnp.float32), ROWS)
    h_c = ctx
    for layer in range(DEPTH):
        last = layer == DEPTH - 1
        mod_l = jnp.split(jax.nn.silu(c) @ ada_w[layer] + ada_b[layer], 6, axis=-1)
        sh1, sc1, g1, sh2, sc2, g2 = [m[:, None, :] for m in mod_l]
        mod_c = jnp.split(jax.nn.silu(c_ctx) @ ada_w[layer] + ada_b[layer], 6, axis=-1)
        csh1, csc1, cg1, csh2, csc2, cg2 = [m[None, None, :] for m in mod_c]

        hl = modulate(rmsnorm(x, norm1_g[layer]), sh1, sc1)
        hc = modulate(rmsnorm(h_c, norm1_g[layer]), csh1, csc1)
        y_l, y_c = token_mixing(hl, hc, rows, cols, w_in[layer], conv_w[layer], conv_b[layer],
                                lru_wa[layer], lru_ba[layer], lru_wx[layer], lru_bx[layer],
                                lru_lambda[layer], w_rnn_proj[layer], w_ret_proj[layer], w_out[layer],
                                not last)
        x = (x + g1 * y_l).astype(x.dtype)
        hl2 = modulate(rmsnorm(x, norm2_g[layer]), sh2, sc2)
        y_moe = moe_ffn(hl2.reshape(-1, D), router_w[layer], router_b[layer], moe_w1[layer],
                        moe_b1[layer], moe_w2[layer], moe_b2[layer]).reshape(B, S, D)
        x = (x + g2 * y_moe).astype(x.dtype)
        if not last:
            h_c = (h_c + cg1 * y_c).astype(h_c.dtype)
            hc2 = modulate(rmsnorm(h_c, norm2_g[layer]), csh2, csc2)
            y_cm = moe_ffn(hc2.reshape(-1, D), router_w[layer], router_b[layer], moe_w1[layer],
                           moe_b1[layer], moe_w2[layer], moe_b2[layer]).reshape(h_c.shape)
            h_c = (h_c + cg2 * y_cm).astype(h_c.dtype)
    return rmsnorm(x, final_g)
```

```python
import functools

import jax
import jax.numpy as jnp
import numpy as np
from jax import lax
from jax.experimental import pallas as pl
from jax.experimental.pallas import tpu as pltpu

F32 = jnp.float32
BF16 = jnp.bfloat16
I32 = jnp.int32

GRID_W = 64
RNN_BLOCKS = 8
CONV_W = 4
LRU_C = 8.0
RET_HEADS = 8
RET_DK = 128
RET_DV = 256
ROPE_BASE = 10000.0
N_EXPERTS = 32
TOP_K = 4
SWIGLU_ALPHA = 1.702
SWIGLU_LIMIT = 7.0
EPS = 1e-6
N_CHUNKS = 10

PROJ_TM = 1024
PROJ_TN = 2048
RNN_TT = 128
RET_C = 256
FIN_TM = 512
MOE_BLOCK = 512
DISP_TM = 256
VMEM_LIMIT = 56 * 1024 * 1024


def _cparams(n_axes):
    return pltpu.CompilerParams(dimension_semantics=("arbitrary",) * n_axes,
                                vmem_limit_bytes=VMEM_LIMIT)


def _sigmoid(x):
    return 0.5 * (jnp.tanh(0.5 * x) + 1.0)


def _rmsnorm(x, g):
    return x * lax.rsqrt(jnp.mean(x * x, axis=-1, keepdims=True) + EPS) * g


def _ada_kernel(c_ref, w_ref, b_ref, o_ref):
    c = c_ref[...]
    s = c * _sigmoid(c)
    o_ref[...] = jnp.dot(s, w_ref[...], preferred_element_type=F32,
                         precision=lax.Precision.HIGHEST) + b_ref[...]


def _ada(cvec, ada_w, ada_b):
    R, D = cvec.shape
    n = ada_w.shape[1] // D
    return pl.pallas_call(
        _ada_kernel,
        grid=(n,),
        in_specs=[pl.BlockSpec((R, D), lambda j: (0, 0)),
                  pl.BlockSpec((D, D), lambda j: (0, j)),
                  pl.BlockSpec((1, D), lambda j: (0, j))],
        out_specs=pl.BlockSpec((R, D), lambda j: (0, j)),
        out_shape=jax.ShapeDtypeStruct((R, n * D), F32),
        compiler_params=_cparams(1),
        name="ada",
    )(cvec, ada_w, ada_b.reshape(1, -1))


def _proj_kernel(x_ref, sh_ref, sc_ref, g_ref, w_ref, tab_ref, o_ref, h_ref):
    j = pl.program_id(2)

    @pl.when(j == 0)
    def _():
        h = _rmsnorm(x_ref[...], g_ref[...])
        h_ref[...] = (h * (1.0 + sc_ref[...]) + sh_ref[...]).astype(BF16)

    acc = jnp.dot(h_ref[...], w_ref[...], preferred_element_type=F32)

    @pl.when(j != 1)
    def _():
        o_ref[...] = acc.astype(o_ref.dtype)

    @pl.when(j == 1)
    def _():
        for part in range(2):
            cos = tab_ref[:, (2 * part) * RET_DK:(2 * part + 1) * RET_DK]
            sin = tab_ref[:, (2 * part + 1) * RET_DK:(2 * part + 2) * RET_DK]
            for hh in range(RET_HEADS):
                c0 = part * RET_HEADS * RET_DK + hh * RET_DK
                t = acc[:, c0:c0 + RET_DK]
                o_ref[:, c0:c0 + RET_DK] = (t * cos + pltpu.roll(t, RET_DK // 2, 1) * sin).astype(o_ref.dtype)


def _proj(x3, mods3, mod_row, norm_g, w_in_b, tab, tm):
    B, S, D = x3.shape
    d_in = w_in_b.shape[1]
    nj = d_in // PROJ_TN
    return pl.pallas_call(
        _proj_kernel,
        grid=(B, S // tm, nj),
        in_specs=[pl.BlockSpec((None, tm, D), lambda b, i, j: (b, i, 0)),
                  pl.BlockSpec((None, 1, D), lambda b, i, j: (mod_row(b), 0, 0)),
                  pl.BlockSpec((None, 1, D), lambda b, i, j: (mod_row(b), 0, 1)),
                  pl.BlockSpec((1, D), lambda b, i, j: (0, 0)),
                  pl.BlockSpec((D, PROJ_TN), lambda b, i, j: (0, j)),
                  pl.BlockSpec((tm, 4 * RET_DK), lambda b, i, j: (i, 0))],
        out_specs=pl.BlockSpec((tm, PROJ_TN), lambda b, i, j: (i, b * nj + j)),
        out_shape=jax.ShapeDtypeStruct((S, B * d_in), BF16),
        scratch_shapes=[pltpu.VMEM((tm, D), BF16)],
        compiler_params=_cparams(3),
        name="proj",
    )(x3, mods3, mods3, norm_g.reshape(1, D), w_in_b, tab)


def _rnn_kernel(xm_ref, xp_ref, xn_ref, cw_ref, cb_ref, wg_ref, ba_ref, bx_ref, lam_ref, h0_ref,
                h_ref, hfin_ref, xs_ref, a_ref, u_ref, hc_ref, *, reverse, n_tiles, tt, nb):
    i = pl.program_id(0)
    tile = (n_tiles - 1 - i) if reverse else i
    R = tt * nb
    halo = 2 * nb
    D = xm_ref.shape[1]
    bw = D // RNN_BLOCKS

    @pl.when(i == 0)
    def _():
        hc_ref[...] = h0_ref[...]

    xs_ref[0:halo, :] = jnp.where(tile > 0, xp_ref[...].astype(F32), 0.0)
    xs_ref[halo:halo + R, :] = xm_ref[...].astype(F32)
    xs_ref[halo + R:2 * halo + R, :] = jnp.where(tile < n_tiles - 1, xn_ref[...].astype(F32), 0.0)

    nl = -lam_ref[...]
    sp = jnp.maximum(nl, 0.0) + jnp.log1p(jnp.exp(-jnp.abs(nl)))
    SUB = 256

    def gates(s, carry):
        r0 = pl.multiple_of(s * SUB, SUB)
        xc = cb_ref[...] + cw_ref[0:1, :] * xs_ref[pl.ds(r0, SUB), :]
        for k in range(1, CONV_W):
            xc = xc + cw_ref[k:k + 1, :] * xs_ref[pl.ds(pl.multiple_of(r0 + k * nb, nb), SUB), :]
        xb = xc.astype(BF16)
        for n in range(RNN_BLOCKS):
            g = jnp.dot(xb[:, n * bw:(n + 1) * bw], wg_ref[n], preferred_element_type=F32)
            cs = slice(n * bw, (n + 1) * bw)
            r = _sigmoid(g[:, :bw] + ba_ref[:, cs])
            ig = _sigmoid(g[:, bw:] + bx_ref[:, cs])
            t = jnp.tanh((-0.5 * LRU_C) * r * sp[:, cs])
            rc = 1.0 / (1.0 - t)
            a_ref[pl.ds(r0, SUB), cs] = (1.0 + t) * rc
            u_ref[pl.ds(r0, SUB), cs] = (2.0 * jnp.sqrt(-t) * rc) * (ig * xc[:, cs])
        return carry

    lax.fori_loop(0, R // SUB, gates, 0)

    def step(t, h):
        ts = (tt - 1 - t) if reverse else t
        r0 = pl.multiple_of(ts * nb, nb)
        h = a_ref[pl.ds(r0, nb), :] * h + u_ref[pl.ds(r0, nb), :]
        u_ref[pl.ds(r0, nb), :] = h
        return h

    h = lax.fori_loop(0, tt, step, hc_ref[...], unroll=8)
    hc_ref[...] = h
    hfin_ref[...] = h
    h_ref[...] = u_ref[...].astype(h_ref.dtype)


def _rnn(z2, conv_w, conv_b, wg, ba, bx, lam, h0, *, reverse, tt, nb):
    rows = z2.shape[0]
    D = conv_w.shape[1]
    T = rows // nb
    tt = min(tt, T)
    n_tiles = T // tt
    R = tt * nb
    halo = 2 * nb
    hb = R // halo

    def tile_of(i):
        return (n_tiles - 1 - i) if reverse else i

    kern = functools.partial(_rnn_kernel, reverse=reverse, n_tiles=n_tiles, tt=tt, nb=nb)
    return pl.pallas_call(
        kern,
        grid=(n_tiles,),
        in_specs=[pl.BlockSpec((R, D), lambda i: (tile_of(i), 0)),
                  pl.BlockSpec((halo, D), lambda i: (jnp.maximum(tile_of(i) * hb - 1, 0), 0)),
                  pl.BlockSpec((halo, D), lambda i: (jnp.minimum((tile_of(i) + 1) * hb, n_tiles * hb - 1), 0)),
                  pl.BlockSpec((CONV_W, D), lambda i: (0, 0)),
                  pl.BlockSpec((1, D), lambda i: (0, 0)),
                  pl.BlockSpec((RNN_BLOCKS, D // RNN_BLOCKS, 2 * D // RNN_BLOCKS), lambda i: (0, 0, 0)),
                  pl.BlockSpec((1, D), lambda i: (0, 0)),
                  pl.BlockSpec((1, D), lambda i: (0, 0)),
                  pl.BlockSpec((1, D), lambda i: (0, 0)),
                  pl.BlockSpec((nb, D), lambda i: (0, 0))],
        out_specs=[pl.BlockSpec((R, D), lambda i: (tile_of(i), 0)),
                   pl.BlockSpec((nb, D), lambda i: (0, 0))],
        out_shape=[jax.ShapeDtypeStruct((rows, D), BF16),
                   jax.ShapeDtypeStruct((nb, D), F32)],
        scratch_shapes=[pltpu.VMEM((R + 2 * halo, D), F32),
                        pltpu.VMEM((R, D), F32),
                        pltpu.VMEM((R, D), F32),
                        pltpu.VMEM((nb, D), F32)],
        compiler_params=_cparams(1),
        name="rnn_bwd" if reverse else "rnn_fwd",
    )(z2, z2, z2, conv_w, conv_b.reshape(1, D), wg, ba.reshape(1, D), bx.reshape(1, D),
      lam.reshape(1, D), h0)


def _tdot(a, b):
    return lax.dot_general(a, b, (((0,), (0,)), ((), ())), preferred_element_type=F32)


def _ret_ctx_kernel(k_ref, v_ref, df_ref, db_ref, sf_ref, sb_ref):
    for hh in range(RET_HEADS):
        kh = k_ref[:, hh * RET_DK:(hh + 1) * RET_DK].astype(F32)
        vh = v_ref[:, hh * RET_DV:(hh + 1) * RET_DV]
        sf_ref[hh] = _tdot((kh * df_ref[hh]).astype(BF16), vh)
        sb_ref[hh] = _tdot((kh * db_ref[hh]).astype(BF16), vh)


def _ret_ctx(zc, B, dec_f, dec_b):
    L = zc.shape[0]
    H = RET_HEADS
    st = jax.ShapeDtypeStruct((B, H, RET_DK, RET_DV), F32)
    return pl.pallas_call(
        _ret_ctx_kernel,
        grid=(B,),
        in_specs=[pl.BlockSpec((L, H * RET_DK), lambda b: (0, b * N_CHUNKS + 3)),
                  pl.BlockSpec((L, H * RET_DV), lambda b: (0, b * (N_CHUNKS // 2) + 2)),
                  pl.BlockSpec((H, L, RET_DK), lambda b: (0, 0, 0)),
                  pl.BlockSpec((H, L, RET_DK), lambda b: (0, 0, 0))],
        out_specs=[pl.BlockSpec((None, H, RET_DK, RET_DV), lambda b: (b, 0, 0, 0))] * 2,
        out_shape=[st, st],
        compiler_params=_cparams(1),
        name="ret_ctx",
    )(zc, zc, dec_f, dec_b)


def _ret_bwd_kernel(k_ref, v_ref, s0_ref, kd_ref, cd_ref, o_ref, st_ref):
    @pl.when(pl.program_id(1) == 0)
    def _():
        st_ref[...] = s0_ref[...]

    for hh in range(RET_HEADS):
        o_ref[hh] = st_ref[hh].astype(o_ref.dtype)
        kh = k_ref[:, hh * RET_DK:(hh + 1) * RET_DK].astype(F32)
        vh = v_ref[:, hh * RET_DV:(hh + 1) * RET_DV]
        st_ref[hh] = st_ref[hh] * cd_ref[hh:hh + 1, :] + _tdot((kh * kd_ref[hh]).astype(BF16), vh)


def _ret_bwd(z, B, s_bwd, kdec_b, cdec):
    S = z.shape[0]
    H, C = RET_HEADS, min(RET_C, S)
    n = S // C
    return pl.pallas_call(
        _ret_bwd_kernel,
        grid=(B, n),
        in_specs=[pl.BlockSpec((C, H * RET_DK), lambda b, j: (n - 1 - j, b * N_CHUNKS + 3)),
                  pl.BlockSpec((C, H * RET_DV), lambda b, j: (n - 1 - j, b * (N_CHUNKS // 2) + 2)),
                  pl.BlockSpec((None, H, RET_DK, RET_DV), lambda b, j: (b, 0, 0, 0)),
                  pl.BlockSpec((H, C, RET_DK), lambda b, j: (0, 0, 0)),
                  pl.BlockSpec((H, RET_DV), lambda b, j: (0, 0))],
        out_specs=pl.BlockSpec((None, None, H, RET_DK, RET_DV), lambda b, j: (b, n - 1 - j, 0, 0, 0)),
        out_shape=jax.ShapeDtypeStruct((B, n, H, RET_DK, RET_DV), BF16),
        scratch_shapes=[pltpu.VMEM((H, RET_DK, RET_DV), F32)],
        compiler_params=_cparams(2),
        name="ret_bwd",
    )(z, z, s_bwd, kdec_b, cdec)


def _ret_fwd_kernel(q_ref, k_ref, v_ref, gs_ref, sb_ref, s0_ref, intra_ref, qf_ref, qb_ref, kf_ref, cd_ref,
                    o_ref, st_ref):
    @pl.when(pl.program_id(1) == 0)
    def _():
        st_ref[...] = s0_ref[...]

    for hh in range(RET_HEADS):
        qh = q_ref[:, hh * RET_DK:(hh + 1) * RET_DK]
        kh = k_ref[:, hh * RET_DK:(hh + 1) * RET_DK]
        vh = v_ref[:, hh * RET_DV:(hh + 1) * RET_DV]
        qf32 = qh.astype(F32)
        s = lax.dot_general(qh, kh, (((1,), (1,)), ((), ())), preferred_element_type=F32) * intra_ref[hh]
        o = jnp.dot(s.astype(BF16), vh, preferred_element_type=F32)
        o = o + jnp.dot((qf32 * qf_ref[hh]).astype(BF16), st_ref[hh].astype(BF16), preferred_element_type=F32)
        o = o + jnp.dot((qf32 * qb_ref[hh]).astype(BF16), sb_ref[hh], preferred_element_type=F32)
        st_ref[hh] = st_ref[hh] * cd_ref[hh:hh + 1, :] + _tdot((kh.astype(F32) * kf_ref[hh]).astype(BF16), vh)
        mu = jnp.mean(o, axis=-1, keepdims=True)
        d = o - mu
        var = jnp.mean(d * d, axis=-1, keepdims=True)
        g = gs_ref[:, hh * RET_DV:(hh + 1) * RET_DV].astype(F32)
        o_ref[:, hh * RET_DV:(hh + 1) * RET_DV] = (g * _sigmoid(g) * d * lax.rsqrt(var + EPS)).astype(o_ref.dtype)


def _ret_fwd(z, B, sb, s_fwd, intra, qdec_f, qdec_b, kdec_f, cdec):
    S = z.shape[0]
    H, C = RET_HEADS, min(RET_C, S)
    n = S // C
    half = N_CHUNKS // 2
    return pl.pallas_call(
        _ret_fwd_kernel,
        grid=(B, n),
        in_specs=[pl.BlockSpec((C, H * RET_DK), lambda b, j: (j, b * N_CHUNKS + 2)),
                  pl.BlockSpec((C, H * RET_DK), lambda b, j: (j, b * N_CHUNKS + 3)),
                  pl.BlockSpec((C, H * RET_DV), lambda b, j: (j, b * half + 2)),
                  pl.BlockSpec((C, H * RET_DV), lambda b, j: (j, b * half + 3)),
                  pl.BlockSpec((None, None, H, RET_DK, RET_DV), lambda b, j: (b, j, 0, 0, 0)),
                  pl.BlockSpec((None, H, RET_DK, RET_DV), lambda b, j: (b, 0, 0, 0)),
                  pl.BlockSpec((H, C, C), lambda b, j: (0, 0, 0)),
                  pl.BlockSpec((H, C, RET_DK), lambda b, j: (0, 0, 0)),
                  pl.BlockSpec((H, C, RET_DK), lambda b, j: (0, 0, 0)),
                  pl.BlockSpec((H, C, RET_DK), lambda b, j: (0, 0, 0)),
                  pl.BlockSpec((H, RET_DV), lambda b, j: (0, 0))],
        out_specs=pl.BlockSpec((None, C, H * RET_DV), lambda b, j: (b, j, 0)),
        out_shape=jax.ShapeDtypeStruct((B, S, H * RET_DV), BF16),
        scratch_shapes=[pltpu.VMEM((H, RET_DK, RET_DV), F32)],
        compiler_params=_cparams(2),
        name="ret_fwd",
    )(z, z, z, z, sb, s_fwd, intra, qdec_f, qdec_b, kdec_f, cdec)


def _finish_kernel(x_ref, hf_ref, hb_ref, gr_ref, gab_ref, ret_ref, g1_ref, sh2_ref, sc2_ref, n2_ref,
                   wr_ref, wt_ref, wo_ref, rwh_ref, rwl_ref, rb_ref,
                   x1_ref, hl_ref, eid_ref, gate_ref, rank_ref, cnt_ref, carry_ref):
    first = jnp.logical_and(pl.program_id(0) == 0, pl.program_id(1) == 0)

    @pl.when(first)
    def _():
        carry_ref[...] = jnp.zeros_like(carry_ref)

    D = x_ref.shape[1]
    tm = x_ref.shape[0]
    rnn = hf_ref[...].astype(F32) + hb_ref[...].astype(F32)
    y_rnn = jnp.dot((rnn * jax.nn.gelu(gr_ref[...].astype(F32))).astype(BF16), wr_ref[...],
                    preferred_element_type=F32)
    y_ret = jnp.dot(ret_ref[...], wt_ref[...], preferred_element_type=F32)
    ga = gab_ref[:, :D].astype(F32)
    gb = gab_ref[:, D:].astype(F32)
    merged = _sigmoid(ga) * y_rnn + _sigmoid(gb) * y_ret
    y = jnp.dot(merged.astype(BF16), wo_ref[...], preferred_element_type=F32)
    x1 = x_ref[...] + g1_ref[...] * y
    x1_ref[...] = x1
    hl = _rmsnorm(x1, n2_ref[...]) * (1.0 + sc2_ref[...]) + sh2_ref[...]
    hl_ref[...] = hl

    hh = hl.astype(BF16)
    hlo = (hl - hh.astype(F32)).astype(BF16)
    logits = (jnp.dot(hh, rwh_ref[...], preferred_element_type=F32)
              + jnp.dot(hlo, rwh_ref[...], preferred_element_type=F32)
              + jnp.dot(hh, rwl_ref[...], preferred_element_type=F32)) + rb_ref[...]

    ne = logits.shape[1]
    lane = lax.broadcasted_iota(I32, (tm, ne), 1)
    work = logits
    vals, idxs, hots = [], [], []
    for _ in range(TOP_K):
        m = jnp.max(work, axis=-1, keepdims=True)
        idx = jnp.min(jnp.where(work == m, lane, ne), axis=-1, keepdims=True)
        hot = lane == idx
        vals.append(m)
        idxs.append(idx)
        hots.append(hot)
        work = jnp.where(hot, -jnp.inf, work)
    es = [jnp.exp(v - vals[0]) for v in vals]
    inv = 1.0 / (es[0] + es[1] + es[2] + es[3])

    oh = jnp.zeros((tm, ne), F32)
    for hot in hots:
        oh = oh + hot.astype(F32)
    row = lax.broadcasted_iota(I32, (tm, tm), 0)
    col = lax.broadcasted_iota(I32, (tm, tm), 1)
    tri = jnp.where(col < row, 1.0, 0.0).astype(BF16)
    before = jnp.dot(tri, oh.astype(BF16), preferred_element_type=F32) + carry_ref[...]
    for k in range(TOP_K):
        eid_ref[:, k:k + 1] = idxs[k]
        gate_ref[:, k:k + 1] = es[k] * inv
        rank_ref[:, k:k + 1] = jnp.sum(jnp.where(hots[k], before, 0.0), axis=-1, keepdims=True).astype(I32)
    carry_ref[...] = carry_ref[...] + jnp.sum(oh, axis=0, keepdims=True)
    cnt_ref[...] = carry_ref[...]


def _finish(x3, hf2, hb2, z, retg, mods3, norm2_g, w_rnn_b, w_ret_b, w_out_b, rw_hi, rw_lo, router_b):
    B, S, D = x3.shape
    tm = min(FIN_TM, S)
    nt = S // tm
    N = B * S
    ne = rw_hi.shape[1]
    half = N_CHUNKS // 2
    const2 = lambda b, i: (0, 0)
    tok = lambda b, i: (b * nt + i, 0)
    return pl.pallas_call(
        _finish_kernel,
        grid=(B, nt),
        in_specs=[pl.BlockSpec((None, tm, D), lambda b, i: (b, i, 0)),
                  pl.BlockSpec((tm, D), lambda b, i: (i, b)),
                  pl.BlockSpec((tm, D), lambda b, i: (i, b)),
                  pl.BlockSpec((tm, D), lambda b, i: (i, b * N_CHUNKS + 1)),
                  pl.BlockSpec((tm, 2 * D), lambda b, i: (i, b * half + 4)),
                  pl.BlockSpec((None, tm, retg.shape[2]), lambda b, i: (b, i, 0)),
                  pl.BlockSpec((None, 1, D), lambda b, i: (b, 0, 2)),
                  pl.BlockSpec((None, 1, D), lambda b, i: (b, 0, 3)),
                  pl.BlockSpec((None, 1, D), lambda b, i: (b, 0, 4)),
                  pl.BlockSpec((1, D), const2),
                  pl.BlockSpec(w_rnn_b.shape, const2),
                  pl.BlockSpec(w_ret_b.shape, const2),
                  pl.BlockSpec(w_out_b.shape, const2),
                  pl.BlockSpec(rw_hi.shape, const2),
                  pl.BlockSpec(rw_lo.shape, const2),
                  pl.BlockSpec((1, ne), const2)],
        out_specs=[pl.BlockSpec((None, tm, D), lambda b, i: (b, i, 0)),
                   pl.BlockSpec((tm, D), tok),
                   pl.BlockSpec((tm, TOP_K), tok),
                   pl.BlockSpec((tm, TOP_K), tok),
                   pl.BlockSpec((tm, TOP_K), tok),
                   pl.BlockSpec((1, ne), const2)],
        out_shape=[jax.ShapeDtypeStruct((B, S, D), F32),
                   jax.ShapeDtypeStruct((N, D), F32),
                   jax.ShapeDtypeStruct((N, TOP_K), I32),
                   jax.ShapeDtypeStruct((N, TOP_K), F32),
                   jax.ShapeDtypeStruct((N, TOP_K), I32),
                   jax.ShapeDtypeStruct((1, ne), F32)],
        scratch_shapes=[pltpu.VMEM((1, ne), F32)],
        compiler_params=_cparams(2),
        name="finish",
    )(x3, hf2, hb2, z, z, retg, mods3, mods3, mods3, norm2_g.reshape(1, D),
      w_rnn_b, w_ret_b, w_out_b, rw_hi, rw_lo, router_b.reshape(1, ne))


def _row_copy(src, s, dst, d, sem):
    return pltpu.make_async_copy(src.at[pl.ds(s, 1), :], dst.at[pl.ds(d, 1), :], sem)


def _dispatch_kernel(dest_ref, x_ref, hp_in_ref, hp_ref, sem):
    del hp_in_ref
    tm = x_ref.shape[0]

    def issue(r, c):
        for k in range(TOP_K):
            _row_copy(x_ref, r, hp_ref, dest_ref[0, r * TOP_K + k], sem).start()
        return c

    lax.fori_loop(0, tm, issue, 0)

    def drain(r, c):
        for k in range(TOP_K):
            _row_copy(x_ref, 0, hp_ref, 0, sem).wait()
        return c

    lax.fori_loop(0, tm, drain, 0)


def _dispatch(hl2, dest3, P):
    N, D = hl2.shape
    tm = dest3.shape[2] // TOP_K
    return pl.pallas_call(
        _dispatch_kernel,
        grid=(N // tm,),
        in_specs=[pl.BlockSpec((None, 1, tm * TOP_K), lambda i: (i, 0, 0), memory_space=pltpu.SMEM),
                  pl.BlockSpec((tm, D), lambda i: (i, 0)),
                  pl.BlockSpec(memory_space=pl.ANY)],
        out_specs=pl.BlockSpec(memory_space=pl.ANY),
        out_shape=jax.ShapeDtypeStruct((P, D), F32),
        scratch_shapes=[pltpu.SemaphoreType.DMA(())],
        input_output_aliases={2: 0},
        compiler_params=_cparams(1),
        name="dispatch",
    )(dest3, hl2, jnp.zeros((P, D), F32))


def _expert_kernel(be_ref, x_ref, w1_ref, b1_ref, w2_ref, b2_ref, o_ref):
    del be_ref
    de = w2_ref.shape[0]
    h = jnp.dot(x_ref[...].astype(BF16), w1_ref[...], preferred_element_type=F32) + b1_ref[...]
    glu = jnp.minimum(h[:, :de], SWIGLU_LIMIT)
    lin = jnp.clip(h[:, de:], -SWIGLU_LIMIT, SWIGLU_LIMIT)
    act = glu * _sigmoid(SWIGLU_ALPHA * glu) * (lin + 1.0)
    o_ref[...] = jnp.dot(act.astype(BF16), w2_ref[...], preferred_element_type=F32) + b2_ref[...]


def _experts(blk_expert, h_pad, w1b, b1p, w2b, b2):
    P, D = h_pad.shape
    ne, _, de2 = w1b.shape
    de = de2 // 2
    nblk = P // MOE_BLOCK
    return pl.pallas_call(
        _expert_kernel,
        grid_spec=pltpu.PrefetchScalarGridSpec(
            num_scalar_prefetch=1,
            grid=(nblk,),
            in_specs=[pl.BlockSpec((MOE_BLOCK, D), lambda j, be: (j, 0)),
                      pl.BlockSpec((None, D, de2), lambda j, be: (be[j], 0, 0)),
                      pl.BlockSpec((None, 1, de2), lambda j, be: (be[j], 0, 0)),
                      pl.BlockSpec((None, de, D), lambda j, be: (be[j], 0, 0)),
                      pl.BlockSpec((None, 1, D), lambda j, be: (be[j], 0, 0))],
            out_specs=pl.BlockSpec((MOE_BLOCK, D), lambda j, be: (j, 0))),
        out_shape=jax.ShapeDtypeStruct((P, D), F32),
        compiler_params=_cparams(1),
        name="experts",
    )(blk_expert, h_pad, w1b, b1p.reshape(ne, 1, de2), w2b, b2.reshape(ne, 1, D))


def _combine_kernel(dest_ref, x1_ref, gate_ref, g2_ref, fg_ref, yp_ref, o_ref, buf_ref, sem):
    tm = x1_ref.shape[0]

    def issue(r, c):
        for k in range(TOP_K):
            _row_copy(yp_ref, dest_ref[0, r * TOP_K + k], buf_ref.at[k], r, sem).start()
        return c

    lax.fori_loop(0, tm, issue, 0)

    def drain(r, c):
        for k in range(TOP_K):
            _row_copy(yp_ref, 0, buf_ref.at[k], 0, sem).wait()
        return c

    lax.fori_loop(0, tm, drain, 0)

    y = gate_ref[:, 0:1] * buf_ref[0]
    for k in range(1, TOP_K):
        y = y + gate_ref[:, k:k + 1] * buf_ref[k]
    x2 = x1_ref[...] + g2_ref[...] * y
    o_ref[...] = _rmsnorm(x2, fg_ref[...])


def _combine(dest3, x1, gates, mods3, final_g, y_pad):
    B, S, D = x1.shape
    tm = dest3.shape[2] // TOP_K
    nt = S // tm
    return pl.pallas_call(
        _combine_kernel,
        grid=(B, nt),
        in_specs=[pl.BlockSpec((None, 1, tm * TOP_K), lambda b, i: (b * nt + i, 0, 0), memory_space=pltpu.SMEM),
                  pl.BlockSpec((None, tm, D), lambda b, i: (b, i, 0)),
                  pl.BlockSpec((tm, TOP_K), lambda b, i: (b * nt + i, 0)),
                  pl.BlockSpec((None, 1, D), lambda b, i: (b, 0, 5)),
                  pl.BlockSpec((1, D), lambda b, i: (0, 0)),
                  pl.BlockSpec(memory_space=pl.ANY)],
        out_specs=pl.BlockSpec((None, tm, D), lambda b, i: (b, i, 0)),
        out_shape=jax.ShapeDtypeStruct((B, S, D), F32),
        scratch_shapes=[pltpu.VMEM((TOP_K, tm, D), F32), pltpu.SemaphoreType.DMA(())],
        compiler_params=_cparams(2),
        name="combine",
    )(dest3, x1, gates, mods3, final_g.reshape(1, D), y_pad)


def _deinterleave_perm(n_heads, dk):
    base = np.concatenate([np.arange(0, dk, 2), np.arange(1, dk, 2)])
    return np.concatenate([h * dk + base for h in range(n_heads)])


def _rope_tables(S, k_scale):
    n_freq = RET_DK // 4
    pos = jnp.arange(S, dtype=F32)
    rows = jnp.floor(pos / GRID_W)
    cols = pos - rows * GRID_W
    inv = ROPE_BASE ** (-jnp.arange(n_freq, dtype=F32) / n_freq)
    ang = jnp.concatenate([rows[:, None] * inv, cols[:, None] * inv], axis=-1)
    cos, sin = jnp.cos(ang), jnp.sin(ang)
    cos2 = jnp.concatenate([cos, cos], axis=-1)
    sin2 = jnp.concatenate([-sin, sin], axis=-1)
    return jnp.concatenate([cos2, sin2, cos2 * k_scale, sin2 * k_scale], axis=-1)


def _identity_tables(L, k_scale):
    one = jnp.ones((L, RET_DK), F32)
    zero = jnp.zeros((L, RET_DK), F32)
    return jnp.concatenate([one, zero, one * k_scale, zero], axis=-1)


def _lanes(t, width):
    return jnp.broadcast_to(t[:, :, None], t.shape + (width,))


def kernel(x, c, ctx, c_ctx, ada_w, ada_b, norm1_g, w_in, conv_w, conv_b, lru_wa, lru_ba, lru_wx, lru_bx,
           lru_lambda, w_rnn_proj, w_ret_proj, w_out, norm2_g, router_w, router_b, moe_w1, moe_b1, moe_w2,
           moe_b2, final_g):
    B, S, D = x.shape
    L = ctx.shape[1]
    N = B * S
    H = RET_HEADS
    lyr = 0
    d_in = w_in.shape[2]
    assert ada_w.shape[0] == 1 and d_in == N_CHUNKS * D and B == 8

    cols = np.arange(d_in)
    cols[2 * D:3 * D] = 2 * D + _deinterleave_perm(H, RET_DK)
    cols[3 * D:4 * D] = 3 * D + _deinterleave_perm(H, RET_DK)
    w_in_b = w_in[lyr][:, cols].astype(BF16)
    wg = [jnp.concatenate([lru_wa[lyr, d], lru_wx[lyr, d]], axis=-1).astype(BF16) for d in range(2)]
    de2 = moe_w1.shape[3]
    glu_lin = np.concatenate([np.arange(0, de2, 2), np.arange(1, de2, 2)])
    w1b = moe_w1[lyr][:, :, glu_lin].astype(BF16)
    b1p = moe_b1[lyr][:, glu_lin]
    w2b = moe_w2[lyr].astype(BF16)
    rw = router_w[lyr]
    rw_hi = rw.astype(BF16)
    rw_lo = (rw - rw_hi.astype(F32)).astype(BF16)

    k_scale = RET_DK ** -0.5
    tab_l = _rope_tables(S, k_scale)
    tab_c = _identity_tables(L, k_scale)
    log_g = jnp.log1p(-jnp.exp2(-5.0 - jnp.arange(H, dtype=F32)))
    C = min(RET_C, S)
    idx = jnp.arange(C, dtype=F32)
    dec = lambda e: jnp.exp(e[None, :] * log_g[:, None])
    intra = jnp.exp(jnp.abs(idx[:, None] - idx[None, :])[None] * log_g[:, None, None])
    qdec_f = _lanes(dec(idx + 1.0), RET_DK)
    qdec_b = _lanes(dec(C - idx), RET_DK)
    kdec_f = _lanes(dec(C - 1.0 - idx), RET_DK)
    kdec_b = _lanes(dec(idx), RET_DK)
    cdec = jnp.broadcast_to(jnp.exp(C * log_g)[:, None], (H, RET_DV))
    pos_c = jnp.arange(L, dtype=F32)
    cdec_f = _lanes(dec(L - 1.0 - pos_c), RET_DK)
    cdec_b = _lanes(dec(pos_c), RET_DK)

    cvec = jnp.zeros((16, D), F32).at[:B].set(c).at[B].set(c_ctx)
    mods3 = _ada(cvec, ada_w[lyr], ada_b[lyr]).reshape(16, 1, 6 * D)

    z_c = _proj(ctx, mods3, lambda b: B, norm1_g[lyr], w_in_b, tab_c, min(PROJ_TM, L))
    z_l = _proj(x, mods3, lambda b: b, norm1_g[lyr], w_in_b, tab_l, min(PROJ_TM, S))

    zc2 = z_c.reshape(L * B, d_in)
    zl2 = z_l.reshape(S * B, d_in)
    zeros = jnp.zeros((B, D), F32)
    hs = []
    for d in range(2):
        args = (conv_w[lyr], conv_b[lyr], wg[d], lru_ba[lyr, d], lru_bx[lyr, d], lru_lambda[lyr, d])
        _, h0 = _rnn(zc2, *args, zeros, reverse=(d == 1), tt=RNN_TT, nb=B)
        h, _ = _rnn(zl2, *args, h0, reverse=(d == 1), tt=RNN_TT, nb=B)
        hs.append(h.reshape(S, B * D))

    s_fwd, s_bwd = _ret_ctx(z_c, B, cdec_f, cdec_b)
    sb = _ret_bwd(z_l, B, s_bwd, kdec_b, cdec)
    retg = _ret_fwd(z_l, B, sb, s_fwd, intra, qdec_f, qdec_b, kdec_f, cdec)

    x1, hl2, eid, gates, rank, counts = _finish(
        x, hs[0], hs[1], z_l, retg, mods3, norm2_g[lyr], w_rnn_proj[lyr].astype(BF16),
        w_ret_proj[lyr].astype(BF16), w_out[lyr].astype(BF16), rw_hi, rw_lo, router_b[lyr])

    cnt = counts[0].astype(I32)
    padded = (cnt + MOE_BLOCK - 1) // MOE_BLOCK * MOE_BLOCK
    pad_end = jnp.cumsum(padded)
    pad_start = pad_end - padded
    dest = pad_start[eid] + rank
    n_blocks = N * TOP_K // MOE_BLOCK + N_EXPERTS
    P = n_blocks * MOE_BLOCK
    blk_start = jnp.arange(n_blocks, dtype=I32) * MOE_BLOCK
    blk_expert = jnp.minimum(jnp.searchsorted(pad_end, blk_start, side="right"), N_EXPERTS - 1).astype(I32)
    tm = min(DISP_TM, S)
    dest3 = dest.reshape(N // tm, 1, tm * TOP_K)

    h_pad = _dispatch(hl2, dest3, P)
    y_pad = _experts(blk_expert, h_pad, w1b, b1p, w2b, moe_b2[lyr])
    return _combine(dest3, x1, gates, mods3, final_g, y_pad)
```

```python
import functools

import jax
import jax.numpy as jnp
import numpy as np
from jax import lax
from jax.experimental import pallas as pl
from jax.experimental.pallas import tpu as pltpu

F32 = jnp.float32
BF16 = jnp.bfloat16
I32 = jnp.int32

GRID_W = 64
RNN_BLOCKS = 8
CONV_W = 4
LRU_C = 8.0
RET_HEADS = 8
RET_DK = 128
RET_DV = 256
ROPE_BASE = 10000.0
N_EXPERTS = 32
TOP_K = 4
SWIGLU_ALPHA = 1.702
SWIGLU_LIMIT = 7.0
EPS = 1e-6
N_CHUNKS = 10

PROJ_TM = 1024
PROJ_TN = 2048
RNN_TT = 128
RET_C = 256
FIN_TM = 512
MOE_BLOCK = 512
DISP_TM = 256
VMEM_LIMIT = 56 * 1024 * 1024


def _cparams(n_axes):
    return pltpu.CompilerParams(dimension_semantics=("arbitrary",) * n_axes,
                                vmem_limit_bytes=VMEM_LIMIT)


def _sigmoid(x):
    return 0.5 * (jnp.tanh(0.5 * x) + 1.0)


def _rmsnorm(x, g):
    return x * lax.rsqrt(jnp.mean(x * x, axis=-1, keepdims=True) + EPS) * g


def _ada_kernel(c_ref, w_ref, b_ref, o_ref):
    c = c_ref[...]
    s = c * _sigmoid(c)
    o_ref[...] = jnp.dot(s, w_ref[...], preferred_element_type=F32,
                         precision=lax.Precision.HIGHEST) + b_ref[...]


def _ada(cvec, ada_w, ada_b):
    R, D = cvec.shape
    n = ada_w.shape[1] // D
    return pl.pallas_call(
        _ada_kernel,
        grid=(n,),
        in_specs=[pl.BlockSpec((R, D), lambda j: (0, 0)),
                  pl.BlockSpec((D, D), lambda j: (0, j)),
                  pl.BlockSpec((1, D), lambda j: (0, j))],
        out_specs=pl.BlockSpec((R, D), lambda j: (0, j)),
        out_shape=jax.ShapeDtypeStruct((R, n * D), F32),
        compiler_params=_cparams(1),
        name="ada",
    )(cvec, ada_w, ada_b.reshape(1, -1))


def _proj_kernel(x_ref, sh_ref, sc_ref, g_ref, w_ref, tab_ref, o_ref, xr_ref, h_ref):
    j = pl.program_id(2)
    D = x_ref.shape[1]

    @pl.when(j == 0)
    def _():
        h = _rmsnorm(x_ref[...], g_ref[...])
        h_ref[...] = (h * (1.0 + sc_ref[...]) + sh_ref[...]).astype(BF16)

    acc = jnp.dot(h_ref[...], w_ref[...], preferred_element_type=F32)

    @pl.when(j != 1)
    def _():
        o_ref[...] = acc.astype(o_ref.dtype)

    @pl.when(j == 0)
    def _():
        xr_ref[...] = acc[:, :D].astype(xr_ref.dtype)

    @pl.when(j == 1)
    def _():
        even = lax.broadcasted_iota(I32, (acc.shape[0], RET_DK), 1) % 2 == 0
        for part in range(2):
            cos = tab_ref[:, (2 * part) * RET_DK:(2 * part + 1) * RET_DK]
            sin = tab_ref[:, (2 * part + 1) * RET_DK:(2 * part + 2) * RET_DK]
            for hh in range(RET_HEADS):
                c0 = part * RET_HEADS * RET_DK + hh * RET_DK
                t = acc[:, c0:c0 + RET_DK]
                sw = jnp.where(even, pltpu.roll(t, RET_DK - 1, 1), pltpu.roll(t, 1, 1))
                o_ref[:, c0:c0 + RET_DK] = (t * cos + sw * sin).astype(o_ref.dtype)


def _proj(x3, mods3, mod_row, norm_g, w_in_b, tab, tm):
    B, S, D = x3.shape
    d_in = w_in_b.shape[1]
    nj = d_in // PROJ_TN
    return pl.pallas_call(
        _proj_kernel,
        grid=(B, S // tm, nj),
        in_specs=[pl.BlockSpec((None, tm, D), lambda b, i, j: (b, i, 0)),
                  pl.BlockSpec((None, 1, D), lambda b, i, j: (mod_row(b), 0, 0)),
                  pl.BlockSpec((None, 1, D), lambda b, i, j: (mod_row(b), 0, 1)),
                  pl.BlockSpec((1, D), lambda b, i, j: (0, 0)),
                  pl.BlockSpec((D, PROJ_TN), lambda b, i, j: (0, j)),
                  pl.BlockSpec((tm, 4 * RET_DK), lambda b, i, j: (i, 0))],
        out_specs=[pl.BlockSpec((tm, PROJ_TN), lambda b, i, j: (i, b * nj + j)),
                   pl.BlockSpec((None, tm, D), lambda b, i, j: (b, i, 0))],
        out_shape=[jax.ShapeDtypeStruct((S, B * d_in), BF16),
                   jax.ShapeDtypeStruct((B, S, D), BF16)],
        scratch_shapes=[pltpu.VMEM((tm, D), BF16)],
        compiler_params=_cparams(3),
        name="proj",
    )(x3, mods3, mods3, norm_g.reshape(1, D), w_in_b, tab)


def _rnn_kernel(xm_ref, xp_ref, xn_ref, cw_ref, cb_ref, wg_ref, ba_ref, bx_ref, lam_ref, h0_ref,
                h_ref, hfin_ref, xs_ref, a_ref, u_ref, hc_ref, *, reverse, n_tiles):
    i = pl.program_id(0)
    tile = (n_tiles - 1 - i) if reverse else i
    nb, tt, D = xm_ref.shape
    bw = D // RNN_BLOCKS
    HALO = xp_ref.shape[1]

    @pl.when(i == 0)
    def _():
        hc_ref[...] = h0_ref[...]

    xs_ref[:, 0:HALO, :] = jnp.where(tile > 0, xp_ref[...].astype(F32), 0.0)
    xs_ref[:, HALO:HALO + tt, :] = xm_ref[...].astype(F32)
    xs_ref[:, HALO + tt:2 * HALO + tt, :] = jnp.where(tile < n_tiles - 1, xn_ref[...].astype(F32), 0.0)

    nl = -lam_ref[...]
    sp = jnp.maximum(nl, 0.0) + jnp.log1p(jnp.exp(-jnp.abs(nl)))

    def gates(b, carry):
        r0 = pl.multiple_of(b * tt, tt)
        xc = cb_ref[...] + cw_ref[0:1, :] * xs_ref[b, HALO - 2:HALO - 2 + tt, :]
        for k in range(1, CONV_W):
            xc = xc + cw_ref[k:k + 1, :] * xs_ref[b, HALO - 2 + k:HALO - 2 + k + tt, :]
        xb = xc.astype(BF16)
        for n in range(RNN_BLOCKS):
            g = jnp.dot(xb[:, n * bw:(n + 1) * bw], wg_ref[n], preferred_element_type=F32)
            cs = slice(n * bw, (n + 1) * bw)
            r = _sigmoid(g[:, :bw] + ba_ref[:, cs])
            ig = _sigmoid(g[:, bw:] + bx_ref[:, cs])
            t = jnp.tanh((-0.5 * LRU_C) * r * sp[:, cs])
            rc = 1.0 / (1.0 - t)
            a_ref[n, pl.ds(r0, tt), :] = (1.0 + t) * rc
            u_ref[n, pl.ds(r0, tt), :] = (2.0 * jnp.sqrt(-t) * rc) * (ig * xc[:, cs])
        return carry

    lax.fori_loop(0, nb, gates, 0)

    def step(t, hs):
        ts = (tt - 1 - t) if reverse else t
        out = []
        for n in range(RNN_BLOCKS):
            rows = pl.ds(ts, nb, stride=tt)
            h = a_ref[n, rows, :] * hs[n] + u_ref[n, rows, :]
            u_ref[n, rows, :] = h
            out.append(h)
        return tuple(out)

    hs = tuple(hc_ref[:, n * bw:(n + 1) * bw] for n in range(RNN_BLOCKS))
    hs = lax.fori_loop(0, tt, step, hs, unroll=8)
    for n in range(RNN_BLOCKS):
        hc_ref[:, n * bw:(n + 1) * bw] = hs[n]
        hfin_ref[:, n * bw:(n + 1) * bw] = hs[n]
        for b in range(nb):
            h_ref[b, :, n * bw:(n + 1) * bw] = u_ref[n, b * tt:(b + 1) * tt, :].astype(h_ref.dtype)


def _rnn(xr, conv_w, conv_b, wg, ba, bx, lam, h0, *, reverse, tt):
    nb, T, D = xr.shape
    tt = min(tt, T)
    n_tiles = T // tt
    HALO = 16
    hb = tt // HALO

    def tile_of(i):
        return (n_tiles - 1 - i) if reverse else i

    kern = functools.partial(_rnn_kernel, reverse=reverse, n_tiles=n_tiles)
    return pl.pallas_call(
        kern,
        grid=(n_tiles,),
        in_specs=[pl.BlockSpec((nb, tt, D), lambda i: (0, tile_of(i), 0)),
                  pl.BlockSpec((nb, HALO, D), lambda i: (0, jnp.maximum(tile_of(i) * hb - 1, 0), 0)),
                  pl.BlockSpec((nb, HALO, D), lambda i: (0, jnp.minimum((tile_of(i) + 1) * hb, n_tiles * hb - 1), 0)),
                  pl.BlockSpec((CONV_W, D), lambda i: (0, 0)),
                  pl.BlockSpec((1, D), lambda i: (0, 0)),
                  pl.BlockSpec((RNN_BLOCKS, D // RNN_BLOCKS, 2 * D // RNN_BLOCKS), lambda i: (0, 0, 0)),
                  pl.BlockSpec((1, D), lambda i: (0, 0)),
                  pl.BlockSpec((1, D), lambda i: (0, 0)),
                  pl.BlockSpec((1, D), lambda i: (0, 0)),
                  pl.BlockSpec((nb, D), lambda i: (0, 0))],
        out_specs=[pl.BlockSpec((nb, tt, D), lambda i: (0, tile_of(i), 0)),
                   pl.BlockSpec((nb, D), lambda i: (0, 0))],
        out_shape=[jax.ShapeDtypeStruct((nb, T, D), BF16),
                   jax.ShapeDtypeStruct((nb, D), F32)],
        scratch_shapes=[pltpu.VMEM((nb, tt + 2 * HALO, D), F32),
                        pltpu.VMEM((RNN_BLOCKS, nb * tt, D // RNN_BLOCKS), F32),
                        pltpu.VMEM((RNN_BLOCKS, nb * tt, D // RNN_BLOCKS), F32),
                        pltpu.VMEM((nb, D), F32)],
        compiler_params=_cparams(1),
        name="rnn_bwd" if reverse else "rnn_fwd",
    )(xr, xr, xr, conv_w, conv_b.reshape(1, D), wg, ba.reshape(1, D), bx.reshape(1, D),
      lam.reshape(1, D), h0)


def _tdot(a, b):
    return lax.dot_general(a, b, (((0,), (0,)), ((), ())), preferred_element_type=F32)


def _ret_ctx_kernel(k_ref, v_ref, df_ref, db_ref, sf_ref, sb_ref):
    for hh in range(RET_HEADS):
        kh = k_ref[:, hh * RET_DK:(hh + 1) * RET_DK].astype(F32)
        vh = v_ref[:, hh * RET_DV:(hh + 1) * RET_DV]
        sf_ref[hh] = _tdot((kh * df_ref[hh]).astype(BF16), vh)
        sb_ref[hh] = _tdot((kh * db_ref[hh]).astype(BF16), vh)


def _ret_ctx(zc, B, dec_f, dec_b):
    L = zc.shape[0]
    H = RET_HEADS
    st = jax.ShapeDtypeStruct((B, H, RET_DK, RET_DV), F32)
    return pl.pallas_call(
        _ret_ctx_kernel,
        grid=(B,),
        in_specs=[pl.BlockSpec((L, H * RET_DK), lambda b: (0, b * N_CHUNKS + 3)),
                  pl.BlockSpec((L, H * RET_DV), lambda b: (0, b * (N_CHUNKS // 2) + 2)),
                  pl.BlockSpec((H, L, RET_DK), lambda b: (0, 0, 0)),
                  pl.BlockSpec((H, L, RET_DK), lambda b: (0, 0, 0))],
        out_specs=[pl.BlockSpec((None, H, RET_DK, RET_DV), lambda b: (b, 0, 0, 0))] * 2,
        out_shape=[st, st],
        compiler_params=_cparams(1),
        name="ret_ctx",
    )(zc, zc, dec_f, dec_b)


def _ret_bwd_kernel(k_ref, v_ref, s0_ref, kd_ref, cd_ref, o_ref, st_ref):
    @pl.when(pl.program_id(1) == 0)
    def _():
        st_ref[...] = s0_ref[...]

    for hh in range(RET_HEADS):
        o_ref[hh] = st_ref[hh].astype(o_ref.dtype)
        kh = k_ref[:, hh * RET_DK:(hh + 1) * RET_DK].astype(F32)
        vh = v_ref[:, hh * RET_DV:(hh + 1) * RET_DV]
        st_ref[hh] = st_ref[hh] * cd_ref[hh:hh + 1, :] + _tdot((kh * kd_ref[hh]).astype(BF16), vh)


def _ret_bwd(z, B, s_bwd, kdec_b, cdec):
    S = z.shape[0]
    H, C = RET_HEADS, min(RET_C, S)
    n = S // C
    return pl.pallas_call(
        _ret_bwd_kernel,
        grid=(B, n),
        in_specs=[pl.BlockSpec((C, H * RET_DK), lambda b, j: (n - 1 - j, b * N_CHUNKS + 3)),
                  pl.BlockSpec((C, H * RET_DV), lambda b, j: (n - 1 - j, b * (N_CHUNKS // 2) + 2)),
                  pl.BlockSpec((None, H, RET_DK, RET_DV), lambda b, j: (b, 0, 0, 0)),
                  pl.BlockSpec((H, C, RET_DK), lambda b, j: (0, 0, 0)),
                  pl.BlockSpec((H, RET_DV), lambda b, j: (0, 0))],
        out_specs=pl.BlockSpec((None, None, H, RET_DK, RET_DV), lambda b, j: (b, n - 1 - j, 0, 0, 0)),
        out_shape=jax.ShapeDtypeStruct((B, n, H, RET_DK, RET_DV), BF16),
        scratch_shapes=[pltpu.VMEM((H, RET_DK, RET_DV), F32)],
        compiler_params=_cparams(2),
        name="ret_bwd",
    )(z, z, s_bwd, kdec_b, cdec)


def _ret_fwd_kernel(q_ref, k_ref, v_ref, gs_ref, sb_ref, s0_ref, intra_ref, qf_ref, qb_ref, kf_ref, cd_ref,
                    o_ref, st_ref):
    @pl.when(pl.program_id(1) == 0)
    def _():
        st_ref[...] = s0_ref[...]

    for hh in range(RET_HEADS):
        qh = q_ref[:, hh * RET_DK:(hh + 1) * RET_DK]
        kh = k_ref[:, hh * RET_DK:(hh + 1) * RET_DK]
        vh = v_ref[:, hh * RET_DV:(hh + 1) * RET_DV]
        qf32 = qh.astype(F32)
        s = lax.dot_general(qh, kh, (((1,), (1,)), ((), ())), preferred_element_type=F32) * intra_ref[hh]
        o = jnp.dot(s.astype(BF16), vh, preferred_element_type=F32)
        o = o + jnp.dot((qf32 * qf_ref[hh]).astype(BF16), st_ref[hh].astype(BF16), preferred_element_type=F32)
        o = o + jnp.dot((qf32 * qb_ref[hh]).astype(BF16), sb_ref[hh], preferred_element_type=F32)
        st_ref[hh] = st_ref[hh] * cd_ref[hh:hh + 1, :] + _tdot((kh.astype(F32) * kf_ref[hh]).astype(BF16), vh)
        mu = jnp.mean(o, axis=-1, keepdims=True)
        d = o - mu
        var = jnp.mean(d * d, axis=-1, keepdims=True)
        g = gs_ref[:, hh * RET_DV:(hh + 1) * RET_DV].astype(F32)
        o_ref[:, hh * RET_DV:(hh + 1) * RET_DV] = (g * _sigmoid(g) * d * lax.rsqrt(var + EPS)).astype(o_ref.dtype)


def _ret_fwd(z, B, sb, s_fwd, intra, qdec_f, qdec_b, kdec_f, cdec):
    S = z.shape[0]
    H, C = RET_HEADS, min(RET_C, S)
    n = S // C
    half = N_CHUNKS // 2
    return pl.pallas_call(
        _ret_fwd_kernel,
        grid=(B, n),
        in_specs=[pl.BlockSpec((C, H * RET_DK), lambda b, j: (j, b * N_CHUNKS + 2)),
                  pl.BlockSpec((C, H * RET_DK), lambda b, j: (j, b * N_CHUNKS + 3)),
                  pl.BlockSpec((C, H * RET_DV), lambda b, j: (j, b * half + 2)),
                  pl.BlockSpec((C, H * RET_DV), lambda b, j: (j, b * half + 3)),
                  pl.BlockSpec((None, None, H, RET_DK, RET_DV), lambda b, j: (b, j, 0, 0, 0)),
                  pl.BlockSpec((None, H, RET_DK, RET_DV), lambda b, j: (b, 0, 0, 0)),
                  pl.BlockSpec((H, C, C), lambda b, j: (0, 0, 0)),
                  pl.BlockSpec((H, C, RET_DK), lambda b, j: (0, 0, 0)),
                  pl.BlockSpec((H, C, RET_DK), lambda b, j: (0, 0, 0)),
                  pl.BlockSpec((H, C, RET_DK), lambda b, j: (0, 0, 0)),
                  pl.BlockSpec((H, RET_DV), lambda b, j: (0, 0))],
        out_specs=pl.BlockSpec((None, C, H * RET_DV), lambda b, j: (b, j, 0)),
        out_shape=jax.ShapeDtypeStruct((B, S, H * RET_DV), BF16),
        scratch_shapes=[pltpu.VMEM((H, RET_DK, RET_DV), F32)],
        compiler_params=_cparams(2),
        name="ret_fwd",
    )(z, z, z, z, sb, s_fwd, intra, qdec_f, qdec_b, kdec_f, cdec)


def _finish_kernel(x_ref, hf_ref, hb_ref, gr_ref, gab_ref, ret_ref, g1_ref, sh2_ref, sc2_ref, n2_ref,
                   wr_ref, wt_ref, wo_ref, rwh_ref, rwl_ref, rb_ref,
                   x1_ref, hl_ref, eid_ref, gate_ref, rank_ref, cnt_ref, carry_ref):
    first = jnp.logical_and(pl.program_id(0) == 0, pl.program_id(1) == 0)

    @pl.when(first)
    def _():
        carry_ref[...] = jnp.zeros_like(carry_ref)

    D = x_ref.shape[1]
    tm = x_ref.shape[0]
    rnn = hf_ref[...].astype(F32) + hb_ref[...].astype(F32)
    y_rnn = jnp.dot((rnn * jax.nn.gelu(gr_ref[...].astype(F32))).astype(BF16), wr_ref[...],
                    preferred_element_type=F32)
    y_ret = jnp.dot(ret_ref[...], wt_ref[...], preferred_element_type=F32)
    ga = gab_ref[:, :D].astype(F32)
    gb = gab_ref[:, D:].astype(F32)
    merged = _sigmoid(ga) * y_rnn + _sigmoid(gb) * y_ret
    y = jnp.dot(merged.astype(BF16), wo_ref[...], preferred_element_type=F32)
    x1 = x_ref[...] + g1_ref[...] * y
    x1_ref[...] = x1
    hl = _rmsnorm(x1, n2_ref[...]) * (1.0 + sc2_ref[...]) + sh2_ref[...]
    hl_ref[...] = hl

    hh = hl.astype(BF16)
    hlo = (hl - hh.astype(F32)).astype(BF16)
    logits = (jnp.dot(hh, rwh_ref[...], preferred_element_type=F32)
              + jnp.dot(hlo, rwh_ref[...], preferred_element_type=F32)
              + jnp.dot(hh, rwl_ref[...], preferred_element_type=F32)) + rb_ref[...]

    ne = logits.shape[1]
    lane = lax.broadcasted_iota(I32, (tm, ne), 1)
    work = logits
    vals, idxs, hots = [], [], []
    for _ in range(TOP_K):
        m = jnp.max(work, axis=-1, keepdims=True)
        idx = jnp.min(jnp.where(work == m, lane, ne), axis=-1, keepdims=True)
        hot = lane == idx
        vals.append(m)
        idxs.append(idx)
        hots.append(hot)
        work = jnp.where(hot, -jnp.inf, work)
    es = [jnp.exp(v - vals[0]) for v in vals]
    inv = 1.0 / (es[0] + es[1] + es[2] + es[3])

    oh = jnp.zeros((tm, ne), F32)
    for hot in hots:
        oh = oh + hot.astype(F32)
    row = lax.broadcasted_iota(I32, (tm, tm), 0)
    col = lax.broadcasted_iota(I32, (tm, tm), 1)
    tri = jnp.where(col < row, 1.0, 0.0).astype(BF16)
    before = jnp.dot(tri, oh.astype(BF16), preferred_element_type=F32) + carry_ref[...]
    for k in range(TOP_K):
        eid_ref[:, k:k + 1] = idxs[k]
        gate_ref[:, k:k + 1] = es[k] * inv
        rank_ref[:, k:k + 1] = jnp.sum(jnp.where(hots[k], before, 0.0), axis=-1, keepdims=True).astype(I32)
    carry_ref[...] = carry_ref[...] + jnp.sum(oh, axis=0, keepdims=True)
    cnt_ref[...] = carry_ref[...]


def _finish(x3, hf2, hb2, z, retg, mods3, norm2_g, w_rnn_b, w_ret_b, w_out_b, rw_hi, rw_lo, router_b):
    B, S, D = x3.shape
    tm = min(FIN_TM, S)
    nt = S // tm
    N = B * S
    ne = rw_hi.shape[1]
    half = N_CHUNKS // 2
    const2 = lambda b, i: (0, 0)
    tok = lambda b, i: (b * nt + i, 0)
    return pl.pallas_call(
        _finish_kernel,
        grid=(B, nt),
        in_specs=[pl.BlockSpec((None, tm, D), lambda b, i: (b, i, 0)),
                  pl.BlockSpec((None, tm, D), lambda b, i: (b, i, 0)),
                  pl.BlockSpec((None, tm, D), lambda b, i: (b, i, 0)),
                  pl.BlockSpec((tm, D), lambda b, i: (i, b * N_CHUNKS + 1)),
                  pl.BlockSpec((tm, 2 * D), lambda b, i: (i, b * half + 4)),
                  pl.BlockSpec((None, tm, retg.shape[2]), lambda b, i: (b, i, 0)),
                  pl.BlockSpec((None, 1, D), lambda b, i: (b, 0, 2)),
                  pl.BlockSpec((None, 1, D), lambda b, i: (b, 0, 3)),
                  pl.BlockSpec((None, 1, D), lambda b, i: (b, 0, 4)),
                  pl.BlockSpec((1, D), const2),
                  pl.BlockSpec(w_rnn_b.shape, const2),
                  pl.BlockSpec(w_ret_b.shape, const2),
                  pl.BlockSpec(w_out_b.shape, const2),
                  pl.BlockSpec(rw_hi.shape, const2),
                  pl.BlockSpec(rw_lo.shape, const2),
                  pl.BlockSpec((1, ne), const2)],
        out_specs=[pl.BlockSpec((None, tm, D), lambda b, i: (b, i, 0)),
                   pl.BlockSpec((tm, D), tok),
                   pl.BlockSpec((tm, TOP_K), tok),
                   pl.BlockSpec((tm, TOP_K), tok),
                   pl.BlockSpec((tm, TOP_K), tok),
                   pl.BlockSpec((1, ne), const2)],
        out_shape=[jax.ShapeDtypeStruct((B, S, D), F32),
                   jax.ShapeDtypeStruct((N, D), F32),
                   jax.ShapeDtypeStruct((N, TOP_K), I32),
                   jax.ShapeDtypeStruct((N, TOP_K), F32),
                   jax.ShapeDtypeStruct((N, TOP_K), I32),
                   jax.ShapeDtypeStruct((1, ne), F32)],
        scratch_shapes=[pltpu.VMEM((1, ne), F32)],
        compiler_params=_cparams(2),
        name="finish",
    )(x3, hf2, hb2, z, z, retg, mods3, mods3, mods3, norm2_g.reshape(1, D),
      w_rnn_b, w_ret_b, w_out_b, rw_hi, rw_lo, router_b.reshape(1, ne))


def _row_copy(src, s, dst, d, sem):
    return pltpu.make_async_copy(src.at[pl.ds(s, 1), :], dst.at[pl.ds(d, 1), :], sem)


def _dispatch_kernel(dest_ref, x_ref, hp_in_ref, hp_ref, sem):
    del hp_in_ref
    tm = x_ref.shape[0]

    def issue(r, c):
        for k in range(TOP_K):
            _row_copy(x_ref, r, hp_ref, dest_ref[0, r * TOP_K + k], sem).start()
        return c

    lax.fori_loop(0, tm, issue, 0)

    def drain(r, c):
        for k in range(TOP_K):
            _row_copy(x_ref, 0, hp_ref, 0, sem).wait()
        return c

    lax.fori_loop(0, tm, drain, 0)


def _dispatch(hl2, dest3, P):
    N, D = hl2.shape
    tm = dest3.shape[2] // TOP_K
    return pl.pallas_call(
        _dispatch_kernel,
        grid=(N // tm,),
        in_specs=[pl.BlockSpec((None, 1, tm * TOP_K), lambda i: (i, 0, 0), memory_space=pltpu.SMEM),
                  pl.BlockSpec((tm, D), lambda i: (i, 0)),
                  pl.BlockSpec(memory_space=pl.ANY)],
        out_specs=pl.BlockSpec(memory_space=pl.ANY),
        out_shape=jax.ShapeDtypeStruct((P, D), F32),
        scratch_shapes=[pltpu.SemaphoreType.DMA(())],
        input_output_aliases={2: 0},
        compiler_params=_cparams(1),
        name="dispatch",
    )(dest3, hl2, jnp.zeros((P, D), F32))


def _expert_kernel(be_ref, x_ref, w1_ref, b1_ref, w2_ref, b2_ref, o_ref):
    del be_ref
    de = w2_ref.shape[0]
    h = jnp.dot(x_ref[...].astype(BF16), w1_ref[...], preferred_element_type=F32) + b1_ref[...]
    glu = jnp.minimum(h[:, :de], SWIGLU_LIMIT)
    lin = jnp.clip(h[:, de:], -SWIGLU_LIMIT, SWIGLU_LIMIT)
    act = glu * _sigmoid(SWIGLU_ALPHA * glu) * (lin + 1.0)
    o_ref[...] = jnp.dot(act.astype(BF16), w2_ref[...], preferred_element_type=F32) + b2_ref[...]


def _experts(blk_expert, h_pad, w1b, b1p, w2b, b2):
    P, D = h_pad.shape
    ne, _, de2 = w1b.shape
    de = de2 // 2
    nblk = P // MOE_BLOCK
    return pl.pallas_call(
        _expert_kernel,
        grid_spec=pltpu.PrefetchScalarGridSpec(
            num_scalar_prefetch=1,
            grid=(nblk,),
            in_specs=[pl.BlockSpec((MOE_BLOCK, D), lambda j, be: (j, 0)),
                      pl.BlockSpec((None, D, de2), lambda j, be: (be[j], 0, 0)),
                      pl.BlockSpec((None, 1, de2), lambda j, be: (be[j], 0, 0)),
                      pl.BlockSpec((None, de, D), lambda j, be: (be[j], 0, 0)),
                      pl.BlockSpec((None, 1, D), lambda j, be: (be[j], 0, 0))],
            out_specs=pl.BlockSpec((MOE_BLOCK, D), lambda j, be: (j, 0))),
        out_shape=jax.ShapeDtypeStruct((P, D), F32),
        compiler_params=_cparams(1),
        name="experts",
    )(blk_expert, h_pad, w1b, b1p.reshape(ne, 1, de2), w2b, b2.reshape(ne, 1, D))


def _combine_kernel(dest_ref, x1_ref, gate_ref, g2_ref, fg_ref, yp_ref, o_ref, buf_ref, sem):
    tm = x1_ref.shape[0]

    def issue(r, c):
        for k in range(TOP_K):
            _row_copy(yp_ref, dest_ref[0, r * TOP_K + k], buf_ref.at[k], r, sem).start()
        return c

    lax.fori_loop(0, tm, issue, 0)

    def drain(r, c):
        for k in range(TOP_K):
            _row_copy(yp_ref, 0, buf_ref.at[k], 0, sem).wait()
        return c

    lax.fori_loop(0, tm, drain, 0)

    y = gate_ref[:, 0:1] * buf_ref[0]
    for k in range(1, TOP_K):
        y = y + gate_ref[:, k:k + 1] * buf_ref[k]
    x2 = x1_ref[...] + g2_ref[...] * y
    o_ref[...] = _rmsnorm(x2, fg_ref[...])


def _combine(dest3, x1, gates, mods3, final_g, y_pad):
    B, S, D = x1.shape
    tm = dest3.shape[2] // TOP_K
    nt = S // tm
    return pl.pallas_call(
        _combine_kernel,
        grid=(B, nt),
        in_specs=[pl.BlockSpec((None, 1, tm * TOP_K), lambda b, i: (b * nt + i, 0, 0), memory_space=pltpu.SMEM),
                  pl.BlockSpec((None, tm, D), lambda b, i: (b, i, 0)),
                  pl.BlockSpec((tm, TOP_K), lambda b, i: (b * nt + i, 0)),
                  pl.BlockSpec((None, 1, D), lambda b, i: (b, 0, 5)),
                  pl.BlockSpec((1, D), lambda b, i: (0, 0)),
                  pl.BlockSpec(memory_space=pl.ANY)],
        out_specs=pl.BlockSpec((None, tm, D), lambda b, i: (b, i, 0)),
        out_shape=jax.ShapeDtypeStruct((B, S, D), F32),
        scratch_shapes=[pltpu.VMEM((TOP_K, tm, D), F32), pltpu.SemaphoreType.DMA(())],
        compiler_params=_cparams(2),
        name="combine",
    )(dest3, x1, gates, mods3, final_g.reshape(1, D), y_pad)


def _rope_tables(S, k_scale):
    n_freq = RET_DK // 4
    pos = jnp.arange(S, dtype=F32)
    rows = jnp.floor(pos / GRID_W)
    cols = pos - rows * GRID_W
    inv = ROPE_BASE ** (-jnp.arange(n_freq, dtype=F32) / n_freq)
    ang = jnp.concatenate([rows[:, None] * inv, cols[:, None] * inv], axis=-1)
    cos, sin = jnp.cos(ang), jnp.sin(ang)
    cos2 = jnp.repeat(cos, 2, axis=-1)
    sin2 = jnp.stack([-sin, sin], axis=-1).reshape(S, RET_DK)
    return jnp.concatenate([cos2, sin2, cos2 * k_scale, sin2 * k_scale], axis=-1)


def _identity_tables(L, k_scale):
    one = jnp.ones((L, RET_DK), F32)
    zero = jnp.zeros((L, RET_DK), F32)
    return jnp.concatenate([one, zero, one * k_scale, zero], axis=-1)


def _lanes(t, width):
    return jnp.broadcast_to(t[:, :, None], t.shape + (width,))


def kernel(x, c, ctx, c_ctx, ada_w, ada_b, norm1_g, w_in, conv_w, conv_b, lru_wa, lru_ba, lru_wx, lru_bx,
           lru_lambda, w_rnn_proj, w_ret_proj, w_out, norm2_g, router_w, router_b, moe_w1, moe_b1, moe_w2,
           moe_b2, final_g):
    B, S, D = x.shape
    L = ctx.shape[1]
    N = B * S
    H = RET_HEADS
    lyr = 0
    d_in = w_in.shape[2]
    assert ada_w.shape[0] == 1 and d_in == N_CHUNKS * D and B == 8

    w_in_b = w_in[lyr].astype(BF16)
    wg = [jnp.concatenate([lru_wa[lyr, d], lru_wx[lyr, d]], axis=-1).astype(BF16) for d in range(2)]
    de2 = moe_w1.shape[3]
    glu_lin = np.concatenate([np.arange(0, de2, 2), np.arange(1, de2, 2)])
    w1b = moe_w1[lyr][:, :, glu_lin].astype(BF16)
    b1p = moe_b1[lyr][:, glu_lin]
    w2b = moe_w2[lyr].astype(BF16)
    rw = router_w[lyr]
    rw_hi = rw.astype(BF16)
    rw_lo = (rw - rw_hi.astype(F32)).astype(BF16)

    k_scale = RET_DK ** -0.5
    tab_l = _rope_tables(S, k_scale)
    tab_c = _identity_tables(L, k_scale)
    log_g = jnp.log1p(-jnp.exp2(-5.0 - jnp.arange(H, dtype=F32)))
    C = min(RET_C, S)
    idx = jnp.arange(C, dtype=F32)
    dec = lambda e: jnp.exp(e[None, :] * log_g[:, None])
    intra = jnp.exp(jnp.abs(idx[:, None] - idx[None, :])[None] * log_g[:, None, None])
    qdec_f = _lanes(dec(idx + 1.0), RET_DK)
    qdec_b = _lanes(dec(C - idx), RET_DK)
    kdec_f = _lanes(dec(C - 1.0 - idx), RET_DK)
    kdec_b = _lanes(dec(idx), RET_DK)
    cdec = jnp.broadcast_to(jnp.exp(C * log_g)[:, None], (H, RET_DV))
    pos_c = jnp.arange(L, dtype=F32)
    cdec_f = _lanes(dec(L - 1.0 - pos_c), RET_DK)
    cdec_b = _lanes(dec(pos_c), RET_DK)

    cvec = jnp.zeros((16, D), F32).at[:B].set(c).at[B].set(c_ctx)
    mods3 = _ada(cvec, ada_w[lyr], ada_b[lyr]).reshape(16, 1, 6 * D)

    z_c, xr_c = _proj(ctx, mods3, lambda b: B, norm1_g[lyr], w_in_b, tab_c, min(PROJ_TM, L))
    z_l, xr_l = _proj(x, mods3, lambda b: b, norm1_g[lyr], w_in_b, tab_l, min(PROJ_TM, S))

    zeros = jnp.zeros((B, D), F32)
    hs = []
    for d in range(2):
        args = (conv_w[lyr], conv_b[lyr], wg[d], lru_ba[lyr, d], lru_bx[lyr, d], lru_lambda[lyr, d])
        _, h0 = _rnn(xr_c, *args, zeros, reverse=(d == 1), tt=RNN_TT)
        h, _ = _rnn(xr_l, *args, h0, reverse=(d == 1), tt=RNN_TT)
        hs.append(h)

    s_fwd, s_bwd = _ret_ctx(z_c, B, cdec_f, cdec_b)
    sb = _ret_bwd(z_l, B, s_bwd, kdec_b, cdec)
    retg = _ret_fwd(z_l, B, sb, s_fwd, intra, qdec_f, qdec_b, kdec_f, cdec)

    x1, hl2, eid, gates, rank, counts = _finish(
        x, hs[0], hs[1], z_l, retg, mods3, norm2_g[lyr], w_rnn_proj[lyr].astype(BF16),
        w_ret_proj[lyr].astype(BF16), w_out[lyr].astype(BF16), rw_hi, rw_lo, router_b[lyr])

    cnt = counts[0].astype(I32)
    padded = (cnt + MOE_BLOCK - 1) // MOE_BLOCK * MOE_BLOCK
    pad_end = jnp.cumsum(padded)
    pad_start = pad_end - padded
    dest = pad_start[eid] + rank
    n_blocks = N * TOP_K // MOE_BLOCK + N_EXPERTS
    P = n_blocks * MOE_BLOCK
    blk_start = jnp.arange(n_blocks, dtype=I32) * MOE_BLOCK
    blk_expert = jnp.minimum(jnp.sum((pad_end[None, :] <= blk_start[:, None]).astype(I32), axis=1), N_EXPERTS - 1)
    tm = min(DISP_TM, S)
    dest3 = dest.reshape(N // tm, 1, tm * TOP_K)

    h_pad = _dispatch(hl2, dest3, P)
    y_pad = _experts(blk_expert, h_pad, w1b, b1p, w2b, moe_b2[lyr])
    return _combine(dest3, x1, gates, mods3, final_g, y_pad)
```

```python
import functools

import jax
import jax.numpy as jnp
import numpy as np
from jax import lax
from jax.experimental import pallas as pl
from jax.experimental.pallas import tpu as pltpu

F32 = jnp.float32
BF16 = jnp.bfloat16
I32 = jnp.int32

GRID_W = 64
RNN_BLOCKS = 8
CONV_W = 4
LRU_C = 8.0
RET_HEADS = 8
RET_DK = 128
RET_DV = 256
ROPE_BASE = 10000.0
N_EXPERTS = 32
TOP_K = 4
SWIGLU_ALPHA = 1.702
SWIGLU_LIMIT = 7.0
EPS = 1e-6
N_CHUNKS = 10

PROJ_TM = 1024
PROJ_TN = 2048
RNN_TT = 128
RET_C = 256
FIN_TM = 512
MOE_BLOCK = 512
DISP_TM = 256
VMEM_LIMIT = 56 * 1024 * 1024


def _cparams(n_axes):
    return pltpu.CompilerParams(dimension_semantics=("arbitrary",) * n_axes,
                                vmem_limit_bytes=VMEM_LIMIT)


def _sigmoid(x):
    return 0.5 * (jnp.tanh(0.5 * x) + 1.0)


def _rmsnorm(x, g):
    return x * lax.rsqrt(jnp.mean(x * x, axis=-1, keepdims=True) + EPS) * g


def _ada_kernel(c_ref, w_ref, b_ref, o_ref):
    c = c_ref[...]
    s = c * _sigmoid(c)
    o_ref[...] = jnp.dot(s, w_ref[...], preferred_element_type=F32,
                         precision=lax.Precision.HIGHEST) + b_ref[...]


def _ada(cvec, ada_w, ada_b):
    R, D = cvec.shape
    n = ada_w.shape[1] // D
    return pl.pallas_call(
        _ada_kernel,
        grid=(n,),
        in_specs=[pl.BlockSpec((R, D), lambda j: (0, 0)),
                  pl.BlockSpec((D, D), lambda j: (0, j)),
                  pl.BlockSpec((1, D), lambda j: (0, j))],
        out_specs=pl.BlockSpec((R, D), lambda j: (0, j)),
        out_shape=jax.ShapeDtypeStruct((R, n * D), F32),
        compiler_params=_cparams(1),
        name="ada",
    )(cvec, ada_w, ada_b.reshape(1, -1))


def _proj_kernel(x_ref, sh_ref, sc_ref, g_ref, w_ref, tab_ref, o_ref, xr_ref, h_ref):
    j = pl.program_id(2)
    D = x_ref.shape[1]

    @pl.when(j == 0)
    def _():
        h = _rmsnorm(x_ref[...], g_ref[...])
        h_ref[...] = (h * (1.0 + sc_ref[...]) + sh_ref[...]).astype(BF16)

    acc = jnp.dot(h_ref[...], w_ref[...], preferred_element_type=F32)

    @pl.when(j != 1)
    def _():
        o_ref[...] = acc.astype(o_ref.dtype)

    @pl.when(j == 0)
    def _():
        xr_ref[...] = acc[:, :D].astype(xr_ref.dtype)

    @pl.when(j == 1)
    def _():
        for part in range(2):
            cos = tab_ref[:, (2 * part) * RET_DK:(2 * part + 1) * RET_DK]
            sin = tab_ref[:, (2 * part + 1) * RET_DK:(2 * part + 2) * RET_DK]
            for hh in range(RET_HEADS):
                c0 = part * RET_HEADS * RET_DK + hh * RET_DK
                t = acc[:, c0:c0 + RET_DK]
                o_ref[:, c0:c0 + RET_DK] = (t * cos + pltpu.roll(t, RET_DK // 2, 1) * sin).astype(o_ref.dtype)


def _proj(x3, mods3, mod_row, norm_g, w_in_b, tab, tm):
    B, S, D = x3.shape
    d_in = w_in_b.shape[1]
    nj = d_in // PROJ_TN
    return pl.pallas_call(
        _proj_kernel,
        grid=(B, S // tm, nj),
        in_specs=[pl.BlockSpec((None, tm, D), lambda b, i, j: (b, i, 0)),
                  pl.BlockSpec((None, 1, D), lambda b, i, j: (mod_row(b), 0, 0)),
                  pl.BlockSpec((None, 1, D), lambda b, i, j: (mod_row(b), 0, 1)),
                  pl.BlockSpec((1, D), lambda b, i, j: (0, 0)),
                  pl.BlockSpec((D, PROJ_TN), lambda b, i, j: (0, j)),
                  pl.BlockSpec((tm, 4 * RET_DK), lambda b, i, j: (i, 0))],
        out_specs=[pl.BlockSpec((tm, PROJ_TN), lambda b, i, j: (i, b * nj + j)),
                   pl.BlockSpec((None, tm, D), lambda b, i, j: (b, i, 0))],
        out_shape=[jax.ShapeDtypeStruct((S, B * d_in), BF16),
                   jax.ShapeDtypeStruct((B, S, D), BF16)],
        scratch_shapes=[pltpu.VMEM((tm, D), BF16)],
        compiler_params=_cparams(3),
        name="proj",
    )(x3, mods3, mods3, norm_g.reshape(1, D), w_in_b, tab)


def _rnn_kernel(xm_ref, xp_ref, xn_ref, cw_ref, cb_ref, wg_ref, ba_ref, bx_ref, lam_ref, h0_ref,
                h_ref, hfin_ref, xs_ref, a_ref, u_ref, hc_ref, *, reverse, n_tiles):
    i = pl.program_id(0)
    tile = (n_tiles - 1 - i) if reverse else i
    nb, tt, D = xm_ref.shape
    bw = D // RNN_BLOCKS
    HALO = xp_ref.shape[1]

    @pl.when(i == 0)
    def _():
        hc_ref[...] = h0_ref[...]

    def time_major(v):
        return jnp.swapaxes(v.astype(F32), 0, 1).reshape(v.shape[1] * nb, D)

    hr = HALO * nb
    R = tt * nb
    xs_ref[0:hr, :] = jnp.where(tile > 0, time_major(xp_ref[...]), 0.0)
    xs_ref[hr:hr + R, :] = time_major(xm_ref[...])
    xs_ref[hr + R:2 * hr + R, :] = jnp.where(tile < n_tiles - 1, time_major(xn_ref[...]), 0.0)

    nl = -lam_ref[...]
    csp = (0.25 * LRU_C) * (jnp.maximum(nl, 0.0) + jnp.log1p(jnp.exp(-jnp.abs(nl))))
    SUB = 256

    def gates(s, carry):
        r0 = pl.multiple_of(s * SUB, SUB)
        base = hr - 2 * nb
        xc = cb_ref[...] + cw_ref[0:1, :] * xs_ref[pl.ds(pl.multiple_of(r0 + base, nb), SUB), :]
        for k in range(1, CONV_W):
            xc = xc + cw_ref[k:k + 1, :] * xs_ref[pl.ds(pl.multiple_of(r0 + base + k * nb, nb), SUB), :]
        xb = xc.astype(BF16)
        for n in range(RNN_BLOCKS):
            g = jnp.dot(xb[:, n * bw:(n + 1) * bw], wg_ref[n], preferred_element_type=F32)
            cs = slice(n * bw, (n + 1) * bw)
            tr = jnp.tanh(g[:, :bw] + ba_ref[:, cs])
            ti = jnp.tanh(g[:, bw:] + bx_ref[:, cs])
            t = jnp.tanh(csp[:, cs] * tr + csp[:, cs])
            rc2 = 2.0 / (1.0 + t)
            a_ref[pl.ds(r0, SUB), cs] = rc2 - 1.0
            u_ref[pl.ds(r0, SUB), cs] = (jnp.sqrt(t) * rc2) * ((0.5 * ti + 0.5) * xc[:, cs])
        return carry

    lax.fori_loop(0, R // SUB, gates, 0)

    def step(t, h):
        ts = (tt - 1 - t) if reverse else t
        r0 = pl.multiple_of(ts * nb, nb)
        h = a_ref[pl.ds(r0, nb), :] * h + u_ref[pl.ds(r0, nb), :]
        u_ref[pl.ds(r0, nb), :] = h
        return h

    h = lax.fori_loop(0, tt, step, hc_ref[...], unroll=8)
    hc_ref[...] = h
    hfin_ref[...] = h
    h_ref[...] = jnp.swapaxes(u_ref[...].reshape(tt, nb, D), 0, 1).astype(h_ref.dtype)


def _rnn(xr, conv_w, conv_b, wg, ba, bx, lam, h0, *, reverse, tt):
    nb, T, D = xr.shape
    tt = min(tt, T)
    n_tiles = T // tt
    HALO = 16
    hb = tt // HALO

    def tile_of(i):
        return (n_tiles - 1 - i) if reverse else i

    kern = functools.partial(_rnn_kernel, reverse=reverse, n_tiles=n_tiles)
    return pl.pallas_call(
        kern,
        grid=(n_tiles,),
        in_specs=[pl.BlockSpec((nb, tt, D), lambda i: (0, tile_of(i), 0)),
                  pl.BlockSpec((nb, HALO, D), lambda i: (0, jnp.maximum(tile_of(i) * hb - 1, 0), 0)),
                  pl.BlockSpec((nb, HALO, D), lambda i: (0, jnp.minimum((tile_of(i) + 1) * hb, n_tiles * hb - 1), 0)),
                  pl.BlockSpec((CONV_W, D), lambda i: (0, 0)),
                  pl.BlockSpec((1, D), lambda i: (0, 0)),
                  pl.BlockSpec((RNN_BLOCKS, D // RNN_BLOCKS, 2 * D // RNN_BLOCKS), lambda i: (0, 0, 0)),
                  pl.BlockSpec((1, D), lambda i: (0, 0)),
                  pl.BlockSpec((1, D), lambda i: (0, 0)),
                  pl.BlockSpec((1, D), lambda i: (0, 0)),
                  pl.BlockSpec((nb, D), lambda i: (0, 0))],
        out_specs=[pl.BlockSpec((nb, tt, D), lambda i: (0, tile_of(i), 0)),
                   pl.BlockSpec((nb, D), lambda i: (0, 0))],
        out_shape=[jax.ShapeDtypeStruct((nb, T, D), BF16),
                   jax.ShapeDtypeStruct((nb, D), F32)],
        scratch_shapes=[pltpu.VMEM(((tt + 2 * HALO) * nb, D), F32),
                        pltpu.VMEM((nb * tt, D), F32),
                        pltpu.VMEM((nb * tt, D), F32),
                        pltpu.VMEM((nb, D), F32)],
        compiler_params=_cparams(1),
        name="rnn_bwd" if reverse else "rnn_fwd",
    )(xr, xr, xr, conv_w, conv_b.reshape(1, D), wg, ba.reshape(1, D), bx.reshape(1, D),
      lam.reshape(1, D), h0)


def _tdot(a, b):
    return lax.dot_general(a, b, (((0,), (0,)), ((), ())), preferred_element_type=F32)


def _ret_ctx_kernel(k_ref, v_ref, df_ref, db_ref, sf_ref, sb_ref):
    for hh in range(RET_HEADS):
        kh = k_ref[:, hh * RET_DK:(hh + 1) * RET_DK].astype(F32)
        vh = v_ref[:, hh * RET_DV:(hh + 1) * RET_DV]
        sf_ref[hh] = _tdot((kh * df_ref[hh]).astype(BF16), vh)
        sb_ref[hh] = _tdot((kh * db_ref[hh]).astype(BF16), vh)


def _ret_ctx(zc, B, dec_f, dec_b):
    L = zc.shape[0]
    H = RET_HEADS
    st = jax.ShapeDtypeStruct((B, H, RET_DK, RET_DV), F32)
    return pl.pallas_call(
        _ret_ctx_kernel,
        grid=(B,),
        in_specs=[pl.BlockSpec((L, H * RET_DK), lambda b: (0, b * N_CHUNKS + 3)),
                  pl.BlockSpec((L, H * RET_DV), lambda b: (0, b * (N_CHUNKS // 2) + 2)),
                  pl.BlockSpec((H, L, RET_DK), lambda b: (0, 0, 0)),
                  pl.BlockSpec((H, L, RET_DK), lambda b: (0, 0, 0))],
        out_specs=[pl.BlockSpec((None, H, RET_DK, RET_DV), lambda b: (b, 0, 0, 0))] * 2,
        out_shape=[st, st],
        compiler_params=_cparams(1),
        name="ret_ctx",
    )(zc, zc, dec_f, dec_b)


def _ret_bwd_kernel(k_ref, v_ref, s0_ref, kd_ref, cd_ref, o_ref, st_ref):
    @pl.when(pl.program_id(1) == 0)
    def _():
        st_ref[...] = s0_ref[...]

    for hh in range(RET_HEADS):
        o_ref[hh] = st_ref[hh].astype(o_ref.dtype)
        kh = k_ref[:, hh * RET_DK:(hh + 1) * RET_DK].astype(F32)
        vh = v_ref[:, hh * RET_DV:(hh + 1) * RET_DV]
        st_ref[hh] = st_ref[hh] * cd_ref[hh:hh + 1, :] + _tdot((kh * kd_ref[hh]).astype(BF16), vh)


def _ret_bwd(z, B, s_bwd, kdec_b, cdec):
    S = z.shape[0]
    H, C = RET_HEADS, min(RET_C, S)
    n = S // C
    return pl.pallas_call(
        _ret_bwd_kernel,
        grid=(B, n),
        in_specs=[pl.BlockSpec((C, H * RET_DK), lambda b, j: (n - 1 - j, b * N_CHUNKS + 3)),
                  pl.BlockSpec((C, H * RET_DV), lambda b, j: (n - 1 - j, b * (N_CHUNKS // 2) + 2)),
                  pl.BlockSpec((None, H, RET_DK, RET_DV), lambda b, j: (b, 0, 0, 0)),
                  pl.BlockSpec((H, C, RET_DK), lambda b, j: (0, 0, 0)),
                  pl.BlockSpec((H, RET_DV), lambda b, j: (0, 0))],
        out_specs=pl.BlockSpec((None, None, H, RET_DK, RET_DV), lambda b, j: (b, n - 1 - j, 0, 0, 0)),
        out_shape=jax.ShapeDtypeStruct((B, n, H, RET_DK, RET_DV), BF16),
        scratch_shapes=[pltpu.VMEM((H, RET_DK, RET_DV), F32)],
        compiler_params=_cparams(2),
        name="ret_bwd",
    )(z, z, s_bwd, kdec_b, cdec)


def _ret_fwd_kernel(q_ref, k_ref, v_ref, gs_ref, sb_ref, s0_ref, intra_ref, qf_ref, qb_ref, kf_ref, cd_ref,
                    o_ref, st_ref):
    @pl.when(pl.program_id(1) == 0)
    def _():
        st_ref[...] = s0_ref[...]

    for hh in range(RET_HEADS):
        qh = q_ref[:, hh * RET_DK:(hh + 1) * RET_DK]
        kh = k_ref[:, hh * RET_DK:(hh + 1) * RET_DK]
        vh = v_ref[:, hh * RET_DV:(hh + 1) * RET_DV]
        qf32 = qh.astype(F32)
        s = lax.dot_general(qh, kh, (((1,), (1,)), ((), ())), preferred_element_type=F32) * intra_ref[hh]
        o = jnp.dot(s.astype(BF16), vh, preferred_element_type=F32)
        o = o + jnp.dot((qf32 * qf_ref[hh]).astype(BF16), st_ref[hh].astype(BF16), preferred_element_type=F32)
        o = o + jnp.dot((qf32 * qb_ref[hh]).astype(BF16), sb_ref[hh], preferred_element_type=F32)
        st_ref[hh] = st_ref[hh] * cd_ref[hh:hh + 1, :] + _tdot((kh.astype(F32) * kf_ref[hh]).astype(BF16), vh)
        mu = jnp.mean(o, axis=-1, keepdims=True)
        d = o - mu
        var = jnp.mean(d * d, axis=-1, keepdims=True)
        g = gs_ref[:, hh * RET_DV:(hh + 1) * RET_DV].astype(F32)
        o_ref[:, hh * RET_DV:(hh + 1) * RET_DV] = (g * _sigmoid(g) * d * lax.rsqrt(var + EPS)).astype(o_ref.dtype)


def _ret_fwd(z, B, sb, s_fwd, intra, qdec_f, qdec_b, kdec_f, cdec):
    S = z.shape[0]
    H, C = RET_HEADS, min(RET_C, S)
    n = S // C
    half = N_CHUNKS // 2
    return pl.pallas_call(
        _ret_fwd_kernel,
        grid=(B, n),
        in_specs=[pl.BlockSpec((C, H * RET_DK), lambda b, j: (j, b * N_CHUNKS + 2)),
                  pl.BlockSpec((C, H * RET_DK), lambda b, j: (j, b * N_CHUNKS + 3)),
                  pl.BlockSpec((C, H * RET_DV), lambda b, j: (j, b * half + 2)),
                  pl.BlockSpec((C, H * RET_DV), lambda b, j: (j, b * half + 3)),
                  pl.BlockSpec((None, None, H, RET_DK, RET_DV), lambda b, j: (b, j, 0, 0, 0)),
                  pl.BlockSpec((None, H, RET_DK, RET_DV), lambda b, j: (b, 0, 0, 0)),
                  pl.BlockSpec((H, C, C), lambda b, j: (0, 0, 0)),
                  pl.BlockSpec((H, C, RET_DK), lambda b, j: (0, 0, 0)),
                  pl.BlockSpec((H, C, RET_DK), lambda b, j: (0, 0, 0)),
                  pl.BlockSpec((H, C, RET_DK), lambda b, j: (0, 0, 0)),
                  pl.BlockSpec((H, RET_DV), lambda b, j: (0, 0))],
        out_specs=pl.BlockSpec((None, C, H * RET_DV), lambda b, j: (b, j, 0)),
        out_shape=jax.ShapeDtypeStruct((B, S, H * RET_DV), BF16),
        scratch_shapes=[pltpu.VMEM((H, RET_DK, RET_DV), F32)],
        compiler_params=_cparams(2),
        name="ret_fwd",
    )(z, z, z, z, sb, s_fwd, intra, qdec_f, qdec_b, kdec_f, cdec)


def _finish_kernel(x_ref, hf_ref, hb_ref, gr_ref, gab_ref, ret_ref, g1_ref, sh2_ref, sc2_ref, n2_ref,
                   wr_ref, wt_ref, wo_ref, rwh_ref, rwl_ref, rb_ref,
                   x1_ref, hl_ref, eid_ref, gate_ref, rank_ref, cnt_ref, carry_ref):
    first = jnp.logical_and(pl.program_id(0) == 0, pl.program_id(1) == 0)

    @pl.when(first)
    def _():
        carry_ref[...] = jnp.zeros_like(carry_ref)

    D = x_ref.shape[1]
    tm = x_ref.shape[0]
    rnn = hf_ref[...].astype(F32) + hb_ref[...].astype(F32)
    y_rnn = jnp.dot((rnn * jax.nn.gelu(gr_ref[...].astype(F32))).astype(BF16), wr_ref[...],
                    preferred_element_type=F32)
    y_ret = jnp.dot(ret_ref[...], wt_ref[...], preferred_element_type=F32)
    ga = gab_ref[:, :D].astype(F32)
    gb = gab_ref[:, D:].astype(F32)
    merged = _sigmoid(ga) * y_rnn + _sigmoid(gb) * y_ret
    y = jnp.dot(merged.astype(BF16), wo_ref[...], preferred_element_type=F32)
    x1 = x_ref[...] + g1_ref[...] * y
    x1_ref[...] = x1
    hl = _rmsnorm(x1, n2_ref[...]) * (1.0 + sc2_ref[...]) + sh2_ref[...]
    hl_ref[...] = hl

    hh = hl.astype(BF16)
    hlo = (hl - hh.astype(F32)).astype(BF16)
    logits = (jnp.dot(hh, rwh_ref[...], preferred_element_type=F32)
              + jnp.dot(hlo, rwh_ref[...], preferred_element_type=F32)
              + jnp.dot(hh, rwl_ref[...], preferred_element_type=F32)) + rb_ref[...]

    ne = logits.shape[1]
    lane = lax.broadcasted_iota(I32, (tm, ne), 1)
    work = logits
    vals, idxs, hots = [], [], []
    for _ in range(TOP_K):
        m = jnp.max(work, axis=-1, keepdims=True)
        idx = jnp.min(jnp.where(work == m, lane, ne), axis=-1, keepdims=True)
        hot = lane == idx
        vals.append(m)
        idxs.append(idx)
        hots.append(hot)
        work = jnp.where(hot, -jnp.inf, work)
    es = [jnp.exp(v - vals[0]) for v in vals]
    inv = 1.0 / (es[0] + es[1] + es[2] + es[3])

    oh = jnp.zeros((tm, ne), F32)
    for hot in hots:
        oh = oh + hot.astype(F32)
    row = lax.broadcasted_iota(I32, (tm, tm), 0)
    col = lax.broadcasted_iota(I32, (tm, tm), 1)
    tri = jnp.where(col < row, 1.0, 0.0).astype(BF16)
    before = jnp.dot(tri, oh.astype(BF16), preferred_element_type=F32) + carry_ref[...]
    for k in range(TOP_K):
        eid_ref[:, k:k + 1] = idxs[k]
        gate_ref[:, k:k + 1] = es[k] * inv
        rank_ref[:, k:k + 1] = jnp.sum(jnp.where(hots[k], before, 0.0), axis=-1, keepdims=True).astype(I32)
    carry_ref[...] = carry_ref[...] + jnp.sum(oh, axis=0, keepdims=True)
    cnt_ref[...] = carry_ref[...]


def _finish(x3, hf2, hb2, z, retg, mods3, norm2_g, w_rnn_b, w_ret_b, w_out_b, rw_hi, rw_lo, router_b):
    B, S, D = x3.shape
    tm = min(FIN_TM, S)
    nt = S // tm
    N = B * S
    ne = rw_hi.shape[1]
    half = N_CHUNKS // 2
    const2 = lambda b, i: (0, 0)
    tok = lambda b, i: (b * nt + i, 0)
    return pl.pallas_call(
        _finish_kernel,
        grid=(B, nt),
        in_specs=[pl.BlockSpec((None, tm, D), lambda b, i: (b, i, 0)),
                  pl.BlockSpec((None, tm, D), lambda b, i: (b, i, 0)),
                  pl.BlockSpec((None, tm, D), lambda b, i: (b, i, 0)),
                  pl.BlockSpec((tm, D), lambda b, i: (i, b * N_CHUNKS + 1)),
                  pl.BlockSpec((tm, 2 * D), lambda b, i: (i, b * half + 4)),
                  pl.BlockSpec((None, tm, retg.shape[2]), lambda b, i: (b, i, 0)),
                  pl.BlockSpec((None, 1, D), lambda b, i: (b, 0, 2)),
                  pl.BlockSpec((None, 1, D), lambda b, i: (b, 0, 3)),
                  pl.BlockSpec((None, 1, D), lambda b, i: (b, 0, 4)),
                  pl.BlockSpec((1, D), const2),
                  pl.BlockSpec(w_rnn_b.shape, const2),
                  pl.BlockSpec(w_ret_b.shape, const2),
                  pl.BlockSpec(w_out_b.shape, const2),
                  pl.BlockSpec(rw_hi.shape, const2),
                  pl.BlockSpec(rw_lo.shape, const2),
                  pl.BlockSpec((1, ne), const2)],
        out_specs=[pl.BlockSpec((None, tm, D), lambda b, i: (b, i, 0)),
                   pl.BlockSpec((tm, D), tok),
                   pl.BlockSpec((tm, TOP_K), tok),
                   pl.BlockSpec((tm, TOP_K), tok),
                   pl.BlockSpec((tm, TOP_K), tok),
                   pl.BlockSpec((1, ne), const2)],
        out_shape=[jax.ShapeDtypeStruct((B, S, D), F32),
                   jax.ShapeDtypeStruct((N, D), F32),
                   jax.ShapeDtypeStruct((N, TOP_K), I32),
                   jax.ShapeDtypeStruct((N, TOP_K), F32),
                   jax.ShapeDtypeStruct((N, TOP_K), I32),
                   jax.ShapeDtypeStruct((1, ne), F32)],
        scratch_shapes=[pltpu.VMEM((1, ne), F32)],
        compiler_params=_cparams(2),
        name="finish",
    )(x3, hf2, hb2, z, z, retg, mods3, mods3, mods3, norm2_g.reshape(1, D),
      w_rnn_b, w_ret_b, w_out_b, rw_hi, rw_lo, router_b.reshape(1, ne))


def _row_copy(src, s, dst, d, sem):
    return pltpu.make_async_copy(src.at[pl.ds(s, 1), :], dst.at[pl.ds(d, 1), :], sem)


def _dispatch_kernel(dest_ref, x_ref, hp_in_ref, hp_ref, sem):
    del hp_in_ref
    tm = x_ref.shape[0]

    def issue(r, c):
        for k in range(TOP_K):
            _row_copy(x_ref, r, hp_ref, dest_ref[0, r * TOP_K + k], sem).start()
        return c

    lax.fori_loop(0, tm, issue, 0)

    def drain(r, c):
        for k in range(TOP_K):
            _row_copy(x_ref, 0, hp_ref, 0, sem).wait()
        return c

    lax.fori_loop(0, tm, drain, 0)


def _dispatch(hl2, dest3, P):
    N, D = hl2.shape
    tm = dest3.shape[2] // TOP_K
    return pl.pallas_call(
        _dispatch_kernel,
        grid=(N // tm,),
        in_specs=[pl.BlockSpec((None, 1, tm * TOP_K), lambda i: (i, 0, 0), memory_space=pltpu.SMEM),
                  pl.BlockSpec((tm, D), lambda i: (i, 0)),
                  pl.BlockSpec(memory_space=pl.ANY)],
        out_specs=pl.BlockSpec(memory_space=pl.ANY),
        out_shape=jax.ShapeDtypeStruct((P, D), F32),
        scratch_shapes=[pltpu.SemaphoreType.DMA(())],
        input_output_aliases={2: 0},
        compiler_params=_cparams(1),
        name="dispatch",
    )(dest3, hl2, jnp.zeros((P, D), F32))


def _expert_kernel(be_ref, x_ref, w1_ref, b1_ref, w2_ref, b2_ref, o_ref):
    del be_ref
    de = w2_ref.shape[0]
    h = jnp.dot(x_ref[...].astype(BF16), w1_ref[...], preferred_element_type=F32) + b1_ref[...]
    glu = jnp.minimum(h[:, :de], SWIGLU_LIMIT)
    lin = jnp.clip(h[:, de:], -SWIGLU_LIMIT, SWIGLU_LIMIT)
    act = glu * _sigmoid(SWIGLU_ALPHA * glu) * (lin + 1.0)
    o_ref[...] = jnp.dot(act.astype(BF16), w2_ref[...], preferred_element_type=F32) + b2_ref[...]


def _experts(blk_expert, h_pad, w1b, b1p, w2b, b2):
    P, D = h_pad.shape
    ne, _, de2 = w1b.shape
    de = de2 // 2
    nblk = P // MOE_BLOCK
    return pl.pallas_call(
        _expert_kernel,
        grid_spec=pltpu.PrefetchScalarGridSpec(
            num_scalar_prefetch=1,
            grid=(nblk,),
            in_specs=[pl.BlockSpec((MOE_BLOCK, D), lambda j, be: (j, 0)),
                      pl.BlockSpec((None, D, de2), lambda j, be: (be[j], 0, 0)),
                      pl.BlockSpec((None, 1, de2), lambda j, be: (be[j], 0, 0)),
                      pl.BlockSpec((None, de, D), lambda j, be: (be[j], 0, 0)),
                      pl.BlockSpec((None, 1, D), lambda j, be: (be[j], 0, 0))],
            out_specs=pl.BlockSpec((MOE_BLOCK, D), lambda j, be: (j, 0))),
        out_shape=jax.ShapeDtypeStruct((P, D), F32),
        compiler_params=_cparams(1),
        name="experts",
    )(blk_expert, h_pad, w1b, b1p.reshape(ne, 1, de2), w2b, b2.reshape(ne, 1, D))


def _combine_kernel(dest_ref, x1_ref, gate_ref, g2_ref, fg_ref, yp_ref, o_ref, buf_ref, sem):
    tm = x1_ref.shape[0]

    def issue(r, c):
        for k in range(TOP_K):
            _row_copy(yp_ref, dest_ref[0, r * TOP_K + k], buf_ref.at[k], r, sem).start()
        return c

    lax.fori_loop(0, tm, issue, 0)

    def drain(r, c):
        for k in range(TOP_K):
            _row_copy(yp_ref, 0, buf_ref.at[k], 0, sem).wait()
        return c

    lax.fori_loop(0, tm, drain, 0)

    y = gate_ref[:, 0:1] * buf_ref[0]
    for k in range(1, TOP_K):
        y = y + gate_ref[:, k:k + 1] * buf_ref[k]
    x2 = x1_ref[...] + g2_ref[...] * y
    o_ref[...] = _rmsnorm(x2, fg_ref[...])


def _combine(dest3, x1, gates, mods3, final_g, y_pad):
    B, S, D = x1.shape
    tm = dest3.shape[2] // TOP_K
    nt = S // tm
    return pl.pallas_call(
        _combine_kernel,
        grid=(B, nt),
        in_specs=[pl.BlockSpec((None, 1, tm * TOP_K), lambda b, i: (b * nt + i, 0, 0), memory_space=pltpu.SMEM),
                  pl.BlockSpec((None, tm, D), lambda b, i: (b, i, 0)),
                  pl.BlockSpec((tm, TOP_K), lambda b, i: (b * nt + i, 0)),
                  pl.BlockSpec((None, 1, D), lambda b, i: (b, 0, 5)),
                  pl.BlockSpec((1, D), lambda b, i: (0, 0)),
                  pl.BlockSpec(memory_space=pl.ANY)],
        out_specs=pl.BlockSpec((None, tm, D), lambda b, i: (b, i, 0)),
        out_shape=jax.ShapeDtypeStruct((B, S, D), F32),
        scratch_shapes=[pltpu.VMEM((TOP_K, tm, D), F32), pltpu.SemaphoreType.DMA(())],
        compiler_params=_cparams(2),
        name="combine",
    )(dest3, x1, gates, mods3, final_g.reshape(1, D), y_pad)


def _rope_tables(S, k_scale):
    n_freq = RET_DK // 4
    pos = jnp.arange(S, dtype=F32)
    rows = jnp.floor(pos / GRID_W)
    cols = pos - rows * GRID_W
    inv = ROPE_BASE ** (-jnp.arange(n_freq, dtype=F32) / n_freq)
    ang = jnp.concatenate([rows[:, None] * inv, cols[:, None] * inv], axis=-1)
    cos, sin = jnp.cos(ang), jnp.sin(ang)
    cos2 = jnp.concatenate([cos, cos], axis=-1)
    sin2 = jnp.concatenate([-sin, sin], axis=-1)
    return jnp.concatenate([cos2, sin2, cos2 * k_scale, sin2 * k_scale], axis=-1)


def _identity_tables(L, k_scale):
    one = jnp.ones((L, RET_DK), F32)
    zero = jnp.zeros((L, RET_DK), F32)
    return jnp.concatenate([one, zero, one * k_scale, zero], axis=-1)


def _lanes(t, width):
    return jnp.broadcast_to(t[:, :, None], t.shape + (width,))


def kernel(x, c, ctx, c_ctx, ada_w, ada_b, norm1_g, w_in, conv_w, conv_b, lru_wa, lru_ba, lru_wx, lru_bx,
           lru_lambda, w_rnn_proj, w_ret_proj, w_out, norm2_g, router_w, router_b, moe_w1, moe_b1, moe_w2,
           moe_b2, final_g):
    B, S, D = x.shape
    L = ctx.shape[1]
    N = B * S
    H = RET_HEADS
    lyr = 0
    d_in = w_in.shape[2]
    assert ada_w.shape[0] == 1 and d_in == N_CHUNKS * D and B == 8

    def pairs_apart(w):
        return jnp.swapaxes(w.reshape(D, H, RET_DK // 2, 2), 2, 3).reshape(D, H * RET_DK)

    w_in_l = w_in[lyr].astype(BF16)
    w_in_b = jnp.concatenate([w_in_l[:, :2 * D], pairs_apart(w_in_l[:, 2 * D:3 * D]),
                              pairs_apart(w_in_l[:, 3 * D:4 * D]), w_in_l[:, 4 * D:]], axis=1)
    wg = [(0.5 * jnp.concatenate([lru_wa[lyr, d], lru_wx[lyr, d]], axis=-1)).astype(BF16) for d in range(2)]
    de2 = moe_w1.shape[3]
    w1b = jnp.swapaxes(moe_w1[lyr].astype(BF16).reshape(N_EXPERTS, D, de2 // 2, 2), 2, 3).reshape(N_EXPERTS, D, de2)
    b1p = jnp.swapaxes(moe_b1[lyr].reshape(N_EXPERTS, de2 // 2, 2), 1, 2).reshape(N_EXPERTS, de2)
    w2b = moe_w2[lyr].astype(BF16)
    rw = router_w[lyr]
    rw_hi = rw.astype(BF16)
    rw_lo = (rw - rw_hi.astype(F32)).astype(BF16)

    k_scale = RET_DK ** -0.5
    tab_l = _rope_tables(S, k_scale)
    tab_c = _identity_tables(L, k_scale)
    log_g = jnp.log1p(-jnp.exp2(-5.0 - jnp.arange(H, dtype=F32)))
    C = min(RET_C, S)
    idx = jnp.arange(C, dtype=F32)
    dec = lambda e: jnp.exp(e[None, :] * log_g[:, None])
    intra = jnp.exp(jnp.abs(idx[:, None] - idx[None, :])[None] * log_g[:, None, None])
    qdec_f = _lanes(dec(idx + 1.0), RET_DK)
    qdec_b = _lanes(dec(C - idx), RET_DK)
    kdec_f = _lanes(dec(C - 1.0 - idx), RET_DK)
    kdec_b = _lanes(dec(idx), RET_DK)
    cdec = jnp.broadcast_to(jnp.exp(C * log_g)[:, None], (H, RET_DV))
    pos_c = jnp.arange(L, dtype=F32)
    cdec_f = _lanes(dec(L - 1.0 - pos_c), RET_DK)
    cdec_b = _lanes(dec(pos_c), RET_DK)

    cvec = jnp.zeros((16, D), F32).at[:B].set(c).at[B].set(c_ctx)
    mods3 = _ada(cvec, ada_w[lyr], ada_b[lyr]).reshape(16, 1, 6 * D)

    z_c, xr_c = _proj(ctx, mods3, lambda b: B, norm1_g[lyr], w_in_b, tab_c, min(PROJ_TM, L))
    z_l, xr_l = _proj(x, mods3, lambda b: b, norm1_g[lyr], w_in_b, tab_l, min(PROJ_TM, S))

    zeros = jnp.zeros((B, D), F32)
    hs = []
    for d in range(2):
        args = (conv_w[lyr], conv_b[lyr], wg[d], 0.5 * lru_ba[lyr, d], 0.5 * lru_bx[lyr, d], lru_lambda[lyr, d])
        _, h0 = _rnn(xr_c, *args, zeros, reverse=(d == 1), tt=RNN_TT)
        h, _ = _rnn(xr_l, *args, h0, reverse=(d == 1), tt=RNN_TT)
        hs.append(h)

    s_fwd, s_bwd = _ret_ctx(z_c, B, cdec_f, cdec_b)
    sb = _ret_bwd(z_l, B, s_bwd, kdec_b, cdec)
    retg = _ret_fwd(z_l, B, sb, s_fwd, intra, qdec_f, qdec_b, kdec_f, cdec)

    x1, hl2, eid, gates, rank, counts = _finish(
        x, hs[0], hs[1], z_l, retg, mods3, norm2_g[lyr], w_rnn_proj[lyr].astype(BF16),
        w_ret_proj[lyr].astype(BF16), w_out[lyr].astype(BF16), rw_hi, rw_lo, router_b[lyr])

    cnt = counts[0].astype(I32)
    padded = (cnt + MOE_BLOCK - 1) // MOE_BLOCK * MOE_BLOCK
    pad_end = jnp.cumsum(padded)
    pad_start = pad_end - padded
    dest = pad_start[eid] + rank
    n_blocks = N * TOP_K // MOE_BLOCK + N_EXPERTS
    P = n_blocks * MOE_BLOCK
    blk_start = jnp.arange(n_blocks, dtype=I32) * MOE_BLOCK
    blk_expert = jnp.minimum(jnp.sum((pad_end[None, :] <= blk_start[:, None]).astype(I32), axis=1), N_EXPERTS - 1)
    tm = min(DISP_TM, S)
    dest3 = dest.reshape(N // tm, 1, tm * TOP_K)

    h_pad = _dispatch(hl2, dest3, P)
    y_pad = _experts(blk_expert, h_pad, w1b, b1p, w2b, moe_b2[lyr])
    return _combine(dest3, x1, gates, mods3, final_g, y_pad)
```

```python
import functools

import jax
import jax.numpy as jnp
import numpy as np
from jax import lax
from jax.experimental import pallas as pl
from jax.experimental.pallas import tpu as pltpu

F32 = jnp.float32
BF16 = jnp.bfloat16
I32 = jnp.int32

GRID_W = 64
RNN_BLOCKS = 8
CONV_W = 4
LRU_C = 8.0
RET_HEADS = 8
RET_DK = 128
RET_DV = 256
ROPE_BASE = 10000.0
N_EXPERTS = 32
TOP_K = 4
SWIGLU_ALPHA = 1.702
SWIGLU_LIMIT = 7.0
EPS = 1e-6
N_CHUNKS = 10

PROJ_TM = 1024
PROJ_TN = 2048
RNN_TT = 128
RET_C = 256
FIN_TM = 512
MOE_BLOCK = 512
DISP_TM = 512
ROW_ALIGN = 8
VMEM_LIMIT = 56 * 1024 * 1024


def _cparams(n_axes):
    return pltpu.CompilerParams(dimension_semantics=("arbitrary",) * n_axes,
                                vmem_limit_bytes=VMEM_LIMIT)


def _sigmoid(x):
    return 0.5 * (jnp.tanh(0.5 * x) + 1.0)


def _rmsnorm(x, g):
    return x * lax.rsqrt(jnp.mean(x * x, axis=-1, keepdims=True) + EPS) * g


def _ada_kernel(c_ref, w_ref, b_ref, o_ref):
    c = c_ref[...]
    s = c * _sigmoid(c)
    o_ref[...] = jnp.dot(s, w_ref[...], preferred_element_type=F32,
                         precision=lax.Precision.HIGHEST) + b_ref[...]


def _ada(cvec, ada_w, ada_b):
    R, D = cvec.shape
    n = ada_w.shape[1] // D
    return pl.pallas_call(
        _ada_kernel,
        grid=(n,),
        in_specs=[pl.BlockSpec((R, D), lambda j: (0, 0)),
                  pl.BlockSpec((D, D), lambda j: (0, j)),
                  pl.BlockSpec((1, D), lambda j: (0, j))],
        out_specs=pl.BlockSpec((R, D), lambda j: (0, j)),
        out_shape=jax.ShapeDtypeStruct((R, n * D), F32),
        compiler_params=_cparams(1),
        name="ada",
    )(cvec, ada_w, ada_b.reshape(1, -1))


def _proj_kernel(x_ref, sh_ref, sc_ref, g_ref, w_ref, tab_ref, o_ref, xr_ref, h_ref):
    j = pl.program_id(2)
    D = x_ref.shape[1]

    @pl.when(j == 0)
    def _():
        h = _rmsnorm(x_ref[...], g_ref[...])
        h_ref[...] = (h * (1.0 + sc_ref[...]) + sh_ref[...]).astype(BF16)

    acc = jnp.dot(h_ref[...], w_ref[...], preferred_element_type=F32)

    @pl.when(j != 1)
    def _():
        o_ref[...] = acc.astype(o_ref.dtype)

    @pl.when(j == 0)
    def _():
        xr_ref[...] = acc[:, :D].astype(xr_ref.dtype)

    @pl.when(j == 1)
    def _():
        for part in range(2):
            cos = tab_ref[:, (2 * part) * RET_DK:(2 * part + 1) * RET_DK]
            sin = tab_ref[:, (2 * part + 1) * RET_DK:(2 * part + 2) * RET_DK]
            for hh in range(RET_HEADS):
                c0 = part * RET_HEADS * RET_DK + hh * RET_DK
                t = acc[:, c0:c0 + RET_DK]
                o_ref[:, c0:c0 + RET_DK] = (t * cos + pltpu.roll(t, RET_DK // 2, 1) * sin).astype(o_ref.dtype)


def _proj(x3, mods3, mod_row, norm_g, w_in_b, tab, tm):
    B, S, D = x3.shape
    d_in = w_in_b.shape[1]
    nj = d_in // PROJ_TN
    return pl.pallas_call(
        _proj_kernel,
        grid=(B, S // tm, nj),
        in_specs=[pl.BlockSpec((None, tm, D), lambda b, i, j: (b, i, 0)),
                  pl.BlockSpec((None, 1, D), lambda b, i, j: (mod_row(b), 0, 0)),
                  pl.BlockSpec((None, 1, D), lambda b, i, j: (mod_row(b), 0, 1)),
                  pl.BlockSpec((1, D), lambda b, i, j: (0, 0)),
                  pl.BlockSpec((D, PROJ_TN), lambda b, i, j: (0, j)),
                  pl.BlockSpec((tm, 4 * RET_DK), lambda b, i, j: (i, 0))],
        out_specs=[pl.BlockSpec((tm, PROJ_TN), lambda b, i, j: (i, b * nj + j)),
                   pl.BlockSpec((None, tm, D), lambda b, i, j: (b, i, 0))],
        out_shape=[jax.ShapeDtypeStruct((S, B * d_in), BF16),
                   jax.ShapeDtypeStruct((B, S, D), BF16)],
        scratch_shapes=[pltpu.VMEM((tm, D), BF16)],
        compiler_params=_cparams(3),
        name="proj",
    )(x3, mods3, mods3, norm_g.reshape(1, D), w_in_b, tab)


def _rnn_kernel(xm_ref, xp_ref, xn_ref, cw_ref, cb_ref, wg_ref, ba_ref, bx_ref, lam_ref, h0_ref,
                h_ref, hfin_ref, xs_ref, a_ref, u_ref, hc_ref, *, reverse, n_tiles):
    i = pl.program_id(0)
    tile = (n_tiles - 1 - i) if reverse else i
    nb, tt, D = xm_ref.shape
    bw = D // RNN_BLOCKS
    HALO = xp_ref.shape[1]

    @pl.when(i == 0)
    def _():
        hc_ref[...] = h0_ref[...]

    def time_major(v):
        return jnp.swapaxes(v.astype(F32), 0, 1).reshape(v.shape[1] * nb, D)

    hr = HALO * nb
    R = tt * nb
    xs_ref[0:hr, :] = jnp.where(tile > 0, time_major(xp_ref[...]), 0.0)
    xs_ref[hr:hr + R, :] = time_major(xm_ref[...])
    xs_ref[hr + R:2 * hr + R, :] = jnp.where(tile < n_tiles - 1, time_major(xn_ref[...]), 0.0)

    nl = -lam_ref[...]
    csp = (0.25 * LRU_C) * (jnp.maximum(nl, 0.0) + jnp.log1p(jnp.exp(-jnp.abs(nl))))
    SUB = 256

    def gates(s, carry):
        r0 = pl.multiple_of(s * SUB, SUB)
        base = hr - 2 * nb
        xc = cb_ref[...] + cw_ref[0:1, :] * xs_ref[pl.ds(pl.multiple_of(r0 + base, nb), SUB), :]
        for k in range(1, CONV_W):
            xc = xc + cw_ref[k:k + 1, :] * xs_ref[pl.ds(pl.multiple_of(r0 + base + k * nb, nb), SUB), :]
        xb = xc.astype(BF16)
        for n in range(RNN_BLOCKS):
            g = jnp.dot(xb[:, n * bw:(n + 1) * bw], wg_ref[n], preferred_element_type=F32)
            cs = slice(n * bw, (n + 1) * bw)
            tr = jnp.tanh(g[:, :bw] + ba_ref[:, cs])
            ti = jnp.tanh(g[:, bw:] + bx_ref[:, cs])
            t = jnp.tanh(csp[:, cs] * tr + csp[:, cs])
            rc2 = 2.0 / (1.0 + t)
            a_ref[pl.ds(r0, SUB), cs] = rc2 - 1.0
            u_ref[pl.ds(r0, SUB), cs] = (jnp.sqrt(t) * rc2) * ((0.5 * ti + 0.5) * xc[:, cs])
        return carry

    lax.fori_loop(0, R // SUB, gates, 0)

    def step(t, h):
        ts = (tt - 1 - t) if reverse else t
        r0 = pl.multiple_of(ts * nb, nb)
        h = a_ref[pl.ds(r0, nb), :] * h + u_ref[pl.ds(r0, nb), :]
        u_ref[pl.ds(r0, nb), :] = h
        return h

    h = lax.fori_loop(0, tt, step, hc_ref[...], unroll=8)
    hc_ref[...] = h
    hfin_ref[...] = h
    h_ref[...] = jnp.swapaxes(u_ref[...].reshape(tt, nb, D), 0, 1).astype(h_ref.dtype)


def _rnn(xr, conv_w, conv_b, wg, ba, bx, lam, h0, *, reverse, tt):
    nb, T, D = xr.shape
    tt = min(tt, T)
    n_tiles = T // tt
    HALO = 16
    hb = tt // HALO

    def tile_of(i):
        return (n_tiles - 1 - i) if reverse else i

    kern = functools.partial(_rnn_kernel, reverse=reverse, n_tiles=n_tiles)
    return pl.pallas_call(
        kern,
        grid=(n_tiles,),
        in_specs=[pl.BlockSpec((nb, tt, D), lambda i: (0, tile_of(i), 0)),
                  pl.BlockSpec((nb, HALO, D), lambda i: (0, jnp.maximum(tile_of(i) * hb - 1, 0), 0)),
                  pl.BlockSpec((nb, HALO, D), lambda i: (0, jnp.minimum((tile_of(i) + 1) * hb, n_tiles * hb - 1), 0)),
                  pl.BlockSpec((CONV_W, D), lambda i: (0, 0)),
                  pl.BlockSpec((1, D), lambda i: (0, 0)),
                  pl.BlockSpec((RNN_BLOCKS, D // RNN_BLOCKS, 2 * D // RNN_BLOCKS), lambda i: (0, 0, 0)),
                  pl.BlockSpec((1, D), lambda i: (0, 0)),
                  pl.BlockSpec((1, D), lambda i: (0, 0)),
                  pl.BlockSpec((1, D), lambda i: (0, 0)),
                  pl.BlockSpec((nb, D), lambda i: (0, 0))],
        out_specs=[pl.BlockSpec((nb, tt, D), lambda i: (0, tile_of(i), 0)),
                   pl.BlockSpec((nb, D), lambda i: (0, 0))],
        out_shape=[jax.ShapeDtypeStruct((nb, T, D), BF16),
                   jax.ShapeDtypeStruct((nb, D), F32)],
        scratch_shapes=[pltpu.VMEM(((tt + 2 * HALO) * nb, D), F32),
                        pltpu.VMEM((nb * tt, D), F32),
                        pltpu.VMEM((nb * tt, D), F32),
                        pltpu.VMEM((nb, D), F32)],
        compiler_params=_cparams(1),
        name="rnn_bwd" if reverse else "rnn_fwd",
    )(xr, xr, xr, conv_w, conv_b.reshape(1, D), wg, ba.reshape(1, D), bx.reshape(1, D),
      lam.reshape(1, D), h0)


def _tdot(a, b):
    return lax.dot_general(a, b, (((0,), (0,)), ((), ())), preferred_element_type=F32)


def _ret_ctx_kernel(k_ref, v_ref, df_ref, db_ref, sf_ref, sb_ref):
    for hh in range(RET_HEADS):
        kh = k_ref[:, hh * RET_DK:(hh + 1) * RET_DK].astype(F32)
        vh = v_ref[:, hh * RET_DV:(hh + 1) * RET_DV]
        sf_ref[hh] = _tdot((kh * df_ref[hh]).astype(BF16), vh)
        sb_ref[hh] = _tdot((kh * db_ref[hh]).astype(BF16), vh)


def _ret_ctx(zc, B, dec_f, dec_b):
    L = zc.shape[0]
    H = RET_HEADS
    st = jax.ShapeDtypeStruct((B, H, RET_DK, RET_DV), F32)
    return pl.pallas_call(
        _ret_ctx_kernel,
        grid=(B,),
        in_specs=[pl.BlockSpec((L, H * RET_DK), lambda b: (0, b * N_CHUNKS + 3)),
                  pl.BlockSpec((L, H * RET_DV), lambda b: (0, b * (N_CHUNKS // 2) + 2)),
                  pl.BlockSpec((H, L, RET_DK), lambda b: (0, 0, 0)),
                  pl.BlockSpec((H, L, RET_DK), lambda b: (0, 0, 0))],
        out_specs=[pl.BlockSpec((None, H, RET_DK, RET_DV), lambda b: (b, 0, 0, 0))] * 2,
        out_shape=[st, st],
        compiler_params=_cparams(1),
        name="ret_ctx",
    )(zc, zc, dec_f, dec_b)


def _ret_bwd_kernel(k_ref, v_ref, s0_ref, kd_ref, cd_ref, o_ref, st_ref):
    @pl.when(pl.program_id(1) == 0)
    def _():
        st_ref[...] = s0_ref[...]

    for hh in range(RET_HEADS):
        o_ref[hh] = st_ref[hh].astype(o_ref.dtype)
        kh = k_ref[:, hh * RET_DK:(hh + 1) * RET_DK].astype(F32)
        vh = v_ref[:, hh * RET_DV:(hh + 1) * RET_DV]
        st_ref[hh] = st_ref[hh] * cd_ref[hh:hh + 1, :] + _tdot((kh * kd_ref[hh]).astype(BF16), vh)


def _ret_bwd(z, B, s_bwd, kdec_b, cdec):
    S = z.shape[0]
    H, C = RET_HEADS, min(RET_C, S)
    n = S // C
    return pl.pallas_call(
        _ret_bwd_kernel,
        grid=(B, n),
        in_specs=[pl.BlockSpec((C, H * RET_DK), lambda b, j: (n - 1 - j, b * N_CHUNKS + 3)),
                  pl.BlockSpec((C, H * RET_DV), lambda b, j: (n - 1 - j, b * (N_CHUNKS // 2) + 2)),
                  pl.BlockSpec((None, H, RET_DK, RET_DV), lambda b, j: (b, 0, 0, 0)),
                  pl.BlockSpec((H, C, RET_DK), lambda b, j: (0, 0, 0)),
                  pl.BlockSpec((H, RET_DV), lambda b, j: (0, 0))],
        out_specs=pl.BlockSpec((None, None, H, RET_DK, RET_DV), lambda b, j: (b, n - 1 - j, 0, 0, 0)),
        out_shape=jax.ShapeDtypeStruct((B, n, H, RET_DK, RET_DV), BF16),
        scratch_shapes=[pltpu.VMEM((H, RET_DK, RET_DV), F32)],
        compiler_params=_cparams(2),
        name="ret_bwd",
    )(z, z, s_bwd, kdec_b, cdec)


def _ret_fwd_kernel(q_ref, k_ref, v_ref, gs_ref, sb_ref, s0_ref, intra_ref, qf_ref, qb_ref, kf_ref, cd_ref,
                    o_ref, st_ref):
    @pl.when(pl.program_id(1) == 0)
    def _():
        st_ref[...] = s0_ref[...]

    for hh in range(RET_HEADS):
        qh = q_ref[:, hh * RET_DK:(hh + 1) * RET_DK]
        kh = k_ref[:, hh * RET_DK:(hh + 1) * RET_DK]
        vh = v_ref[:, hh * RET_DV:(hh + 1) * RET_DV]
        qf32 = qh.astype(F32)
        s = lax.dot_general(qh, kh, (((1,), (1,)), ((), ())), preferred_element_type=F32) * intra_ref[hh]
        o = jnp.dot(s.astype(BF16), vh, preferred_element_type=F32)
        o = o + jnp.dot((qf32 * qf_ref[hh]).astype(BF16), st_ref[hh].astype(BF16), preferred_element_type=F32)
        o = o + jnp.dot((qf32 * qb_ref[hh]).astype(BF16), sb_ref[hh], preferred_element_type=F32)
        st_ref[hh] = st_ref[hh] * cd_ref[hh:hh + 1, :] + _tdot((kh.astype(F32) * kf_ref[hh]).astype(BF16), vh)
        mu = jnp.mean(o, axis=-1, keepdims=True)
        d = o - mu
        var = jnp.mean(d * d, axis=-1, keepdims=True)
        g = gs_ref[:, hh * RET_DV:(hh + 1) * RET_DV].astype(F32)
        o_ref[:, hh * RET_DV:(hh + 1) * RET_DV] = (g * _sigmoid(g) * d * lax.rsqrt(var + EPS)).astype(o_ref.dtype)


def _ret_fwd(z, B, sb, s_fwd, intra, qdec_f, qdec_b, kdec_f, cdec):
    S = z.shape[0]
    H, C = RET_HEADS, min(RET_C, S)
    n = S // C
    half = N_CHUNKS // 2
    return pl.pallas_call(
        _ret_fwd_kernel,
        grid=(B, n),
        in_specs=[pl.BlockSpec((C, H * RET_DK), lambda b, j: (j, b * N_CHUNKS + 2)),
                  pl.BlockSpec((C, H * RET_DK), lambda b, j: (j, b * N_CHUNKS + 3)),
                  pl.BlockSpec((C, H * RET_DV), lambda b, j: (j, b * half + 2)),
                  pl.BlockSpec((C, H * RET_DV), lambda b, j: (j, b * half + 3)),
                  pl.BlockSpec((None, None, H, RET_DK, RET_DV), lambda b, j: (b, j, 0, 0, 0)),
                  pl.BlockSpec((None, H, RET_DK, RET_DV), lambda b, j: (b, 0, 0, 0)),
                  pl.BlockSpec((H, C, C), lambda b, j: (0, 0, 0)),
                  pl.BlockSpec((H, C, RET_DK), lambda b, j: (0, 0, 0)),
                  pl.BlockSpec((H, C, RET_DK), lambda b, j: (0, 0, 0)),
                  pl.BlockSpec((H, C, RET_DK), lambda b, j: (0, 0, 0)),
                  pl.BlockSpec((H, RET_DV), lambda b, j: (0, 0))],
        out_specs=pl.BlockSpec((None, C, H * RET_DV), lambda b, j: (b, j, 0)),
        out_shape=jax.ShapeDtypeStruct((B, S, H * RET_DV), BF16),
        scratch_shapes=[pltpu.VMEM((H, RET_DK, RET_DV), F32)],
        compiler_params=_cparams(2),
        name="ret_fwd",
    )(z, z, z, z, sb, s_fwd, intra, qdec_f, qdec_b, kdec_f, cdec)


def _dot_t(a, b):
    return lax.dot_general(a, b, (((1,), (1,)), ((), ())), preferred_element_type=F32)


def _finish_kernel(x_ref, hf_ref, hb_ref, gr_ref, gab_ref, ret_ref, g1_ref, sh2_ref, sc2_ref, n2_ref,
                   wr_ref, wt_ref, wo_ref, rwh_ref, rwl_ref, rb_ref,
                   x1_ref, hl_ref, eid_ref, gate_ref, cnt_ref):
    D = x_ref.shape[1]
    tm = x_ref.shape[0]
    dt = eid_ref.shape[2]
    rnn = hf_ref[...].astype(F32) + hb_ref[...].astype(F32)
    y_rnn = jnp.dot((rnn * jax.nn.gelu(gr_ref[...].astype(F32))).astype(BF16), wr_ref[...],
                    preferred_element_type=F32)
    y_ret = jnp.dot(ret_ref[...], wt_ref[...], preferred_element_type=F32)
    ga = gab_ref[:, :D].astype(F32)
    gb = gab_ref[:, D:].astype(F32)
    merged = _sigmoid(ga) * y_rnn + _sigmoid(gb) * y_ret
    y = jnp.dot(merged.astype(BF16), wo_ref[...], preferred_element_type=F32)
    x1 = x_ref[...] + g1_ref[...] * y
    x1_ref[...] = x1
    hl = _rmsnorm(x1, n2_ref[...]) * (1.0 + sc2_ref[...]) + sh2_ref[...]
    hh = hl.astype(BF16)
    hl_ref[...] = hh

    hlo = (hl - hh.astype(F32)).astype(BF16)
    logits = (_dot_t(rwh_ref[...], hh) + _dot_t(rwh_ref[...], hlo) + _dot_t(rwl_ref[...], hh)) + rb_ref[...]

    ne = logits.shape[0]
    sub = lax.broadcasted_iota(I32, (ne, tm), 0)
    work = logits
    vals, idxs = [], []
    oh = jnp.zeros((ne, tm), F32)
    for _ in range(TOP_K):
        m = jnp.max(work, axis=0, keepdims=True)
        idx = jnp.min(jnp.where(work == m, sub, ne), axis=0, keepdims=True)
        hot = sub == idx
        vals.append(m)
        idxs.append(idx)
        oh = oh + jnp.where(hot, 1.0, 0.0)
        work = jnp.where(hot, -jnp.inf, work)
    es = [jnp.exp(v - vals[0]) for v in vals]
    inv = 1.0 / (es[0] + es[1] + es[2] + es[3])
    for part in range(tm // dt):
        ls = slice(part * dt, (part + 1) * dt)
        for k in range(TOP_K):
            eid_ref[part, k:k + 1, :] = idxs[k][:, ls]
            gate_ref[part, k:k + 1, :] = (es[k] * inv)[:, ls]
        cnt_ref[part] = jnp.sum(oh[:, ls], axis=1, keepdims=True)


def _finish(x3, hf, hb, z, retg, mods3, norm2_g, w_rnn_b, w_ret_b, w_out_b, rwt_hi, rwt_lo, router_b, dt):
    B, S, D = x3.shape
    tm = min(FIN_TM, S)
    nt = S // tm
    N = B * S
    ne = rwt_hi.shape[0]
    half = N_CHUNKS // 2
    per = tm // dt
    const2 = lambda b, i: (0, 0)
    tile3 = lambda b, i: (b * nt + i, 0, 0)
    return pl.pallas_call(
        _finish_kernel,
        grid=(B, nt),
        in_specs=[pl.BlockSpec((None, tm, D), lambda b, i: (b, i, 0)),
                  pl.BlockSpec((None, tm, D), lambda b, i: (b, i, 0)),
                  pl.BlockSpec((None, tm, D), lambda b, i: (b, i, 0)),
                  pl.BlockSpec((tm, D), lambda b, i: (i, b * N_CHUNKS + 1)),
                  pl.BlockSpec((tm, 2 * D), lambda b, i: (i, b * half + 4)),
                  pl.BlockSpec((None, tm, retg.shape[2]), lambda b, i: (b, i, 0)),
                  pl.BlockSpec((None, 1, D), lambda b, i: (b, 0, 2)),
                  pl.BlockSpec((None, 1, D), lambda b, i: (b, 0, 3)),
                  pl.BlockSpec((None, 1, D), lambda b, i: (b, 0, 4)),
                  pl.BlockSpec((1, D), const2),
                  pl.BlockSpec(w_rnn_b.shape, const2),
                  pl.BlockSpec(w_ret_b.shape, const2),
                  pl.BlockSpec(w_out_b.shape, const2),
                  pl.BlockSpec(rwt_hi.shape, const2),
                  pl.BlockSpec(rwt_lo.shape, const2),
                  pl.BlockSpec((ne, 1), const2)],
        out_specs=[pl.BlockSpec((None, tm, D), lambda b, i: (b, i, 0)),
                   pl.BlockSpec((tm, D), lambda b, i: (b * nt + i, 0)),
                   pl.BlockSpec((per, TOP_K, dt), tile3),
                   pl.BlockSpec((per, TOP_K, dt), tile3),
                   pl.BlockSpec((per, ne, 1), tile3)],
        out_shape=[jax.ShapeDtypeStruct((B, S, D), F32),
                   jax.ShapeDtypeStruct((N, D), BF16),
                   jax.ShapeDtypeStruct((N // dt, TOP_K, dt), I32),
                   jax.ShapeDtypeStruct((N // dt, TOP_K, dt), F32),
                   jax.ShapeDtypeStruct((N // dt, ne, 1), F32)],
        compiler_params=_cparams(2),
        name="finish",
    )(x3, hf, hb, z, z, retg, mods3, mods3, mods3, norm2_g.reshape(1, D),
      w_rnn_b, w_ret_b, w_out_b, rwt_hi, rwt_lo, router_b.reshape(ne, 1))


def _piece_sizes(max_rows):
    return tuple(2 ** p for p in range(max_rows.bit_length() - 1, ROW_ALIGN.bit_length() - 2, -1))


def _local_rows(dt):
    return TOP_K * dt + N_EXPERTS * ROW_ALIGN


def _local_slots(eid_ref):
    dt = eid_ref.shape[1]
    ne = N_EXPERTS
    sub = lax.broadcasted_iota(I32, (ne, dt), 0)
    hots = [sub == eid_ref[k:k + 1, :] for k in range(TOP_K)]
    oh = jnp.zeros((ne, dt), F32)
    for hot in hots:
        oh = oh + jnp.where(hot, 1.0, 0.0)
    earlier = jnp.where(lax.broadcasted_iota(I32, (dt, dt), 0) < lax.broadcasted_iota(I32, (dt, dt), 1), 1.0, 0.0)
    before = jnp.dot(oh.astype(BF16), earlier.astype(BF16), preferred_element_type=F32)
    cnt = jnp.broadcast_to(jnp.sum(oh, axis=1, keepdims=True), (ne, dt))
    cnt = jnp.ceil(cnt * (1.0 / ROW_ALIGN)) * ROW_ALIGN
    lower = jnp.where(lax.broadcasted_iota(I32, (ne, ne), 1) < lax.broadcasted_iota(I32, (ne, ne), 0), 1.0, 0.0)
    base = before + jnp.dot(lower.astype(BF16), cnt.astype(BF16), preferred_element_type=F32)
    return [jnp.sum(jnp.where(hot, base, 0.0), axis=0, keepdims=True).astype(I32) for hot in hots]


def _for_each_run_piece(meta_ref, max_rows, fn):
    def body(e, c):
        lo = pl.multiple_of(meta_ref[0, e], ROW_ALIGN)
        n = meta_ref[0, N_EXPERTS + e]
        gs = pl.multiple_of(meta_ref[0, 2 * N_EXPERTS + e], ROW_ALIGN)
        off = 0
        for p in _piece_sizes(max_rows):
            take = (n & p) != 0

            @pl.when(take)
            def _(off=off, p=p):
                fn(pl.multiple_of(lo + off, ROW_ALIGN), pl.multiple_of(gs + off, ROW_ALIGN), p)

            off = off + jnp.where(take, p, 0)
        return c

    lax.fori_loop(0, N_EXPERTS, body, 0)


def _dispatch_kernel(meta_ref, zmeta_ref, eid_ref, x_ref, hp_ref, sbuf_ref, zbuf_ref, sem, zsem):
    dt = x_ref.shape[0]

    def zero_copy(_, g, p):
        return pltpu.make_async_copy(zbuf_ref.at[pl.ds(0, p), :], hp_ref.at[pl.ds(g, p), :], zsem)

    @pl.when(pl.program_id(0) == 0)
    def _():
        zbuf_ref[...] = jnp.zeros_like(zbuf_ref)
        _for_each_run_piece(zmeta_ref, MOE_BLOCK - 1, lambda l, g, p: zero_copy(l, g, p).start())
        _for_each_run_piece(zmeta_ref, MOE_BLOCK - 1, lambda l, g, p: zero_copy(l, g, p).wait())

    slots = _local_slots(eid_ref)
    rowi = lax.broadcasted_iota(I32, (_local_rows(dt), dt), 0)
    pm = jnp.zeros((_local_rows(dt), dt), F32)
    for s in slots:
        pm = pm + jnp.where(rowi == s, 1.0, 0.0)
    sbuf_ref[...] = jnp.dot(pm.astype(BF16), x_ref[...], preferred_element_type=F32)

    def run_copy(l, g, p):
        return pltpu.make_async_copy(sbuf_ref.at[pl.ds(l, p), :], hp_ref.at[pl.ds(g, p), :], sem)

    _for_each_run_piece(meta_ref, dt, lambda l, g, p: run_copy(l, g, p).start())
    _for_each_run_piece(meta_ref, dt, lambda l, g, p: run_copy(l, g, p).wait())


def _dispatch(hl2, eid3, meta3, zmeta, P):
    N, D = hl2.shape
    nt, _, dt = eid3.shape
    nm = meta3.shape[2]
    return pl.pallas_call(
        _dispatch_kernel,
        grid=(nt,),
        in_specs=[pl.BlockSpec((None, 1, nm), lambda i: (i, 0, 0), memory_space=pltpu.SMEM),
                  pl.BlockSpec((1, nm), lambda i: (0, 0), memory_space=pltpu.SMEM),
                  pl.BlockSpec((None, TOP_K, dt), lambda i: (i, 0, 0)),
                  pl.BlockSpec((dt, D), lambda i: (i, 0))],
        out_specs=pl.BlockSpec(memory_space=pl.ANY),
        out_shape=jax.ShapeDtypeStruct((P, D), F32),
        scratch_shapes=[pltpu.VMEM((_local_rows(dt), D), F32),
                        pltpu.VMEM((_piece_sizes(MOE_BLOCK - 1)[0], D), F32),
                        pltpu.SemaphoreType.DMA(()),
                        pltpu.SemaphoreType.DMA(())],
        compiler_params=_cparams(1),
        name="dispatch",
    )(meta3, zmeta, eid3, hl2)


def _expert_kernel(be_ref, nu_ref, x_ref, w1_ref, b1_ref, w2_ref, b2_ref, o_ref):
    del be_ref

    @pl.when(pl.program_id(0) < nu_ref[0])
    def _():
        de = w2_ref.shape[0]
        h = jnp.dot(x_ref[...].astype(BF16), w1_ref[...], preferred_element_type=F32) + b1_ref[...]
        glu = jnp.minimum(h[:, :de], SWIGLU_LIMIT)
        lin = jnp.clip(h[:, de:], -SWIGLU_LIMIT, SWIGLU_LIMIT)
        act = glu * _sigmoid(SWIGLU_ALPHA * glu) * (lin + 1.0)
        o_ref[...] = jnp.dot(act.astype(BF16), w2_ref[...], preferred_element_type=F32) + b2_ref[...]


def _experts(blk_expert, n_used, h_pad, w1b, b1p, w2b, b2):
    P, D = h_pad.shape
    ne, _, de2 = w1b.shape
    de = de2 // 2
    nblk = P // MOE_BLOCK
    blk = lambda j, be, nu: (jnp.minimum(j, nu[0] - 1), 0)
    wsel = lambda j, be, nu: (be[jnp.minimum(j, nu[0] - 1)], 0, 0)
    return pl.pallas_call(
        _expert_kernel,
        grid_spec=pltpu.PrefetchScalarGridSpec(
            num_scalar_prefetch=2,
            grid=(nblk,),
            in_specs=[pl.BlockSpec((MOE_BLOCK, D), blk),
                      pl.BlockSpec((None, D, de2), wsel),
                      pl.BlockSpec((None, 1, de2), wsel),
                      pl.BlockSpec((None, de, D), wsel),
                      pl.BlockSpec((None, 1, D), wsel)],
            out_specs=pl.BlockSpec((MOE_BLOCK, D), blk)),
        out_shape=jax.ShapeDtypeStruct((P, D), F32),
        compiler_params=_cparams(1),
        name="experts",
    )(blk_expert, n_used, h_pad, w1b, b1p.reshape(ne, 1, de2), w2b, b2.reshape(ne, 1, D))


def _combine_kernel(meta_ref, eid_ref, gate_ref, x1_ref, g2_ref, fg_ref, yp_ref, o_ref, ybuf_ref, sem):
    dt = x1_ref.shape[0]

    def run_copy(l, g, p):
        return pltpu.make_async_copy(yp_ref.at[pl.ds(g, p), :], ybuf_ref.at[pl.ds(l, p), :], sem)

    @pl.when(jnp.logical_and(pl.program_id(0) == 0, pl.program_id(1) == 0))
    def _():
        ybuf_ref[...] = jnp.zeros_like(ybuf_ref)

    _for_each_run_piece(meta_ref, dt, lambda l, g, p: run_copy(l, g, p).start())

    slots = _local_slots(eid_ref)
    rowi = lax.broadcasted_iota(I32, (_local_rows(dt), dt), 0)
    gm = jnp.zeros((_local_rows(dt), dt), F32)
    for k, s in enumerate(slots):
        gm = gm + jnp.where(rowi == s, gate_ref[k:k + 1, :], 0.0)

    _for_each_run_piece(meta_ref, dt, lambda l, g, p: run_copy(l, g, p).wait())

    y = _tdot(gm.astype(BF16), ybuf_ref[...].astype(BF16))
    x2 = x1_ref[...] + g2_ref[...] * y
    o_ref[...] = _rmsnorm(x2, fg_ref[...])


def _combine(meta3, eid3, gate3, x1, mods3, final_g, y_pad):
    B, S, D = x1.shape
    _, _, dt = eid3.shape
    nt = S // dt
    nm = meta3.shape[2]
    tile3 = lambda b, i: (b * nt + i, 0, 0)
    return pl.pallas_call(
        _combine_kernel,
        grid=(B, nt),
        in_specs=[pl.BlockSpec((None, 1, nm), tile3, memory_space=pltpu.SMEM),
                  pl.BlockSpec((None, TOP_K, dt), tile3),
                  pl.BlockSpec((None, TOP_K, dt), tile3),
                  pl.BlockSpec((None, dt, D), lambda b, i: (b, i, 0)),
                  pl.BlockSpec((None, 1, D), lambda b, i: (b, 0, 5)),
                  pl.BlockSpec((1, D), lambda b, i: (0, 0)),
                  pl.BlockSpec(memory_space=pl.ANY)],
        out_specs=pl.BlockSpec((None, dt, D), lambda b, i: (b, i, 0)),
        out_shape=jax.ShapeDtypeStruct((B, S, D), F32),
        scratch_shapes=[pltpu.VMEM((_local_rows(dt), D), F32), pltpu.SemaphoreType.DMA(())],
        compiler_params=_cparams(2),
        name="combine",
    )(meta3, eid3, gate3, x1, mods3, final_g.reshape(1, D), y_pad)


def _rope_tables(S, k_scale):
    n_freq = RET_DK // 4
    pos = jnp.arange(S, dtype=F32)
    rows = jnp.floor(pos / GRID_W)
    cols = pos - rows * GRID_W
    inv = ROPE_BASE ** (-jnp.arange(n_freq, dtype=F32) / n_freq)
    ang = jnp.concatenate([rows[:, None] * inv, cols[:, None] * inv], axis=-1)
    cos, sin = jnp.cos(ang), jnp.sin(ang)
    cos2 = jnp.concatenate([cos, cos], axis=-1)
    sin2 = jnp.concatenate([-sin, sin], axis=-1)
    return jnp.concatenate([cos2, sin2, cos2 * k_scale, sin2 * k_scale], axis=-1)


def _identity_tables(L, k_scale):
    one = jnp.ones((L, RET_DK), F32)
    zero = jnp.zeros((L, RET_DK), F32)
    return jnp.concatenate([one, zero, one * k_scale, zero], axis=-1)


def _lanes(t, width):
    return jnp.broadcast_to(t[:, :, None], t.shape + (width,))


def kernel(x, c, ctx, c_ctx, ada_w, ada_b, norm1_g, w_in, conv_w, conv_b, lru_wa, lru_ba, lru_wx, lru_bx,
           lru_lambda, w_rnn_proj, w_ret_proj, w_out, norm2_g, router_w, router_b, moe_w1, moe_b1, moe_w2,
           moe_b2, final_g):
    B, S, D = x.shape
    L = ctx.shape[1]
    N = B * S
    H = RET_HEADS
    lyr = 0
    d_in = w_in.shape[2]
    assert ada_w.shape[0] == 1 and d_in == N_CHUNKS * D and B == 8

    def pairs_apart(w):
        return jnp.swapaxes(w.reshape(D, H, RET_DK // 2, 2), 2, 3).reshape(D, H * RET_DK)

    w_in_l = w_in[lyr].astype(BF16)
    w_in_b = jnp.concatenate([w_in_l[:, :2 * D], pairs_apart(w_in_l[:, 2 * D:3 * D]),
                              pairs_apart(w_in_l[:, 3 * D:4 * D]), w_in_l[:, 4 * D:]], axis=1)
    wg = [(0.5 * jnp.concatenate([lru_wa[lyr, d], lru_wx[lyr, d]], axis=-1)).astype(BF16) for d in range(2)]
    de2 = moe_w1.shape[3]
    glu_lin = np.concatenate([np.arange(0, de2, 2), np.arange(1, de2, 2)])
    w1b = moe_w1[lyr][:, :, glu_lin].astype(BF16)
    b1p = moe_b1[lyr][:, glu_lin]
    w2b = moe_w2[lyr].astype(BF16)
    rwt = router_w[lyr].T
    rwt_hi = rwt.astype(BF16)
    rwt_lo = (rwt - rwt_hi.astype(F32)).astype(BF16)

    k_scale = RET_DK ** -0.5
    tab_l = _rope_tables(S, k_scale)
    tab_c = _identity_tables(L, k_scale)
    log_g = jnp.log1p(-jnp.exp2(-5.0 - jnp.arange(H, dtype=F32)))
    C = min(RET_C, S)
    idx = jnp.arange(C, dtype=F32)
    dec = lambda e: jnp.exp(e[None, :] * log_g[:, None])
    intra = jnp.exp(jnp.abs(idx[:, None] - idx[None, :])[None] * log_g[:, None, None])
    qdec_f = _lanes(dec(idx + 1.0), RET_DK)
    qdec_b = _lanes(dec(C - idx), RET_DK)
    kdec_f = _lanes(dec(C - 1.0 - idx), RET_DK)
    kdec_b = _lanes(dec(idx), RET_DK)
    cdec = jnp.broadcast_to(jnp.exp(C * log_g)[:, None], (H, RET_DV))
    pos_c = jnp.arange(L, dtype=F32)
    cdec_f = _lanes(dec(L - 1.0 - pos_c), RET_DK)
    cdec_b = _lanes(dec(pos_c), RET_DK)

    cvec = jnp.zeros((16, D), F32).at[:B].set(c).at[B].set(c_ctx)
    mods3 = _ada(cvec, ada_w[lyr], ada_b[lyr]).reshape(16, 1, 6 * D)

    z_c, xr_c = _proj(ctx, mods3, lambda b: B, norm1_g[lyr], w_in_b, tab_c, min(PROJ_TM, L))
    z_l, xr_l = _proj(x, mods3, lambda b: b, norm1_g[lyr], w_in_b, tab_l, min(PROJ_TM, S))

    zeros = jnp.zeros((B, D), F32)
    hs = []
    for d in range(2):
        args = (conv_w[lyr], conv_b[lyr], wg[d], 0.5 * lru_ba[lyr, d], 0.5 * lru_bx[lyr, d], lru_lambda[lyr, d])
        _, h0 = _rnn(xr_c, *args, zeros, reverse=(d == 1), tt=RNN_TT)
        h, _ = _rnn(xr_l, *args, h0, reverse=(d == 1), tt=RNN_TT)
        hs.append(h)

    s_fwd, s_bwd = _ret_ctx(z_c, B, cdec_f, cdec_b)
    sb = _ret_bwd(z_l, B, s_bwd, kdec_b, cdec)
    retg = _ret_fwd(z_l, B, sb, s_fwd, intra, qdec_f, qdec_b, kdec_f, cdec)

    dt = min(DISP_TM, S)
    x1, hl2, eid3, gate3, cnt3 = _finish(
        x, hs[0], hs[1], z_l, retg, mods3, norm2_g[lyr], w_rnn_proj[lyr].astype(BF16),
        w_ret_proj[lyr].astype(BF16), w_out[lyr].astype(BF16), rwt_hi, rwt_lo, router_b[lyr], dt)

    cnt_t = cnt3[:, :, 0].astype(I32)
    cnt_t = (cnt_t + ROW_ALIGN - 1) // ROW_ALIGN * ROW_ALIGN
    cnt = jnp.sum(cnt_t, axis=0)
    padded = (cnt + MOE_BLOCK - 1) // MOE_BLOCK * MOE_BLOCK
    pad_end = jnp.cumsum(padded)
    pad_start = pad_end - padded
    gstart = pad_start[None, :] + jnp.cumsum(cnt_t, axis=0) - cnt_t
    loff = jnp.cumsum(cnt_t, axis=1) - cnt_t
    meta3 = jnp.concatenate([loff, cnt_t, gstart], axis=1).reshape(N // dt, 1, 3 * N_EXPERTS)
    zmeta = jnp.concatenate([jnp.zeros_like(cnt), padded - cnt, pad_start + cnt]).reshape(1, 3 * N_EXPERTS)
    n_blocks = -(-(N * TOP_K + (N // dt) * N_EXPERTS * (ROW_ALIGN - 1)) // MOE_BLOCK) + N_EXPERTS
    P = n_blocks * MOE_BLOCK
    blk_start = jnp.arange(n_blocks, dtype=I32) * MOE_BLOCK
    blk_expert = jnp.minimum(jnp.sum((pad_end[None, :] <= blk_start[:, None]).astype(I32), axis=1), N_EXPERTS - 1)
    n_used = (pad_end[-1:] // MOE_BLOCK).astype(I32)

    h_pad = _dispatch(hl2, eid3, meta3, zmeta, P)
    y_pad = _experts(blk_expert, n_used, h_pad, w1b, b1p, w2b, moe_b2[lyr])
    return _combine(meta3, eid3, gate3, x1, mods3, final_g, y_pad)
```

```python
import functools

import jax
import jax.numpy as jnp
import numpy as np
from jax import lax
from jax.experimental import pallas as pl
from jax.experimental.pallas import tpu as pltpu

F32 = jnp.float32
BF16 = jnp.bfloat16
I32 = jnp.int32

GRID_W = 64
RNN_BLOCKS = 8
CONV_W = 4
LRU_C = 8.0
RET_HEADS = 8
RET_DK = 128
RET_DV = 256
ROPE_BASE = 10000.0
N_EXPERTS = 32
TOP_K = 4
SWIGLU_ALPHA = 1.702
SWIGLU_LIMIT = 7.0
EPS = 1e-6
N_CHUNKS = 10

PROJ_TM = 1024
PROJ_TN = 2048
RNN_TT = 128
RET_C = 256
FIN_TM = 512
MOE_BLOCK = 512
DISP_TM = 512
ROW_ALIGN = 8
VMEM_LIMIT = 56 * 1024 * 1024


def _cparams(n_axes):
    return pltpu.CompilerParams(dimension_semantics=("arbitrary",) * n_axes,
                                vmem_limit_bytes=VMEM_LIMIT)


def _sigmoid(x):
    return 0.5 * (jnp.tanh(0.5 * x) + 1.0)


def _rmsnorm(x, g):
    return x * lax.rsqrt(jnp.mean(x * x, axis=-1, keepdims=True) + EPS) * g


def _ada_kernel(c_ref, w_ref, b_ref, o_ref):
    c = c_ref[...]
    s = c * _sigmoid(c)
    o_ref[...] = jnp.dot(s, w_ref[...], preferred_element_type=F32,
                         precision=lax.Precision.HIGHEST) + b_ref[...]


def _ada(cvec, ada_w, ada_b):
    R, D = cvec.shape
    n = ada_w.shape[1] // D
    return pl.pallas_call(
        _ada_kernel,
        grid=(n,),
        in_specs=[pl.BlockSpec((R, D), lambda j: (0, 0)),
                  pl.BlockSpec((D, D), lambda j: (0, j)),
                  pl.BlockSpec((1, D), lambda j: (0, j))],
        out_specs=pl.BlockSpec((R, D), lambda j: (0, j)),
        out_shape=jax.ShapeDtypeStruct((R, n * D), F32),
        compiler_params=_cparams(1),
        name="ada",
    )(cvec, ada_w, ada_b.reshape(1, -1))


def _proj_kernel(x_ref, sh_ref, sc_ref, g_ref, w_ref, tab_ref, o_ref, xr_ref, h_ref):
    j = pl.program_id(2)
    D = x_ref.shape[1]

    @pl.when(j == 0)
    def _():
        h = _rmsnorm(x_ref[...], g_ref[...])
        h_ref[...] = (h * (1.0 + sc_ref[...]) + sh_ref[...]).astype(BF16)

    acc = jnp.dot(h_ref[...], w_ref[...], preferred_element_type=F32)

    @pl.when(j != 1)
    def _():
        o_ref[...] = acc.astype(o_ref.dtype)

    @pl.when(j == 0)
    def _():
        xr_ref[...] = acc[:, :D].astype(xr_ref.dtype)

    @pl.when(j == 1)
    def _():
        for part in range(2):
            cos = tab_ref[:, (2 * part) * RET_DK:(2 * part + 1) * RET_DK]
            sin = tab_ref[:, (2 * part + 1) * RET_DK:(2 * part + 2) * RET_DK]
            for hh in range(RET_HEADS):
                c0 = part * RET_HEADS * RET_DK + hh * RET_DK
                t = acc[:, c0:c0 + RET_DK]
                o_ref[:, c0:c0 + RET_DK] = (t * cos + pltpu.roll(t, RET_DK // 2, 1) * sin).astype(o_ref.dtype)


def _proj(x3, mods3, mod_row, norm_g, w_in_b, tab, tm):
    B, S, D = x3.shape
    d_in = w_in_b.shape[1]
    nj = d_in // PROJ_TN
    return pl.pallas_call(
        _proj_kernel,
        grid=(B, S // tm, nj),
        in_specs=[pl.BlockSpec((None, tm, D), lambda b, i, j: (b, i, 0)),
                  pl.BlockSpec((None, 1, D), lambda b, i, j: (mod_row(b), 0, 0)),
                  pl.BlockSpec((None, 1, D), lambda b, i, j: (mod_row(b), 0, 1)),
                  pl.BlockSpec((1, D), lambda b, i, j: (0, 0)),
                  pl.BlockSpec((D, PROJ_TN), lambda b, i, j: (0, j)),
                  pl.BlockSpec((tm, 4 * RET_DK), lambda b, i, j: (i, 0))],
        out_specs=[pl.BlockSpec((tm, PROJ_TN), lambda b, i, j: (i, b * nj + j)),
                   pl.BlockSpec((None, tm, D), lambda b, i, j: (b, i, 0))],
        out_shape=[jax.ShapeDtypeStruct((S, B * d_in), BF16),
                   jax.ShapeDtypeStruct((B, S, D), BF16)],
        scratch_shapes=[pltpu.VMEM((tm, D), BF16)],
        compiler_params=_cparams(3),
        name="proj",
    )(x3, mods3, mods3, norm_g.reshape(1, D), w_in_b, tab)


def _rnn_kernel(xm_ref, xp_ref, xn_ref, cw_ref, cb_ref, wg_ref, ba_ref, bx_ref, lam_ref, h0_ref,
                h_ref, hfin_ref, xs_ref, a_ref, u_ref, hc_ref, *, reverse, n_tiles):
    i = pl.program_id(0)
    tile = (n_tiles - 1 - i) if reverse else i
    nb, tt, D = xm_ref.shape
    bw = D // RNN_BLOCKS
    HALO = xp_ref.shape[1]

    @pl.when(i == 0)
    def _():
        hc_ref[...] = h0_ref[...]

    def time_major(v):
        return jnp.swapaxes(v.astype(F32), 0, 1).reshape(v.shape[1] * nb, D)

    hr = HALO * nb
    R = tt * nb
    xs_ref[0:hr, :] = jnp.where(tile > 0, time_major(xp_ref[...]), 0.0)
    xs_ref[hr:hr + R, :] = time_major(xm_ref[...])
    xs_ref[hr + R:2 * hr + R, :] = jnp.where(tile < n_tiles - 1, time_major(xn_ref[...]), 0.0)

    nl = -lam_ref[...]
    csp = (0.25 * LRU_C) * (jnp.maximum(nl, 0.0) + jnp.log1p(jnp.exp(-jnp.abs(nl))))
    SUB = 256

    def gates(s, carry):
        r0 = pl.multiple_of(s * SUB, SUB)
        base = hr - 2 * nb
        xc = cb_ref[...] + cw_ref[0:1, :] * xs_ref[pl.ds(pl.multiple_of(r0 + base, nb), SUB), :]
        for k in range(1, CONV_W):
            xc = xc + cw_ref[k:k + 1, :] * xs_ref[pl.ds(pl.multiple_of(r0 + base + k * nb, nb), SUB), :]
        xb = xc.astype(BF16)
        for n in range(RNN_BLOCKS):
            g = jnp.dot(xb[:, n * bw:(n + 1) * bw], wg_ref[n], preferred_element_type=F32)
            cs = slice(n * bw, (n + 1) * bw)
            tr = jnp.tanh(g[:, :bw] + ba_ref[:, cs])
            ti = jnp.tanh(g[:, bw:] + bx_ref[:, cs])
            t = jnp.tanh(csp[:, cs] * tr + csp[:, cs])
            rc2 = 2.0 / (1.0 + t)
            a_ref[pl.ds(r0, SUB), cs] = rc2 - 1.0
            u_ref[pl.ds(r0, SUB), cs] = (jnp.sqrt(t) * rc2) * ((0.5 * ti + 0.5) * xc[:, cs])
        return carry

    lax.fori_loop(0, R // SUB, gates, 0)

    def step(t, h):
        ts = (tt - 1 - t) if reverse else t
        r0 = pl.multiple_of(ts * nb, nb)
        h = a_ref[pl.ds(r0, nb), :] * h + u_ref[pl.ds(r0, nb), :]
        u_ref[pl.ds(r0, nb), :] = h
        return h

    h = lax.fori_loop(0, tt, step, hc_ref[...], unroll=8)
    hc_ref[...] = h
    hfin_ref[...] = h
    h_ref[...] = jnp.swapaxes(u_ref[...].reshape(tt, nb, D), 0, 1).astype(h_ref.dtype)


def _rnn(xr, conv_w, conv_b, wg, ba, bx, lam, h0, *, reverse, tt):
    nb, T, D = xr.shape
    tt = min(tt, T)
    n_tiles = T // tt
    HALO = 16
    hb = tt // HALO

    def tile_of(i):
        return (n_tiles - 1 - i) if reverse else i

    kern = functools.partial(_rnn_kernel, reverse=reverse, n_tiles=n_tiles)
    return pl.pallas_call(
        kern,
        grid=(n_tiles,),
        in_specs=[pl.BlockSpec((nb, tt, D), lambda i: (0, tile_of(i), 0)),
                  pl.BlockSpec((nb, HALO, D), lambda i: (0, jnp.maximum(tile_of(i) * hb - 1, 0), 0)),
                  pl.BlockSpec((nb, HALO, D), lambda i: (0, jnp.minimum((tile_of(i) + 1) * hb, n_tiles * hb - 1), 0)),
                  pl.BlockSpec((CONV_W, D), lambda i: (0, 0)),
                  pl.BlockSpec((1, D), lambda i: (0, 0)),
                  pl.BlockSpec((RNN_BLOCKS, D // RNN_BLOCKS, 2 * D // RNN_BLOCKS), lambda i: (0, 0, 0)),
                  pl.BlockSpec((1, D), lambda i: (0, 0)),
                  pl.BlockSpec((1, D), lambda i: (0, 0)),
                  pl.BlockSpec((1, D), lambda i: (0, 0)),
                  pl.BlockSpec((nb, D), lambda i: (0, 0))],
        out_specs=[pl.BlockSpec((nb, tt, D), lambda i: (0, tile_of(i), 0)),
                   pl.BlockSpec((nb, D), lambda i: (0, 0))],
        out_shape=[jax.ShapeDtypeStruct((nb, T, D), BF16),
                   jax.ShapeDtypeStruct((nb, D), F32)],
        scratch_shapes=[pltpu.VMEM(((tt + 2 * HALO) * nb, D), F32),
                        pltpu.VMEM((nb * tt, D), F32),
                        pltpu.VMEM((nb * tt, D), F32),
                        pltpu.VMEM((nb, D), F32)],
        compiler_params=_cparams(1),
        name="rnn_bwd" if reverse else "rnn_fwd",
    )(xr, xr, xr, conv_w, conv_b.reshape(1, D), wg, ba.reshape(1, D), bx.reshape(1, D),
      lam.reshape(1, D), h0)


def _tdot(a, b):
    return lax.dot_general(a, b, (((0,), (0,)), ((), ())), preferred_element_type=F32)


def _ret_ctx_kernel(k_ref, v_ref, df_ref, db_ref, sf_ref, sb_ref):
    for hh in range(RET_HEADS):
        kh = k_ref[:, hh * RET_DK:(hh + 1) * RET_DK].astype(F32)
        vh = v_ref[:, hh * RET_DV:(hh + 1) * RET_DV]
        sf_ref[hh] = _tdot((kh * df_ref[hh]).astype(BF16), vh)
        sb_ref[hh] = _tdot((kh * db_ref[hh]).astype(BF16), vh)


def _ret_ctx(zc, B, dec_f, dec_b):
    L = zc.shape[0]
    H = RET_HEADS
    st = jax.ShapeDtypeStruct((B, H, RET_DK, RET_DV), F32)
    return pl.pallas_call(
        _ret_ctx_kernel,
        grid=(B,),
        in_specs=[pl.BlockSpec((L, H * RET_DK), lambda b: (0, b * N_CHUNKS + 3)),
                  pl.BlockSpec((L, H * RET_DV), lambda b: (0, b * (N_CHUNKS // 2) + 2)),
                  pl.BlockSpec((H, L, RET_DK), lambda b: (0, 0, 0)),
                  pl.BlockSpec((H, L, RET_DK), lambda b: (0, 0, 0))],
        out_specs=[pl.BlockSpec((None, H, RET_DK, RET_DV), lambda b: (b, 0, 0, 0))] * 2,
        out_shape=[st, st],
        compiler_params=_cparams(1),
        name="ret_ctx",
    )(zc, zc, dec_f, dec_b)


def _ret_bwd_kernel(k_ref, v_ref, s0_ref, kd_ref, cd_ref, o_ref, st_ref):
    @pl.when(pl.program_id(1) == 0)
    def _():
        st_ref[...] = s0_ref[...]

    for hh in range(RET_HEADS):
        o_ref[hh] = st_ref[hh].astype(o_ref.dtype)
        kh = k_ref[:, hh * RET_DK:(hh + 1) * RET_DK].astype(F32)
        vh = v_ref[:, hh * RET_DV:(hh + 1) * RET_DV]
        st_ref[hh] = st_ref[hh] * cd_ref[hh:hh + 1, :] + _tdot((kh * kd_ref[hh]).astype(BF16), vh)


def _ret_bwd(z, B, s_bwd, kdec_b, cdec):
    S = z.shape[0]
    H, C = RET_HEADS, min(RET_C, S)
    n = S // C
    return pl.pallas_call(
        _ret_bwd_kernel,
        grid=(B, n),
        in_specs=[pl.BlockSpec((C, H * RET_DK), lambda b, j: (n - 1 - j, b * N_CHUNKS + 3)),
                  pl.BlockSpec((C, H * RET_DV), lambda b, j: (n - 1 - j, b * (N_CHUNKS // 2) + 2)),
                  pl.BlockSpec((None, H, RET_DK, RET_DV), lambda b, j: (b, 0, 0, 0)),
                  pl.BlockSpec((H, C, RET_DK), lambda b, j: (0, 0, 0)),
                  pl.BlockSpec((H, RET_DV), lambda b, j: (0, 0))],
        out_specs=pl.BlockSpec((None, None, H, RET_DK, RET_DV), lambda b, j: (b, n - 1 - j, 0, 0, 0)),
        out_shape=jax.ShapeDtypeStruct((B, n, H, RET_DK, RET_DV), BF16),
        scratch_shapes=[pltpu.VMEM((H, RET_DK, RET_DV), F32)],
        compiler_params=_cparams(2),
        name="ret_bwd",
    )(z, z, s_bwd, kdec_b, cdec)


def _ret_fwd_kernel(q_ref, k_ref, v_ref, gs_ref, sb_ref, s0_ref, intra_ref, qf_ref, qb_ref, kf_ref, cd_ref,
                    o_ref, st_ref):
    @pl.when(pl.program_id(1) == 0)
    def _():
        st_ref[...] = s0_ref[...]

    for hh in range(RET_HEADS):
        qh = q_ref[:, hh * RET_DK:(hh + 1) * RET_DK]
        kh = k_ref[:, hh * RET_DK:(hh + 1) * RET_DK]
        vh = v_ref[:, hh * RET_DV:(hh + 1) * RET_DV]
        qf32 = qh.astype(F32)
        s = lax.dot_general(qh, kh, (((1,), (1,)), ((), ())), preferred_element_type=F32) * intra_ref[hh]
        o = jnp.dot(s.astype(BF16), vh, preferred_element_type=F32)
        o = o + jnp.dot((qf32 * qf_ref[hh]).astype(BF16), st_ref[hh].astype(BF16), preferred_element_type=F32)
        o = o + jnp.dot((qf32 * qb_ref[hh]).astype(BF16), sb_ref[hh], preferred_element_type=F32)
        st_ref[hh] = st_ref[hh] * cd_ref[hh:hh + 1, :] + _tdot((kh.astype(F32) * kf_ref[hh]).astype(BF16), vh)
        mu = jnp.mean(o, axis=-1, keepdims=True)
        d = o - mu
        var = jnp.mean(d * d, axis=-1, keepdims=True)
        g = gs_ref[:, hh * RET_DV:(hh + 1) * RET_DV].astype(F32)
        o_ref[:, hh * RET_DV:(hh + 1) * RET_DV] = (g * _sigmoid(g) * d * lax.rsqrt(var + EPS)).astype(o_ref.dtype)


def _ret_fwd(z, B, sb, s_fwd, intra, qdec_f, qdec_b, kdec_f, cdec):
    S = z.shape[0]
    H, C = RET_HEADS, min(RET_C, S)
    n = S // C
    half = N_CHUNKS // 2
    return pl.pallas_call(
        _ret_fwd_kernel,
        grid=(B, n),
        in_specs=[pl.BlockSpec((C, H * RET_DK), lambda b, j: (j, b * N_CHUNKS + 2)),
                  pl.BlockSpec((C, H * RET_DK), lambda b, j: (j, b * N_CHUNKS + 3)),
                  pl.BlockSpec((C, H * RET_DV), lambda b, j: (j, b * half + 2)),
                  pl.BlockSpec((C, H * RET_DV), lambda b, j: (j, b * half + 3)),
                  pl.BlockSpec((None, None, H, RET_DK, RET_DV), lambda b, j: (b, j, 0, 0, 0)),
                  pl.BlockSpec((None, H, RET_DK, RET_DV), lambda b, j: (b, 0, 0, 0)),
                  pl.BlockSpec((H, C, C), lambda b, j: (0, 0, 0)),
                  pl.BlockSpec((H, C, RET_DK), lambda b, j: (0, 0, 0)),
                  pl.BlockSpec((H, C, RET_DK), lambda b, j: (0, 0, 0)),
                  pl.BlockSpec((H, C, RET_DK), lambda b, j: (0, 0, 0)),
                  pl.BlockSpec((H, RET_DV), lambda b, j: (0, 0))],
        out_specs=pl.BlockSpec((None, C, H * RET_DV), lambda b, j: (b, j, 0)),
        out_shape=jax.ShapeDtypeStruct((B, S, H * RET_DV), BF16),
        scratch_shapes=[pltpu.VMEM((H, RET_DK, RET_DV), F32)],
        compiler_params=_cparams(2),
        name="ret_fwd",
    )(z, z, z, z, sb, s_fwd, intra, qdec_f, qdec_b, kdec_f, cdec)


def _dot_t(a, b):
    return lax.dot_general(a, b, (((1,), (1,)), ((), ())), preferred_element_type=F32)


def _finish_kernel(x_ref, hf_ref, hb_ref, gr_ref, gab_ref, ret_ref, g1_ref, sh2_ref, sc2_ref, n2_ref,
                   wr_ref, wt_ref, wo_ref, rwh_ref, rwl_ref, rb_ref,
                   x1_ref, hl_ref, eid_ref, gate_ref, cnt_ref):
    D = x_ref.shape[1]
    tm = x_ref.shape[0]
    dt = eid_ref.shape[2]
    rnn = hf_ref[...].astype(F32) + hb_ref[...].astype(F32)
    y_rnn = jnp.dot((rnn * jax.nn.gelu(gr_ref[...].astype(F32))).astype(BF16), wr_ref[...],
                    preferred_element_type=F32)
    y_ret = jnp.dot(ret_ref[...], wt_ref[...], preferred_element_type=F32)
    ga = gab_ref[:, :D].astype(F32)
    gb = gab_ref[:, D:].astype(F32)
    merged = _sigmoid(ga) * y_rnn + _sigmoid(gb) * y_ret
    y = jnp.dot(merged.astype(BF16), wo_ref[...], preferred_element_type=F32)
    x1 = x_ref[...] + g1_ref[...] * y
    x1_ref[...] = x1
    hl = _rmsnorm(x1, n2_ref[...]) * (1.0 + sc2_ref[...]) + sh2_ref[...]
    hh = hl.astype(BF16)
    hl_ref[...] = hh

    hlo = (hl - hh.astype(F32)).astype(BF16)
    logits = (_dot_t(rwh_ref[...], hh) + _dot_t(rwh_ref[...], hlo) + _dot_t(rwl_ref[...], hh)) + rb_ref[...]

    ne = logits.shape[0]
    sub = lax.broadcasted_iota(I32, (ne, tm), 0)
    work = logits
    vals, idxs = [], []
    oh = jnp.zeros((ne, tm), F32)
    for _ in range(TOP_K):
        m = jnp.max(work, axis=0, keepdims=True)
        idx = jnp.min(jnp.where(work == m, sub, ne), axis=0, keepdims=True)
        hot = sub == idx
        vals.append(m)
        idxs.append(idx)
        oh = oh + jnp.where(hot, 1.0, 0.0)
        work = jnp.where(hot, -jnp.inf, work)
    es = [jnp.exp(v - vals[0]) for v in vals]
    inv = 1.0 / (es[0] + es[1] + es[2] + es[3])
    for part in range(tm // dt):
        ls = slice(part * dt, (part + 1) * dt)
        for k in range(TOP_K):
            eid_ref[part, k:k + 1, :] = idxs[k][:, ls]
            gate_ref[part, k:k + 1, :] = (es[k] * inv)[:, ls]
        cnt_ref[part] = jnp.sum(oh[:, ls], axis=1, keepdims=True)


def _finish(x3, hf, hb, z, retg, mods3, norm2_g, w_rnn_b, w_ret_b, w_out_b, rwt_hi, rwt_lo, router_b, dt):
    B, S, D = x3.shape
    tm = min(FIN_TM, S)
    nt = S // tm
    N = B * S
    ne = rwt_hi.shape[0]
    half = N_CHUNKS // 2
    per = tm // dt
    const2 = lambda b, i: (0, 0)
    tile3 = lambda b, i: (b * nt + i, 0, 0)
    return pl.pallas_call(
        _finish_kernel,
        grid=(B, nt),
        in_specs=[pl.BlockSpec((None, tm, D), lambda b, i: (b, i, 0)),
                  pl.BlockSpec((None, tm, D), lambda b, i: (b, i, 0)),
                  pl.BlockSpec((None, tm, D), lambda b, i: (b, i, 0)),
                  pl.BlockSpec((tm, D), lambda b, i: (i, b * N_CHUNKS + 1)),
                  pl.BlockSpec((tm, 2 * D), lambda b, i: (i, b * half + 4)),
                  pl.BlockSpec((None, tm, retg.shape[2]), lambda b, i: (b, i, 0)),
                  pl.BlockSpec((None, 1, D), lambda b, i: (b, 0, 2)),
                  pl.BlockSpec((None, 1, D), lambda b, i: (b, 0, 3)),
                  pl.BlockSpec((None, 1, D), lambda b, i: (b, 0, 4)),
                  pl.BlockSpec((1, D), const2),
                  pl.BlockSpec(w_rnn_b.shape, const2),
                  pl.BlockSpec(w_ret_b.shape, const2),
                  pl.BlockSpec(w_out_b.shape, const2),
                  pl.BlockSpec(rwt_hi.shape, const2),
                  pl.BlockSpec(rwt_lo.shape, const2),
                  pl.BlockSpec((ne, 1), const2)],
        out_specs=[pl.BlockSpec((None, tm, D), lambda b, i: (b, i, 0)),
                   pl.BlockSpec((tm, D), lambda b, i: (b * nt + i, 0)),
                   pl.BlockSpec((per, TOP_K, dt), tile3),
                   pl.BlockSpec((per, TOP_K, dt), tile3),
                   pl.BlockSpec((per, ne, 1), tile3)],
        out_shape=[jax.ShapeDtypeStruct((B, S, D), F32),
                   jax.ShapeDtypeStruct((N, D), BF16),
                   jax.ShapeDtypeStruct((N // dt, TOP_K, dt), I32),
                   jax.ShapeDtypeStruct((N // dt, TOP_K, dt), F32),
                   jax.ShapeDtypeStruct((N // dt, ne, 1), F32)],
        compiler_params=_cparams(2),
        name="finish",
    )(x3, hf, hb, z, z, retg, mods3, mods3, mods3, norm2_g.reshape(1, D),
      w_rnn_b, w_ret_b, w_out_b, rwt_hi, rwt_lo, router_b.reshape(ne, 1))


def _piece_sizes(max_rows):
    return tuple(2 ** p for p in range(max_rows.bit_length() - 1, ROW_ALIGN.bit_length() - 2, -1))


def _local_rows(dt):
    return TOP_K * dt + N_EXPERTS * ROW_ALIGN


def _local_slots(eid_ref):
    dt = eid_ref.shape[1]
    ne = N_EXPERTS
    sub = lax.broadcasted_iota(I32, (ne, dt), 0)
    hots = [sub == eid_ref[k:k + 1, :] for k in range(TOP_K)]
    oh = jnp.zeros((ne, dt), F32)
    for hot in hots:
        oh = oh + jnp.where(hot, 1.0, 0.0)
    earlier = jnp.where(lax.broadcasted_iota(I32, (dt, dt), 0) < lax.broadcasted_iota(I32, (dt, dt), 1), 1.0, 0.0)
    before = jnp.dot(oh.astype(BF16), earlier.astype(BF16), preferred_element_type=F32)
    cnt = jnp.broadcast_to(jnp.sum(oh, axis=1, keepdims=True), (ne, dt))
    cnt = jnp.ceil(cnt * (1.0 / ROW_ALIGN)) * ROW_ALIGN
    lower = jnp.where(lax.broadcasted_iota(I32, (ne, ne), 1) < lax.broadcasted_iota(I32, (ne, ne), 0), 1.0, 0.0)
    base = before + jnp.dot(lower.astype(BF16), cnt.astype(BF16), preferred_element_type=F32)
    return [jnp.sum(jnp.where(hot, base, 0.0), axis=0, keepdims=True).astype(I32) for hot in hots]


def _slot_matrix(slots, weights, out_ref):
    rows, dt = out_ref.shape
    ch = 64
    rel = lax.broadcasted_iota(I32, (ch, dt), 0).astype(F32).astype(BF16)
    slots_f = [s.astype(F32) for s in slots]
    weights_b = [jnp.asarray(w, F32).astype(BF16) for w in weights]
    zero = jnp.zeros((), BF16)
    for c in range(rows // ch):
        acc = None
        for s, w in zip(slots_f, weights_b):
            term = jnp.where(rel == (s - float(c * ch)).astype(BF16), w, zero)
            acc = term if acc is None else acc + term
        out_ref[c * ch:(c + 1) * ch, :] = acc


def _for_each_run_piece(meta_ref, max_rows, fn):
    def body(e, c):
        lo = pl.multiple_of(meta_ref[0, e], ROW_ALIGN)
        n = meta_ref[0, N_EXPERTS + e]
        gs = pl.multiple_of(meta_ref[0, 2 * N_EXPERTS + e], ROW_ALIGN)
        off = 0
        for p in _piece_sizes(max_rows):
            take = (n & p) != 0

            @pl.when(take)
            def _(off=off, p=p):
                fn(pl.multiple_of(lo + off, ROW_ALIGN), pl.multiple_of(gs + off, ROW_ALIGN), p)

            off = off + jnp.where(take, p, 0)
        return c

    lax.fori_loop(0, N_EXPERTS, body, 0)


def _dispatch_kernel(meta_ref, zmeta_ref, eid_ref, x_ref, hp_ref, sbuf_ref, zbuf_ref, pm_ref, sem, zsem):
    dt = x_ref.shape[0]

    def zero_copy(_, g, p):
        return pltpu.make_async_copy(zbuf_ref.at[pl.ds(0, p), :], hp_ref.at[pl.ds(g, p), :], zsem)

    @pl.when(pl.program_id(0) == 0)
    def _():
        zbuf_ref[...] = jnp.zeros_like(zbuf_ref)
        _for_each_run_piece(zmeta_ref, MOE_BLOCK - 1, lambda l, g, p: zero_copy(l, g, p).start())
        _for_each_run_piece(zmeta_ref, MOE_BLOCK - 1, lambda l, g, p: zero_copy(l, g, p).wait())

    _slot_matrix(_local_slots(eid_ref), [1.0] * TOP_K, pm_ref)
    sbuf_ref[...] = jnp.dot(pm_ref[...], x_ref[...], preferred_element_type=F32)

    def run_copy(l, g, p):
        return pltpu.make_async_copy(sbuf_ref.at[pl.ds(l, p), :], hp_ref.at[pl.ds(g, p), :], sem)

    _for_each_run_piece(meta_ref, dt, lambda l, g, p: run_copy(l, g, p).start())
    _for_each_run_piece(meta_ref, dt, lambda l, g, p: run_copy(l, g, p).wait())


def _dispatch(hl2, eid3, meta3, zmeta, P):
    N, D = hl2.shape
    nt, _, dt = eid3.shape
    nm = meta3.shape[2]
    return pl.pallas_call(
        _dispatch_kernel,
        grid=(nt,),
        in_specs=[pl.BlockSpec((None, 1, nm), lambda i: (i, 0, 0), memory_space=pltpu.SMEM),
                  pl.BlockSpec((1, nm), lambda i: (0, 0), memory_space=pltpu.SMEM),
                  pl.BlockSpec((None, TOP_K, dt), lambda i: (i, 0, 0)),
                  pl.BlockSpec((dt, D), lambda i: (i, 0))],
        out_specs=pl.BlockSpec(memory_space=pl.ANY),
        out_shape=jax.ShapeDtypeStruct((P, D), F32),
        scratch_shapes=[pltpu.VMEM((_local_rows(dt), D), F32),
                        pltpu.VMEM((_piece_sizes(MOE_BLOCK - 1)[0], D), F32),
                        pltpu.VMEM((_local_rows(dt), dt), BF16),
                        pltpu.SemaphoreType.DMA(()),
                        pltpu.SemaphoreType.DMA(())],
        compiler_params=_cparams(1),
        name="dispatch",
    )(meta3, zmeta, eid3, hl2)


def _expert_kernel(be_ref, nu_ref, x_ref, w1_ref, b1_ref, w2_ref, b2_ref, o_ref):
    del be_ref

    @pl.when(pl.program_id(0) < nu_ref[0])
    def _():
        de = w2_ref.shape[0]
        h = jnp.dot(x_ref[...].astype(BF16), w1_ref[...], preferred_element_type=F32) + b1_ref[...]
        glu = jnp.minimum(h[:, :de], SWIGLU_LIMIT)
        lin = jnp.clip(h[:, de:], -SWIGLU_LIMIT, SWIGLU_LIMIT)
        act = glu * _sigmoid(SWIGLU_ALPHA * glu) * (lin + 1.0)
        o_ref[...] = jnp.dot(act.astype(BF16), w2_ref[...], preferred_element_type=F32) + b2_ref[...]


def _experts(blk_expert, n_used, h_pad, w1b, b1p, w2b, b2):
    P, D = h_pad.shape
    ne, _, de2 = w1b.shape
    de = de2 // 2
    nblk = P // MOE_BLOCK
    blk = lambda j, be, nu: (jnp.minimum(j, nu[0] - 1), 0)
    wsel = lambda j, be, nu: (be[jnp.minimum(j, nu[0] - 1)], 0, 0)
    return pl.pallas_call(
        _expert_kernel,
        grid_spec=pltpu.PrefetchScalarGridSpec(
            num_scalar_prefetch=2,
            grid=(nblk,),
            in_specs=[pl.BlockSpec((MOE_BLOCK, D), blk),
                      pl.BlockSpec((None, D, de2), wsel),
                      pl.BlockSpec((None, 1, de2), wsel),
                      pl.BlockSpec((None, de, D), wsel),
                      pl.BlockSpec((None, 1, D), wsel)],
            out_specs=pl.BlockSpec((MOE_BLOCK, D), blk)),
        out_shape=jax.ShapeDtypeStruct((P, D), F32),
        compiler_params=_cparams(1),
        name="experts",
    )(blk_expert, n_used, h_pad, w1b, b1p.reshape(ne, 1, de2), w2b, b2.reshape(ne, 1, D))


def _combine_kernel(meta_ref, eid_ref, gate_ref, x1_ref, g2_ref, fg_ref, yp_ref, o_ref, ybuf_ref, gm_ref, sem):
    dt = x1_ref.shape[0]

    def run_copy(l, g, p):
        return pltpu.make_async_copy(yp_ref.at[pl.ds(g, p), :], ybuf_ref.at[pl.ds(l, p), :], sem)

    @pl.when(jnp.logical_and(pl.program_id(0) == 0, pl.program_id(1) == 0))
    def _():
        ybuf_ref[...] = jnp.zeros_like(ybuf_ref)

    _for_each_run_piece(meta_ref, dt, lambda l, g, p: run_copy(l, g, p).start())

    _slot_matrix(_local_slots(eid_ref), [gate_ref[k:k + 1, :] for k in range(TOP_K)], gm_ref)

    _for_each_run_piece(meta_ref, dt, lambda l, g, p: run_copy(l, g, p).wait())

    y = _tdot(gm_ref[...], ybuf_ref[...].astype(BF16))
    x2 = x1_ref[...] + g2_ref[...] * y
    o_ref[...] = _rmsnorm(x2, fg_ref[...])


def _combine(meta3, eid3, gate3, x1, mods3, final_g, y_pad):
    B, S, D = x1.shape
    _, _, dt = eid3.shape
    nt = S // dt
    nm = meta3.shape[2]
    tile3 = lambda b, i: (b * nt + i, 0, 0)
    return pl.pallas_call(
        _combine_kernel,
        grid=(B, nt),
        in_specs=[pl.BlockSpec((None, 1, nm), tile3, memory_space=pltpu.SMEM),
                  pl.BlockSpec((None, TOP_K, dt), tile3),
                  pl.BlockSpec((None, TOP_K, dt), tile3),
                  pl.BlockSpec((None, dt, D), lambda b, i: (b, i, 0)),
                  pl.BlockSpec((None, 1, D), lambda b, i: (b, 0, 5)),
                  pl.BlockSpec((1, D), lambda b, i: (0, 0)),
                  pl.BlockSpec(memory_space=pl.ANY)],
        out_specs=pl.BlockSpec((None, dt, D), lambda b, i: (b, i, 0)),
        out_shape=jax.ShapeDtypeStruct((B, S, D), F32),
        scratch_shapes=[pltpu.VMEM((_local_rows(dt), D), F32), pltpu.VMEM((_local_rows(dt), dt), BF16),
                        pltpu.SemaphoreType.DMA(())],
        compiler_params=_cparams(2),
        name="combine",
    )(meta3, eid3, gate3, x1, mods3, final_g.reshape(1, D), y_pad)


def _rope_tables(S, k_scale):
    n_freq = RET_DK // 4
    pos = jnp.arange(S, dtype=F32)
    rows = jnp.floor(pos / GRID_W)
    cols = pos - rows * GRID_W
    inv = ROPE_BASE ** (-jnp.arange(n_freq, dtype=F32) / n_freq)
    ang = jnp.concatenate([rows[:, None] * inv, cols[:, None] * inv], axis=-1)
    cos, sin = jnp.cos(ang), jnp.sin(ang)
    cos2 = jnp.concatenate([cos, cos], axis=-1)
    sin2 = jnp.concatenate([-sin, sin], axis=-1)
    return jnp.concatenate([cos2, sin2, cos2 * k_scale, sin2 * k_scale], axis=-1)


def _identity_tables(L, k_scale):
    one = jnp.ones((L, RET_DK), F32)
    zero = jnp.zeros((L, RET_DK), F32)
    return jnp.concatenate([one, zero, one * k_scale, zero], axis=-1)


def _lanes(t, width):
    return jnp.broadcast_to(t[:, :, None], t.shape + (width,))


def kernel(x, c, ctx, c_ctx, ada_w, ada_b, norm1_g, w_in, conv_w, conv_b, lru_wa, lru_ba, lru_wx, lru_bx,
           lru_lambda, w_rnn_proj, w_ret_proj, w_out, norm2_g, router_w, router_b, moe_w1, moe_b1, moe_w2,
           moe_b2, final_g):
    B, S, D = x.shape
    L = ctx.shape[1]
    N = B * S
    H = RET_HEADS
    lyr = 0
    d_in = w_in.shape[2]
    assert ada_w.shape[0] == 1 and d_in == N_CHUNKS * D and B == 8

    def pairs_apart(w):
        return jnp.swapaxes(w.reshape(D, H, RET_DK // 2, 2), 2, 3).reshape(D, H * RET_DK)

    w_in_l = w_in[lyr].astype(BF16)
    w_in_b = jnp.concatenate([w_in_l[:, :2 * D], pairs_apart(w_in_l[:, 2 * D:3 * D]),
                              pairs_apart(w_in_l[:, 3 * D:4 * D]), w_in_l[:, 4 * D:]], axis=1)
    wg = [(0.5 * jnp.concatenate([lru_wa[lyr, d], lru_wx[lyr, d]], axis=-1)).astype(BF16) for d in range(2)]
    de2 = moe_w1.shape[3]
    glu_lin = np.concatenate([np.arange(0, de2, 2), np.arange(1, de2, 2)])
    w1b = moe_w1[lyr][:, :, glu_lin].astype(BF16)
    b1p = moe_b1[lyr][:, glu_lin]
    w2b = moe_w2[lyr].astype(BF16)
    rwt = router_w[lyr].T
    rwt_hi = rwt.astype(BF16)
    rwt_lo = (rwt - rwt_hi.astype(F32)).astype(BF16)

    k_scale = RET_DK ** -0.5
    tab_l = _rope_tables(S, k_scale)
    tab_c = _identity_tables(L, k_scale)
    log_g = jnp.log1p(-jnp.exp2(-5.0 - jnp.arange(H, dtype=F32)))
    C = min(RET_C, S)
    idx = jnp.arange(C, dtype=F32)
    dec = lambda e: jnp.exp(e[None, :] * log_g[:, None])
    intra = jnp.exp(jnp.abs(idx[:, None] - idx[None, :])[None] * log_g[:, None, None])
    qdec_f = _lanes(dec(idx + 1.0), RET_DK)
    qdec_b = _lanes(dec(C - idx), RET_DK)
    kdec_f = _lanes(dec(C - 1.0 - idx), RET_DK)
    kdec_b = _lanes(dec(idx), RET_DK)
    cdec = jnp.broadcast_to(jnp.exp(C * log_g)[:, None], (H, RET_DV))
    pos_c = jnp.arange(L, dtype=F32)
    cdec_f = _lanes(dec(L - 1.0 - pos_c), RET_DK)
    cdec_b = _lanes(dec(pos_c), RET_DK)

    cvec = jnp.zeros((16, D), F32).at[:B].set(c).at[B].set(c_ctx)
    mods3 = _ada(cvec, ada_w[lyr], ada_b[lyr]).reshape(16, 1, 6 * D)

    z_c, xr_c = _proj(ctx, mods3, lambda b: B, norm1_g[lyr], w_in_b, tab_c, min(PROJ_TM, L))
    z_l, xr_l = _proj(x, mods3, lambda b: b, norm1_g[lyr], w_in_b, tab_l, min(PROJ_TM, S))

    zeros = jnp.zeros((B, D), F32)
    hs = []
    for d in range(2):
        args = (conv_w[lyr], conv_b[lyr], wg[d], 0.5 * lru_ba[lyr, d], 0.5 * lru_bx[lyr, d], lru_lambda[lyr, d])
        _, h0 = _rnn(xr_c, *args, zeros, reverse=(d == 1), tt=RNN_TT)
        h, _ = _rnn(xr_l, *args, h0, reverse=(d == 1), tt=RNN_TT)
        hs.append(h)

    s_fwd, s_bwd = _ret_ctx(z_c, B, cdec_f, cdec_b)
    sb = _ret_bwd(z_l, B, s_bwd, kdec_b, cdec)
    retg = _ret_fwd(z_l, B, sb, s_fwd, intra, qdec_f, qdec_b, kdec_f, cdec)

    dt = min(DISP_TM, S)
    x1, hl2, eid3, gate3, cnt3 = _finish(
        x, hs[0], hs[1], z_l, retg, mods3, norm2_g[lyr], w_rnn_proj[lyr].astype(BF16),
        w_ret_proj[lyr].astype(BF16), w_out[lyr].astype(BF16), rwt_hi, rwt_lo, router_b[lyr], dt)

    cnt_t = cnt3[:, :, 0].astype(I32)
    cnt_t = (cnt_t + ROW_ALIGN - 1) // ROW_ALIGN * ROW_ALIGN
    cnt = jnp.sum(cnt_t, axis=0)
    padded = (cnt + MOE_BLOCK - 1) // MOE_BLOCK * MOE_BLOCK
    pad_end = jnp.cumsum(padded)
    pad_start = pad_end - padded
    gstart = pad_start[None, :] + jnp.cumsum(cnt_t, axis=0) - cnt_t
    loff = jnp.cumsum(cnt_t, axis=1) - cnt_t
    meta3 = jnp.concatenate([loff, cnt_t, gstart], axis=1).reshape(N // dt, 1, 3 * N_EXPERTS)
    zmeta = jnp.concatenate([jnp.zeros_like(cnt), padded - cnt, pad_start + cnt]).reshape(1, 3 * N_EXPERTS)
    n_blocks = -(-(N * TOP_K + (N // dt) * N_EXPERTS * (ROW_ALIGN - 1)) // MOE_BLOCK) + N_EXPERTS
    P = n_blocks * MOE_BLOCK
    blk_start = jnp.arange(n_blocks, dtype=I32) * MOE_BLOCK
    blk_expert = jnp.minimum(jnp.sum((pad_end[None, :] <= blk_start[:, None]).astype(I32), axis=1), N_EXPERTS - 1)
    n_used = (pad_end[-1:] // MOE_BLOCK).astype(I32)

    h_pad = _dispatch(hl2, eid3, meta3, zmeta, P)
    y_pad = _experts(blk_expert, n_used, h_pad, w1b, b1p, w2b, moe_b2[lyr])
    return _combine(meta3, eid3, gate3, x1, mods3, final_g, y_pad)
```

```python
import functools

import jax
import jax.numpy as jnp
import numpy as np
from jax import lax
from jax.experimental import pallas as pl
from jax.experimental.pallas import tpu as pltpu

F32 = jnp.float32
BF16 = jnp.bfloat16
I32 = jnp.int32

GRID_W = 64
RNN_BLOCKS = 8
CONV_W = 4
LRU_C = 8.0
RET_HEADS = 8
RET_DK = 128
RET_DV = 256
ROPE_BASE = 10000.0
N_EXPERTS = 32
TOP_K = 4
SWIGLU_ALPHA = 1.702
SWIGLU_LIMIT = 7.0
EPS = 1e-6
N_CHUNKS = 10

PROJ_TM = 1024
PROJ_TN = 2048
RNN_TT = 128
RET_C = 256
FIN_TM = 512
MOE_BLOCK = 512
DISP_TM = 512
ROW_ALIGN = 8
VMEM_LIMIT = 56 * 1024 * 1024


def _cparams(n_axes):
    return pltpu.CompilerParams(dimension_semantics=("arbitrary",) * n_axes,
                                vmem_limit_bytes=VMEM_LIMIT)


def _sigmoid(x):
    return 0.5 * (jnp.tanh(0.5 * x) + 1.0)


def _rmsnorm(x, g):
    return x * lax.rsqrt(jnp.mean(x * x, axis=-1, keepdims=True) + EPS) * g


def _ada_kernel(c_ref, w_ref, b_ref, o_ref):
    c = c_ref[...]
    s = c * _sigmoid(c)
    o_ref[...] = jnp.dot(s, w_ref[...], preferred_element_type=F32,
                         precision=lax.Precision.HIGHEST) + b_ref[...]


def _ada(cvec, ada_w, ada_b):
    R, D = cvec.shape
    n = ada_w.shape[1] // D
    return pl.pallas_call(
        _ada_kernel,
        grid=(n,),
        in_specs=[pl.BlockSpec((R, D), lambda j: (0, 0)),
                  pl.BlockSpec((D, D), lambda j: (0, j)),
                  pl.BlockSpec((1, D), lambda j: (0, j))],
        out_specs=pl.BlockSpec((R, D), lambda j: (0, j)),
        out_shape=jax.ShapeDtypeStruct((R, n * D), F32),
        compiler_params=_cparams(1),
        name="ada",
    )(cvec, ada_w, ada_b.reshape(1, -1))


def _proj_kernel(x_ref, sh_ref, sc_ref, g_ref, w_ref, tab_ref, o_ref, xr_ref, h_ref):
    j = pl.program_id(2)
    D = x_ref.shape[1]

    @pl.when(j == 0)
    def _():
        h = _rmsnorm(x_ref[...], g_ref[...])
        h_ref[...] = (h * (1.0 + sc_ref[...]) + sh_ref[...]).astype(BF16)

    acc = jnp.dot(h_ref[...], w_ref[...], preferred_element_type=F32)

    @pl.when(j != 1)
    def _():
        o_ref[...] = acc.astype(o_ref.dtype)

    @pl.when(j == 0)
    def _():
        xr_ref[...] = acc[:, :D].astype(xr_ref.dtype)

    @pl.when(j == 1)
    def _():
        for part in range(2):
            cos = tab_ref[:, (2 * part) * RET_DK:(2 * part + 1) * RET_DK]
            sin = tab_ref[:, (2 * part + 1) * RET_DK:(2 * part + 2) * RET_DK]
            for hh in range(RET_HEADS):
                c0 = part * RET_HEADS * RET_DK + hh * RET_DK
                t = acc[:, c0:c0 + RET_DK]
                o_ref[:, c0:c0 + RET_DK] = (t * cos + pltpu.roll(t, RET_DK // 2, 1) * sin).astype(o_ref.dtype)


def _proj(x3, mods3, mod_row, norm_g, w_in_b, tab, tm):
    B, S, D = x3.shape
    d_in = w_in_b.shape[1]
    nj = d_in // PROJ_TN
    return pl.pallas_call(
        _proj_kernel,
        grid=(B, S // tm, nj),
        in_specs=[pl.BlockSpec((None, tm, D), lambda b, i, j: (b, i, 0)),
                  pl.BlockSpec((None, 1, D), lambda b, i, j: (mod_row(b), 0, 0)),
                  pl.BlockSpec((None, 1, D), lambda b, i, j: (mod_row(b), 0, 1)),
                  pl.BlockSpec((1, D), lambda b, i, j: (0, 0)),
                  pl.BlockSpec((D, PROJ_TN), lambda b, i, j: (0, j)),
                  pl.BlockSpec((tm, 4 * RET_DK), lambda b, i, j: (i, 0))],
        out_specs=[pl.BlockSpec((tm, PROJ_TN), lambda b, i, j: (i, b * nj + j)),
                   pl.BlockSpec((None, tm, D), lambda b, i, j: (b, i, 0))],
        out_shape=[jax.ShapeDtypeStruct((S, B * d_in), BF16),
                   jax.ShapeDtypeStruct((B, S, D), BF16)],
        scratch_shapes=[pltpu.VMEM((tm, D), BF16)],
        compiler_params=_cparams(3),
        name="proj",
    )(x3, mods3, mods3, norm_g.reshape(1, D), w_in_b, tab)


def _rnn_kernel(xm_ref, xp_ref, xn_ref, cw_ref, cb_ref, wg_ref, ba_ref, bx_ref, lam_ref, h0_ref,
                h_ref, hfin_ref, xs_ref, a_ref, u_ref, hc_ref, *, reverse, n_tiles):
    i = pl.program_id(0)
    tile = (n_tiles - 1 - i) if reverse else i
    nb, tt, D = xm_ref.shape
    bw = D // RNN_BLOCKS
    HALO = xp_ref.shape[1]

    @pl.when(i == 0)
    def _():
        hc_ref[...] = h0_ref[...]

    def time_major(v):
        return jnp.swapaxes(v.astype(F32), 0, 1).reshape(v.shape[1] * nb, D)

    hr = HALO * nb
    R = tt * nb
    xs_ref[0:hr, :] = jnp.where(tile > 0, time_major(xp_ref[...]), 0.0)
    xs_ref[hr:hr + R, :] = time_major(xm_ref[...])
    xs_ref[hr + R:2 * hr + R, :] = jnp.where(tile < n_tiles - 1, time_major(xn_ref[...]), 0.0)

    nl = -lam_ref[...]
    csp = (0.25 * LRU_C) * (jnp.maximum(nl, 0.0) + jnp.log1p(jnp.exp(-jnp.abs(nl))))
    SUB = 256

    def gates(s, carry):
        r0 = pl.multiple_of(s * SUB, SUB)
        base = hr - 2 * nb
        xc = cb_ref[...] + cw_ref[0:1, :] * xs_ref[pl.ds(pl.multiple_of(r0 + base, nb), SUB), :]
        for k in range(1, CONV_W):
            xc = xc + cw_ref[k:k + 1, :] * xs_ref[pl.ds(pl.multiple_of(r0 + base + k * nb, nb), SUB), :]
        xb = xc.astype(BF16)
        for n in range(RNN_BLOCKS):
            g = jnp.dot(xb[:, n * bw:(n + 1) * bw], wg_ref[n], preferred_element_type=F32)
            cs = slice(n * bw, (n + 1) * bw)
            tr = jnp.tanh(g[:, :bw] + ba_ref[:, cs])
            ti = jnp.tanh(g[:, bw:] + bx_ref[:, cs])
            t = jnp.tanh(csp[:, cs] * tr + csp[:, cs])
            rc2 = 2.0 / (1.0 + t)
            a_ref[pl.ds(r0, SUB), cs] = rc2 - 1.0
            u_ref[pl.ds(r0, SUB), cs] = (jnp.sqrt(t) * rc2) * ((0.5 * ti + 0.5) * xc[:, cs])
        return carry

    lax.fori_loop(0, R // SUB, gates, 0)

    def step(t, h):
        ts = (tt - 1 - t) if reverse else t
        r0 = pl.multiple_of(ts * nb, nb)
        h = a_ref[pl.ds(r0, nb), :] * h + u_ref[pl.ds(r0, nb), :]
        u_ref[pl.ds(r0, nb), :] = h
        return h

    h = lax.fori_loop(0, tt, step, hc_ref[...], unroll=8)
    hc_ref[...] = h
    hfin_ref[...] = h
    h_ref[...] = jnp.swapaxes(u_ref[...].reshape(tt, nb, D), 0, 1).astype(h_ref.dtype)


def _rnn(xr, conv_w, conv_b, wg, ba, bx, lam, h0, *, reverse, tt):
    nb, T, D = xr.shape
    tt = min(tt, T)
    n_tiles = T // tt
    HALO = 16
    hb = tt // HALO

    def tile_of(i):
        return (n_tiles - 1 - i) if reverse else i

    kern = functools.partial(_rnn_kernel, reverse=reverse, n_tiles=n_tiles)
    return pl.pallas_call(
        kern,
        grid=(n_tiles,),
        in_specs=[pl.BlockSpec((nb, tt, D), lambda i: (0, tile_of(i), 0)),
                  pl.BlockSpec((nb, HALO, D), lambda i: (0, jnp.maximum(tile_of(i) * hb - 1, 0), 0)),
                  pl.BlockSpec((nb, HALO, D), lambda i: (0, jnp.minimum((tile_of(i) + 1) * hb, n_tiles * hb - 1), 0)),
                  pl.BlockSpec((CONV_W, D), lambda i: (0, 0)),
                  pl.BlockSpec((1, D), lambda i: (0, 0)),
                  pl.BlockSpec((RNN_BLOCKS, D // RNN_BLOCKS, 2 * D // RNN_BLOCKS), lambda i: (0, 0, 0)),
                  pl.BlockSpec((1, D), lambda i: (0, 0)),
                  pl.BlockSpec((1, D), lambda i: (0, 0)),
                  pl.BlockSpec((1, D), lambda i: (0, 0)),
                  pl.BlockSpec((nb, D), lambda i: (0, 0))],
        out_specs=[pl.BlockSpec((nb, tt, D), lambda i: (0, tile_of(i), 0)),
                   pl.BlockSpec((nb, D), lambda i: (0, 0))],
        out_shape=[jax.ShapeDtypeStruct((nb, T, D), BF16),
                   jax.ShapeDtypeStruct((nb, D), F32)],
        scratch_shapes=[pltpu.VMEM(((tt + 2 * HALO) * nb, D), F32),
                        pltpu.VMEM((nb * tt, D), F32),
                        pltpu.VMEM((nb * tt, D), F32),
                        pltpu.VMEM((nb, D), F32)],
        compiler_params=_cparams(1),
        name="rnn_bwd" if reverse else "rnn_fwd",
    )(xr, xr, xr, conv_w, conv_b.reshape(1, D), wg, ba.reshape(1, D), bx.reshape(1, D),
      lam.reshape(1, D), h0)


def _tdot(a, b):
    return lax.dot_general(a, b, (((0,), (0,)), ((), ())), preferred_element_type=F32)


def _ret_ctx_kernel(k_ref, v_ref, df_ref, db_ref, sf_ref, sb_ref):
    for hh in range(RET_HEADS):
        kh = k_ref[:, hh * RET_DK:(hh + 1) * RET_DK].astype(F32)
        vh = v_ref[:, hh * RET_DV:(hh + 1) * RET_DV]
        sf_ref[hh] = _tdot((kh * df_ref[hh]).astype(BF16), vh)
        sb_ref[hh] = _tdot((kh * db_ref[hh]).astype(BF16), vh)


def _ret_ctx(zc, B, dec_f, dec_b):
    L = zc.shape[0]
    H = RET_HEADS
    st = jax.ShapeDtypeStruct((B, H, RET_DK, RET_DV), F32)
    return pl.pallas_call(
        _ret_ctx_kernel,
        grid=(B,),
        in_specs=[pl.BlockSpec((L, H * RET_DK), lambda b: (0, b * N_CHUNKS + 3)),
                  pl.BlockSpec((L, H * RET_DV), lambda b: (0, b * (N_CHUNKS // 2) + 2)),
                  pl.BlockSpec((H, L, RET_DK), lambda b: (0, 0, 0)),
                  pl.BlockSpec((H, L, RET_DK), lambda b: (0, 0, 0))],
        out_specs=[pl.BlockSpec((None, H, RET_DK, RET_DV), lambda b: (b, 0, 0, 0))] * 2,
        out_shape=[st, st],
        compiler_params=_cparams(1),
        name="ret_ctx",
    )(zc, zc, dec_f, dec_b)


def _ret_bwd_kernel(k_ref, v_ref, s0_ref, kd_ref, cd_ref, o_ref, st_ref):
    @pl.when(pl.program_id(1) == 0)
    def _():
        st_ref[...] = s0_ref[...]

    for hh in range(RET_HEADS):
        o_ref[hh] = st_ref[hh].astype(o_ref.dtype)
        kh = k_ref[:, hh * RET_DK:(hh + 1) * RET_DK].astype(F32)
        vh = v_ref[:, hh * RET_DV:(hh + 1) * RET_DV]
        st_ref[hh] = st_ref[hh] * cd_ref[hh:hh + 1, :] + _tdot((kh * kd_ref[hh]).astype(BF16), vh)


def _ret_bwd(z, B, s_bwd, kdec_b, cdec):
    S = z.shape[0]
    H, C = RET_HEADS, min(RET_C, S)
    n = S // C
    return pl.pallas_call(
        _ret_bwd_kernel,
        grid=(B, n),
        in_specs=[pl.BlockSpec((C, H * RET_DK), lambda b, j: (n - 1 - j, b * N_CHUNKS + 3)),
                  pl.BlockSpec((C, H * RET_DV), lambda b, j: (n - 1 - j, b * (N_CHUNKS // 2) + 2)),
                  pl.BlockSpec((None, H, RET_DK, RET_DV), lambda b, j: (b, 0, 0, 0)),
                  pl.BlockSpec((H, C, RET_DK), lambda b, j: (0, 0, 0)),
                  pl.BlockSpec((H, RET_DV), lambda b, j: (0, 0))],
        out_specs=pl.BlockSpec((None, None, H, RET_DK, RET_DV), lambda b, j: (b, n - 1 - j, 0, 0, 0)),
        out_shape=jax.ShapeDtypeStruct((B, n, H, RET_DK, RET_DV), BF16),
        scratch_shapes=[pltpu.VMEM((H, RET_DK, RET_DV), F32)],
        compiler_params=_cparams(2),
        name="ret_bwd",
    )(z, z, s_bwd, kdec_b, cdec)


def _ret_fwd_kernel(q_ref, k_ref, v_ref, gs_ref, sb_ref, s0_ref, intra_ref, qf_ref, qb_ref, kf_ref, cd_ref,
                    o_ref, st_ref):
    @pl.when(pl.program_id(1) == 0)
    def _():
        st_ref[...] = s0_ref[...]

    for hh in range(RET_HEADS):
        qh = q_ref[:, hh * RET_DK:(hh + 1) * RET_DK]
        kh = k_ref[:, hh * RET_DK:(hh + 1) * RET_DK]
        vh = v_ref[:, hh * RET_DV:(hh + 1) * RET_DV]
        qf32 = qh.astype(F32)
        s = lax.dot_general(qh, kh, (((1,), (1,)), ((), ())), preferred_element_type=F32) * intra_ref[hh]
        o = jnp.dot(s.astype(BF16), vh, preferred_element_type=F32)
        o = o + jnp.dot((qf32 * qf_ref[hh]).astype(BF16), st_ref[hh].astype(BF16), preferred_element_type=F32)
        o = o + jnp.dot((qf32 * qb_ref[hh]).astype(BF16), sb_ref[hh], preferred_element_type=F32)
        st_ref[hh] = st_ref[hh] * cd_ref[hh:hh + 1, :] + _tdot((kh.astype(F32) * kf_ref[hh]).astype(BF16), vh)
        mu = jnp.mean(o, axis=-1, keepdims=True)
        d = o - mu
        var = jnp.mean(d * d, axis=-1, keepdims=True)
        g = gs_ref[:, hh * RET_DV:(hh + 1) * RET_DV].astype(F32)
        o_ref[:, hh * RET_DV:(hh + 1) * RET_DV] = (g * _sigmoid(g) * d * lax.rsqrt(var + EPS)).astype(o_ref.dtype)


def _ret_fwd(z, B, sb, s_fwd, intra, qdec_f, qdec_b, kdec_f, cdec):
    S = z.shape[0]
    H, C = RET_HEADS, min(RET_C, S)
    n = S // C
    half = N_CHUNKS // 2
    return pl.pallas_call(
        _ret_fwd_kernel,
        grid=(B, n),
        in_specs=[pl.BlockSpec((C, H * RET_DK), lambda b, j: (j, b * N_CHUNKS + 2)),
                  pl.BlockSpec((C, H * RET_DK), lambda b, j: (j, b * N_CHUNKS + 3)),
                  pl.BlockSpec((C, H * RET_DV), lambda b, j: (j, b * half + 2)),
                  pl.BlockSpec((C, H * RET_DV), lambda b, j: (j, b * half + 3)),
                  pl.BlockSpec((None, None, H, RET_DK, RET_DV), lambda b, j: (b, j, 0, 0, 0)),
                  pl.BlockSpec((None, H, RET_DK, RET_DV), lambda b, j: (b, 0, 0, 0)),
                  pl.BlockSpec((H, C, C), lambda b, j: (0, 0, 0)),
                  pl.BlockSpec((H, C, RET_DK), lambda b, j: (0, 0, 0)),
                  pl.BlockSpec((H, C, RET_DK), lambda b, j: (0, 0, 0)),
                  pl.BlockSpec((H, C, RET_DK), lambda b, j: (0, 0, 0)),
                  pl.BlockSpec((H, RET_DV), lambda b, j: (0, 0))],
        out_specs=pl.BlockSpec((None, C, H * RET_DV), lambda b, j: (b, j, 0)),
        out_shape=jax.ShapeDtypeStruct((B, S, H * RET_DV), BF16),
        scratch_shapes=[pltpu.VMEM((H, RET_DK, RET_DV), F32)],
        compiler_params=_cparams(2),
        name="ret_fwd",
    )(z, z, z, z, sb, s_fwd, intra, qdec_f, qdec_b, kdec_f, cdec)


def _dot_t(a, b):
    return lax.dot_general(a, b, (((1,), (1,)), ((), ())), preferred_element_type=F32)


def _finish_kernel(x_ref, hf_ref, hb_ref, gr_ref, gab_ref, ret_ref, g1_ref, sh2_ref, sc2_ref, n2_ref,
                   wr_ref, wt_ref, wo_ref, rwh_ref, rwl_ref, rb_ref,
                   x1_ref, hl_ref, eid_ref, gate_ref, cnt_ref):
    D = x_ref.shape[1]
    tm = x_ref.shape[0]
    dt = eid_ref.shape[2]
    rnn = hf_ref[...].astype(F32) + hb_ref[...].astype(F32)
    y_rnn = jnp.dot((rnn * jax.nn.gelu(gr_ref[...].astype(F32))).astype(BF16), wr_ref[...],
                    preferred_element_type=F32)
    y_ret = jnp.dot(ret_ref[...], wt_ref[...], preferred_element_type=F32)
    ga = gab_ref[:, :D].astype(F32)
    gb = gab_ref[:, D:].astype(F32)
    merged = _sigmoid(ga) * y_rnn + _sigmoid(gb) * y_ret
    y = jnp.dot(merged.astype(BF16), wo_ref[...], preferred_element_type=F32)
    x1 = x_ref[...] + g1_ref[...] * y
    x1_ref[...] = x1
    hl = _rmsnorm(x1, n2_ref[...]) * (1.0 + sc2_ref[...]) + sh2_ref[...]
    hh = hl.astype(BF16)
    hl_ref[...] = hh

    hlo = (hl - hh.astype(F32)).astype(BF16)
    logits = (_dot_t(rwh_ref[...], hh) + _dot_t(rwh_ref[...], hlo) + _dot_t(rwl_ref[...], hh)) + rb_ref[...]

    ne = logits.shape[0]
    sub = lax.broadcasted_iota(I32, (ne, tm), 0)
    work = logits
    vals, idxs = [], []
    oh = jnp.zeros((ne, tm), F32)
    for _ in range(TOP_K):
        m = jnp.max(work, axis=0, keepdims=True)
        idx = jnp.min(jnp.where(work == m, sub, ne), axis=0, keepdims=True)
        hot = sub == idx
        vals.append(m)
        idxs.append(idx)
        oh = oh + jnp.where(hot, 1.0, 0.0)
        work = jnp.where(hot, -jnp.inf, work)
    es = [jnp.exp(v - vals[0]) for v in vals]
    inv = 1.0 / (es[0] + es[1] + es[2] + es[3])
    for part in range(tm // dt):
        ls = slice(part * dt, (part + 1) * dt)
        for k in range(TOP_K):
            eid_ref[part, k:k + 1, :] = idxs[k][:, ls]
            gate_ref[part, k:k + 1, :] = (es[k] * inv)[:, ls]
        cnt_ref[part] = jnp.sum(oh[:, ls], axis=1, keepdims=True)


def _finish(x3, hf, hb, z, retg, mods3, norm2_g, w_rnn_b, w_ret_b, w_out_b, rwt_hi, rwt_lo, router_b, dt):
    B, S, D = x3.shape
    tm = min(FIN_TM, S)
    nt = S // tm
    N = B * S
    ne = rwt_hi.shape[0]
    half = N_CHUNKS // 2
    per = tm // dt
    const2 = lambda b, i: (0, 0)
    tile3 = lambda b, i: (b * nt + i, 0, 0)
    return pl.pallas_call(
        _finish_kernel,
        grid=(B, nt),
        in_specs=[pl.BlockSpec((None, tm, D), lambda b, i: (b, i, 0)),
                  pl.BlockSpec((None, tm, D), lambda b, i: (b, i, 0)),
                  pl.BlockSpec((None, tm, D), lambda b, i: (b, i, 0)),
                  pl.BlockSpec((tm, D), lambda b, i: (i, b * N_CHUNKS + 1)),
                  pl.BlockSpec((tm, 2 * D), lambda b, i: (i, b * half + 4)),
                  pl.BlockSpec((None, tm, retg.shape[2]), lambda b, i: (b, i, 0)),
                  pl.BlockSpec((None, 1, D), lambda b, i: (b, 0, 2)),
                  pl.BlockSpec((None, 1, D), lambda b, i: (b, 0, 3)),
                  pl.BlockSpec((None, 1, D), lambda b, i: (b, 0, 4)),
                  pl.BlockSpec((1, D), const2),
                  pl.BlockSpec(w_rnn_b.shape, const2),
                  pl.BlockSpec(w_ret_b.shape, const2),
                  pl.BlockSpec(w_out_b.shape, const2),
                  pl.BlockSpec(rwt_hi.shape, const2),
                  pl.BlockSpec(rwt_lo.shape, const2),
                  pl.BlockSpec((ne, 1), const2)],
        out_specs=[pl.BlockSpec((None, tm, D), lambda b, i: (b, i, 0)),
                   pl.BlockSpec((tm, D), lambda b, i: (b * nt + i, 0)),
                   pl.BlockSpec((per, TOP_K, dt), tile3),
                   pl.BlockSpec((per, TOP_K, dt), tile3),
                   pl.BlockSpec((per, ne, 1), tile3)],
        out_shape=[jax.ShapeDtypeStruct((B, S, D), F32),
                   jax.ShapeDtypeStruct((N, D), BF16),
                   jax.ShapeDtypeStruct((N // dt, TOP_K, dt), I32),
                   jax.ShapeDtypeStruct((N // dt, TOP_K, dt), F32),
                   jax.ShapeDtypeStruct((N // dt, ne, 1), F32)],
        compiler_params=_cparams(2),
        name="finish",
    )(x3, hf, hb, z, z, retg, mods3, mods3, mods3, norm2_g.reshape(1, D),
      w_rnn_b, w_ret_b, w_out_b, rwt_hi, rwt_lo, router_b.reshape(ne, 1))


def _local_rows(dt):
    return TOP_K * dt + N_EXPERTS * ROW_ALIGN


def _local_slots(eid_ref):
    dt = eid_ref.shape[1]
    ne = N_EXPERTS
    sub = lax.broadcasted_iota(I32, (ne, dt), 0)
    hots = [sub == eid_ref[k:k + 1, :] for k in range(TOP_K)]
    oh = jnp.zeros((ne, dt), F32)
    for hot in hots:
        oh = oh + jnp.where(hot, 1.0, 0.0)
    earlier = jnp.where(lax.broadcasted_iota(I32, (dt, dt), 0) < lax.broadcasted_iota(I32, (dt, dt), 1), 1.0, 0.0)
    before = jnp.dot(oh.astype(BF16), earlier.astype(BF16), preferred_element_type=F32)
    cnt = jnp.broadcast_to(jnp.sum(oh, axis=1, keepdims=True), (ne, dt))
    cnt = jnp.ceil(cnt * (1.0 / ROW_ALIGN)) * ROW_ALIGN
    lower = jnp.where(lax.broadcasted_iota(I32, (ne, ne), 1) < lax.broadcasted_iota(I32, (ne, ne), 0), 1.0, 0.0)
    base = before + jnp.dot(lower.astype(BF16), cnt.astype(BF16), preferred_element_type=F32)
    return [jnp.sum(jnp.where(hot, base, 0.0), axis=0, keepdims=True).astype(I32) for hot in hots]


def _slot_matrix(slots, weights, out_ref):
    rows, dt = out_ref.shape
    ch = 64
    rel = lax.broadcasted_iota(I32, (ch, dt), 0).astype(F32).astype(BF16)
    slots_f = [s.astype(F32) for s in slots]
    weights_b = [jnp.asarray(w, F32).astype(BF16) for w in weights]
    zero = jnp.zeros((), BF16)
    for c in range(rows // ch):
        acc = None
        for s, w in zip(slots_f, weights_b):
            term = jnp.where(rel == (s - float(c * ch)).astype(BF16), w, zero)
            acc = term if acc is None else acc + term
        out_ref[c * ch:(c + 1) * ch, :] = acc


def _for_each_run(meta_ref, fn):
    def body(e, c):
        n = pl.multiple_of(meta_ref[0, N_EXPERTS + e], ROW_ALIGN)

        @pl.when(n > 0)
        def _():
            fn(pl.multiple_of(meta_ref[0, e], ROW_ALIGN), pl.multiple_of(meta_ref[0, 2 * N_EXPERTS + e], ROW_ALIGN), n)

        return c

    lax.fori_loop(0, N_EXPERTS, body, 0)


def _run_rows(meta_ref):
    last = N_EXPERTS - 1
    return pl.multiple_of(meta_ref[0, last] + meta_ref[0, N_EXPERTS + last], ROW_ALIGN)


def _dispatch_kernel(meta_ref, zmeta_ref, eid_ref, x_ref, hp_ref, sbuf_ref, zbuf_ref, pm_ref, rows_ref, sems, zsem):
    t = pl.program_id(0)
    slot = t % 2

    def zero_copy(_, g, n):
        return pltpu.make_async_copy(zbuf_ref.at[pl.ds(0, n), :], hp_ref.at[pl.ds(g, n), :], zsem)

    def run_copy(sl, l, g, n):
        return pltpu.make_async_copy(sbuf_ref.at[sl, pl.ds(l, n), :], hp_ref.at[pl.ds(g, n), :], sems.at[sl])

    @pl.when(t == 0)
    def _():
        zbuf_ref[...] = jnp.zeros_like(zbuf_ref)
        _for_each_run(zmeta_ref, lambda l, g, n: zero_copy(l, g, n).start())
        _for_each_run(zmeta_ref, lambda l, g, n: zero_copy(l, g, n).wait())

    _slot_matrix(_local_slots(eid_ref), [1.0] * TOP_K, pm_ref)
    sbuf_ref[slot] = jnp.dot(pm_ref[...], x_ref[...], preferred_element_type=F32)

    @pl.when(t > 0)
    def _():
        run_copy(1 - slot, 0, 0, pl.multiple_of(rows_ref[1 - slot], ROW_ALIGN)).wait()

    _for_each_run(meta_ref, lambda l, g, n: run_copy(slot, l, g, n).start())
    rows_ref[slot] = _run_rows(meta_ref)

    @pl.when(t == pl.num_programs(0) - 1)
    def _():
        run_copy(slot, 0, 0, _run_rows(meta_ref)).wait()


def _dispatch(hl2, eid3, meta3, zmeta, P):
    N, D = hl2.shape
    nt, _, dt = eid3.shape
    nm = meta3.shape[2]
    return pl.pallas_call(
        _dispatch_kernel,
        grid=(nt,),
        in_specs=[pl.BlockSpec((None, 1, nm), lambda i: (i, 0, 0), memory_space=pltpu.SMEM),
                  pl.BlockSpec((1, nm), lambda i: (0, 0), memory_space=pltpu.SMEM),
                  pl.BlockSpec((None, TOP_K, dt), lambda i: (i, 0, 0)),
                  pl.BlockSpec((dt, D), lambda i: (i, 0))],
        out_specs=pl.BlockSpec(memory_space=pl.ANY),
        out_shape=jax.ShapeDtypeStruct((P, D), F32),
        scratch_shapes=[pltpu.VMEM((2, _local_rows(dt), D), F32),
                        pltpu.VMEM((MOE_BLOCK, D), F32),
                        pltpu.VMEM((_local_rows(dt), dt), BF16),
                        pltpu.SMEM((2,), I32),
                        pltpu.SemaphoreType.DMA((2,)),
                        pltpu.SemaphoreType.DMA(())],
        compiler_params=_cparams(1),
        name="dispatch",
    )(meta3, zmeta, eid3, hl2)


def _expert_kernel(be_ref, nu_ref, x_ref, w1_ref, b1_ref, w2_ref, b2_ref, o_ref):
    del be_ref

    @pl.when(pl.program_id(0) < nu_ref[0])
    def _():
        de = w2_ref.shape[0]
        h = jnp.dot(x_ref[...].astype(BF16), w1_ref[...], preferred_element_type=F32) + b1_ref[...]
        glu = jnp.minimum(h[:, :de], SWIGLU_LIMIT)
        lin = jnp.clip(h[:, de:], -SWIGLU_LIMIT, SWIGLU_LIMIT)
        act = glu * _sigmoid(SWIGLU_ALPHA * glu) * (lin + 1.0)
        o_ref[...] = jnp.dot(act.astype(BF16), w2_ref[...], preferred_element_type=F32) + b2_ref[...]


def _experts(blk_expert, n_used, h_pad, w1b, b1p, w2b, b2):
    P, D = h_pad.shape
    ne, _, de2 = w1b.shape
    de = de2 // 2
    nblk = P // MOE_BLOCK
    blk = lambda j, be, nu: (jnp.minimum(j, nu[0] - 1), 0)
    wsel = lambda j, be, nu: (be[jnp.minimum(j, nu[0] - 1)], 0, 0)
    return pl.pallas_call(
        _expert_kernel,
        grid_spec=pltpu.PrefetchScalarGridSpec(
            num_scalar_prefetch=2,
            grid=(nblk,),
            in_specs=[pl.BlockSpec((MOE_BLOCK, D), blk),
                      pl.BlockSpec((None, D, de2), wsel),
                      pl.BlockSpec((None, 1, de2), wsel),
                      pl.BlockSpec((None, de, D), wsel),
                      pl.BlockSpec((None, 1, D), wsel)],
            out_specs=pl.BlockSpec((MOE_BLOCK, D), blk)),
        out_shape=jax.ShapeDtypeStruct((P, D), F32),
        compiler_params=_cparams(1),
        name="experts",
    )(blk_expert, n_used, h_pad, w1b, b1p.reshape(ne, 1, de2), w2b, b2.reshape(ne, 1, D))


def _combine_kernel(meta_ref, nmeta_ref, eid_ref, gate_ref, x1_ref, g2_ref, fg_ref, yp_ref, o_ref,
                    ybuf_ref, gm_ref, sems):
    t = pl.program_id(0) * pl.num_programs(1) + pl.program_id(1)
    n_tiles = pl.num_programs(0) * pl.num_programs(1)
    slot = t % 2

    def run_copy(sl, l, g, n):
        return pltpu.make_async_copy(yp_ref.at[pl.ds(g, n), :], ybuf_ref.at[sl, pl.ds(l, n), :], sems.at[sl])

    @pl.when(t == 0)
    def _():
        ybuf_ref[...] = jnp.zeros_like(ybuf_ref)
        _for_each_run(meta_ref, lambda l, g, n: run_copy(0, l, g, n).start())

    @pl.when(t + 1 < n_tiles)
    def _():
        _for_each_run(nmeta_ref, lambda l, g, n: run_copy(1 - slot, l, g, n).start())

    run_copy(slot, 0, 0, _run_rows(meta_ref)).wait()

    _slot_matrix(_local_slots(eid_ref), [gate_ref[k:k + 1, :] for k in range(TOP_K)], gm_ref)
    y = _tdot(gm_ref[...], ybuf_ref[slot].astype(BF16))
    x2 = x1_ref[...] + g2_ref[...] * y
    o_ref[...] = _rmsnorm(x2, fg_ref[...])


def _combine(meta3, eid3, gate3, x1, mods3, final_g, y_pad):
    B, S, D = x1.shape
    n_tiles, _, dt = eid3.shape
    nt = S // dt
    nm = meta3.shape[2]
    tile3 = lambda b, i: (b * nt + i, 0, 0)
    next3 = lambda b, i: (jnp.minimum(b * nt + i + 1, n_tiles - 1), 0, 0)
    return pl.pallas_call(
        _combine_kernel,
        grid=(B, nt),
        in_specs=[pl.BlockSpec((None, 1, nm), tile3, memory_space=pltpu.SMEM),
                  pl.BlockSpec((None, 1, nm), next3, memory_space=pltpu.SMEM),
                  pl.BlockSpec((None, TOP_K, dt), tile3),
                  pl.BlockSpec((None, TOP_K, dt), tile3),
                  pl.BlockSpec((None, dt, D), lambda b, i: (b, i, 0)),
                  pl.BlockSpec((None, 1, D), lambda b, i: (b, 0, 5)),
                  pl.BlockSpec((1, D), lambda b, i: (0, 0)),
                  pl.BlockSpec(memory_space=pl.ANY)],
        out_specs=pl.BlockSpec((None, dt, D), lambda b, i: (b, i, 0)),
        out_shape=jax.ShapeDtypeStruct((B, S, D), F32),
        scratch_shapes=[pltpu.VMEM((2, _local_rows(dt), D), F32), pltpu.VMEM((_local_rows(dt), dt), BF16),
                        pltpu.SemaphoreType.DMA((2,))],
        compiler_params=_cparams(2),
        name="combine",
    )(meta3, meta3, eid3, gate3, x1, mods3, final_g.reshape(1, D), y_pad)


def _rope_tables(S, k_scale):
    n_freq = RET_DK // 4
    pos = jnp.arange(S, dtype=F32)
    rows = jnp.floor(pos / GRID_W)
    cols = pos - rows * GRID_W
    inv = ROPE_BASE ** (-jnp.arange(n_freq, dtype=F32) / n_freq)
    ang = jnp.concatenate([rows[:, None] * inv, cols[:, None] * inv], axis=-1)
    cos, sin = jnp.cos(ang), jnp.sin(ang)
    cos2 = jnp.concatenate([cos, cos], axis=-1)
    sin2 = jnp.concatenate([-sin, sin], axis=-1)
    return jnp.concatenate([cos2, sin2, cos2 * k_scale, sin2 * k_scale], axis=-1)


def _identity_tables(L, k_scale):
    one = jnp.ones((L, RET_DK), F32)
    zero = jnp.zeros((L, RET_DK), F32)
    return jnp.concatenate([one, zero, one * k_scale, zero], axis=-1)


def _lanes(t, width):
    return jnp.broadcast_to(t[:, :, None], t.shape + (width,))


def kernel(x, c, ctx, c_ctx, ada_w, ada_b, norm1_g, w_in, conv_w, conv_b, lru_wa, lru_ba, lru_wx, lru_bx,
           lru_lambda, w_rnn_proj, w_ret_proj, w_out, norm2_g, router_w, router_b, moe_w1, moe_b1, moe_w2,
           moe_b2, final_g):
    B, S, D = x.shape
    L = ctx.shape[1]
    N = B * S
    H = RET_HEADS
    lyr = 0
    d_in = w_in.shape[2]
    assert ada_w.shape[0] == 1 and d_in == N_CHUNKS * D and B == 8

    def pairs_apart(w):
        return jnp.swapaxes(w.reshape(D, H, RET_DK // 2, 2), 2, 3).reshape(D, H * RET_DK)

    w_in_l = w_in[lyr].astype(BF16)
    w_in_b = jnp.concatenate([w_in_l[:, :2 * D], pairs_apart(w_in_l[:, 2 * D:3 * D]),
                              pairs_apart(w_in_l[:, 3 * D:4 * D]), w_in_l[:, 4 * D:]], axis=1)
    wg = [(0.5 * jnp.concatenate([lru_wa[lyr, d], lru_wx[lyr, d]], axis=-1)).astype(BF16) for d in range(2)]
    de2 = moe_w1.shape[3]
    glu_lin = np.concatenate([np.arange(0, de2, 2), np.arange(1, de2, 2)])
    w1b = moe_w1[lyr][:, :, glu_lin].astype(BF16)
    b1p = moe_b1[lyr][:, glu_lin]
    w2b = moe_w2[lyr].astype(BF16)
    rwt = router_w[lyr].T
    rwt_hi = rwt.astype(BF16)
    rwt_lo = (rwt - rwt_hi.astype(F32)).astype(BF16)

    k_scale = RET_DK ** -0.5
    tab_l = _rope_tables(S, k_scale)
    tab_c = _identity_tables(L, k_scale)
    log_g = jnp.log1p(-jnp.exp2(-5.0 - jnp.arange(H, dtype=F32)))
    C = min(RET_C, S)
    idx = jnp.arange(C, dtype=F32)
    dec = lambda e: jnp.exp(e[None, :] * log_g[:, None])
    intra = jnp.exp(jnp.abs(idx[:, None] - idx[None, :])[None] * log_g[:, None, None])
    qdec_f = _lanes(dec(idx + 1.0), RET_DK)
    qdec_b = _lanes(dec(C - idx), RET_DK)
    kdec_f = _lanes(dec(C - 1.0 - idx), RET_DK)
    kdec_b = _lanes(dec(idx), RET_DK)
    cdec = jnp.broadcast_to(jnp.exp(C * log_g)[:, None], (H, RET_DV))
    pos_c = jnp.arange(L, dtype=F32)
    cdec_f = _lanes(dec(L - 1.0 - pos_c), RET_DK)
    cdec_b = _lanes(dec(pos_c), RET_DK)

    cvec = jnp.zeros((16, D), F32).at[:B].set(c).at[B].set(c_ctx)
    mods3 = _ada(cvec, ada_w[lyr], ada_b[lyr]).reshape(16, 1, 6 * D)

    z_c, xr_c = _proj(ctx, mods3, lambda b: B, norm1_g[lyr], w_in_b, tab_c, min(PROJ_TM, L))
    z_l, xr_l = _proj(x, mods3, lambda b: b, norm1_g[lyr], w_in_b, tab_l, min(PROJ_TM, S))

    zeros = jnp.zeros((B, D), F32)
    hs = []
    for d in range(2):
        args = (conv_w[lyr], conv_b[lyr], wg[d], 0.5 * lru_ba[lyr, d], 0.5 * lru_bx[lyr, d], lru_lambda[lyr, d])
        _, h0 = _rnn(xr_c, *args, zeros, reverse=(d == 1), tt=RNN_TT)
        h, _ = _rnn(xr_l, *args, h0, reverse=(d == 1), tt=RNN_TT)
        hs.append(h)

    s_fwd, s_bwd = _ret_ctx(z_c, B, cdec_f, cdec_b)
    sb = _ret_bwd(z_l, B, s_bwd, kdec_b, cdec)
    retg = _ret_fwd(z_l, B, sb, s_fwd, intra, qdec_f, qdec_b, kdec_f, cdec)

    dt = min(DISP_TM, S)
    x1, hl2, eid3, gate3, cnt3 = _finish(
        x, hs[0], hs[1], z_l, retg, mods3, norm2_g[lyr], w_rnn_proj[lyr].astype(BF16),
        w_ret_proj[lyr].astype(BF16), w_out[lyr].astype(BF16), rwt_hi, rwt_lo, router_b[lyr], dt)

    cnt_t = cnt3[:, :, 0].astype(I32)
    cnt_t = (cnt_t + ROW_ALIGN - 1) // ROW_ALIGN * ROW_ALIGN
    cnt = jnp.sum(cnt_t, axis=0)
    padded = (cnt + MOE_BLOCK - 1) // MOE_BLOCK * MOE_BLOCK
    pad_end = jnp.cumsum(padded)
    pad_start = pad_end - padded
    gstart = pad_start[None, :] + jnp.cumsum(cnt_t, axis=0) - cnt_t
    loff = jnp.cumsum(cnt_t, axis=1) - cnt_t
    meta3 = jnp.concatenate([loff, cnt_t, gstart], axis=1).reshape(N // dt, 1, 3 * N_EXPERTS)
    zmeta = jnp.concatenate([jnp.zeros_like(cnt), padded - cnt, pad_start + cnt]).reshape(1, 3 * N_EXPERTS)
    n_blocks = -(-(N * TOP_K + (N // dt) * N_EXPERTS * (ROW_ALIGN - 1)) // MOE_BLOCK) + N_EXPERTS
    P = n_blocks * MOE_BLOCK
    blk_start = jnp.arange(n_blocks, dtype=I32) * MOE_BLOCK
    blk_expert = jnp.minimum(jnp.sum((pad_end[None, :] <= blk_start[:, None]).astype(I32), axis=1), N_EXPERTS - 1)
    n_used = (pad_end[-1:] // MOE_BLOCK).astype(I32)

    h_pad = _dispatch(hl2, eid3, meta3, zmeta, P)
    y_pad = _experts(blk_expert, n_used, h_pad, w1b, b1p, w2b, moe_b2[lyr])
    return _combine(meta3, eid3, gate3, x1, mods3, final_g, y_pad)
```

```python
import functools

import jax
import jax.numpy as jnp
import numpy as np
from jax import lax
from jax.experimental import pallas as pl
from jax.experimental.pallas import tpu as pltpu

F32 = jnp.float32
BF16 = jnp.bfloat16
I32 = jnp.int32

GRID_W = 64
RNN_BLOCKS = 8
CONV_W = 4
LRU_C = 8.0
RET_HEADS = 8
RET_DK = 128
RET_DV = 256
ROPE_BASE = 10000.0
N_EXPERTS = 32
TOP_K = 4
SWIGLU_ALPHA = 1.702
SWIGLU_LIMIT = 7.0
EPS = 1e-6
N_CHUNKS = 10

PROJ_TM = 1024
PROJ_TN = 2048
RNN_TT = 128
RET_C = 256
FIN_TM = 512
MOE_BLOCK = 512
DISP_TM = 512
ROW_ALIGN = 8
VMEM_LIMIT = 56 * 1024 * 1024


def _cparams(n_axes):
    return pltpu.CompilerParams(dimension_semantics=("arbitrary",) * n_axes,
                                vmem_limit_bytes=VMEM_LIMIT)


def _sigmoid(x):
    return 0.5 * (jnp.tanh(0.5 * x) + 1.0)


def _rmsnorm(x, g):
    return x * lax.rsqrt(jnp.mean(x * x, axis=-1, keepdims=True) + EPS) * g


def _ada_kernel(c_ref, w_ref, b_ref, o_ref):
    c = c_ref[...]
    s = c * _sigmoid(c)
    o_ref[...] = jnp.dot(s, w_ref[...], preferred_element_type=F32,
                         precision=lax.Precision.HIGHEST) + b_ref[...]


def _ada(cvec, ada_w, ada_b):
    R, D = cvec.shape
    n = ada_w.shape[1] // D
    return pl.pallas_call(
        _ada_kernel,
        grid=(n,),
        in_specs=[pl.BlockSpec((R, D), lambda j: (0, 0)),
                  pl.BlockSpec((D, D), lambda j: (0, j)),
                  pl.BlockSpec((1, D), lambda j: (0, j))],
        out_specs=pl.BlockSpec((R, D), lambda j: (0, j)),
        out_shape=jax.ShapeDtypeStruct((R, n * D), F32),
        compiler_params=_cparams(1),
        name="ada",
    )(cvec, ada_w, ada_b.reshape(1, -1))


def _proj_kernel(x_ref, sh_ref, sc_ref, g_ref, w_ref, tab_ref, o_ref, xr_ref, h_ref):
    j = pl.program_id(2)
    D = x_ref.shape[1]

    @pl.when(j == 0)
    def _():
        h = _rmsnorm(x_ref[...], g_ref[...])
        h_ref[...] = (h * (1.0 + sc_ref[...]) + sh_ref[...]).astype(BF16)

    acc = jnp.dot(h_ref[...], w_ref[...], preferred_element_type=F32)

    @pl.when(j != 1)
    def _():
        o_ref[...] = acc.astype(o_ref.dtype)

    @pl.when(j == 0)
    def _():
        xr_ref[...] = acc[:, :D].astype(xr_ref.dtype)

    @pl.when(j == 1)
    def _():
        for part in range(2):
            cos = tab_ref[:, (2 * part) * RET_DK:(2 * part + 1) * RET_DK]
            sin = tab_ref[:, (2 * part + 1) * RET_DK:(2 * part + 2) * RET_DK]
            for hh in range(RET_HEADS):
                c0 = part * RET_HEADS * RET_DK + hh * RET_DK
                t = acc[:, c0:c0 + RET_DK]
                o_ref[:, c0:c0 + RET_DK] = (t * cos + pltpu.roll(t, RET_DK // 2, 1) * sin).astype(o_ref.dtype)


def _proj(x3, mods3, mod_row, norm_g, w_in_b, tab, tm):
    B, S, D = x3.shape
    d_in = w_in_b.shape[1]
    nj = d_in // PROJ_TN
    return pl.pallas_call(
        _proj_kernel,
        grid=(B, S // tm, nj),
        in_specs=[pl.BlockSpec((None, tm, D), lambda b, i, j: (b, i, 0)),
                  pl.BlockSpec((None, 1, D), lambda b, i, j: (mod_row(b), 0, 0)),
                  pl.BlockSpec((None, 1, D), lambda b, i, j: (mod_row(b), 0, 1)),
                  pl.BlockSpec((1, D), lambda b, i, j: (0, 0)),
                  pl.BlockSpec((D, PROJ_TN), lambda b, i, j: (0, j)),
                  pl.BlockSpec((tm, 4 * RET_DK), lambda b, i, j: (i, 0))],
        out_specs=[pl.BlockSpec((tm, PROJ_TN), lambda b, i, j: (i, b * nj + j)),
                   pl.BlockSpec((None, tm, D), lambda b, i, j: (b, i, 0))],
        out_shape=[jax.ShapeDtypeStruct((S, B * d_in), BF16),
                   jax.ShapeDtypeStruct((B, S, D), BF16)],
        scratch_shapes=[pltpu.VMEM((tm, D), BF16)],
        compiler_params=_cparams(3),
        name="proj",
    )(x3, mods3, mods3, norm_g.reshape(1, D), w_in_b, tab)


def _rnn_kernel(xm_ref, xp_ref, xn_ref, cw_ref, cb_ref, wg_ref, ba_ref, bx_ref, lam_ref, h0_ref,
                h_ref, hfin_ref, xs_ref, a_ref, u_ref, hc_ref, *, reverse, n_tiles):
    i = pl.program_id(0)
    tile = (n_tiles - 1 - i) if reverse else i
    nb, tt, D = xm_ref.shape
    bw = D // RNN_BLOCKS
    HALO = xp_ref.shape[1]

    @pl.when(i == 0)
    def _():
        hc_ref[...] = h0_ref[...]

    def time_major(v):
        return jnp.swapaxes(v.astype(F32), 0, 1).reshape(v.shape[1] * nb, D)

    hr = HALO * nb
    R = tt * nb
    xs_ref[0:hr, :] = jnp.where(tile > 0, time_major(xp_ref[...]), 0.0)
    xs_ref[hr:hr + R, :] = time_major(xm_ref[...])
    xs_ref[hr + R:2 * hr + R, :] = jnp.where(tile < n_tiles - 1, time_major(xn_ref[...]), 0.0)

    nl = -lam_ref[...]
    csp = (0.25 * LRU_C) * (jnp.maximum(nl, 0.0) + jnp.log1p(jnp.exp(-jnp.abs(nl))))
    SUB = 256

    def gates(s, carry):
        r0 = pl.multiple_of(s * SUB, SUB)
        base = hr - 2 * nb
        xc = cb_ref[...] + cw_ref[0:1, :] * xs_ref[pl.ds(pl.multiple_of(r0 + base, nb), SUB), :]
        for k in range(1, CONV_W):
            xc = xc + cw_ref[k:k + 1, :] * xs_ref[pl.ds(pl.multiple_of(r0 + base + k * nb, nb), SUB), :]
        xb = xc.astype(BF16)
        for n in range(RNN_BLOCKS):
            g = jnp.dot(xb[:, n * bw:(n + 1) * bw], wg_ref[n], preferred_element_type=F32)
            cs = slice(n * bw, (n + 1) * bw)
            tr = jnp.tanh(g[:, :bw] + ba_ref[:, cs])
            ti = jnp.tanh(g[:, bw:] + bx_ref[:, cs])
            t = jnp.tanh(csp[:, cs] * tr + csp[:, cs])
            rc2 = 2.0 / (1.0 + t)
            a_ref[pl.ds(r0, SUB), cs] = rc2 - 1.0
            u_ref[pl.ds(r0, SUB), cs] = (jnp.sqrt(t) * rc2) * ((0.5 * ti + 0.5) * xc[:, cs])
        return carry

    lax.fori_loop(0, R // SUB, gates, 0)

    def step(t, h):
        ts = (tt - 1 - t) if reverse else t
        r0 = pl.multiple_of(ts * nb, nb)
        h = a_ref[pl.ds(r0, nb), :] * h + u_ref[pl.ds(r0, nb), :]
        u_ref[pl.ds(r0, nb), :] = h
        return h

    h = lax.fori_loop(0, tt, step, hc_ref[...], unroll=8)
    hc_ref[...] = h
    hfin_ref[...] = h
    h_ref[...] = jnp.swapaxes(u_ref[...].reshape(tt, nb, D), 0, 1).astype(h_ref.dtype)


def _rnn(xr, conv_w, conv_b, wg, ba, bx, lam, h0, *, reverse, tt):
    nb, T, D = xr.shape
    tt = min(tt, T)
    n_tiles = T // tt
    HALO = 16
    hb = tt // HALO

    def tile_of(i):
        return (n_tiles - 1 - i) if reverse else i

    kern = functools.partial(_rnn_kernel, reverse=reverse, n_tiles=n_tiles)
    return pl.pallas_call(
        kern,
        grid=(n_tiles,),
        in_specs=[pl.BlockSpec((nb, tt, D), lambda i: (0, tile_of(i), 0)),
                  pl.BlockSpec((nb, HALO, D), lambda i: (0, jnp.maximum(tile_of(i) * hb - 1, 0), 0)),
                  pl.BlockSpec((nb, HALO, D), lambda i: (0, jnp.minimum((tile_of(i) + 1) * hb, n_tiles * hb - 1), 0)),
                  pl.BlockSpec((CONV_W, D), lambda i: (0, 0)),
                  pl.BlockSpec((1, D), lambda i: (0, 0)),
                  pl.BlockSpec((RNN_BLOCKS, D // RNN_BLOCKS, 2 * D // RNN_BLOCKS), lambda i: (0, 0, 0)),
                  pl.BlockSpec((1, D), lambda i: (0, 0)),
                  pl.BlockSpec((1, D), lambda i: (0, 0)),
                  pl.BlockSpec((1, D), lambda i: (0, 0)),
                  pl.BlockSpec((nb, D), lambda i: (0, 0))],
        out_specs=[pl.BlockSpec((nb, tt, D), lambda i: (0, tile_of(i), 0)),
                   pl.BlockSpec((nb, D), lambda i: (0, 0))],
        out_shape=[jax.ShapeDtypeStruct((nb, T, D), BF16),
                   jax.ShapeDtypeStruct((nb, D), F32)],
        scratch_shapes=[pltpu.VMEM(((tt + 2 * HALO) * nb, D), F32),
                        pltpu.VMEM((nb * tt, D), F32),
                        pltpu.VMEM((nb * tt, D), F32),
                        pltpu.VMEM((nb, D), F32)],
        compiler_params=_cparams(1),
        name="rnn_bwd" if reverse else "rnn_fwd",
    )(xr, xr, xr, conv_w, conv_b.reshape(1, D), wg, ba.reshape(1, D), bx.reshape(1, D),
      lam.reshape(1, D), h0)


def _tdot(a, b):
    return lax.dot_general(a, b, (((0,), (0,)), ((), ())), preferred_element_type=F32)


def _ret_ctx_kernel(k_ref, v_ref, df_ref, db_ref, sf_ref, sb_ref):
    for hh in range(RET_HEADS):
        kh = k_ref[:, hh * RET_DK:(hh + 1) * RET_DK].astype(F32)
        vh = v_ref[:, hh * RET_DV:(hh + 1) * RET_DV]
        sf_ref[hh] = _tdot((kh * df_ref[hh]).astype(BF16), vh)
        sb_ref[hh] = _tdot((kh * db_ref[hh]).astype(BF16), vh)


def _ret_ctx(zc, B, dec_f, dec_b):
    L = zc.shape[0] // B
    H = RET_HEADS
    st = jax.ShapeDtypeStruct((B, H, RET_DK, RET_DV), F32)
    return pl.pallas_call(
        _ret_ctx_kernel,
        grid=(B,),
        in_specs=[pl.BlockSpec((L, H * RET_DK), lambda b: (b, 3)),
                  pl.BlockSpec((L, H * RET_DV), lambda b: (b, 2)),
                  pl.BlockSpec((H, L, RET_DK), lambda b: (0, 0, 0)),
                  pl.BlockSpec((H, L, RET_DK), lambda b: (0, 0, 0))],
        out_specs=[pl.BlockSpec((None, H, RET_DK, RET_DV), lambda b: (b, 0, 0, 0))] * 2,
        out_shape=[st, st],
        compiler_params=_cparams(1),
        name="ret_ctx",
    )(zc, zc, dec_f, dec_b)


def _ret_bwd_kernel(k_ref, v_ref, s0_ref, kd_ref, cd_ref, o_ref, st_ref):
    @pl.when(pl.program_id(1) == 0)
    def _():
        st_ref[...] = s0_ref[...]

    for hh in range(RET_HEADS):
        o_ref[hh] = st_ref[hh].astype(o_ref.dtype)
        kh = k_ref[:, hh * RET_DK:(hh + 1) * RET_DK].astype(F32)
        vh = v_ref[:, hh * RET_DV:(hh + 1) * RET_DV]
        st_ref[hh] = st_ref[hh] * cd_ref[hh:hh + 1, :] + _tdot((kh * kd_ref[hh]).astype(BF16), vh)


def _ret_bwd(z, B, s_bwd, kdec_b, cdec):
    S = z.shape[0]
    H, C = RET_HEADS, min(RET_C, S)
    n = S // C
    return pl.pallas_call(
        _ret_bwd_kernel,
        grid=(B, n),
        in_specs=[pl.BlockSpec((C, H * RET_DK), lambda b, j: (n - 1 - j, b * N_CHUNKS + 3)),
                  pl.BlockSpec((C, H * RET_DV), lambda b, j: (n - 1 - j, b * (N_CHUNKS // 2) + 2)),
                  pl.BlockSpec((None, H, RET_DK, RET_DV), lambda b, j: (b, 0, 0, 0)),
                  pl.BlockSpec((H, C, RET_DK), lambda b, j: (0, 0, 0)),
                  pl.BlockSpec((H, RET_DV), lambda b, j: (0, 0))],
        out_specs=pl.BlockSpec((None, None, H, RET_DK, RET_DV), lambda b, j: (b, n - 1 - j, 0, 0, 0)),
        out_shape=jax.ShapeDtypeStruct((B, n, H, RET_DK, RET_DV), BF16),
        scratch_shapes=[pltpu.VMEM((H, RET_DK, RET_DV), F32)],
        compiler_params=_cparams(2),
        name="ret_bwd",
    )(z, z, s_bwd, kdec_b, cdec)


def _ret_fwd_kernel(q_ref, k_ref, v_ref, gs_ref, sb_ref, s0_ref, intra_ref, qf_ref, qb_ref, kf_ref, cd_ref,
                    o_ref, st_ref):
    @pl.when(pl.program_id(1) == 0)
    def _():
        st_ref[...] = s0_ref[...]

    for hh in range(RET_HEADS):
        qh = q_ref[:, hh * RET_DK:(hh + 1) * RET_DK]
        kh = k_ref[:, hh * RET_DK:(hh + 1) * RET_DK]
        vh = v_ref[:, hh * RET_DV:(hh + 1) * RET_DV]
        s = lax.dot_general(qh, kh, (((1,), (1,)), ((), ())), preferred_element_type=F32)
        o = jnp.dot(s.astype(BF16) * intra_ref[hh], vh, preferred_element_type=F32)
        q2 = jnp.concatenate([qh * qf_ref[hh], qh * qb_ref[hh]], axis=1)
        s2 = jnp.concatenate([st_ref[hh].astype(BF16), sb_ref[hh]], axis=0)
        o = o + jnp.dot(q2, s2, preferred_element_type=F32)
        st_ref[hh] = st_ref[hh] * cd_ref[hh:hh + 1, :] + _tdot(kh * kf_ref[hh], vh)
        mu = jnp.mean(o, axis=-1, keepdims=True)
        d = o - mu
        var = jnp.mean(d * d, axis=-1, keepdims=True)
        g = gs_ref[:, hh * RET_DV:(hh + 1) * RET_DV]
        gate = g * _sigmoid(g)
        o_ref[:, hh * RET_DV:(hh + 1) * RET_DV] = (d * lax.rsqrt(var + EPS)).astype(o_ref.dtype) * gate


def _ret_fwd(z, B, sb, s_fwd, intra, qdec_f, qdec_b, kdec_f, cdec):
    S = z.shape[0]
    H, C = RET_HEADS, min(RET_C, S)
    n = S // C
    half = N_CHUNKS // 2
    return pl.pallas_call(
        _ret_fwd_kernel,
        grid=(B, n),
        in_specs=[pl.BlockSpec((C, H * RET_DK), lambda b, j: (j, b * N_CHUNKS + 2)),
                  pl.BlockSpec((C, H * RET_DK), lambda b, j: (j, b * N_CHUNKS + 3)),
                  pl.BlockSpec((C, H * RET_DV), lambda b, j: (j, b * half + 2)),
                  pl.BlockSpec((C, H * RET_DV), lambda b, j: (j, b * half + 3)),
                  pl.BlockSpec((None, None, H, RET_DK, RET_DV), lambda b, j: (b, j, 0, 0, 0)),
                  pl.BlockSpec((None, H, RET_DK, RET_DV), lambda b, j: (b, 0, 0, 0)),
                  pl.BlockSpec((H, C, C), lambda b, j: (0, 0, 0)),
                  pl.BlockSpec((H, C, RET_DK), lambda b, j: (0, 0, 0)),
                  pl.BlockSpec((H, C, RET_DK), lambda b, j: (0, 0, 0)),
                  pl.BlockSpec((H, C, RET_DK), lambda b, j: (0, 0, 0)),
                  pl.BlockSpec((H, RET_DV), lambda b, j: (0, 0))],
        out_specs=pl.BlockSpec((None, C, H * RET_DV), lambda b, j: (b, j, 0)),
        out_shape=jax.ShapeDtypeStruct((B, S, H * RET_DV), BF16),
        scratch_shapes=[pltpu.VMEM((H, RET_DK, RET_DV), F32)],
        compiler_params=_cparams(2),
        name="ret_fwd",
    )(z, z, z, z, sb, s_fwd, intra, qdec_f, qdec_b, kdec_f, cdec)


def _dot_t(a, b):
    return lax.dot_general(a, b, (((1,), (1,)), ((), ())), preferred_element_type=F32)


def _finish_kernel(x_ref, hf_ref, hb_ref, gr_ref, gab_ref, ret_ref, g1_ref, sh2_ref, sc2_ref, n2_ref,
                   wr_ref, wt_ref, wo_ref, rwh_ref, rwl_ref, rb_ref,
                   x1_ref, hl_ref, eid_ref, gate_ref, cnt_ref):
    D = x_ref.shape[1]
    tm = x_ref.shape[0]
    dt = eid_ref.shape[2]
    rnn = hf_ref[...].astype(F32) + hb_ref[...].astype(F32)
    y_rnn = jnp.dot((rnn * jax.nn.gelu(gr_ref[...].astype(F32))).astype(BF16), wr_ref[...],
                    preferred_element_type=F32)
    y_ret = jnp.dot(ret_ref[...], wt_ref[...], preferred_element_type=F32)
    ga = gab_ref[:, :D].astype(F32)
    gb = gab_ref[:, D:].astype(F32)
    merged = _sigmoid(ga) * y_rnn + _sigmoid(gb) * y_ret
    y = jnp.dot(merged.astype(BF16), wo_ref[...], preferred_element_type=F32)
    x1 = x_ref[...] + g1_ref[...] * y
    x1_ref[...] = x1
    hl = _rmsnorm(x1, n2_ref[...]) * (1.0 + sc2_ref[...]) + sh2_ref[...]
    hh = hl.astype(BF16)
    hl_ref[...] = hh

    hlo = (hl - hh.astype(F32)).astype(BF16)
    logits = (_dot_t(rwh_ref[...], hh) + _dot_t(rwh_ref[...], hlo) + _dot_t(rwl_ref[...], hh)) + rb_ref[...]

    ne = logits.shape[0]
    sub = lax.broadcasted_iota(I32, (ne, tm), 0)
    work = logits
    vals, idxs = [], []
    oh = jnp.zeros((ne, tm), F32)
    for _ in range(TOP_K):
        m = jnp.max(work, axis=0, keepdims=True)
        idx = jnp.min(jnp.where(work == m, sub, ne), axis=0, keepdims=True)
        hot = sub == idx
        vals.append(m)
        idxs.append(idx)
        oh = oh + jnp.where(hot, 1.0, 0.0)
        work = jnp.where(hot, -jnp.inf, work)
    es = [jnp.exp(v - vals[0]) for v in vals]
    inv = 1.0 / (es[0] + es[1] + es[2] + es[3])
    for part in range(tm // dt):
        ls = slice(part * dt, (part + 1) * dt)
        for k in range(TOP_K):
            eid_ref[part, k:k + 1, :] = idxs[k][:, ls]
            gate_ref[part, k:k + 1, :] = (es[k] * inv)[:, ls]
        cnt_ref[part] = jnp.sum(oh[:, ls], axis=1, keepdims=True)


def _finish(x3, hf, hb, z, retg, mods3, norm2_g, w_rnn_b, w_ret_b, w_out_b, rwt_hi, rwt_lo, router_b, dt):
    B, S, D = x3.shape
    tm = min(FIN_TM, S)
    nt = S // tm
    N = B * S
    ne = rwt_hi.shape[0]
    half = N_CHUNKS // 2
    per = tm // dt
    const2 = lambda b, i: (0, 0)
    tile3 = lambda b, i: (b * nt + i, 0, 0)
    return pl.pallas_call(
        _finish_kernel,
        grid=(B, nt),
        in_specs=[pl.BlockSpec((None, tm, D), lambda b, i: (b, i, 0)),
                  pl.BlockSpec((None, tm, D), lambda b, i: (b, i, 0)),
                  pl.BlockSpec((None, tm, D), lambda b, i: (b, i, 0)),
                  pl.BlockSpec((tm, D), lambda b, i: (i, b * N_CHUNKS + 1)),
                  pl.BlockSpec((tm, 2 * D), lambda b, i: (i, b * half + 4)),
                  pl.BlockSpec((None, tm, retg.shape[2]), lambda b, i: (b, i, 0)),
                  pl.BlockSpec((None, 1, D), lambda b, i: (b, 0, 2)),
                  pl.BlockSpec((None, 1, D), lambda b, i: (b, 0, 3)),
                  pl.BlockSpec((None, 1, D), lambda b, i: (b, 0, 4)),
                  pl.BlockSpec((1, D), const2),
                  pl.BlockSpec(w_rnn_b.shape, const2),
                  pl.BlockSpec(w_ret_b.shape, const2),
                  pl.BlockSpec(w_out_b.shape, const2),
                  pl.BlockSpec(rwt_hi.shape, const2),
                  pl.BlockSpec(rwt_lo.shape, const2),
                  pl.BlockSpec((ne, 1), const2)],
        out_specs=[pl.BlockSpec((None, tm, D), lambda b, i: (b, i, 0)),
                   pl.BlockSpec((tm, D), lambda b, i: (b * nt + i, 0)),
                   pl.BlockSpec((per, TOP_K, dt), tile3),
                   pl.BlockSpec((per, TOP_K, dt), tile3),
                   pl.BlockSpec((per, ne, 1), tile3)],
        out_shape=[jax.ShapeDtypeStruct((B, S, D), F32),
                   jax.ShapeDtypeStruct((N, D), BF16),
                   jax.ShapeDtypeStruct((N // dt, TOP_K, dt), I32),
                   jax.ShapeDtypeStruct((N // dt, TOP_K, dt), F32),
                   jax.ShapeDtypeStruct((N // dt, ne, 1), F32)],
        compiler_params=_cparams(2),
        name="finish",
    )(x3, hf, hb, z, z, retg, mods3, mods3, mods3, norm2_g.reshape(1, D),
      w_rnn_b, w_ret_b, w_out_b, rwt_hi, rwt_lo, router_b.reshape(ne, 1))


def _local_rows(dt):
    return TOP_K * dt + N_EXPERTS * ROW_ALIGN


def _local_slots(eid_ref):
    dt = eid_ref.shape[1]
    ne = N_EXPERTS
    sub = lax.broadcasted_iota(I32, (ne, dt), 0)
    hots = [sub == eid_ref[k:k + 1, :] for k in range(TOP_K)]
    oh = jnp.zeros((ne, dt), F32)
    for hot in hots:
        oh = oh + jnp.where(hot, 1.0, 0.0)
    earlier = jnp.where(lax.broadcasted_iota(I32, (dt, dt), 0) < lax.broadcasted_iota(I32, (dt, dt), 1), 1.0, 0.0)
    before = jnp.dot(oh.astype(BF16), earlier.astype(BF16), preferred_element_type=F32)
    cnt = jnp.broadcast_to(jnp.sum(oh, axis=1, keepdims=True), (ne, dt))
    cnt = jnp.ceil(cnt * (1.0 / ROW_ALIGN)) * ROW_ALIGN
    lower = jnp.where(lax.broadcasted_iota(I32, (ne, ne), 1) < lax.broadcasted_iota(I32, (ne, ne), 0), 1.0, 0.0)
    base = before + jnp.dot(lower.astype(BF16), cnt.astype(BF16), preferred_element_type=F32)
    return [jnp.sum(jnp.where(hot, base, 0.0), axis=0, keepdims=True).astype(I32) for hot in hots]


def _slot_matrix(slots, weights, out_ref):
    rows, dt = out_ref.shape
    ch = 64
    rel = lax.broadcasted_iota(I32, (ch, dt), 0).astype(F32).astype(BF16)
    slots_f = [s.astype(F32) for s in slots]
    weights_b = [jnp.asarray(w, F32).astype(BF16) for w in weights]
    zero = jnp.zeros((), BF16)
    for c in range(rows // ch):
        acc = None
        for s, w in zip(slots_f, weights_b):
            term = jnp.where(rel == (s - float(c * ch)).astype(BF16), w, zero)
            acc = term if acc is None else acc + term
        out_ref[c * ch:(c + 1) * ch, :] = acc


def _for_each_run(meta_ref, fn):
    def body(e, c):
        n = pl.multiple_of(meta_ref[0, N_EXPERTS + e], ROW_ALIGN)

        @pl.when(n > 0)
        def _():
            fn(pl.multiple_of(meta_ref[0, e], ROW_ALIGN), pl.multiple_of(meta_ref[0, 2 * N_EXPERTS + e], ROW_ALIGN), n)

        return c

    lax.fori_loop(0, N_EXPERTS, body, 0)


def _run_rows(meta_ref):
    last = N_EXPERTS - 1
    return pl.multiple_of(meta_ref[0, last] + meta_ref[0, N_EXPERTS + last], ROW_ALIGN)


def _dispatch_kernel(meta_ref, zmeta_ref, eid_ref, x_ref, hp_ref, sbuf_ref, zbuf_ref, pm_ref, rows_ref, sems, zsem):
    t = pl.program_id(0)
    slot = t % 2

    def zero_copy(_, g, n):
        return pltpu.make_async_copy(zbuf_ref.at[pl.ds(0, n), :], hp_ref.at[pl.ds(g, n), :], zsem)

    def run_copy(sl, l, g, n):
        return pltpu.make_async_copy(sbuf_ref.at[sl, pl.ds(l, n), :], hp_ref.at[pl.ds(g, n), :], sems.at[sl])

    @pl.when(t == 0)
    def _():
        zbuf_ref[...] = jnp.zeros_like(zbuf_ref)
        _for_each_run(zmeta_ref, lambda l, g, n: zero_copy(l, g, n).start())
        _for_each_run(zmeta_ref, lambda l, g, n: zero_copy(l, g, n).wait())

    _slot_matrix(_local_slots(eid_ref), [1.0] * TOP_K, pm_ref)
    sbuf_ref[slot] = jnp.dot(pm_ref[...], x_ref[...], preferred_element_type=F32)

    @pl.when(t > 0)
    def _():
        run_copy(1 - slot, 0, 0, pl.multiple_of(rows_ref[1 - slot], ROW_ALIGN)).wait()

    _for_each_run(meta_ref, lambda l, g, n: run_copy(slot, l, g, n).start())
    rows_ref[slot] = _run_rows(meta_ref)

    @pl.when(t == pl.num_programs(0) - 1)
    def _():
        run_copy(slot, 0, 0, _run_rows(meta_ref)).wait()


def _dispatch(hl2, eid3, meta3, zmeta, P):
    N, D = hl2.shape
    nt, _, dt = eid3.shape
    nm = meta3.shape[2]
    return pl.pallas_call(
        _dispatch_kernel,
        grid=(nt,),
        in_specs=[pl.BlockSpec((None, 1, nm), lambda i: (i, 0, 0), memory_space=pltpu.SMEM),
                  pl.BlockSpec((1, nm), lambda i: (0, 0), memory_space=pltpu.SMEM),
                  pl.BlockSpec((None, TOP_K, dt), lambda i: (i, 0, 0)),
                  pl.BlockSpec((dt, D), lambda i: (i, 0))],
        out_specs=pl.BlockSpec(memory_space=pl.ANY),
        out_shape=jax.ShapeDtypeStruct((P, D), F32),
        scratch_shapes=[pltpu.VMEM((2, _local_rows(dt), D), F32),
                        pltpu.VMEM((MOE_BLOCK, D), F32),
                        pltpu.VMEM((_local_rows(dt), dt), BF16),
                        pltpu.SMEM((2,), I32),
                        pltpu.SemaphoreType.DMA((2,)),
                        pltpu.SemaphoreType.DMA(())],
        compiler_params=_cparams(1),
        name="dispatch",
    )(meta3, zmeta, eid3, hl2)


def _regroup_kernel(w_ref, sel_ref, o_ref):
    half = w_ref.shape[1] // 2
    g = sel_ref.shape[0]
    wb = w_ref[...].astype(BF16)
    for j in range(w_ref.shape[1] // g):
        r = jnp.dot(wb[:, g * j:g * (j + 1)], sel_ref[...], preferred_element_type=F32)
        o_ref[:, (g // 2) * j:(g // 2) * (j + 1)] = r[:, :g // 2].astype(o_ref.dtype)
        o_ref[:, half + (g // 2) * j:half + (g // 2) * (j + 1)] = r[:, g // 2:].astype(o_ref.dtype)


def _regroup_glu_lin(w1):
    ne, D, de2 = w1.shape
    g = 256
    sel = np.zeros((g, g), np.float32)
    sel[np.arange(0, g, 2), np.arange(g // 2)] = 1.0
    sel[np.arange(1, g, 2), g // 2 + np.arange(g // 2)] = 1.0
    out = pl.pallas_call(
        _regroup_kernel,
        grid=(ne,),
        in_specs=[pl.BlockSpec((D, de2), lambda e: (e, 0)),
                  pl.BlockSpec((g, g), lambda e: (0, 0))],
        out_specs=pl.BlockSpec((D, de2), lambda e: (e, 0)),
        out_shape=jax.ShapeDtypeStruct((ne * D, de2), BF16),
        compiler_params=_cparams(1),
        name="regroup",
    )(w1.reshape(ne * D, de2), jnp.asarray(sel, BF16))
    return out.reshape(ne, D, de2)


def _expert_kernel(be_ref, nu_ref, x_ref, w1_ref, b1_ref, w2_ref, b2_ref, o_ref):
    del be_ref

    @pl.when(pl.program_id(0) < nu_ref[0])
    def _():
        de = w2_ref.shape[0]
        h = jnp.dot(x_ref[...].astype(BF16), w1_ref[...], preferred_element_type=F32) + b1_ref[...]
        glu = jnp.minimum(h[:, :de], SWIGLU_LIMIT)
        lin = jnp.clip(h[:, de:], -SWIGLU_LIMIT, SWIGLU_LIMIT)
        act = glu * _sigmoid(SWIGLU_ALPHA * glu) * (lin + 1.0)
        o_ref[...] = jnp.dot(act.astype(BF16), w2_ref[...], preferred_element_type=F32) + b2_ref[...]


def _experts(blk_expert, n_used, h_pad, w1b, b1p, w2b, b2):
    P, D = h_pad.shape
    ne, _, de2 = w1b.shape
    de = de2 // 2
    nblk = P // MOE_BLOCK
    blk = lambda j, be, nu: (jnp.minimum(j, nu[0] - 1), 0)
    wsel = lambda j, be, nu: (be[jnp.minimum(j, nu[0] - 1)], 0, 0)
    return pl.pallas_call(
        _expert_kernel,
        grid_spec=pltpu.PrefetchScalarGridSpec(
            num_scalar_prefetch=2,
            grid=(nblk,),
            in_specs=[pl.BlockSpec((MOE_BLOCK, D), blk),
                      pl.BlockSpec((None, D, de2), wsel),
                      pl.BlockSpec((None, 1, de2), wsel),
                      pl.BlockSpec((None, de, D), wsel),
                      pl.BlockSpec((None, 1, D), wsel)],
            out_specs=pl.BlockSpec((MOE_BLOCK, D), blk)),
        out_shape=jax.ShapeDtypeStruct((P, D), F32),
        compiler_params=_cparams(1),
        name="experts",
    )(blk_expert, n_used, h_pad, w1b, b1p.reshape(ne, 1, de2), w2b, b2.reshape(ne, 1, D))


def _combine_kernel(meta_ref, nmeta_ref, eid_ref, gate_ref, x1_ref, g2_ref, fg_ref, yp_ref, o_ref,
                    ybuf_ref, gm_ref, sems):
    t = pl.program_id(0) * pl.num_programs(1) + pl.program_id(1)
    n_tiles = pl.num_programs(0) * pl.num_programs(1)
    slot = t % 2

    def run_copy(sl, l, g, n):
        return pltpu.make_async_copy(yp_ref.at[pl.ds(g, n), :], ybuf_ref.at[sl, pl.ds(l, n), :], sems.at[sl])

    @pl.when(t == 0)
    def _():
        ybuf_ref[...] = jnp.zeros_like(ybuf_ref)
        _for_each_run(meta_ref, lambda l, g, n: run_copy(0, l, g, n).start())

    @pl.when(t + 1 < n_tiles)
    def _():
        _for_each_run(nmeta_ref, lambda l, g, n: run_copy(1 - slot, l, g, n).start())

    run_copy(slot, 0, 0, _run_rows(meta_ref)).wait()

    _slot_matrix(_local_slots(eid_ref), [gate_ref[k:k + 1, :] for k in range(TOP_K)], gm_ref)
    y = _tdot(gm_ref[...], ybuf_ref[slot].astype(BF16))
    x2 = x1_ref[...] + g2_ref[...] * y
    o_ref[...] = _rmsnorm(x2, fg_ref[...])


def _combine(meta3, eid3, gate3, x1, mods3, final_g, y_pad):
    B, S, D = x1.shape
    n_tiles, _, dt = eid3.shape
    nt = S // dt
    nm = meta3.shape[2]
    tile3 = lambda b, i: (b * nt + i, 0, 0)
    next3 = lambda b, i: (jnp.minimum(b * nt + i + 1, n_tiles - 1), 0, 0)
    return pl.pallas_call(
        _combine_kernel,
        grid=(B, nt),
        in_specs=[pl.BlockSpec((None, 1, nm), tile3, memory_space=pltpu.SMEM),
                  pl.BlockSpec((None, 1, nm), next3, memory_space=pltpu.SMEM),
                  pl.BlockSpec((None, TOP_K, dt), tile3),
                  pl.BlockSpec((None, TOP_K, dt), tile3),
                  pl.BlockSpec((None, dt, D), lambda b, i: (b, i, 0)),
                  pl.BlockSpec((None, 1, D), lambda b, i: (b, 0, 5)),
                  pl.BlockSpec((1, D), lambda b, i: (0, 0)),
                  pl.BlockSpec(memory_space=pl.ANY)],
        out_specs=pl.BlockSpec((None, dt, D), lambda b, i: (b, i, 0)),
        out_shape=jax.ShapeDtypeStruct((B, S, D), F32),
        scratch_shapes=[pltpu.VMEM((2, _local_rows(dt), D), F32), pltpu.VMEM((_local_rows(dt), dt), BF16),
                        pltpu.SemaphoreType.DMA((2,))],
        compiler_params=_cparams(2),
        name="combine",
    )(meta3, meta3, eid3, gate3, x1, mods3, final_g.reshape(1, D), y_pad)


def _rope_tables(S, k_scale):
    n_freq = RET_DK // 4
    pos = jnp.arange(S, dtype=F32)
    rows = jnp.floor(pos / GRID_W)
    cols = pos - rows * GRID_W
    inv = ROPE_BASE ** (-jnp.arange(n_freq, dtype=F32) / n_freq)
    ang = jnp.concatenate([rows[:, None] * inv, cols[:, None] * inv], axis=-1)
    cos, sin = jnp.cos(ang), jnp.sin(ang)
    cos2 = jnp.concatenate([cos, cos], axis=-1)
    sin2 = jnp.concatenate([-sin, sin], axis=-1)
    return jnp.concatenate([cos2, sin2, cos2 * k_scale, sin2 * k_scale], axis=-1)


def _identity_tables(L, k_scale):
    one = jnp.ones((L, RET_DK), F32)
    zero = jnp.zeros((L, RET_DK), F32)
    return jnp.concatenate([one, zero, one * k_scale, zero], axis=-1)


def _lanes(t, width):
    return jnp.broadcast_to(t[:, :, None], t.shape + (width,))


def kernel(x, c, ctx, c_ctx, ada_w, ada_b, norm1_g, w_in, conv_w, conv_b, lru_wa, lru_ba, lru_wx, lru_bx,
           lru_lambda, w_rnn_proj, w_ret_proj, w_out, norm2_g, router_w, router_b, moe_w1, moe_b1, moe_w2,
           moe_b2, final_g):
    B, S, D = x.shape
    L = ctx.shape[1]
    N = B * S
    H = RET_HEADS
    lyr = 0
    d_in = w_in.shape[2]
    assert ada_w.shape[0] == 1 and d_in == N_CHUNKS * D and B == 8

    def pairs_apart(w):
        return jnp.swapaxes(w.reshape(D, H, RET_DK // 2, 2), 2, 3).reshape(D, H * RET_DK)

    w_in_l = w_in[lyr].astype(BF16)
    w_in_b = jnp.concatenate([w_in_l[:, :2 * D], pairs_apart(w_in_l[:, 2 * D:3 * D]),
                              pairs_apart(w_in_l[:, 3 * D:4 * D]), w_in_l[:, 4 * D:]], axis=1)
    wg = [(0.5 * jnp.concatenate([lru_wa[lyr, d], lru_wx[lyr, d]], axis=-1)).astype(BF16) for d in range(2)]
    de2 = moe_w1.shape[3]
    glu_lin = np.concatenate([np.arange(0, de2, 2), np.arange(1, de2, 2)])
    w1b = _regroup_glu_lin(moe_w1[lyr])
    b1p = moe_b1[lyr][:, glu_lin]
    w2b = moe_w2[lyr].astype(BF16)
    rwt = router_w[lyr].T
    rwt_hi = rwt.astype(BF16)
    rwt_lo = (rwt - rwt_hi.astype(F32)).astype(BF16)

    k_scale = RET_DK ** -0.5
    tab_l = _rope_tables(S, k_scale)
    tab_c = _identity_tables(B * L, k_scale)
    log_g = jnp.log1p(-jnp.exp2(-5.0 - jnp.arange(H, dtype=F32)))
    C = min(RET_C, S)
    idx = jnp.arange(C, dtype=F32)
    dec = lambda e: jnp.exp(e[None, :] * log_g[:, None])
    intra = jnp.exp(jnp.abs(idx[:, None] - idx[None, :])[None] * log_g[:, None, None])
    qdec_f = _lanes(dec(idx + 1.0), RET_DK)
    qdec_b = _lanes(dec(C - idx), RET_DK)
    kdec_f = _lanes(dec(C - 1.0 - idx), RET_DK)
    kdec_b = _lanes(dec(idx), RET_DK)
    cdec = jnp.broadcast_to(jnp.exp(C * log_g)[:, None], (H, RET_DV))
    pos_c = jnp.arange(L, dtype=F32)
    cdec_f = _lanes(dec(L - 1.0 - pos_c), RET_DK)
    cdec_b = _lanes(dec(pos_c), RET_DK)

    cvec = jnp.zeros((16, D), F32).at[:B].set(c).at[B].set(c_ctx)
    mods3 = _ada(cvec, ada_w[lyr], ada_b[lyr]).reshape(16, 1, 6 * D)

    z_c, xr_c = _proj(ctx.reshape(1, B * L, D), mods3, lambda b: B, norm1_g[lyr], w_in_b, tab_c, min(PROJ_TM, B * L))
    xr_c = xr_c.reshape(B, L, D)
    z_l, xr_l = _proj(x, mods3, lambda b: b, norm1_g[lyr], w_in_b, tab_l, min(PROJ_TM, S))

    zeros = jnp.zeros((B, D), F32)
    hs = []
    for d in range(2):
        args = (conv_w[lyr], conv_b[lyr], wg[d], 0.5 * lru_ba[lyr, d], 0.5 * lru_bx[lyr, d], lru_lambda[lyr, d])
        _, h0 = _rnn(xr_c, *args, zeros, reverse=(d == 1), tt=RNN_TT)
        h, _ = _rnn(xr_l, *args, h0, reverse=(d == 1), tt=RNN_TT)
        hs.append(h)

    s_fwd, s_bwd = _ret_ctx(z_c, B, cdec_f, cdec_b)
    sb = _ret_bwd(z_l, B, s_bwd, kdec_b, cdec)
    retg = _ret_fwd(z_l, B, sb, s_fwd, intra.astype(BF16), qdec_f.astype(BF16), qdec_b.astype(BF16),
                    kdec_f.astype(BF16), cdec)

    dt = min(DISP_TM, S)
    x1, hl2, eid3, gate3, cnt3 = _finish(
        x, hs[0], hs[1], z_l, retg, mods3, norm2_g[lyr], w_rnn_proj[lyr].astype(BF16),
        w_ret_proj[lyr].astype(BF16), w_out[lyr].astype(BF16), rwt_hi, rwt_lo, router_b[lyr], dt)

    cnt_t = cnt3[:, :, 0].astype(I32)
    cnt_t = (cnt_t + ROW_ALIGN - 1) // ROW_ALIGN * ROW_ALIGN
    cnt = jnp.sum(cnt_t, axis=0)
    padded = (cnt + MOE_BLOCK - 1) // MOE_BLOCK * MOE_BLOCK
    pad_end = jnp.cumsum(padded)
    pad_start = pad_end - padded
    gstart = pad_start[None, :] + jnp.cumsum(cnt_t, axis=0) - cnt_t
    loff = jnp.cumsum(cnt_t, axis=1) - cnt_t
    meta3 = jnp.concatenate([loff, cnt_t, gstart], axis=1).reshape(N // dt, 1, 3 * N_EXPERTS)
    zmeta = jnp.concatenate([jnp.zeros_like(cnt), padded - cnt, pad_start + cnt]).reshape(1, 3 * N_EXPERTS)
    n_blocks = -(-(N * TOP_K + (N // dt) * N_EXPERTS * (ROW_ALIGN - 1)) // MOE_BLOCK) + N_EXPERTS
    P = n_blocks * MOE_BLOCK
    blk_start = jnp.arange(n_blocks, dtype=I32) * MOE_BLOCK
    blk_expert = jnp.minimum(jnp.sum((pad_end[None, :] <= blk_start[:, None]).astype(I32), axis=1), N_EXPERTS - 1)
    n_used = (pad_end[-1:] // MOE_BLOCK).astype(I32)

    h_pad = _dispatch(hl2, eid3, meta3, zmeta, P)
    y_pad = _experts(blk_expert, n_used, h_pad, w1b, b1p, w2b, moe_b2[lyr])
    return _combine(meta3, eid3, gate3, x1, mods3, final_g, y_pad)
```

```python
import functools

import jax
import jax.numpy as jnp
import numpy as np
from jax import lax
from jax.experimental import pallas as pl
from jax.experimental.pallas import tpu as pltpu

F32 = jnp.float32
BF16 = jnp.bfloat16
I32 = jnp.int32

GRID_W = 64
RNN_BLOCKS = 8
CONV_W = 4
LRU_C = 8.0
RET_HEADS = 8
RET_DK = 128
RET_DV = 256
ROPE_BASE = 10000.0
N_EXPERTS = 32
TOP_K = 4
SWIGLU_ALPHA = 1.702
SWIGLU_LIMIT = 7.0
EPS = 1e-6
N_CHUNKS = 10

PROJ_TM = 512
PROJ_TN = 2048
RNN_TT = 128
RET_C = 256
FIN_TM = 512
MOE_BLOCK = 512
DISP_TM = 512
ROW_ALIGN = 8
VMEM_LIMIT = 56 * 1024 * 1024


def _cparams(n_axes):
    return pltpu.CompilerParams(dimension_semantics=("arbitrary",) * n_axes,
                                vmem_limit_bytes=VMEM_LIMIT)


def _sigmoid(x):
    return 0.5 * (jnp.tanh(0.5 * x) + 1.0)


def _rmsnorm(x, g):
    return x * lax.rsqrt(jnp.mean(x * x, axis=-1, keepdims=True) + EPS) * g


def _ada_kernel(c_ref, w_ref, b_ref, o_ref):
    c = c_ref[...]
    s = c * _sigmoid(c)
    o_ref[...] = jnp.dot(s, w_ref[...], preferred_element_type=F32,
                         precision=lax.Precision.HIGHEST) + b_ref[...]


def _ada(cvec, ada_w, ada_b):
    R, D = cvec.shape
    n = ada_w.shape[1] // D
    return pl.pallas_call(
        _ada_kernel,
        grid=(n,),
        in_specs=[pl.BlockSpec((R, D), lambda j: (0, 0)),
                  pl.BlockSpec((D, D), lambda j: (0, j)),
                  pl.BlockSpec((1, D), lambda j: (0, j))],
        out_specs=pl.BlockSpec((R, D), lambda j: (0, j)),
        out_shape=jax.ShapeDtypeStruct((R, n * D), F32),
        compiler_params=_cparams(1),
        name="ada",
    )(cvec, ada_w, ada_b.reshape(1, -1))


def _proj_kernel(x_ref, sh_ref, sc_ref, g_ref, w_ref, tab_ref, o_ref, xr_ref):
    D = x_ref.shape[1]
    h = _rmsnorm(x_ref[...], g_ref[...])
    hb = (h * (1.0 + sc_ref[...]) + sh_ref[...]).astype(BF16)
    for j in range(w_ref.shape[1] // PROJ_TN):
        j0 = j * PROJ_TN
        acc = jnp.dot(hb, w_ref[:, j0:j0 + PROJ_TN], preferred_element_type=F32)
        if j == 0:
            xr_ref[...] = acc[:, :D].astype(xr_ref.dtype)
        if j != 1:
            o_ref[:, j0:j0 + PROJ_TN] = acc.astype(o_ref.dtype)
            continue
        for part in range(2):
            cos = tab_ref[:, (2 * part) * RET_DK:(2 * part + 1) * RET_DK]
            sin = tab_ref[:, (2 * part + 1) * RET_DK:(2 * part + 2) * RET_DK]
            for hh in range(RET_HEADS):
                c0 = part * RET_HEADS * RET_DK + hh * RET_DK
                t = acc[:, c0:c0 + RET_DK]
                o_ref[:, j0 + c0:j0 + c0 + RET_DK] = (
                    t * cos + pltpu.roll(t, RET_DK // 2, 1) * sin).astype(o_ref.dtype)


def _proj(x3, mods3, mod_row, norm_g, w_in_b, tab, tm):
    B, S, D = x3.shape
    d_in = w_in_b.shape[1]
    return pl.pallas_call(
        _proj_kernel,
        grid=(B, S // tm),
        in_specs=[pl.BlockSpec((None, tm, D), lambda b, i: (b, i, 0)),
                  pl.BlockSpec((None, 1, D), lambda b, i: (mod_row(b), 0, 0)),
                  pl.BlockSpec((None, 1, D), lambda b, i: (mod_row(b), 0, 1)),
                  pl.BlockSpec((1, D), lambda b, i: (0, 0)),
                  pl.BlockSpec((D, d_in), lambda b, i: (0, 0), pipeline_mode=pl.Buffered(1)),
                  pl.BlockSpec((tm, 4 * RET_DK), lambda b, i: (i, 0))],
        out_specs=[pl.BlockSpec((tm, d_in), lambda b, i: (i, b)),
                   pl.BlockSpec((None, tm, D), lambda b, i: (b, i, 0))],
        out_shape=[jax.ShapeDtypeStruct((S, B * d_in), BF16),
                   jax.ShapeDtypeStruct((B, S, D), BF16)],
        compiler_params=_cparams(2),
        name="proj",
    )(x3, mods3, mods3, norm_g.reshape(1, D), w_in_b, tab)


def _rnn_kernel(xm_ref, xp_ref, xn_ref, cw_ref, cb_ref, wg_ref, ba_ref, bx_ref, lam_ref, h0_ref,
                h_ref, hfin_ref, xs_ref, a_ref, u_ref, hc_ref, *, reverse, n_tiles):
    i = pl.program_id(0)
    tile = (n_tiles - 1 - i) if reverse else i
    nb, tt, D = xm_ref.shape
    bw = D // RNN_BLOCKS
    HALO = xp_ref.shape[1]

    @pl.when(i == 0)
    def _():
        hc_ref[...] = h0_ref[...]

    def time_major(v):
        return jnp.swapaxes(v.astype(F32), 0, 1).reshape(v.shape[1] * nb, D)

    hr = HALO * nb
    R = tt * nb
    xs_ref[0:hr, :] = jnp.where(tile > 0, time_major(xp_ref[...]), 0.0)
    xs_ref[hr:hr + R, :] = time_major(xm_ref[...])
    xs_ref[hr + R:2 * hr + R, :] = jnp.where(tile < n_tiles - 1, time_major(xn_ref[...]), 0.0)

    nl = -lam_ref[...]
    csp = (0.25 * LRU_C) * (jnp.maximum(nl, 0.0) + jnp.log1p(jnp.exp(-jnp.abs(nl))))
    SUB = 256

    def gates(s, carry):
        r0 = pl.multiple_of(s * SUB, SUB)
        base = hr - 2 * nb
        xc = cb_ref[...] + cw_ref[0:1, :] * xs_ref[pl.ds(pl.multiple_of(r0 + base, nb), SUB), :]
        for k in range(1, CONV_W):
            xc = xc + cw_ref[k:k + 1, :] * xs_ref[pl.ds(pl.multiple_of(r0 + base + k * nb, nb), SUB), :]
        xb = xc.astype(BF16)
        for n in range(RNN_BLOCKS):
            g = jnp.dot(xb[:, n * bw:(n + 1) * bw], wg_ref[n], preferred_element_type=F32)
            cs = slice(n * bw, (n + 1) * bw)
            tr = jnp.tanh(g[:, :bw] + ba_ref[:, cs])
            ti = jnp.tanh(g[:, bw:] + bx_ref[:, cs])
            t = jnp.tanh(csp[:, cs] * tr + csp[:, cs])
            rc2 = 2.0 / (1.0 + t)
            a_ref[pl.ds(r0, SUB), cs] = rc2 - 1.0
            u_ref[pl.ds(r0, SUB), cs] = (jnp.sqrt(t) * rc2) * ((0.5 * ti + 0.5) * xc[:, cs])
        return carry

    lax.fori_loop(0, R // SUB, gates, 0)

    def step(t, h):
        ts = (tt - 1 - t) if reverse else t
        r0 = pl.multiple_of(ts * nb, nb)
        h = a_ref[pl.ds(r0, nb), :] * h + u_ref[pl.ds(r0, nb), :]
        u_ref[pl.ds(r0, nb), :] = h
        return h

    h = lax.fori_loop(0, tt, step, hc_ref[...], unroll=8)
    hc_ref[...] = h
    hfin_ref[...] = h
    h_ref[...] = jnp.swapaxes(u_ref[...].reshape(tt, nb, D), 0, 1).astype(h_ref.dtype)


def _rnn(xr, conv_w, conv_b, wg, ba, bx, lam, h0, *, reverse, tt):
    nb, T, D = xr.shape
    tt = min(tt, T)
    n_tiles = T // tt
    HALO = 16
    hb = tt // HALO

    def tile_of(i):
        return (n_tiles - 1 - i) if reverse else i

    kern = functools.partial(_rnn_kernel, reverse=reverse, n_tiles=n_tiles)
    return pl.pallas_call(
        kern,
        grid=(n_tiles,),
        in_specs=[pl.BlockSpec((nb, tt, D), lambda i: (0, tile_of(i), 0)),
                  pl.BlockSpec((nb, HALO, D), lambda i: (0, jnp.maximum(tile_of(i) * hb - 1, 0), 0)),
                  pl.BlockSpec((nb, HALO, D), lambda i: (0, jnp.minimum((tile_of(i) + 1) * hb, n_tiles * hb - 1), 0)),
                  pl.BlockSpec((CONV_W, D), lambda i: (0, 0)),
                  pl.BlockSpec((1, D), lambda i: (0, 0)),
                  pl.BlockSpec((RNN_BLOCKS, D // RNN_BLOCKS, 2 * D // RNN_BLOCKS), lambda i: (0, 0, 0)),
                  pl.BlockSpec((1, D), lambda i: (0, 0)),
                  pl.BlockSpec((1, D), lambda i: (0, 0)),
                  pl.BlockSpec((1, D), lambda i: (0, 0)),
                  pl.BlockSpec((nb, D), lambda i: (0, 0))],
        out_specs=[pl.BlockSpec((nb, tt, D), lambda i: (0, tile_of(i), 0)),
                   pl.BlockSpec((nb, D), lambda i: (0, 0))],
        out_shape=[jax.ShapeDtypeStruct((nb, T, D), BF16),
                   jax.ShapeDtypeStruct((nb, D), F32)],
        scratch_shapes=[pltpu.VMEM(((tt + 2 * HALO) * nb, D), F32),
                        pltpu.VMEM((nb * tt, D), F32),
                        pltpu.VMEM((nb * tt, D), F32),
                        pltpu.VMEM((nb, D), F32)],
        compiler_params=_cparams(1),
        name="rnn_bwd" if reverse else "rnn_fwd",
    )(xr, xr, xr, conv_w, conv_b.reshape(1, D), wg, ba.reshape(1, D), bx.reshape(1, D),
      lam.reshape(1, D), h0)


def _tdot(a, b):
    return lax.dot_general(a, b, (((0,), (0,)), ((), ())), preferred_element_type=F32)


def _ret_ctx_kernel(k_ref, v_ref, df_ref, db_ref, sf_ref, sb_ref):
    for hh in range(RET_HEADS):
        kh = k_ref[:, hh * RET_DK:(hh + 1) * RET_DK].astype(F32)
        vh = v_ref[:, hh * RET_DV:(hh + 1) * RET_DV]
        sf_ref[hh] = _tdot((kh * df_ref[hh]).astype(BF16), vh)
        sb_ref[hh] = _tdot((kh * db_ref[hh]).astype(BF16), vh)


def _ret_ctx(zc, B, dec_f, dec_b):
    L = zc.shape[0] // B
    H = RET_HEADS
    st = jax.ShapeDtypeStruct((B, H, RET_DK, RET_DV), F32)
    return pl.pallas_call(
        _ret_ctx_kernel,
        grid=(B,),
        in_specs=[pl.BlockSpec((L, H * RET_DK), lambda b: (b, 3)),
                  pl.BlockSpec((L, H * RET_DV), lambda b: (b, 2)),
                  pl.BlockSpec((H, L, RET_DK), lambda b: (0, 0, 0)),
                  pl.BlockSpec((H, L, RET_DK), lambda b: (0, 0, 0))],
        out_specs=[pl.BlockSpec((None, H, RET_DK, RET_DV), lambda b: (b, 0, 0, 0))] * 2,
        out_shape=[st, st],
        compiler_params=_cparams(1),
        name="ret_ctx",
    )(zc, zc, dec_f, dec_b)


def _ret_bwd_kernel(k_ref, v_ref, s0_ref, kd_ref, cd_ref, o_ref, st_ref):
    @pl.when(pl.program_id(1) == 0)
    def _():
        st_ref[...] = s0_ref[...]

    for hh in range(RET_HEADS):
        o_ref[hh] = st_ref[hh].astype(o_ref.dtype)
        kh = k_ref[:, hh * RET_DK:(hh + 1) * RET_DK].astype(F32)
        vh = v_ref[:, hh * RET_DV:(hh + 1) * RET_DV]
        st_ref[hh] = st_ref[hh] * cd_ref[hh:hh + 1, :] + _tdot((kh * kd_ref[hh]).astype(BF16), vh)


def _ret_bwd(z, B, s_bwd, kdec_b, cdec):
    S = z.shape[0]
    H, C = RET_HEADS, min(RET_C, S)
    n = S // C
    return pl.pallas_call(
        _ret_bwd_kernel,
        grid=(B, n),
        in_specs=[pl.BlockSpec((C, H * RET_DK), lambda b, j: (n - 1 - j, b * N_CHUNKS + 3)),
                  pl.BlockSpec((C, H * RET_DV), lambda b, j: (n - 1 - j, b * (N_CHUNKS // 2) + 2)),
                  pl.BlockSpec((None, H, RET_DK, RET_DV), lambda b, j: (b, 0, 0, 0)),
                  pl.BlockSpec((H, C, RET_DK), lambda b, j: (0, 0, 0)),
                  pl.BlockSpec((H, RET_DV), lambda b, j: (0, 0))],
        out_specs=pl.BlockSpec((None, None, H, RET_DK, RET_DV), lambda b, j: (b, n - 1 - j, 0, 0, 0)),
        out_shape=jax.ShapeDtypeStruct((B, n, H, RET_DK, RET_DV), BF16),
        scratch_shapes=[pltpu.VMEM((H, RET_DK, RET_DV), F32)],
        compiler_params=_cparams(2),
        name="ret_bwd",
    )(z, z, s_bwd, kdec_b, cdec)


def _ret_fwd_kernel(q_ref, k_ref, v_ref, gs_ref, sb_ref, s0_ref, intra_ref, qf_ref, qb_ref, kf_ref, cd_ref,
                    o_ref, st_ref):
    @pl.when(pl.program_id(1) == 0)
    def _():
        st_ref[...] = s0_ref[...]

    for hh in range(RET_HEADS):
        qh = q_ref[:, hh * RET_DK:(hh + 1) * RET_DK]
        kh = k_ref[:, hh * RET_DK:(hh + 1) * RET_DK]
        vh = v_ref[:, hh * RET_DV:(hh + 1) * RET_DV]
        s = lax.dot_general(qh, kh, (((1,), (1,)), ((), ())), preferred_element_type=F32)
        o = jnp.dot(s.astype(BF16) * intra_ref[hh], vh, preferred_element_type=F32)
        q2 = jnp.concatenate([qh * qf_ref[hh], qh * qb_ref[hh]], axis=1)
        s2 = jnp.concatenate([st_ref[hh].astype(BF16), sb_ref[hh]], axis=0)
        o = o + jnp.dot(q2, s2, preferred_element_type=F32)
        st_ref[hh] = st_ref[hh] * cd_ref[hh:hh + 1, :] + _tdot(kh * kf_ref[hh], vh)
        mu = jnp.mean(o, axis=-1, keepdims=True)
        d = o - mu
        var = jnp.mean(d * d, axis=-1, keepdims=True)
        g = gs_ref[:, hh * RET_DV:(hh + 1) * RET_DV]
        gate = g * _sigmoid(g)
        o_ref[:, hh * RET_DV:(hh + 1) * RET_DV] = (d * lax.rsqrt(var + EPS)).astype(o_ref.dtype) * gate


def _ret_fwd(z, B, sb, s_fwd, intra, qdec_f, qdec_b, kdec_f, cdec):
    S = z.shape[0]
    H, C = RET_HEADS, min(RET_C, S)
    n = S // C
    half = N_CHUNKS // 2
    return pl.pallas_call(
        _ret_fwd_kernel,
        grid=(B, n),
        in_specs=[pl.BlockSpec((C, H * RET_DK), lambda b, j: (j, b * N_CHUNKS + 2)),
                  pl.BlockSpec((C, H * RET_DK), lambda b, j: (j, b * N_CHUNKS + 3)),
                  pl.BlockSpec((C, H * RET_DV), lambda b, j: (j, b * half + 2)),
                  pl.BlockSpec((C, H * RET_DV), lambda b, j: (j, b * half + 3)),
                  pl.BlockSpec((None, None, H, RET_DK, RET_DV), lambda b, j: (b, j, 0, 0, 0)),
                  pl.BlockSpec((None, H, RET_DK, RET_DV), lambda b, j: (b, 0, 0, 0)),
                  pl.BlockSpec((H, C, C), lambda b, j: (0, 0, 0)),
                  pl.BlockSpec((H, C, RET_DK), lambda b, j: (0, 0, 0)),
                  pl.BlockSpec((H, C, RET_DK), lambda b, j: (0, 0, 0)),
                  pl.BlockSpec((H, C, RET_DK), lambda b, j: (0, 0, 0)),
                  pl.BlockSpec((H, RET_DV), lambda b, j: (0, 0))],
        out_specs=pl.BlockSpec((None, C, H * RET_DV), lambda b, j: (b, j, 0)),
        out_shape=jax.ShapeDtypeStruct((B, S, H * RET_DV), BF16),
        scratch_shapes=[pltpu.VMEM((H, RET_DK, RET_DV), F32)],
        compiler_params=_cparams(2),
        name="ret_fwd",
    )(z, z, z, z, sb, s_fwd, intra, qdec_f, qdec_b, kdec_f, cdec)


def _dot_t(a, b):
    return lax.dot_general(a, b, (((1,), (1,)), ((), ())), preferred_element_type=F32)


def _finish_kernel(x_ref, hf_ref, hb_ref, gr_ref, gab_ref, ret_ref, g1_ref, sh2_ref, sc2_ref, n2_ref,
                   wr_ref, wt_ref, wo_ref, rwh_ref, rwl_ref, rb_ref,
                   x1_ref, hl_ref, eid_ref, gate_ref, cnt_ref):
    D = x_ref.shape[1]
    tm = x_ref.shape[0]
    dt = eid_ref.shape[2]
    rnn = hf_ref[...].astype(F32) + hb_ref[...].astype(F32)
    y_rnn = jnp.dot((rnn * jax.nn.gelu(gr_ref[...].astype(F32))).astype(BF16), wr_ref[...],
                    preferred_element_type=F32)
    y_ret = jnp.dot(ret_ref[...], wt_ref[...], preferred_element_type=F32)
    ga = gab_ref[:, :D].astype(F32)
    gb = gab_ref[:, D:].astype(F32)
    merged = _sigmoid(ga) * y_rnn + _sigmoid(gb) * y_ret
    y = jnp.dot(merged.astype(BF16), wo_ref[...], preferred_element_type=F32)
    x1 = x_ref[...] + g1_ref[...] * y
    x1_ref[...] = x1
    hl = _rmsnorm(x1, n2_ref[...]) * (1.0 + sc2_ref[...]) + sh2_ref[...]
    hh = hl.astype(BF16)
    hl_ref[...] = hh

    hlo = (hl - hh.astype(F32)).astype(BF16)
    logits = (_dot_t(rwh_ref[...], hh) + _dot_t(rwh_ref[...], hlo) + _dot_t(rwl_ref[...], hh)) + rb_ref[...]

    ne = logits.shape[0]
    sub = lax.broadcasted_iota(I32, (ne, tm), 0)
    work = logits
    vals, idxs = [], []
    oh = jnp.zeros((ne, tm), F32)
    for _ in range(TOP_K):
        m = jnp.max(work, axis=0, keepdims=True)
        idx = jnp.min(jnp.where(work == m, sub, ne), axis=0, keepdims=True)
        hot = sub == idx
        vals.append(m)
        idxs.append(idx)
        oh = oh + jnp.where(hot, 1.0, 0.0)
        work = jnp.where(hot, -jnp.inf, work)
    es = [jnp.exp(v - vals[0]) for v in vals]
    inv = 1.0 / (es[0] + es[1] + es[2] + es[3])
    for part in range(tm // dt):
        ls = slice(part * dt, (part + 1) * dt)
        for k in range(TOP_K):
            eid_ref[part, k:k + 1, :] = idxs[k][:, ls]
            gate_ref[part, k:k + 1, :] = (es[k] * inv)[:, ls]
        cnt_ref[part] = jnp.sum(oh[:, ls], axis=1, keepdims=True)


def _finish(x3, hf, hb, z, retg, mods3, norm2_g, w_rnn_b, w_ret_b, w_out_b, rwt_hi, rwt_lo, router_b, dt):
    B, S, D = x3.shape
    tm = min(FIN_TM, S)
    nt = S // tm
    N = B * S
    ne = rwt_hi.shape[0]
    half = N_CHUNKS // 2
    per = tm // dt
    const2 = lambda b, i: (0, 0)
    tile3 = lambda b, i: (b * nt + i, 0, 0)
    return pl.pallas_call(
        _finish_kernel,
        grid=(B, nt),
        in_specs=[pl.BlockSpec((None, tm, D), lambda b, i: (b, i, 0)),
                  pl.BlockSpec((None, tm, D), lambda b, i: (b, i, 0)),
                  pl.BlockSpec((None, tm, D), lambda b, i: (b, i, 0)),
                  pl.BlockSpec((tm, D), lambda b, i: (i, b * N_CHUNKS + 1)),
                  pl.BlockSpec((tm, 2 * D), lambda b, i: (i, b * half + 4)),
                  pl.BlockSpec((None, tm, retg.shape[2]), lambda b, i: (b, i, 0)),
                  pl.BlockSpec((None, 1, D), lambda b, i: (b, 0, 2)),
                  pl.BlockSpec((None, 1, D), lambda b, i: (b, 0, 3)),
                  pl.BlockSpec((None, 1, D), lambda b, i: (b, 0, 4)),
                  pl.BlockSpec((1, D), const2),
                  pl.BlockSpec(w_rnn_b.shape, const2),
                  pl.BlockSpec(w_ret_b.shape, const2),
                  pl.BlockSpec(w_out_b.shape, const2),
                  pl.BlockSpec(rwt_hi.shape, const2),
                  pl.BlockSpec(rwt_lo.shape, const2),
                  pl.BlockSpec((ne, 1), const2)],
        out_specs=[pl.BlockSpec((None, tm, D), lambda b, i: (b, i, 0)),
                   pl.BlockSpec((tm, D), lambda b, i: (b * nt + i, 0)),
                   pl.BlockSpec((per, TOP_K, dt), tile3),
                   pl.BlockSpec((per, TOP_K, dt), tile3),
                   pl.BlockSpec((per, ne, 1), tile3)],
        out_shape=[jax.ShapeDtypeStruct((B, S, D), F32),
                   jax.ShapeDtypeStruct((N, D), BF16),
                   jax.ShapeDtypeStruct((N // dt, TOP_K, dt), I32),
                   jax.ShapeDtypeStruct((N // dt, TOP_K, dt), F32),
                   jax.ShapeDtypeStruct((N // dt, ne, 1), F32)],
        compiler_params=_cparams(2),
        name="finish",
    )(x3, hf, hb, z, z, retg, mods3, mods3, mods3, norm2_g.reshape(1, D),
      w_rnn_b, w_ret_b, w_out_b, rwt_hi, rwt_lo, router_b.reshape(ne, 1))


def _local_rows(dt):
    return TOP_K * dt + N_EXPERTS * ROW_ALIGN


def _local_slots(eid_ref):
    dt = eid_ref.shape[1]
    ne = N_EXPERTS
    sub = lax.broadcasted_iota(I32, (ne, dt), 0)
    hots = [sub == eid_ref[k:k + 1, :] for k in range(TOP_K)]
    oh = jnp.zeros((ne, dt), F32)
    for hot in hots:
        oh = oh + jnp.where(hot, 1.0, 0.0)
    earlier = jnp.where(lax.broadcasted_iota(I32, (dt, dt), 0) < lax.broadcasted_iota(I32, (dt, dt), 1), 1.0, 0.0)
    before = jnp.dot(oh.astype(BF16), earlier.astype(BF16), preferred_element_type=F32)
    cnt = jnp.broadcast_to(jnp.sum(oh, axis=1, keepdims=True), (ne, dt))
    cnt = jnp.ceil(cnt * (1.0 / ROW_ALIGN)) * ROW_ALIGN
    lower = jnp.where(lax.broadcasted_iota(I32, (ne, ne), 1) < lax.broadcasted_iota(I32, (ne, ne), 0), 1.0, 0.0)
    base = before + jnp.dot(lower.astype(BF16), cnt.astype(BF16), preferred_element_type=F32)
    return [jnp.sum(jnp.where(hot, base, 0.0), axis=0, keepdims=True).astype(I32) for hot in hots]


def _slot_matrix(slots, weights, out_ref):
    rows, dt = out_ref.shape
    ch = 64
    rel = lax.broadcasted_iota(I32, (ch, dt), 0).astype(F32).astype(BF16)
    slots_f = [s.astype(F32) for s in slots]
    weights_b = [jnp.asarray(w, F32).astype(BF16) for w in weights]
    zero = jnp.zeros((), BF16)
    for c in range(rows // ch):
        acc = None
        for s, w in zip(slots_f, weights_b):
            term = jnp.where(rel == (s - float(c * ch)).astype(BF16), w, zero)
            acc = term if acc is None else acc + term
        out_ref[c * ch:(c + 1) * ch, :] = acc


def _for_each_run(meta_ref, fn):
    def body(e, c):
        n = pl.multiple_of(meta_ref[0, N_EXPERTS + e], ROW_ALIGN)

        @pl.when(n > 0)
        def _():
            fn(pl.multiple_of(meta_ref[0, e], ROW_ALIGN), pl.multiple_of(meta_ref[0, 2 * N_EXPERTS + e], ROW_ALIGN), n)

        return c

    lax.fori_loop(0, N_EXPERTS, body, 0)


def _run_rows(meta_ref):
    last = N_EXPERTS - 1
    return pl.multiple_of(meta_ref[0, last] + meta_ref[0, N_EXPERTS + last], ROW_ALIGN)


def _dispatch_kernel(meta_ref, zmeta_ref, eid_ref, x_ref, hp_ref, sbuf_ref, zbuf_ref, pm_ref, rows_ref, sems, zsem):
    t = pl.program_id(0)
    slot = t % 2

    def zero_copy(_, g, n):
        return pltpu.make_async_copy(zbuf_ref.at[pl.ds(0, n), :], hp_ref.at[pl.ds(g, n), :], zsem)

    def run_copy(sl, l, g, n):
        return pltpu.make_async_copy(sbuf_ref.at[sl, pl.ds(l, n), :], hp_ref.at[pl.ds(g, n), :], sems.at[sl])

    @pl.when(t == 0)
    def _():
        zbuf_ref[...] = jnp.zeros_like(zbuf_ref)
        _for_each_run(zmeta_ref, lambda l, g, n: zero_copy(l, g, n).start())
        _for_each_run(zmeta_ref, lambda l, g, n: zero_copy(l, g, n).wait())

    _slot_matrix(_local_slots(eid_ref), [1.0] * TOP_K, pm_ref)
    sbuf_ref[slot] = jnp.dot(pm_ref[...], x_ref[...], preferred_element_type=F32)

    @pl.when(t > 0)
    def _():
        run_copy(1 - slot, 0, 0, pl.multiple_of(rows_ref[1 - slot], ROW_ALIGN)).wait()

    _for_each_run(meta_ref, lambda l, g, n: run_copy(slot, l, g, n).start())
    rows_ref[slot] = _run_rows(meta_ref)

    @pl.when(t == pl.num_programs(0) - 1)
    def _():
        run_copy(slot, 0, 0, _run_rows(meta_ref)).wait()


def _dispatch(hl2, eid3, meta3, zmeta, P):
    N, D = hl2.shape
    nt, _, dt = eid3.shape
    nm = meta3.shape[2]
    return pl.pallas_call(
        _dispatch_kernel,
        grid=(nt,),
        in_specs=[pl.BlockSpec((None, 1, nm), lambda i: (i, 0, 0), memory_space=pltpu.SMEM),
                  pl.BlockSpec((1, nm), lambda i: (0, 0), memory_space=pltpu.SMEM),
                  pl.BlockSpec((None, TOP_K, dt), lambda i: (i, 0, 0)),
                  pl.BlockSpec((dt, D), lambda i: (i, 0))],
        out_specs=pl.BlockSpec(memory_space=pl.ANY),
        out_shape=jax.ShapeDtypeStruct((P, D), F32),
        scratch_shapes=[pltpu.VMEM((2, _local_rows(dt), D), F32),
                        pltpu.VMEM((MOE_BLOCK, D), F32),
                        pltpu.VMEM((_local_rows(dt), dt), BF16),
                        pltpu.SMEM((2,), I32),
                        pltpu.SemaphoreType.DMA((2,)),
                        pltpu.SemaphoreType.DMA(())],
        compiler_params=_cparams(1),
        name="dispatch",
    )(meta3, zmeta, eid3, hl2)


def _regroup_kernel(w_ref, sel_ref, o_ref):
    half = w_ref.shape[1] // 2
    g = sel_ref.shape[0]
    wb = w_ref[...].astype(BF16)
    for j in range(w_ref.shape[1] // g):
        r = jnp.dot(wb[:, g * j:g * (j + 1)], sel_ref[...], preferred_element_type=F32)
        o_ref[:, (g // 2) * j:(g // 2) * (j + 1)] = r[:, :g // 2].astype(o_ref.dtype)
        o_ref[:, half + (g // 2) * j:half + (g // 2) * (j + 1)] = r[:, g // 2:].astype(o_ref.dtype)


def _regroup_glu_lin(w1):
    ne, D, de2 = w1.shape
    g = 256
    sel = np.zeros((g, g), np.float32)
    sel[np.arange(0, g, 2), np.arange(g // 2)] = 1.0
    sel[np.arange(1, g, 2), g // 2 + np.arange(g // 2)] = 1.0
    out = pl.pallas_call(
        _regroup_kernel,
        grid=(ne,),
        in_specs=[pl.BlockSpec((D, de2), lambda e: (e, 0)),
                  pl.BlockSpec((g, g), lambda e: (0, 0))],
        out_specs=pl.BlockSpec((D, de2), lambda e: (e, 0)),
        out_shape=jax.ShapeDtypeStruct((ne * D, de2), BF16),
        compiler_params=_cparams(1),
        name="regroup",
    )(w1.reshape(ne * D, de2), jnp.asarray(sel, BF16))
    return out.reshape(ne, D, de2)


def _expert_kernel(be_ref, nu_ref, x_ref, w1_ref, b1_ref, w2_ref, b2_ref, o_ref):
    del be_ref

    @pl.when(pl.program_id(0) < nu_ref[0])
    def _():
        de = w2_ref.shape[0]
        h = jnp.dot(x_ref[...].astype(BF16), w1_ref[...], preferred_element_type=F32) + b1_ref[...]
        glu = jnp.minimum(h[:, :de], SWIGLU_LIMIT)
        lin = jnp.clip(h[:, de:], -SWIGLU_LIMIT, SWIGLU_LIMIT)
        act = glu * _sigmoid(SWIGLU_ALPHA * glu) * (lin + 1.0)
        o_ref[...] = jnp.dot(act.astype(BF16), w2_ref[...], preferred_element_type=F32) + b2_ref[...]


def _experts(blk_expert, n_used, h_pad, w1b, b1p, w2b, b2):
    P, D = h_pad.shape
    ne, _, de2 = w1b.shape
    de = de2 // 2
    nblk = P // MOE_BLOCK
    blk = lambda j, be, nu: (jnp.minimum(j, nu[0] - 1), 0)
    wsel = lambda j, be, nu: (be[jnp.minimum(j, nu[0] - 1)], 0, 0)
    return pl.pallas_call(
        _expert_kernel,
        grid_spec=pltpu.PrefetchScalarGridSpec(
            num_scalar_prefetch=2,
            grid=(nblk,),
            in_specs=[pl.BlockSpec((MOE_BLOCK, D), blk),
                      pl.BlockSpec((None, D, de2), wsel),
                      pl.BlockSpec((None, 1, de2), wsel),
                      pl.BlockSpec((None, de, D), wsel),
                      pl.BlockSpec((None, 1, D), wsel)],
            out_specs=pl.BlockSpec((MOE_BLOCK, D), blk)),
        out_shape=jax.ShapeDtypeStruct((P, D), F32),
        compiler_params=_cparams(1),
        name="experts",
    )(blk_expert, n_used, h_pad, w1b, b1p.reshape(ne, 1, de2), w2b, b2.reshape(ne, 1, D))


def _combine_kernel(meta_ref, nmeta_ref, eid_ref, gate_ref, x1_ref, g2_ref, fg_ref, yp_ref, o_ref,
                    ybuf_ref, gm_ref, sems):
    t = pl.program_id(0) * pl.num_programs(1) + pl.program_id(1)
    n_tiles = pl.num_programs(0) * pl.num_programs(1)
    slot = t % 2

    def run_copy(sl, l, g, n):
        return pltpu.make_async_copy(yp_ref.at[pl.ds(g, n), :], ybuf_ref.at[sl, pl.ds(l, n), :], sems.at[sl])

    @pl.when(t == 0)
    def _():
        ybuf_ref[...] = jnp.zeros_like(ybuf_ref)
        _for_each_run(meta_ref, lambda l, g, n: run_copy(0, l, g, n).start())

    @pl.when(t + 1 < n_tiles)
    def _():
        _for_each_run(nmeta_ref, lambda l, g, n: run_copy(1 - slot, l, g, n).start())

    run_copy(slot, 0, 0, _run_rows(meta_ref)).wait()

    _slot_matrix(_local_slots(eid_ref), [gate_ref[k:k + 1, :] for k in range(TOP_K)], gm_ref)
    y = _tdot(gm_ref[...], ybuf_ref[slot].astype(BF16))
    x2 = x1_ref[...] + g2_ref[...] * y
    o_ref[...] = _rmsnorm(x2, fg_ref[...])


def _combine(meta3, eid3, gate3, x1, mods3, final_g, y_pad):
    B, S, D = x1.shape
    n_tiles, _, dt = eid3.shape
    nt = S // dt
    nm = meta3.shape[2]
    tile3 = lambda b, i: (b * nt + i, 0, 0)
    next3 = lambda b, i: (jnp.minimum(b * nt + i + 1, n_tiles - 1), 0, 0)
    return pl.pallas_call(
        _combine_kernel,
        grid=(B, nt),
        in_specs=[pl.BlockSpec((None, 1, nm), tile3, memory_space=pltpu.SMEM),
                  pl.BlockSpec((None, 1, nm), next3, memory_space=pltpu.SMEM),
                  pl.BlockSpec((None, TOP_K, dt), tile3),
                  pl.BlockSpec((None, TOP_K, dt), tile3),
                  pl.BlockSpec((None, dt, D), lambda b, i: (b, i, 0)),
                  pl.BlockSpec((None, 1, D), lambda b, i: (b, 0, 5)),
                  pl.BlockSpec((1, D), lambda b, i: (0, 0)),
                  pl.BlockSpec(memory_space=pl.ANY)],
        out_specs=pl.BlockSpec((None, dt, D), lambda b, i: (b, i, 0)),
        out_shape=jax.ShapeDtypeStruct((B, S, D), F32),
        scratch_shapes=[pltpu.VMEM((2, _local_rows(dt), D), F32), pltpu.VMEM((_local_rows(dt), dt), BF16),
                        pltpu.SemaphoreType.DMA((2,))],
        compiler_params=_cparams(2),
        name="combine",
    )(meta3, meta3, eid3, gate3, x1, mods3, final_g.reshape(1, D), y_pad)


def _rope_tables(S, k_scale):
    n_freq = RET_DK // 4
    pos = jnp.arange(S, dtype=F32)
    rows = jnp.floor(pos / GRID_W)
    cols = pos - rows * GRID_W
    inv = ROPE_BASE ** (-jnp.arange(n_freq, dtype=F32) / n_freq)
    ang = jnp.concatenate([rows[:, None] * inv, cols[:, None] * inv], axis=-1)
    cos, sin = jnp.cos(ang), jnp.sin(ang)
    cos2 = jnp.concatenate([cos, cos], axis=-1)
    sin2 = jnp.concatenate([-sin, sin], axis=-1)
    return jnp.concatenate([cos2, sin2, cos2 * k_scale, sin2 * k_scale], axis=-1)


def _identity_tables(L, k_scale):
    one = jnp.ones((L, RET_DK), F32)
    zero = jnp.zeros((L, RET_DK), F32)
    return jnp.concatenate([one, zero, one * k_scale, zero], axis=-1)


def _lanes(t, width):
    return jnp.broadcast_to(t[:, :, None], t.shape + (width,))


def kernel(x, c, ctx, c_ctx, ada_w, ada_b, norm1_g, w_in, conv_w, conv_b, lru_wa, lru_ba, lru_wx, lru_bx,
           lru_lambda, w_rnn_proj, w_ret_proj, w_out, norm2_g, router_w, router_b, moe_w1, moe_b1, moe_w2,
           moe_b2, final_g):
    B, S, D = x.shape
    L = ctx.shape[1]
    N = B * S
    H = RET_HEADS
    lyr = 0
    d_in = w_in.shape[2]
    assert ada_w.shape[0] == 1 and d_in == N_CHUNKS * D and B == 8

    def pairs_apart(w):
        return jnp.swapaxes(w.reshape(D, H, RET_DK // 2, 2), 2, 3).reshape(D, H * RET_DK)

    w_in_l = w_in[lyr].astype(BF16)
    w_in_b = jnp.concatenate([w_in_l[:, :2 * D], pairs_apart(w_in_l[:, 2 * D:3 * D]),
                              pairs_apart(w_in_l[:, 3 * D:4 * D]), w_in_l[:, 4 * D:]], axis=1)
    wg = [(0.5 * jnp.concatenate([lru_wa[lyr, d], lru_wx[lyr, d]], axis=-1)).astype(BF16) for d in range(2)]
    de2 = moe_w1.shape[3]
    glu_lin = np.concatenate([np.arange(0, de2, 2), np.arange(1, de2, 2)])
    w1b = _regroup_glu_lin(moe_w1[lyr])
    b1p = moe_b1[lyr][:, glu_lin]
    w2b = moe_w2[lyr].astype(BF16)
    rwt = router_w[lyr].T
    rwt_hi = rwt.astype(BF16)
    rwt_lo = (rwt - rwt_hi.astype(F32)).astype(BF16)

    k_scale = RET_DK ** -0.5
    tab_l = _rope_tables(S, k_scale)
    tab_c = _identity_tables(B * L, k_scale)
    log_g = jnp.log1p(-jnp.exp2(-5.0 - jnp.arange(H, dtype=F32)))
    C = min(RET_C, S)
    idx = jnp.arange(C, dtype=F32)
    dec = lambda e: jnp.exp(e[None, :] * log_g[:, None])
    intra = jnp.exp(jnp.abs(idx[:, None] - idx[None, :])[None] * log_g[:, None, None])
    qdec_f = _lanes(dec(idx + 1.0), RET_DK)
    qdec_b = _lanes(dec(C - idx), RET_DK)
    kdec_f = _lanes(dec(C - 1.0 - idx), RET_DK)
    kdec_b = _lanes(dec(idx), RET_DK)
    cdec = jnp.broadcast_to(jnp.exp(C * log_g)[:, None], (H, RET_DV))
    pos_c = jnp.arange(L, dtype=F32)
    cdec_f = _lanes(dec(L - 1.0 - pos_c), RET_DK)
    cdec_b = _lanes(dec(pos_c), RET_DK)

    cvec = jnp.zeros((16, D), F32).at[:B].set(c).at[B].set(c_ctx)
    mods3 = _ada(cvec, ada_w[lyr], ada_b[lyr]).reshape(16, 1, 6 * D)

    z_c, xr_c = _proj(ctx.reshape(1, B * L, D), mods3, lambda b: B, norm1_g[lyr], w_in_b, tab_c, min(PROJ_TM, B * L))
    xr_c = xr_c.reshape(B, L, D)
    z_l, xr_l = _proj(x, mods3, lambda b: b, norm1_g[lyr], w_in_b, tab_l, min(PROJ_TM, S))

    zeros = jnp.zeros((B, D), F32)
    hs = []
    for d in range(2):
        args = (conv_w[lyr], conv_b[lyr], wg[d], 0.5 * lru_ba[lyr, d], 0.5 * lru_bx[lyr, d], lru_lambda[lyr, d])
        _, h0 = _rnn(xr_c, *args, zeros, reverse=(d == 1), tt=RNN_TT)
        h, _ = _rnn(xr_l, *args, h0, reverse=(d == 1), tt=RNN_TT)
        hs.append(h)

    s_fwd, s_bwd = _ret_ctx(z_c, B, cdec_f, cdec_b)
    sb = _ret_bwd(z_l, B, s_bwd, kdec_b, cdec)
    retg = _ret_fwd(z_l, B, sb, s_fwd, intra.astype(BF16), qdec_f.astype(BF16), qdec_b.astype(BF16),
                    kdec_f.astype(BF16), cdec)

    dt = min(DISP_TM, S)
    x1, hl2, eid3, gate3, cnt3 = _finish(
        x, hs[0], hs[1], z_l, retg, mods3, norm2_g[lyr], w_rnn_proj[lyr].astype(BF16),
        w_ret_proj[lyr].astype(BF16), w_out[lyr].astype(BF16), rwt_hi, rwt_lo, router_b[lyr], dt)

    cnt_t = cnt3[:, :, 0].astype(I32)
    cnt_t = (cnt_t + ROW_ALIGN - 1) // ROW_ALIGN * ROW_ALIGN
    cnt = jnp.sum(cnt_t, axis=0)
    padded = (cnt + MOE_BLOCK - 1) // MOE_BLOCK * MOE_BLOCK
    pad_end = jnp.cumsum(padded)
    pad_start = pad_end - padded
    gstart = pad_start[None, :] + jnp.cumsum(cnt_t, axis=0) - cnt_t
    loff = jnp.cumsum(cnt_t, axis=1) - cnt_t
    meta3 = jnp.concatenate([loff, cnt_t, gstart], axis=1).reshape(N // dt, 1, 3 * N_EXPERTS)
    zmeta = jnp.concatenate([jnp.zeros_like(cnt), padded - cnt, pad_start + cnt]).reshape(1, 3 * N_EXPERTS)
    n_blocks = -(-(N * TOP_K + (N // dt) * N_EXPERTS * (ROW_ALIGN - 1)) // MOE_BLOCK) + N_EXPERTS
    P = n_blocks * MOE_BLOCK
    blk_start = jnp.arange(n_blocks, dtype=I32) * MOE_BLOCK
    blk_expert = jnp.minimum(jnp.sum((pad_end[None, :] <= blk_start[:, None]).astype(I32), axis=1), N_EXPERTS - 1)
    n_used = (pad_end[-1:] // MOE_BLOCK).astype(I32)

    h_pad = _dispatch(hl2, eid3, meta3, zmeta, P)
    y_pad = _experts(blk_expert, n_used, h_pad, w1b, b1p, w2b, moe_b2[lyr])
    return _combine(meta3, eid3, gate3, x1, mods3, final_g, y_pad)
```

```python
import functools

import jax
import jax.numpy as jnp
import numpy as np
from jax import lax
from jax.experimental import pallas as pl
from jax.experimental.pallas import tpu as pltpu

F32 = jnp.float32
BF16 = jnp.bfloat16
I32 = jnp.int32

GRID_W = 64
RNN_BLOCKS = 8
CONV_W = 4
LRU_C = 8.0
RET_HEADS = 8
RET_DK = 128
RET_DV = 256
ROPE_BASE = 10000.0
N_EXPERTS = 32
TOP_K = 4
SWIGLU_ALPHA = 1.702
SWIGLU_LIMIT = 7.0
EPS = 1e-6
N_CHUNKS = 10

PROJ_TM = 512
PROJ_TN = 2048
RNN_TT = 128
RET_C = 256
RET_CPS = 4
FIN_TM = 512
MOE_BLOCK = 512
DISP_TM = 512
ROW_ALIGN = 8
VMEM_LIMIT = 56 * 1024 * 1024


def _cparams(n_axes):
    return pltpu.CompilerParams(dimension_semantics=("arbitrary",) * n_axes,
                                vmem_limit_bytes=VMEM_LIMIT)


def _sigmoid(x):
    return 0.5 * (jnp.tanh(0.5 * x) + 1.0)


def _rmsnorm(x, g):
    return x * lax.rsqrt(jnp.mean(x * x, axis=-1, keepdims=True) + EPS) * g


def _ada_kernel(c_ref, w_ref, b_ref, o_ref):
    c = c_ref[...]
    s = c * _sigmoid(c)
    o_ref[...] = jnp.dot(s, w_ref[...], preferred_element_type=F32,
                         precision=lax.Precision.HIGHEST) + b_ref[...]


def _ada(cvec, ada_w, ada_b):
    R, D = cvec.shape
    n = ada_w.shape[1] // D
    return pl.pallas_call(
        _ada_kernel,
        grid=(n,),
        in_specs=[pl.BlockSpec((R, D), lambda j: (0, 0)),
                  pl.BlockSpec((D, D), lambda j: (0, j)),
                  pl.BlockSpec((1, D), lambda j: (0, j))],
        out_specs=pl.BlockSpec((R, D), lambda j: (0, j)),
        out_shape=jax.ShapeDtypeStruct((R, n * D), F32),
        compiler_params=_cparams(1),
        name="ada",
    )(cvec, ada_w, ada_b.reshape(1, -1))


def _proj_kernel(x_ref, sh_ref, sc_ref, g_ref, w_ref, tab_ref, o_ref, xr_ref):
    D = x_ref.shape[1]
    h = _rmsnorm(x_ref[...], g_ref[...])
    hb = (h * (1.0 + sc_ref[...]) + sh_ref[...]).astype(BF16)
    for j in range(w_ref.shape[1] // PROJ_TN):
        j0 = j * PROJ_TN
        acc = jnp.dot(hb, w_ref[:, j0:j0 + PROJ_TN], preferred_element_type=F32)
        if j == 0:
            xr_ref[...] = acc[:, :D].astype(xr_ref.dtype)
        if j != 1:
            o_ref[:, j0:j0 + PROJ_TN] = acc.astype(o_ref.dtype)
            continue
        for part in range(2):
            cos = tab_ref[:, (2 * part) * RET_DK:(2 * part + 1) * RET_DK]
            sin = tab_ref[:, (2 * part + 1) * RET_DK:(2 * part + 2) * RET_DK]
            for hh in range(RET_HEADS):
                c0 = part * RET_HEADS * RET_DK + hh * RET_DK
                t = acc[:, c0:c0 + RET_DK]
                o_ref[:, j0 + c0:j0 + c0 + RET_DK] = (
                    t * cos + pltpu.roll(t, RET_DK // 2, 1) * sin).astype(o_ref.dtype)


def _proj(x3, mods3, mod_row, norm_g, w_in_b, tab, tm):
    B, S, D = x3.shape
    d_in = w_in_b.shape[1]
    return pl.pallas_call(
        _proj_kernel,
        grid=(B, S // tm),
        in_specs=[pl.BlockSpec((None, tm, D), lambda b, i: (b, i, 0)),
                  pl.BlockSpec((None, 1, D), lambda b, i: (mod_row(b), 0, 0)),
                  pl.BlockSpec((None, 1, D), lambda b, i: (mod_row(b), 0, 1)),
                  pl.BlockSpec((1, D), lambda b, i: (0, 0)),
                  pl.BlockSpec((D, d_in), lambda b, i: (0, 0), pipeline_mode=pl.Buffered(1)),
                  pl.BlockSpec((tm, 4 * RET_DK), lambda b, i: (i, 0))],
        out_specs=[pl.BlockSpec((tm, d_in), lambda b, i: (i, b)),
                   pl.BlockSpec((None, tm, D), lambda b, i: (b, i, 0))],
        out_shape=[jax.ShapeDtypeStruct((S, B * d_in), BF16),
                   jax.ShapeDtypeStruct((B, S, D), BF16)],
        compiler_params=_cparams(2),
        name="proj",
    )(x3, mods3, mods3, norm_g.reshape(1, D), w_in_b, tab)


def _rnn_kernel(xm_ref, xp_ref, xn_ref, cw_ref, cb_ref, wg_ref, ba_ref, bx_ref, lam_ref, h0_ref,
                h_ref, hfin_ref, xs_ref, a_ref, u_ref, hc_ref, *, reverse, n_tiles):
    i = pl.program_id(0)
    tile = (n_tiles - 1 - i) if reverse else i
    nb, tt, D = xm_ref.shape
    bw = D // RNN_BLOCKS
    HALO = xp_ref.shape[1]

    @pl.when(i == 0)
    def _():
        hc_ref[...] = h0_ref[...]

    def time_major(v):
        return jnp.swapaxes(v.astype(F32), 0, 1).reshape(v.shape[1] * nb, D)

    hr = HALO * nb
    R = tt * nb
    xs_ref[0:hr, :] = jnp.where(tile > 0, time_major(xp_ref[...]), 0.0)
    xs_ref[hr:hr + R, :] = time_major(xm_ref[...])
    xs_ref[hr + R:2 * hr + R, :] = jnp.where(tile < n_tiles - 1, time_major(xn_ref[...]), 0.0)

    nl = -lam_ref[...]
    csp = (0.25 * LRU_C) * (jnp.maximum(nl, 0.0) + jnp.log1p(jnp.exp(-jnp.abs(nl))))
    SUB = 256

    def gates(s, carry):
        r0 = pl.multiple_of(s * SUB, SUB)
        base = hr - 2 * nb
        xc = cb_ref[...] + cw_ref[0:1, :] * xs_ref[pl.ds(pl.multiple_of(r0 + base, nb), SUB), :]
        for k in range(1, CONV_W):
            xc = xc + cw_ref[k:k + 1, :] * xs_ref[pl.ds(pl.multiple_of(r0 + base + k * nb, nb), SUB), :]
        xb = xc.astype(BF16)
        for n in range(RNN_BLOCKS):
            g = jnp.dot(xb[:, n * bw:(n + 1) * bw], wg_ref[n], preferred_element_type=F32)
            cs = slice(n * bw, (n + 1) * bw)
            tr = jnp.tanh(g[:, :bw] + ba_ref[:, cs])
            ti = jnp.tanh(g[:, bw:] + bx_ref[:, cs])
            t = jnp.tanh(csp[:, cs] * tr + csp[:, cs])
            rc = 1.0 / (1.0 + t)
            a_ref[pl.ds(r0, SUB), cs] = (1.0 - t) * rc
            root = t * lax.rsqrt(jnp.maximum(t, 1e-30))
            u_ref[pl.ds(r0, SUB), cs] = (root * rc) * ((ti + 1.0) * xc[:, cs])
        return carry

    lax.fori_loop(0, R // SUB, gates, 0)

    def step(t, h):
        ts = (tt - 1 - t) if reverse else t
        r0 = pl.multiple_of(ts * nb, nb)
        h = a_ref[pl.ds(r0, nb), :] * h + u_ref[pl.ds(r0, nb), :]
        u_ref[pl.ds(r0, nb), :] = h
        return h

    h = lax.fori_loop(0, tt, step, hc_ref[...], unroll=8)
    hc_ref[...] = h
    hfin_ref[...] = h
    h_ref[...] = jnp.swapaxes(u_ref[...].reshape(tt, nb, D), 0, 1).astype(h_ref.dtype)


def _rnn(xr, conv_w, conv_b, wg, ba, bx, lam, h0, *, reverse, tt):
    nb, T, D = xr.shape
    tt = min(tt, T)
    n_tiles = T // tt
    HALO = 16
    hb = tt // HALO

    def tile_of(i):
        return (n_tiles - 1 - i) if reverse else i

    kern = functools.partial(_rnn_kernel, reverse=reverse, n_tiles=n_tiles)
    return pl.pallas_call(
        kern,
        grid=(n_tiles,),
        in_specs=[pl.BlockSpec((nb, tt, D), lambda i: (0, tile_of(i), 0)),
                  pl.BlockSpec((nb, HALO, D), lambda i: (0, jnp.maximum(tile_of(i) * hb - 1, 0), 0)),
                  pl.BlockSpec((nb, HALO, D), lambda i: (0, jnp.minimum((tile_of(i) + 1) * hb, n_tiles * hb - 1), 0)),
                  pl.BlockSpec((CONV_W, D), lambda i: (0, 0)),
                  pl.BlockSpec((1, D), lambda i: (0, 0)),
                  pl.BlockSpec((RNN_BLOCKS, D // RNN_BLOCKS, 2 * D // RNN_BLOCKS), lambda i: (0, 0, 0)),
                  pl.BlockSpec((1, D), lambda i: (0, 0)),
                  pl.BlockSpec((1, D), lambda i: (0, 0)),
                  pl.BlockSpec((1, D), lambda i: (0, 0)),
                  pl.BlockSpec((nb, D), lambda i: (0, 0))],
        out_specs=[pl.BlockSpec((nb, tt, D), lambda i: (0, tile_of(i), 0)),
                   pl.BlockSpec((nb, D), lambda i: (0, 0))],
        out_shape=[jax.ShapeDtypeStruct((nb, T, D), BF16),
                   jax.ShapeDtypeStruct((nb, D), F32)],
        scratch_shapes=[pltpu.VMEM(((tt + 2 * HALO) * nb, D), F32),
                        pltpu.VMEM((nb * tt, D), F32),
                        pltpu.VMEM((nb * tt, D), F32),
                        pltpu.VMEM((nb, D), F32)],
        compiler_params=_cparams(1),
        name="rnn_bwd" if reverse else "rnn_fwd",
    )(xr, xr, xr, conv_w, conv_b.reshape(1, D), wg, ba.reshape(1, D), bx.reshape(1, D),
      lam.reshape(1, D), h0)


def _tdot(a, b):
    return lax.dot_general(a, b, (((0,), (0,)), ((), ())), preferred_element_type=F32)


def _ret_ctx_kernel(k_ref, v_ref, df_ref, db_ref, sf_ref, sb_ref):
    for hh in range(RET_HEADS):
        kh = k_ref[:, hh * RET_DK:(hh + 1) * RET_DK].astype(F32)
        vh = v_ref[:, hh * RET_DV:(hh + 1) * RET_DV]
        sf_ref[hh] = _tdot((kh * df_ref[hh]).astype(BF16), vh)
        sb_ref[hh] = _tdot((kh * db_ref[hh]).astype(BF16), vh)


def _ret_ctx(zc, B, dec_f, dec_b):
    L = zc.shape[0] // B
    H = RET_HEADS
    st = jax.ShapeDtypeStruct((B, H, RET_DK, RET_DV), F32)
    return pl.pallas_call(
        _ret_ctx_kernel,
        grid=(B,),
        in_specs=[pl.BlockSpec((L, H * RET_DK), lambda b: (b, 3)),
                  pl.BlockSpec((L, H * RET_DV), lambda b: (b, 2)),
                  pl.BlockSpec((H, L, RET_DK), lambda b: (0, 0, 0)),
                  pl.BlockSpec((H, L, RET_DK), lambda b: (0, 0, 0))],
        out_specs=[pl.BlockSpec((None, H, RET_DK, RET_DV), lambda b: (b, 0, 0, 0))] * 2,
        out_shape=[st, st],
        compiler_params=_cparams(1),
        name="ret_ctx",
    )(zc, zc, dec_f, dec_b)


def _ret_bwd_kernel(k_ref, v_ref, s0_ref, kd_ref, cd_ref, o_ref, st_ref):
    @pl.when(pl.program_id(1) == 0)
    def _():
        st_ref[...] = s0_ref[...]

    C = kd_ref.shape[1]
    for cc in reversed(range(k_ref.shape[0] // C)):
        rs = slice(cc * C, (cc + 1) * C)
        for hh in range(RET_HEADS):
            o_ref[cc, hh] = st_ref[hh].astype(o_ref.dtype)
            kh = k_ref[rs, hh * RET_DK:(hh + 1) * RET_DK].astype(F32)
            vh = v_ref[rs, hh * RET_DV:(hh + 1) * RET_DV]
            st_ref[hh] = st_ref[hh] * cd_ref[hh:hh + 1, :] + _tdot((kh * kd_ref[hh]).astype(BF16), vh)


def _ret_bwd(z, B, s_bwd, kdec_b, cdec):
    S = z.shape[0]
    H, C = RET_HEADS, min(RET_C, S)
    cps = min(RET_CPS, S // C)
    n = S // (C * cps)
    return pl.pallas_call(
        _ret_bwd_kernel,
        grid=(B, n),
        in_specs=[pl.BlockSpec((cps * C, H * RET_DK), lambda b, j: (n - 1 - j, b * N_CHUNKS + 3)),
                  pl.BlockSpec((cps * C, H * RET_DV), lambda b, j: (n - 1 - j, b * (N_CHUNKS // 2) + 2)),
                  pl.BlockSpec((None, H, RET_DK, RET_DV), lambda b, j: (b, 0, 0, 0)),
                  pl.BlockSpec((H, C, RET_DK), lambda b, j: (0, 0, 0)),
                  pl.BlockSpec((H, RET_DV), lambda b, j: (0, 0))],
        out_specs=pl.BlockSpec((None, cps, H, RET_DK, RET_DV), lambda b, j: (b, n - 1 - j, 0, 0, 0)),
        out_shape=jax.ShapeDtypeStruct((B, n * cps, H, RET_DK, RET_DV), BF16),
        scratch_shapes=[pltpu.VMEM((H, RET_DK, RET_DV), F32)],
        compiler_params=_cparams(2),
        name="ret_bwd",
    )(z, z, s_bwd, kdec_b, cdec)


def _ret_fwd_kernel(q_ref, k_ref, v_ref, gs_ref, sb_ref, s0_ref, intra_ref, qf_ref, qb_ref, kf_ref, cd_ref,
                    o_ref, st_ref):
    @pl.when(pl.program_id(1) == 0)
    def _():
        st_ref[...] = s0_ref[...]

    C = intra_ref.shape[1]
    for cc in range(q_ref.shape[0] // C):
        rs = slice(cc * C, (cc + 1) * C)
        for hh in range(RET_HEADS):
            qh = q_ref[rs, hh * RET_DK:(hh + 1) * RET_DK]
            kh = k_ref[rs, hh * RET_DK:(hh + 1) * RET_DK]
            vh = v_ref[rs, hh * RET_DV:(hh + 1) * RET_DV]
            s = lax.dot_general(qh, kh, (((1,), (1,)), ((), ())), preferred_element_type=F32)
            o = jnp.dot(s.astype(BF16) * intra_ref[hh], vh, preferred_element_type=F32)
            q2 = jnp.concatenate([qh * qf_ref[hh], qh * qb_ref[hh]], axis=1)
            s2 = jnp.concatenate([st_ref[hh].astype(BF16), sb_ref[cc, hh]], axis=0)
            o = o + jnp.dot(q2, s2, preferred_element_type=F32)
            st_ref[hh] = st_ref[hh] * cd_ref[hh:hh + 1, :] + _tdot(kh * kf_ref[hh], vh)
            mu = jnp.mean(o, axis=-1, keepdims=True)
            d = o - mu
            var = jnp.mean(d * d, axis=-1, keepdims=True)
            g = gs_ref[rs, hh * RET_DV:(hh + 1) * RET_DV]
            gate = g * _sigmoid(g)
            o_ref[rs, hh * RET_DV:(hh + 1) * RET_DV] = (d * lax.rsqrt(var + EPS)).astype(o_ref.dtype) * gate


def _ret_fwd(z, B, sb, s_fwd, intra, qdec_f, qdec_b, kdec_f, cdec):
    S = z.shape[0]
    H, C = RET_HEADS, min(RET_C, S)
    cps = min(RET_CPS, S // C)
    n = S // (C * cps)
    half = N_CHUNKS // 2
    return pl.pallas_call(
        _ret_fwd_kernel,
        grid=(B, n),
        in_specs=[pl.BlockSpec((cps * C, H * RET_DK), lambda b, j: (j, b * N_CHUNKS + 2)),
                  pl.BlockSpec((cps * C, H * RET_DK), lambda b, j: (j, b * N_CHUNKS + 3)),
                  pl.BlockSpec((cps * C, H * RET_DV), lambda b, j: (j, b * half + 2)),
                  pl.BlockSpec((cps * C, H * RET_DV), lambda b, j: (j, b * half + 3)),
                  pl.BlockSpec((None, cps, H, RET_DK, RET_DV), lambda b, j: (b, j, 0, 0, 0)),
                  pl.BlockSpec((None, H, RET_DK, RET_DV), lambda b, j: (b, 0, 0, 0)),
                  pl.BlockSpec((H, C, C), lambda b, j: (0, 0, 0)),
                  pl.BlockSpec((H, C, RET_DK), lambda b, j: (0, 0, 0)),
                  pl.BlockSpec((H, C, RET_DK), lambda b, j: (0, 0, 0)),
                  pl.BlockSpec((H, C, RET_DK), lambda b, j: (0, 0, 0)),
                  pl.BlockSpec((H, RET_DV), lambda b, j: (0, 0))],
        out_specs=pl.BlockSpec((None, cps * C, H * RET_DV), lambda b, j: (b, j, 0)),
        out_shape=jax.ShapeDtypeStruct((B, S, H * RET_DV), BF16),
        scratch_shapes=[pltpu.VMEM((H, RET_DK, RET_DV), F32)],
        compiler_params=_cparams(2),
        name="ret_fwd",
    )(z, z, z, z, sb, s_fwd, intra, qdec_f, qdec_b, kdec_f, cdec)


def _dot_t(a, b):
    return lax.dot_general(a, b, (((1,), (1,)), ((), ())), preferred_element_type=F32)


def _finish_kernel(x_ref, hf_ref, hb_ref, gr_ref, gab_ref, ret_ref, g1_ref, sh2_ref, sc2_ref, n2_ref,
                   wr_ref, wt_ref, wo_ref, rwh_ref, rwl_ref, rb_ref,
                   x1_ref, hl_ref, eid_ref, gate_ref, cnt_ref):
    D = x_ref.shape[1]
    tm = x_ref.shape[0]
    dt = eid_ref.shape[2]
    rnn = hf_ref[...].astype(F32) + hb_ref[...].astype(F32)
    y_rnn = jnp.dot((rnn * jax.nn.gelu(gr_ref[...].astype(F32))).astype(BF16), wr_ref[...],
                    preferred_element_type=F32)
    y_ret = jnp.dot(ret_ref[...], wt_ref[...], preferred_element_type=F32)
    ga = gab_ref[:, :D].astype(F32)
    gb = gab_ref[:, D:].astype(F32)
    merged = _sigmoid(ga) * y_rnn + _sigmoid(gb) * y_ret
    y = jnp.dot(merged.astype(BF16), wo_ref[...], preferred_element_type=F32)
    x1 = x_ref[...] + g1_ref[...] * y
    x1_ref[...] = x1
    hl = _rmsnorm(x1, n2_ref[...]) * (1.0 + sc2_ref[...]) + sh2_ref[...]
    hh = hl.astype(BF16)
    hl_ref[...] = hh

    hlo = (hl - hh.astype(F32)).astype(BF16)
    logits = (_dot_t(rwh_ref[...], hh) + _dot_t(rwh_ref[...], hlo) + _dot_t(rwl_ref[...], hh)) + rb_ref[...]

    ne = logits.shape[0]
    sub = lax.broadcasted_iota(I32, (ne, tm), 0)
    work = logits
    vals, idxs = [], []
    oh = jnp.zeros((ne, tm), F32)
    for _ in range(TOP_K):
        m = jnp.max(work, axis=0, keepdims=True)
        idx = jnp.min(jnp.where(work == m, sub, ne), axis=0, keepdims=True)
        hot = sub == idx
        vals.append(m)
        idxs.append(idx)
        oh = oh + jnp.where(hot, 1.0, 0.0)
        work = jnp.where(hot, -jnp.inf, work)
    es = [jnp.exp(v - vals[0]) for v in vals]
    inv = 1.0 / (es[0] + es[1] + es[2] + es[3])
    for part in range(tm // dt):
        ls = slice(part * dt, (part + 1) * dt)
        for k in range(TOP_K):
            eid_ref[part, k:k + 1, :] = idxs[k][:, ls]
            gate_ref[part, k:k + 1, :] = (es[k] * inv)[:, ls]
        cnt_ref[part] = jnp.sum(oh[:, ls], axis=1, keepdims=True)


def _finish(x3, hf, hb, z, retg, mods3, norm2_g, w_rnn_b, w_ret_b, w_out_b, rwt_hi, rwt_lo, router_b, dt):
    B, S, D = x3.shape
    tm = min(FIN_TM, S)
    nt = S // tm
    N = B * S
    ne = rwt_hi.shape[0]
    half = N_CHUNKS // 2
    per = tm // dt
    const2 = lambda b, i: (0, 0)
    tile3 = lambda b, i: (b * nt + i, 0, 0)
    return pl.pallas_call(
        _finish_kernel,
        grid=(B, nt),
        in_specs=[pl.BlockSpec((None, tm, D), lambda b, i: (b, i, 0)),
                  pl.BlockSpec((None, tm, D), lambda b, i: (b, i, 0)),
                  pl.BlockSpec((None, tm, D), lambda b, i: (b, i, 0)),
                  pl.BlockSpec((tm, D), lambda b, i: (i, b * N_CHUNKS + 1)),
                  pl.BlockSpec((tm, 2 * D), lambda b, i: (i, b * half + 4)),
                  pl.BlockSpec((None, tm, retg.shape[2]), lambda b, i: (b, i, 0)),
                  pl.BlockSpec((None, 1, D), lambda b, i: (b, 0, 2)),
                  pl.BlockSpec((None, 1, D), lambda b, i: (b, 0, 3)),
                  pl.BlockSpec((None, 1, D), lambda b, i: (b, 0, 4)),
                  pl.BlockSpec((1, D), const2),
                  pl.BlockSpec(w_rnn_b.shape, const2),
                  pl.BlockSpec(w_ret_b.shape, const2),
                  pl.BlockSpec(w_out_b.shape, const2),
                  pl.BlockSpec(rwt_hi.shape, const2),
                  pl.BlockSpec(rwt_lo.shape, const2),
                  pl.BlockSpec((ne, 1), const2)],
        out_specs=[pl.BlockSpec((None, tm, D), lambda b, i: (b, i, 0)),
                   pl.BlockSpec((tm, D), lambda b, i: (b * nt + i, 0)),
                   pl.BlockSpec((per, TOP_K, dt), tile3),
                   pl.BlockSpec((per, TOP_K, dt), tile3),
                   pl.BlockSpec((per, ne, 1), tile3)],
        out_shape=[jax.ShapeDtypeStruct((B, S, D), F32),
                   jax.ShapeDtypeStruct((N, D), BF16),
                   jax.ShapeDtypeStruct((N // dt, TOP_K, dt), I32),
                   jax.ShapeDtypeStruct((N // dt, TOP_K, dt), F32),
                   jax.ShapeDtypeStruct((N // dt, ne, 1), F32)],
        compiler_params=_cparams(2),
        name="finish",
    )(x3, hf, hb, z, z, retg, mods3, mods3, mods3, norm2_g.reshape(1, D),
      w_rnn_b, w_ret_b, w_out_b, rwt_hi, rwt_lo, router_b.reshape(ne, 1))


def _local_rows(dt):
    return TOP_K * dt + N_EXPERTS * ROW_ALIGN


def _local_slots(eid_ref):
    dt = eid_ref.shape[1]
    ne = N_EXPERTS
    sub = lax.broadcasted_iota(I32, (ne, dt), 0)
    hots = [sub == eid_ref[k:k + 1, :] for k in range(TOP_K)]
    oh = jnp.zeros((ne, dt), F32)
    for hot in hots:
        oh = oh + jnp.where(hot, 1.0, 0.0)
    earlier = jnp.where(lax.broadcasted_iota(I32, (dt, dt), 0) < lax.broadcasted_iota(I32, (dt, dt), 1), 1.0, 0.0)
    before = jnp.dot(oh.astype(BF16), earlier.astype(BF16), preferred_element_type=F32)
    cnt = jnp.broadcast_to(jnp.sum(oh, axis=1, keepdims=True), (ne, dt))
    cnt = jnp.ceil(cnt * (1.0 / ROW_ALIGN)) * ROW_ALIGN
    lower = jnp.where(lax.broadcasted_iota(I32, (ne, ne), 1) < lax.broadcasted_iota(I32, (ne, ne), 0), 1.0, 0.0)
    base = before + jnp.dot(lower.astype(BF16), cnt.astype(BF16), preferred_element_type=F32)
    return [jnp.sum(jnp.where(hot, base, 0.0), axis=0, keepdims=True).astype(I32) for hot in hots]


def _slot_matrix(slots, weights, out_ref):
    rows, dt = out_ref.shape
    ch = 64
    rel = lax.broadcasted_iota(I32, (ch, dt), 0).astype(F32).astype(BF16)
    slots_f = [s.astype(F32) for s in slots]
    weights_b = [jnp.asarray(w, F32).astype(BF16) for w in weights]
    zero = jnp.zeros((), BF16)
    for c in range(rows // ch):
        acc = None
        for s, w in zip(slots_f, weights_b):
            term = jnp.where(rel == (s - float(c * ch)).astype(BF16), w, zero)
            acc = term if acc is None else acc + term
        out_ref[c * ch:(c + 1) * ch, :] = acc


def _for_each_run(meta_ref, fn):
    def body(e, c):
        n = pl.multiple_of(meta_ref[0, N_EXPERTS + e], ROW_ALIGN)

        @pl.when(n > 0)
        def _():
            fn(pl.multiple_of(meta_ref[0, e], ROW_ALIGN), pl.multiple_of(meta_ref[0, 2 * N_EXPERTS + e], ROW_ALIGN), n)

        return c

    lax.fori_loop(0, N_EXPERTS, body, 0)


def _run_rows(meta_ref):
    last = N_EXPERTS - 1
    return pl.multiple_of(meta_ref[0, last] + meta_ref[0, N_EXPERTS + last], ROW_ALIGN)


def _dispatch_kernel(meta_ref, zmeta_ref, eid_ref, x_ref, hp_ref, sbuf_ref, zbuf_ref, pm_ref, rows_ref, sems, zsem):
    t = pl.program_id(0)
    slot = t % 2

    def zero_copy(_, g, n):
        return pltpu.make_async_copy(zbuf_ref.at[pl.ds(0, n), :], hp_ref.at[pl.ds(g, n), :], zsem)

    def run_copy(sl, l, g, n):
        return pltpu.make_async_copy(sbuf_ref.at[sl, pl.ds(l, n), :], hp_ref.at[pl.ds(g, n), :], sems.at[sl])

    @pl.when(t == 0)
    def _():
        zbuf_ref[...] = jnp.zeros_like(zbuf_ref)
        _for_each_run(zmeta_ref, lambda l, g, n: zero_copy(l, g, n).start())
        _for_each_run(zmeta_ref, lambda l, g, n: zero_copy(l, g, n).wait())

    _slot_matrix(_local_slots(eid_ref), [1.0] * TOP_K, pm_ref)
    sbuf_ref[slot] = jnp.dot(pm_ref[...], x_ref[...], preferred_element_type=F32)

    @pl.when(t > 0)
    def _():
        run_copy(1 - slot, 0, 0, pl.multiple_of(rows_ref[1 - slot], ROW_ALIGN)).wait()

    _for_each_run(meta_ref, lambda l, g, n: run_copy(slot, l, g, n).start())
    rows_ref[slot] = _run_rows(meta_ref)

    @pl.when(t == pl.num_programs(0) - 1)
    def _():
        run_copy(slot, 0, 0, _run_rows(meta_ref)).wait()


def _dispatch(hl2, eid3, meta3, zmeta, P):
    N, D = hl2.shape
    nt, _, dt = eid3.shape
    nm = meta3.shape[2]
    return pl.pallas_call(
        _dispatch_kernel,
        grid=(nt,),
        in_specs=[pl.BlockSpec((None, 1, nm), lambda i: (i, 0, 0), memory_space=pltpu.SMEM),
                  pl.BlockSpec((1, nm), lambda i: (0, 0), memory_space=pltpu.SMEM),
                  pl.BlockSpec((None, TOP_K, dt), lambda i: (i, 0, 0)),
                  pl.BlockSpec((dt, D), lambda i: (i, 0))],
        out_specs=pl.BlockSpec(memory_space=pl.ANY),
        out_shape=jax.ShapeDtypeStruct((P, D), F32),
        scratch_shapes=[pltpu.VMEM((2, _local_rows(dt), D), F32),
                        pltpu.VMEM((MOE_BLOCK, D), F32),
                        pltpu.VMEM((_local_rows(dt), dt), BF16),
                        pltpu.SMEM((2,), I32),
                        pltpu.SemaphoreType.DMA((2,)),
                        pltpu.SemaphoreType.DMA(())],
        compiler_params=_cparams(1),
        name="dispatch",
    )(meta3, zmeta, eid3, hl2)


def _regroup_kernel(w_ref, sel_ref, o_ref):
    half = w_ref.shape[1] // 2
    g = sel_ref.shape[0]
    wb = w_ref[...].astype(BF16)
    for j in range(w_ref.shape[1] // g):
        r = jnp.dot(wb[:, g * j:g * (j + 1)], sel_ref[...], preferred_element_type=F32)
        o_ref[:, (g // 2) * j:(g // 2) * (j + 1)] = r[:, :g // 2].astype(o_ref.dtype)
        o_ref[:, half + (g // 2) * j:half + (g // 2) * (j + 1)] = r[:, g // 2:].astype(o_ref.dtype)


def _regroup_glu_lin(w1):
    ne, D, de2 = w1.shape
    g = 256
    sel = np.zeros((g, g), np.float32)
    sel[np.arange(0, g, 2), np.arange(g // 2)] = 1.0
    sel[np.arange(1, g, 2), g // 2 + np.arange(g // 2)] = 1.0
    out = pl.pallas_call(
        _regroup_kernel,
        grid=(ne,),
        in_specs=[pl.BlockSpec((D, de2), lambda e: (e, 0)),
                  pl.BlockSpec((g, g), lambda e: (0, 0))],
        out_specs=pl.BlockSpec((D, de2), lambda e: (e, 0)),
        out_shape=jax.ShapeDtypeStruct((ne * D, de2), BF16),
        compiler_params=_cparams(1),
        name="regroup",
    )(w1.reshape(ne * D, de2), jnp.asarray(sel, BF16))
    return out.reshape(ne, D, de2)


def _expert_kernel(be_ref, nu_ref, x_ref, w1_ref, b1_ref, w2_ref, b2_ref, o_ref):
    del be_ref

    @pl.when(pl.program_id(0) < nu_ref[0])
    def _():
        de = w2_ref.shape[0]
        h = jnp.dot(x_ref[...].astype(BF16), w1_ref[...], preferred_element_type=F32) + b1_ref[...]
        glu = jnp.minimum(h[:, :de], SWIGLU_LIMIT)
        lin = jnp.clip(h[:, de:], -SWIGLU_LIMIT, SWIGLU_LIMIT)
        act = glu * _sigmoid(SWIGLU_ALPHA * glu) * (lin + 1.0)
        o_ref[...] = jnp.dot(act.astype(BF16), w2_ref[...].astype(BF16), preferred_element_type=F32) + b2_ref[...]


def _experts(blk_expert, n_used, h_pad, w1b, b1p, w2b, b2):
    P, D = h_pad.shape
    ne, _, de2 = w1b.shape
    de = de2 // 2
    nblk = P // MOE_BLOCK
    blk = lambda j, be, nu: (jnp.minimum(j, nu[0] - 1), 0)
    wsel = lambda j, be, nu: (be[jnp.minimum(j, nu[0] - 1)], 0, 0)
    return pl.pallas_call(
        _expert_kernel,
        grid_spec=pltpu.PrefetchScalarGridSpec(
            num_scalar_prefetch=2,
            grid=(nblk,),
            in_specs=[pl.BlockSpec((MOE_BLOCK, D), blk),
                      pl.BlockSpec((None, D, de2), wsel),
                      pl.BlockSpec((None, 1, de2), wsel),
                      pl.BlockSpec((None, de, D), wsel),
                      pl.BlockSpec((None, 1, D), wsel)],
            out_specs=pl.BlockSpec((MOE_BLOCK, D), blk)),
        out_shape=jax.ShapeDtypeStruct((P, D), F32),
        compiler_params=_cparams(1),
        name="experts",
    )(blk_expert, n_used, h_pad, w1b, b1p.reshape(ne, 1, de2), w2b, b2.reshape(ne, 1, D))


def _combine_kernel(meta_ref, nmeta_ref, eid_ref, gate_ref, x1_ref, g2_ref, fg_ref, yp_ref, o_ref,
                    ybuf_ref, gm_ref, sems):
    t = pl.program_id(0) * pl.num_programs(1) + pl.program_id(1)
    n_tiles = pl.num_programs(0) * pl.num_programs(1)
    slot = t % 2

    def run_copy(sl, l, g, n):
        return pltpu.make_async_copy(yp_ref.at[pl.ds(g, n), :], ybuf_ref.at[sl, pl.ds(l, n), :], sems.at[sl])

    @pl.when(t == 0)
    def _():
        ybuf_ref[...] = jnp.zeros_like(ybuf_ref)
        _for_each_run(meta_ref, lambda l, g, n: run_copy(0, l, g, n).start())

    @pl.when(t + 1 < n_tiles)
    def _():
        _for_each_run(nmeta_ref, lambda l, g, n: run_copy(1 - slot, l, g, n).start())

    run_copy(slot, 0, 0, _run_rows(meta_ref)).wait()

    _slot_matrix(_local_slots(eid_ref), [gate_ref[k:k + 1, :] for k in range(TOP_K)], gm_ref)
    y = _tdot(gm_ref[...], ybuf_ref[slot].astype(BF16))
    x2 = x1_ref[...] + g2_ref[...] * y
    o_ref[...] = _rmsnorm(x2, fg_ref[...])


def _combine(meta3, eid3, gate3, x1, mods3, final_g, y_pad):
    B, S, D = x1.shape
    n_tiles, _, dt = eid3.shape
    nt = S // dt
    nm = meta3.shape[2]
    tile3 = lambda b, i: (b * nt + i, 0, 0)
    next3 = lambda b, i: (jnp.minimum(b * nt + i + 1, n_tiles - 1), 0, 0)
    return pl.pallas_call(
        _combine_kernel,
        grid=(B, nt),
        in_specs=[pl.BlockSpec((None, 1, nm), tile3, memory_space=pltpu.SMEM),
                  pl.BlockSpec((None, 1, nm), next3, memory_space=pltpu.SMEM),
                  pl.BlockSpec((None, TOP_K, dt), tile3),
                  pl.BlockSpec((None, TOP_K, dt), tile3),
                  pl.BlockSpec((None, dt, D), lambda b, i: (b, i, 0)),
                  pl.BlockSpec((None, 1, D), lambda b, i: (b, 0, 5)),
                  pl.BlockSpec((1, D), lambda b, i: (0, 0)),
                  pl.BlockSpec(memory_space=pl.ANY)],
        out_specs=pl.BlockSpec((None, dt, D), lambda b, i: (b, i, 0)),
        out_shape=jax.ShapeDtypeStruct((B, S, D), F32),
        scratch_shapes=[pltpu.VMEM((2, _local_rows(dt), D), F32), pltpu.VMEM((_local_rows(dt), dt), BF16),
                        pltpu.SemaphoreType.DMA((2,))],
        compiler_params=_cparams(2),
        name="combine",
    )(meta3, meta3, eid3, gate3, x1, mods3, final_g.reshape(1, D), y_pad)


def _rope_tables(S, k_scale):
    n_freq = RET_DK // 4
    pos = jnp.arange(S, dtype=F32)
    rows = jnp.floor(pos / GRID_W)
    cols = pos - rows * GRID_W
    inv = ROPE_BASE ** (-jnp.arange(n_freq, dtype=F32) / n_freq)
    ang = jnp.concatenate([rows[:, None] * inv, cols[:, None] * inv], axis=-1)
    cos, sin = jnp.cos(ang), jnp.sin(ang)
    cos2 = jnp.concatenate([cos, cos], axis=-1)
    sin2 = jnp.concatenate([-sin, sin], axis=-1)
    return jnp.concatenate([cos2, sin2, cos2 * k_scale, sin2 * k_scale], axis=-1)


def _identity_tables(L, k_scale):
    one = jnp.ones((L, RET_DK), F32)
    zero = jnp.zeros((L, RET_DK), F32)
    return jnp.concatenate([one, zero, one * k_scale, zero], axis=-1)


def _lanes(t, width):
    return jnp.broadcast_to(t[:, :, None], t.shape + (width,))


def kernel(x, c, ctx, c_ctx, ada_w, ada_b, norm1_g, w_in, conv_w, conv_b, lru_wa, lru_ba, lru_wx, lru_bx,
           lru_lambda, w_rnn_proj, w_ret_proj, w_out, norm2_g, router_w, router_b, moe_w1, moe_b1, moe_w2,
           moe_b2, final_g):
    B, S, D = x.shape
    L = ctx.shape[1]
    N = B * S
    H = RET_HEADS
    lyr = 0
    d_in = w_in.shape[2]
    assert ada_w.shape[0] == 1 and d_in == N_CHUNKS * D and B == 8

    def pairs_apart(w):
        return jnp.swapaxes(w.reshape(D, H, RET_DK // 2, 2), 2, 3).reshape(D, H * RET_DK)

    w_in_l = w_in[lyr].astype(BF16)
    w_in_b = jnp.concatenate([w_in_l[:, :2 * D], pairs_apart(w_in_l[:, 2 * D:3 * D]),
                              pairs_apart(w_in_l[:, 3 * D:4 * D]), w_in_l[:, 4 * D:]], axis=1)
    wg = [(0.5 * jnp.concatenate([lru_wa[lyr, d], lru_wx[lyr, d]], axis=-1)).astype(BF16) for d in range(2)]
    de2 = moe_w1.shape[3]
    glu_lin = np.concatenate([np.arange(0, de2, 2), np.arange(1, de2, 2)])
    w1b = _regroup_glu_lin(moe_w1[lyr])
    b1p = moe_b1[lyr][:, glu_lin]
    rwt = router_w[lyr].T
    rwt_hi = rwt.astype(BF16)
    rwt_lo = (rwt - rwt_hi.astype(F32)).astype(BF16)

    k_scale = RET_DK ** -0.5
    tab_l = _rope_tables(S, k_scale)
    tab_c = _identity_tables(B * L, k_scale)
    log_g = jnp.log1p(-jnp.exp2(-5.0 - jnp.arange(H, dtype=F32)))
    C = min(RET_C, S)
    idx = jnp.arange(C, dtype=F32)
    dec = lambda e: jnp.exp(e[None, :] * log_g[:, None])
    intra = jnp.exp(jnp.abs(idx[:, None] - idx[None, :])[None] * log_g[:, None, None])
    qdec_f = _lanes(dec(idx + 1.0), RET_DK)
    qdec_b = _lanes(dec(C - idx), RET_DK)
    kdec_f = _lanes(dec(C - 1.0 - idx), RET_DK)
    kdec_b = _lanes(dec(idx), RET_DK)
    cdec = jnp.broadcast_to(jnp.exp(C * log_g)[:, None], (H, RET_DV))
    pos_c = jnp.arange(L, dtype=F32)
    cdec_f = _lanes(dec(L - 1.0 - pos_c), RET_DK)
    cdec_b = _lanes(dec(pos_c), RET_DK)

    cvec = jnp.zeros((16, D), F32).at[:B].set(c).at[B].set(c_ctx)
    mods3 = _ada(cvec, ada_w[lyr], ada_b[lyr]).reshape(16, 1, 6 * D)

    z_c, xr_c = _proj(ctx.reshape(1, B * L, D), mods3, lambda b: B, norm1_g[lyr], w_in_b, tab_c, min(PROJ_TM, B * L))
    xr_c = xr_c.reshape(B, L, D)
    z_l, xr_l = _proj(x, mods3, lambda b: b, norm1_g[lyr], w_in_b, tab_l, min(PROJ_TM, S))

    zeros = jnp.zeros((B, D), F32)
    hs = []
    for d in range(2):
        args = (conv_w[lyr], conv_b[lyr], wg[d], 0.5 * lru_ba[lyr, d], 0.5 * lru_bx[lyr, d], lru_lambda[lyr, d])
        _, h0 = _rnn(xr_c, *args, zeros, reverse=(d == 1), tt=RNN_TT)
        h, _ = _rnn(xr_l, *args, h0, reverse=(d == 1), tt=RNN_TT)
        hs.append(h)

    s_fwd, s_bwd = _ret_ctx(z_c, B, cdec_f, cdec_b)
    sb = _ret_bwd(z_l, B, s_bwd, kdec_b, cdec)
    retg = _ret_fwd(z_l, B, sb, s_fwd, intra.astype(BF16), qdec_f.astype(BF16), qdec_b.astype(BF16),
                    kdec_f.astype(BF16), cdec)

    dt = min(DISP_TM, S)
    x1, hl2, eid3, gate3, cnt3 = _finish(
        x, hs[0], hs[1], z_l, retg, mods3, norm2_g[lyr], w_rnn_proj[lyr].astype(BF16),
        w_ret_proj[lyr].astype(BF16), w_out[lyr].astype(BF16), rwt_hi, rwt_lo, router_b[lyr], dt)

    cnt_t = cnt3[:, :, 0].astype(I32)
    cnt_t = (cnt_t + ROW_ALIGN - 1) // ROW_ALIGN * ROW_ALIGN
    cnt = jnp.sum(cnt_t, axis=0)
    padded = (cnt + MOE_BLOCK - 1) // MOE_BLOCK * MOE_BLOCK
    pad_end = jnp.cumsum(padded)
    pad_start = pad_end - padded
    gstart = pad_start[None, :] + jnp.cumsum(cnt_t, axis=0) - cnt_t
    loff = jnp.cumsum(cnt_t, axis=1) - cnt_t
    meta3 = jnp.concatenate([loff, cnt_t, gstart], axis=1).reshape(N // dt, 1, 3 * N_EXPERTS)
    zmeta = jnp.concatenate([jnp.zeros_like(cnt), padded - cnt, pad_start + cnt]).reshape(1, 3 * N_EXPERTS)
    n_blocks = -(-(N * TOP_K + (N // dt) * N_EXPERTS * (ROW_ALIGN - 1)) // MOE_BLOCK) + N_EXPERTS
    P = n_blocks * MOE_BLOCK
    blk_start = jnp.arange(n_blocks, dtype=I32) * MOE_BLOCK
    blk_expert = jnp.minimum(jnp.sum((pad_end[None, :] <= blk_start[:, None]).astype(I32), axis=1), N_EXPERTS - 1)
    n_used = (pad_end[-1:] // MOE_BLOCK).astype(I32)

    h_pad = _dispatch(hl2, eid3, meta3, zmeta, P)
    y_pad = _experts(blk_expert, n_used, h_pad, w1b, b1p, moe_w2[lyr], moe_b2[lyr])
    return _combine(meta3, eid3, gate3, x1, mods3, final_g, y_pad)
```

```python
import functools

import jax
import jax.numpy as jnp
import numpy as np
from jax import lax
from jax.experimental import pallas as pl
from jax.experimental.pallas import tpu as pltpu

F32 = jnp.float32
BF16 = jnp.bfloat16
I32 = jnp.int32

GRID_W = 64
RNN_BLOCKS = 8
CONV_W = 4
LRU_C = 8.0
RET_HEADS = 8
RET_DK = 128
RET_DV = 256
ROPE_BASE = 10000.0
N_EXPERTS = 32
TOP_K = 4
SWIGLU_ALPHA = 1.702
SWIGLU_LIMIT = 7.0
EPS = 1e-6
N_CHUNKS = 10

PROJ_TM = 512
PROJ_TN = 2048
RNN_TT = 256
RET_C = 256
RET_CPS = 4
FIN_TM = 512
MOE_BLOCK = 1024
DISP_TM = 512
ROW_ALIGN = 8
VMEM_LIMIT = 56 * 1024 * 1024


def _cparams(n_axes):
    return pltpu.CompilerParams(dimension_semantics=("arbitrary",) * n_axes,
                                vmem_limit_bytes=VMEM_LIMIT)


def _sigmoid(x):
    return 0.5 * (jnp.tanh(0.5 * x) + 1.0)


def _rmsnorm(x, g):
    return x * lax.rsqrt(jnp.mean(x * x, axis=-1, keepdims=True) + EPS) * g


def _ada_kernel(c_ref, w_ref, b_ref, o_ref):
    c = c_ref[...]
    s = c * _sigmoid(c)
    o_ref[...] = jnp.dot(s, w_ref[...], preferred_element_type=F32,
                         precision=lax.Precision.HIGHEST) + b_ref[...]


def _ada(cvec, ada_w, ada_b):
    R, D = cvec.shape
    n = ada_w.shape[1] // D
    return pl.pallas_call(
        _ada_kernel,
        grid=(n,),
        in_specs=[pl.BlockSpec((R, D), lambda j: (0, 0)),
                  pl.BlockSpec((D, D), lambda j: (0, j)),
                  pl.BlockSpec((1, D), lambda j: (0, j))],
        out_specs=pl.BlockSpec((R, D), lambda j: (0, j)),
        out_shape=jax.ShapeDtypeStruct((R, n * D), F32),
        compiler_params=_cparams(1),
        name="ada",
    )(cvec, ada_w, ada_b.reshape(1, -1))


def _proj_kernel(x_ref, sh_ref, sc_ref, g_ref, w_ref, wqk_ref, tab_ref, o_ref, xr_ref):
    D = x_ref.shape[1]
    h = _rmsnorm(x_ref[...], g_ref[...])
    hb = (h * (1.0 + sc_ref[...]) + sh_ref[...]).astype(BF16)
    for j in range(w_ref.shape[1] // PROJ_TN):
        j0 = j * PROJ_TN
        wj = wqk_ref[...] if j == 1 else w_ref[:, j0:j0 + PROJ_TN]
        acc = jnp.dot(hb, wj, preferred_element_type=F32)
        if j == 0:
            xr_ref[...] = acc[:, :D].astype(xr_ref.dtype)
        if j != 1:
            o_ref[:, j0:j0 + PROJ_TN] = acc.astype(o_ref.dtype)
            continue
        for part in range(2):
            cos = tab_ref[:, (2 * part) * RET_DK:(2 * part + 1) * RET_DK]
            sin = tab_ref[:, (2 * part + 1) * RET_DK:(2 * part + 2) * RET_DK]
            for hh in range(RET_HEADS):
                c0 = part * RET_HEADS * RET_DK + hh * RET_DK
                t = acc[:, c0:c0 + RET_DK]
                o_ref[:, j0 + c0:j0 + c0 + RET_DK] = (
                    t * cos + pltpu.roll(t, RET_DK // 2, 1) * sin).astype(o_ref.dtype)


def _proj(x3, mods3, mod_row, norm_g, w_in_b, w_qk, tab, tm):
    B, S, D = x3.shape
    d_in = w_in_b.shape[1]
    return pl.pallas_call(
        _proj_kernel,
        grid=(B, S // tm),
        in_specs=[pl.BlockSpec((None, tm, D), lambda b, i: (b, i, 0)),
                  pl.BlockSpec((None, 1, D), lambda b, i: (mod_row(b), 0, 0)),
                  pl.BlockSpec((None, 1, D), lambda b, i: (mod_row(b), 0, 1)),
                  pl.BlockSpec((1, D), lambda b, i: (0, 0)),
                  pl.BlockSpec((D, d_in), lambda b, i: (0, 0), pipeline_mode=pl.Buffered(1)),
                  pl.BlockSpec((D, PROJ_TN), lambda b, i: (0, 0), pipeline_mode=pl.Buffered(1)),
                  pl.BlockSpec((tm, 4 * RET_DK), lambda b, i: (i, 0))],
        out_specs=[pl.BlockSpec((tm, d_in), lambda b, i: (i, b)),
                   pl.BlockSpec((None, tm, D), lambda b, i: (b, i, 0))],
        out_shape=[jax.ShapeDtypeStruct((S, B * d_in), BF16),
                   jax.ShapeDtypeStruct((B, S, D), BF16)],
        compiler_params=_cparams(2),
        name="proj",
    )(x3, mods3, mods3, norm_g.reshape(1, D), w_in_b, w_qk, tab)


def _rnn_kernel(xm_ref, xp_ref, xn_ref, cw_ref, cb_ref, wg_ref, ba_ref, bx_ref, lam_ref, h0_ref,
                h_ref, hfin_ref, xs_ref, a_ref, u_ref, hc_ref, *, reverse, n_tiles):
    i = pl.program_id(0)
    tile = (n_tiles - 1 - i) if reverse else i
    nb, tt, D = xm_ref.shape
    bw = D // RNN_BLOCKS
    HALO = xp_ref.shape[1]

    @pl.when(i == 0)
    def _():
        hc_ref[...] = h0_ref[...]

    def time_major(v):
        return jnp.swapaxes(v.astype(F32), 0, 1).reshape(v.shape[1] * nb, D)

    hr = HALO * nb
    R = tt * nb
    xs_ref[0:hr, :] = jnp.where(tile > 0, time_major(xp_ref[...]), 0.0)
    xs_ref[hr:hr + R, :] = time_major(xm_ref[...])
    xs_ref[hr + R:2 * hr + R, :] = jnp.where(tile < n_tiles - 1, time_major(xn_ref[...]), 0.0)

    nl = -lam_ref[...]
    csp = (0.25 * LRU_C) * (jnp.maximum(nl, 0.0) + jnp.log1p(jnp.exp(-jnp.abs(nl))))
    SUB = 256

    def gates(s, carry):
        r0 = pl.multiple_of(s * SUB, SUB)
        base = hr - 2 * nb
        xc = cb_ref[...] + cw_ref[0:1, :] * xs_ref[pl.ds(pl.multiple_of(r0 + base, nb), SUB), :]
        for k in range(1, CONV_W):
            xc = xc + cw_ref[k:k + 1, :] * xs_ref[pl.ds(pl.multiple_of(r0 + base + k * nb, nb), SUB), :]
        xb = xc.astype(BF16)
        for n in range(RNN_BLOCKS):
            g = jnp.dot(xb[:, n * bw:(n + 1) * bw], wg_ref[n], preferred_element_type=F32)
            cs = slice(n * bw, (n + 1) * bw)
            tr = jnp.tanh(g[:, :bw] + ba_ref[:, cs])
            ti = jnp.tanh(g[:, bw:] + bx_ref[:, cs])
            t = jnp.tanh(csp[:, cs] * tr + csp[:, cs])
            rc = 1.0 / (1.0 + t)
            a_ref[pl.ds(r0, SUB), cs] = (1.0 - t) * rc
            root = t * lax.rsqrt(jnp.maximum(t, 1e-30))
            u_ref[pl.ds(r0, SUB), cs] = (root * rc) * ((ti + 1.0) * xc[:, cs])
        return carry

    lax.fori_loop(0, R // SUB, gates, 0)

    def step(t, h):
        ts = (tt - 1 - t) if reverse else t
        r0 = pl.multiple_of(ts * nb, nb)
        h = a_ref[pl.ds(r0, nb), :] * h + u_ref[pl.ds(r0, nb), :]
        u_ref[pl.ds(r0, nb), :] = h
        return h

    h = lax.fori_loop(0, tt, step, hc_ref[...], unroll=8)
    hc_ref[...] = h
    hfin_ref[...] = h
    h_ref[...] = jnp.swapaxes(u_ref[...].reshape(tt, nb, D), 0, 1).astype(h_ref.dtype)


def _rnn(xr, conv_w, conv_b, wg, ba, bx, lam, h0, *, reverse, tt):
    nb, T, D = xr.shape
    tt = min(tt, T)
    n_tiles = T // tt
    HALO = 16
    hb = tt // HALO

    def tile_of(i):
        return (n_tiles - 1 - i) if reverse else i

    kern = functools.partial(_rnn_kernel, reverse=reverse, n_tiles=n_tiles)
    return pl.pallas_call(
        kern,
        grid=(n_tiles,),
        in_specs=[pl.BlockSpec((nb, tt, D), lambda i: (0, tile_of(i), 0)),
                  pl.BlockSpec((nb, HALO, D), lambda i: (0, jnp.maximum(tile_of(i) * hb - 1, 0), 0)),
                  pl.BlockSpec((nb, HALO, D), lambda i: (0, jnp.minimum((tile_of(i) + 1) * hb, n_tiles * hb - 1), 0)),
                  pl.BlockSpec((CONV_W, D), lambda i: (0, 0)),
                  pl.BlockSpec((1, D), lambda i: (0, 0)),
                  pl.BlockSpec((RNN_BLOCKS, D // RNN_BLOCKS, 2 * D // RNN_BLOCKS), lambda i: (0, 0, 0)),
                  pl.BlockSpec((1, D), lambda i: (0, 0)),
                  pl.BlockSpec((1, D), lambda i: (0, 0)),
                  pl.BlockSpec((1, D), lambda i: (0, 0)),
                  pl.BlockSpec((nb, D), lambda i: (0, 0))],
        out_specs=[pl.BlockSpec((nb, tt, D), lambda i: (0, tile_of(i), 0)),
                   pl.BlockSpec((nb, D), lambda i: (0, 0))],
        out_shape=[jax.ShapeDtypeStruct((nb, T, D), BF16),
                   jax.ShapeDtypeStruct((nb, D), F32)],
        scratch_shapes=[pltpu.VMEM(((tt + 2 * HALO) * nb, D), F32),
                        pltpu.VMEM((nb * tt, D), F32),
                        pltpu.VMEM((nb * tt, D), F32),
                        pltpu.VMEM((nb, D), F32)],
        compiler_params=_cparams(1),
        name="rnn_bwd" if reverse else "rnn_fwd",
    )(xr, xr, xr, conv_w, conv_b.reshape(1, D), wg, ba.reshape(1, D), bx.reshape(1, D),
      lam.reshape(1, D), h0)


def _tdot(a, b):
    return lax.dot_general(a, b, (((0,), (0,)), ((), ())), preferred_element_type=F32)


def _ret_ctx_kernel(k_ref, v_ref, df_ref, db_ref, sf_ref, sb_ref):
    for hh in range(RET_HEADS):
        kh = k_ref[:, hh * RET_DK:(hh + 1) * RET_DK].astype(F32)
        vh = v_ref[:, hh * RET_DV:(hh + 1) * RET_DV]
        sf_ref[hh] = _tdot((kh * df_ref[hh]).astype(BF16), vh)
        sb_ref[hh] = _tdot((kh * db_ref[hh]).astype(BF16), vh)


def _ret_ctx(zc, B, dec_f, dec_b):
    L = zc.shape[0] // B
    H = RET_HEADS
    st = jax.ShapeDtypeStruct((B, H, RET_DK, RET_DV), F32)
    return pl.pallas_call(
        _ret_ctx_kernel,
        grid=(B,),
        in_specs=[pl.BlockSpec((L, H * RET_DK), lambda b: (b, 3)),
                  pl.BlockSpec((L, H * RET_DV), lambda b: (b, 2)),
                  pl.BlockSpec((H, L, RET_DK), lambda b: (0, 0, 0)),
                  pl.BlockSpec((H, L, RET_DK), lambda b: (0, 0, 0))],
        out_specs=[pl.BlockSpec((None, H, RET_DK, RET_DV), lambda b: (b, 0, 0, 0))] * 2,
        out_shape=[st, st],
        compiler_params=_cparams(1),
        name="ret_ctx",
    )(zc, zc, dec_f, dec_b)


def _ret_bwd_kernel(k_ref, v_ref, s0_ref, kd_ref, cd_ref, o_ref, st_ref):
    @pl.when(pl.program_id(1) == 0)
    def _():
        st_ref[...] = s0_ref[...]

    C = kd_ref.shape[1]
    for cc in reversed(range(k_ref.shape[0] // C)):
        rs = slice(cc * C, (cc + 1) * C)
        for hh in range(RET_HEADS):
            o_ref[cc, hh] = st_ref[hh].astype(o_ref.dtype)
            kh = k_ref[rs, hh * RET_DK:(hh + 1) * RET_DK].astype(F32)
            vh = v_ref[rs, hh * RET_DV:(hh + 1) * RET_DV]
            st_ref[hh] = st_ref[hh] * cd_ref[hh:hh + 1, :] + _tdot((kh * kd_ref[hh]).astype(BF16), vh)


def _ret_bwd(z, B, s_bwd, kdec_b, cdec):
    S = z.shape[0]
    H, C = RET_HEADS, min(RET_C, S)
    cps = min(RET_CPS, S // C)
    n = S // (C * cps)
    return pl.pallas_call(
        _ret_bwd_kernel,
        grid=(B, n),
        in_specs=[pl.BlockSpec((cps * C, H * RET_DK), lambda b, j: (n - 1 - j, b * N_CHUNKS + 3)),
                  pl.BlockSpec((cps * C, H * RET_DV), lambda b, j: (n - 1 - j, b * (N_CHUNKS // 2) + 2)),
                  pl.BlockSpec((None, H, RET_DK, RET_DV), lambda b, j: (b, 0, 0, 0)),
                  pl.BlockSpec((H, C, RET_DK), lambda b, j: (0, 0, 0)),
                  pl.BlockSpec((H, RET_DV), lambda b, j: (0, 0))],
        out_specs=pl.BlockSpec((None, cps, H, RET_DK, RET_DV), lambda b, j: (b, n - 1 - j, 0, 0, 0)),
        out_shape=jax.ShapeDtypeStruct((B, n * cps, H, RET_DK, RET_DV), BF16),
        scratch_shapes=[pltpu.VMEM((H, RET_DK, RET_DV), F32)],
        compiler_params=_cparams(2),
        name="ret_bwd",
    )(z, z, s_bwd, kdec_b, cdec)


def _ret_fwd_kernel(q_ref, k_ref, v_ref, gs_ref, sb_ref, s0_ref, intra_ref, qf_ref, qb_ref, kf_ref, cd_ref,
                    o_ref, st_ref):
    @pl.when(pl.program_id(1) == 0)
    def _():
        st_ref[...] = s0_ref[...]

    C = intra_ref.shape[1]
    for cc in range(q_ref.shape[0] // C):
        rs = slice(cc * C, (cc + 1) * C)
        for hh in range(RET_HEADS):
            qh = q_ref[rs, hh * RET_DK:(hh + 1) * RET_DK]
            kh = k_ref[rs, hh * RET_DK:(hh + 1) * RET_DK]
            vh = v_ref[rs, hh * RET_DV:(hh + 1) * RET_DV]
            s = lax.dot_general(qh, kh, (((1,), (1,)), ((), ())), preferred_element_type=F32)
            o = jnp.dot(s.astype(BF16) * intra_ref[hh], vh, preferred_element_type=F32)
            q2 = jnp.concatenate([qh * qf_ref[hh], qh * qb_ref[hh]], axis=1)
            s2 = jnp.concatenate([st_ref[hh].astype(BF16), sb_ref[cc, hh]], axis=0)
            o = o + jnp.dot(q2, s2, preferred_element_type=F32)
            st_ref[hh] = st_ref[hh] * cd_ref[hh:hh + 1, :] + _tdot(kh * kf_ref[hh], vh)
            mu = jnp.mean(o, axis=-1, keepdims=True)
            d = o - mu
            var = jnp.mean(d * d, axis=-1, keepdims=True)
            g = gs_ref[rs, hh * RET_DV:(hh + 1) * RET_DV]
            gate = g * _sigmoid(g)
            o_ref[rs, hh * RET_DV:(hh + 1) * RET_DV] = (d * lax.rsqrt(var + EPS)).astype(o_ref.dtype) * gate


def _ret_fwd(z, B, sb, s_fwd, intra, qdec_f, qdec_b, kdec_f, cdec):
    S = z.shape[0]
    H, C = RET_HEADS, min(RET_C, S)
    cps = min(RET_CPS, S // C)
    n = S // (C * cps)
    half = N_CHUNKS // 2
    return pl.pallas_call(
        _ret_fwd_kernel,
        grid=(B, n),
        in_specs=[pl.BlockSpec((cps * C, H * RET_DK), lambda b, j: (j, b * N_CHUNKS + 2)),
                  pl.BlockSpec((cps * C, H * RET_DK), lambda b, j: (j, b * N_CHUNKS + 3)),
                  pl.BlockSpec((cps * C, H * RET_DV), lambda b, j: (j, b * half + 2)),
                  pl.BlockSpec((cps * C, H * RET_DV), lambda b, j: (j, b * half + 3)),
                  pl.BlockSpec((None, cps, H, RET_DK, RET_DV), lambda b, j: (b, j, 0, 0, 0)),
                  pl.BlockSpec((None, H, RET_DK, RET_DV), lambda b, j: (b, 0, 0, 0)),
                  pl.BlockSpec((H, C, C), lambda b, j: (0, 0, 0)),
                  pl.BlockSpec((H, C, RET_DK), lambda b, j: (0, 0, 0)),
                  pl.BlockSpec((H, C, RET_DK), lambda b, j: (0, 0, 0)),
                  pl.BlockSpec((H, C, RET_DK), lambda b, j: (0, 0, 0)),
                  pl.BlockSpec((H, RET_DV), lambda b, j: (0, 0))],
        out_specs=pl.BlockSpec((None, cps * C, H * RET_DV), lambda b, j: (b, j, 0)),
        out_shape=jax.ShapeDtypeStruct((B, S, H * RET_DV), BF16),
        scratch_shapes=[pltpu.VMEM((H, RET_DK, RET_DV), F32)],
        compiler_params=_cparams(2),
        name="ret_fwd",
    )(z, z, z, z, sb, s_fwd, intra, qdec_f, qdec_b, kdec_f, cdec)


def _dot_t(a, b):
    return lax.dot_general(a, b, (((1,), (1,)), ((), ())), preferred_element_type=F32)


def _finish_kernel(x_ref, hf_ref, hb_ref, gr_ref, gab_ref, ret_ref, g1_ref, sh2_ref, sc2_ref, n2_ref,
                   wr_ref, wt_ref, wo_ref, rwh_ref, rwl_ref, rb_ref,
                   x1_ref, hl_ref, eid_ref, gate_ref, cnt_ref):
    D = x_ref.shape[1]
    tm = x_ref.shape[0]
    dt = eid_ref.shape[2]
    rnn = hf_ref[...].astype(F32) + hb_ref[...].astype(F32)
    y_rnn = jnp.dot((rnn * jax.nn.gelu(gr_ref[...].astype(F32))).astype(BF16), wr_ref[...],
                    preferred_element_type=F32)
    y_ret = jnp.dot(ret_ref[...], wt_ref[...], preferred_element_type=F32)
    ga = gab_ref[:, :D].astype(F32)
    gb = gab_ref[:, D:].astype(F32)
    merged = _sigmoid(ga) * y_rnn + _sigmoid(gb) * y_ret
    y = jnp.dot(merged.astype(BF16), wo_ref[...], preferred_element_type=F32)
    x1 = x_ref[...] + g1_ref[...] * y
    x1_ref[...] = x1
    hl = _rmsnorm(x1, n2_ref[...]) * (1.0 + sc2_ref[...]) + sh2_ref[...]
    hh = hl.astype(BF16)
    hl_ref[...] = hh

    hlo = (hl - hh.astype(F32)).astype(BF16)
    logits = (_dot_t(rwh_ref[...], hh) + _dot_t(rwh_ref[...], hlo) + _dot_t(rwl_ref[...], hh)) + rb_ref[...]

    ne = logits.shape[0]
    sub = lax.broadcasted_iota(I32, (ne, tm), 0)
    work = logits
    vals, idxs = [], []
    oh = jnp.zeros((ne, tm), F32)
    for _ in range(TOP_K):
        m = jnp.max(work, axis=0, keepdims=True)
        idx = jnp.min(jnp.where(work == m, sub, ne), axis=0, keepdims=True)
        hot = sub == idx
        vals.append(m)
        idxs.append(idx)
        oh = oh + jnp.where(hot, 1.0, 0.0)
        work = jnp.where(hot, -jnp.inf, work)
    es = [jnp.exp(v - vals[0]) for v in vals]
    inv = 1.0 / (es[0] + es[1] + es[2] + es[3])
    for part in range(tm // dt):
        ls = slice(part * dt, (part + 1) * dt)
        for k in range(TOP_K):
            eid_ref[part, k:k + 1, :] = idxs[k][:, ls]
            gate_ref[part, k:k + 1, :] = (es[k] * inv)[:, ls]
        cnt_ref[part] = jnp.sum(oh[:, ls], axis=1, keepdims=True)


def _finish(x3, hf, hb, z, retg, mods3, norm2_g, w_rnn_b, w_ret_b, w_out_b, rwt_hi, rwt_lo, router_b, dt):
    B, S, D = x3.shape
    tm = min(FIN_TM, S)
    nt = S // tm
    N = B * S
    ne = rwt_hi.shape[0]
    half = N_CHUNKS // 2
    per = tm // dt
    const2 = lambda b, i: (0, 0)
    tile3 = lambda b, i: (b * nt + i, 0, 0)
    return pl.pallas_call(
        _finish_kernel,
        grid=(B, nt),
        in_specs=[pl.BlockSpec((None, tm, D), lambda b, i: (b, i, 0)),
                  pl.BlockSpec((None, tm, D), lambda b, i: (b, i, 0)),
                  pl.BlockSpec((None, tm, D), lambda b, i: (b, i, 0)),
                  pl.BlockSpec((tm, D), lambda b, i: (i, b * N_CHUNKS + 1)),
                  pl.BlockSpec((tm, 2 * D), lambda b, i: (i, b * half + 4)),
                  pl.BlockSpec((None, tm, retg.shape[2]), lambda b, i: (b, i, 0)),
                  pl.BlockSpec((None, 1, D), lambda b, i: (b, 0, 2)),
                  pl.BlockSpec((None, 1, D), lambda b, i: (b, 0, 3)),
                  pl.BlockSpec((None, 1, D), lambda b, i: (b, 0, 4)),
                  pl.BlockSpec((1, D), const2),
                  pl.BlockSpec(w_rnn_b.shape, const2),
                  pl.BlockSpec(w_ret_b.shape, const2),
                  pl.BlockSpec(w_out_b.shape, const2),
                  pl.BlockSpec(rwt_hi.shape, const2),
                  pl.BlockSpec(rwt_lo.shape, const2),
                  pl.BlockSpec((ne, 1), const2)],
        out_specs=[pl.BlockSpec((None, tm, D), lambda b, i: (b, i, 0)),
                   pl.BlockSpec((tm, D), lambda b, i: (b * nt + i, 0)),
                   pl.BlockSpec((per, TOP_K, dt), tile3),
                   pl.BlockSpec((per, TOP_K, dt), tile3),
                   pl.BlockSpec((per, ne, 1), tile3)],
        out_shape=[jax.ShapeDtypeStruct((B, S, D), F32),
                   jax.ShapeDtypeStruct((N, D), BF16),
                   jax.ShapeDtypeStruct((N // dt, TOP_K, dt), I32),
                   jax.ShapeDtypeStruct((N // dt, TOP_K, dt), F32),
                   jax.ShapeDtypeStruct((N // dt, ne, 1), F32)],
        compiler_params=_cparams(2),
        name="finish",
    )(x3, hf, hb, z, z, retg, mods3, mods3, mods3, norm2_g.reshape(1, D),
      w_rnn_b, w_ret_b, w_out_b, rwt_hi, rwt_lo, router_b.reshape(ne, 1))


def _local_rows(dt):
    return TOP_K * dt + N_EXPERTS * ROW_ALIGN


def _local_slots(eid_ref):
    dt = eid_ref.shape[1]
    ne = N_EXPERTS
    sub = lax.broadcasted_iota(I32, (ne, dt), 0)
    hots = [sub == eid_ref[k:k + 1, :] for k in range(TOP_K)]
    oh = jnp.zeros((ne, dt), F32)
    for hot in hots:
        oh = oh + jnp.where(hot, 1.0, 0.0)
    earlier = jnp.where(lax.broadcasted_iota(I32, (dt, dt), 0) < lax.broadcasted_iota(I32, (dt, dt), 1), 1.0, 0.0)
    before = jnp.dot(oh.astype(BF16), earlier.astype(BF16), preferred_element_type=F32)
    cnt = jnp.broadcast_to(jnp.sum(oh, axis=1, keepdims=True), (ne, dt))
    cnt = jnp.ceil(cnt * (1.0 / ROW_ALIGN)) * ROW_ALIGN
    lower = jnp.where(lax.broadcasted_iota(I32, (ne, ne), 1) < lax.broadcasted_iota(I32, (ne, ne), 0), 1.0, 0.0)
    base = before + jnp.dot(lower.astype(BF16), cnt.astype(BF16), preferred_element_type=F32)
    return [jnp.sum(jnp.where(hot, base, 0.0), axis=0, keepdims=True).astype(I32) for hot in hots]


def _slot_matrix(slots, weights, out_ref):
    rows, dt = out_ref.shape
    ch = 64
    rel = lax.broadcasted_iota(I32, (ch, dt), 0).astype(F32).astype(BF16)
    slots_f = [s.astype(F32) for s in slots]
    weights_b = [jnp.asarray(w, F32).astype(BF16) for w in weights]
    zero = jnp.zeros((), BF16)
    for c in range(rows // ch):
        acc = None
        for s, w in zip(slots_f, weights_b):
            term = jnp.where(rel == (s - float(c * ch)).astype(BF16), w, zero)
            acc = term if acc is None else acc + term
        out_ref[c * ch:(c + 1) * ch, :] = acc


def _for_each_run(meta_ref, fn):
    def body(e, c):
        n = pl.multiple_of(meta_ref[0, N_EXPERTS + e], ROW_ALIGN)

        @pl.when(n > 0)
        def _():
            fn(pl.multiple_of(meta_ref[0, e], ROW_ALIGN), pl.multiple_of(meta_ref[0, 2 * N_EXPERTS + e], ROW_ALIGN), n)

        return c

    lax.fori_loop(0, N_EXPERTS, body, 0)


def _run_rows(meta_ref):
    last = N_EXPERTS - 1
    return pl.multiple_of(meta_ref[0, last] + meta_ref[0, N_EXPERTS + last], ROW_ALIGN)


def _dispatch_kernel(meta_ref, zmeta_ref, eid_ref, x_ref, hp_ref, sbuf_ref, zbuf_ref, pm_ref, rows_ref, sems, zsem):
    t = pl.program_id(0)
    slot = t % 2

    def zero_copy(_, g, n):
        return pltpu.make_async_copy(zbuf_ref.at[pl.ds(0, n), :], hp_ref.at[pl.ds(g, n), :], zsem)

    def run_copy(sl, l, g, n):
        return pltpu.make_async_copy(sbuf_ref.at[sl, pl.ds(l, n), :], hp_ref.at[pl.ds(g, n), :], sems.at[sl])

    @pl.when(t == 0)
    def _():
        zbuf_ref[...] = jnp.zeros_like(zbuf_ref)
        _for_each_run(zmeta_ref, lambda l, g, n: zero_copy(l, g, n).start())
        _for_each_run(zmeta_ref, lambda l, g, n: zero_copy(l, g, n).wait())

    _slot_matrix(_local_slots(eid_ref), [1.0] * TOP_K, pm_ref)
    sbuf_ref[slot] = jnp.dot(pm_ref[...], x_ref[...], preferred_element_type=F32)

    @pl.when(t > 0)
    def _():
        run_copy(1 - slot, 0, 0, pl.multiple_of(rows_ref[1 - slot], ROW_ALIGN)).wait()

    _for_each_run(meta_ref, lambda l, g, n: run_copy(slot, l, g, n).start())
    rows_ref[slot] = _run_rows(meta_ref)

    @pl.when(t == pl.num_programs(0) - 1)
    def _():
        run_copy(slot, 0, 0, _run_rows(meta_ref)).wait()


def _dispatch(hl2, eid3, meta3, zmeta, P):
    N, D = hl2.shape
    nt, _, dt = eid3.shape
    nm = meta3.shape[2]
    return pl.pallas_call(
        _dispatch_kernel,
        grid=(nt,),
        in_specs=[pl.BlockSpec((None, 1, nm), lambda i: (i, 0, 0), memory_space=pltpu.SMEM),
                  pl.BlockSpec((1, nm), lambda i: (0, 0), memory_space=pltpu.SMEM),
                  pl.BlockSpec((None, TOP_K, dt), lambda i: (i, 0, 0)),
                  pl.BlockSpec((dt, D), lambda i: (i, 0))],
        out_specs=pl.BlockSpec(memory_space=pl.ANY),
        out_shape=jax.ShapeDtypeStruct((P, D), F32),
        scratch_shapes=[pltpu.VMEM((2, _local_rows(dt), D), F32),
                        pltpu.VMEM((MOE_BLOCK, D), F32),
                        pltpu.VMEM((_local_rows(dt), dt), BF16),
                        pltpu.SMEM((2,), I32),
                        pltpu.SemaphoreType.DMA((2,)),
                        pltpu.SemaphoreType.DMA(())],
        compiler_params=_cparams(1),
        name="dispatch",
    )(meta3, zmeta, eid3, hl2)


def _regroup_kernel(w_ref, sel_ref, o_ref):
    half = w_ref.shape[1] // 2
    g = sel_ref.shape[0]
    wb = w_ref[...].astype(BF16)
    for j in range(w_ref.shape[1] // g):
        r = jnp.dot(wb[:, g * j:g * (j + 1)], sel_ref[...], preferred_element_type=F32)
        o_ref[:, (g // 2) * j:(g // 2) * (j + 1)] = r[:, :g // 2].astype(o_ref.dtype)
        o_ref[:, half + (g // 2) * j:half + (g // 2) * (j + 1)] = r[:, g // 2:].astype(o_ref.dtype)


def _regroup_glu_lin(w1):
    ne, D, de2 = w1.shape
    g = 256
    sel = np.zeros((g, g), np.float32)
    sel[np.arange(0, g, 2), np.arange(g // 2)] = 1.0
    sel[np.arange(1, g, 2), g // 2 + np.arange(g // 2)] = 1.0
    out = pl.pallas_call(
        _regroup_kernel,
        grid=(ne,),
        in_specs=[pl.BlockSpec((D, de2), lambda e: (e, 0)),
                  pl.BlockSpec((g, g), lambda e: (0, 0))],
        out_specs=pl.BlockSpec((D, de2), lambda e: (e, 0)),
        out_shape=jax.ShapeDtypeStruct((ne * D, de2), BF16),
        compiler_params=_cparams(1),
        name="regroup",
    )(w1.reshape(ne * D, de2), jnp.asarray(sel, BF16))
    return out.reshape(ne, D, de2)


def _expert_kernel(be_ref, nu_ref, x_ref, w1_ref, b1_ref, w2_ref, b2_ref, o_ref):
    del be_ref

    @pl.when(pl.program_id(0) < nu_ref[0])
    def _():
        de = w2_ref.shape[0]
        h = jnp.dot(x_ref[...].astype(BF16), w1_ref[...], preferred_element_type=F32) + b1_ref[...]
        glu = jnp.minimum(h[:, :de], SWIGLU_LIMIT)
        lin = jnp.clip(h[:, de:], -SWIGLU_LIMIT, SWIGLU_LIMIT)
        act = glu * _sigmoid(SWIGLU_ALPHA * glu) * (lin + 1.0)
        o_ref[...] = jnp.dot(act.astype(BF16), w2_ref[...].astype(BF16), preferred_element_type=F32) + b2_ref[...]


def _experts(blk_expert, n_used, h_pad, w1b, b1p, w2b, b2):
    P, D = h_pad.shape
    ne, _, de2 = w1b.shape
    de = de2 // 2
    nblk = P // MOE_BLOCK
    blk = lambda j, be, nu: (jnp.minimum(j, nu[0] - 1), 0)
    wsel = lambda j, be, nu: (be[jnp.minimum(j, nu[0] - 1)], 0, 0)
    return pl.pallas_call(
        _expert_kernel,
        grid_spec=pltpu.PrefetchScalarGridSpec(
            num_scalar_prefetch=2,
            grid=(nblk,),
            in_specs=[pl.BlockSpec((MOE_BLOCK, D), blk),
                      pl.BlockSpec((None, D, de2), wsel),
                      pl.BlockSpec((None, 1, de2), wsel),
                      pl.BlockSpec((None, de, D), wsel),
                      pl.BlockSpec((None, 1, D), wsel)],
            out_specs=pl.BlockSpec((MOE_BLOCK, D), blk)),
        out_shape=jax.ShapeDtypeStruct((P, D), F32),
        compiler_params=_cparams(1),
        name="experts",
    )(blk_expert, n_used, h_pad, w1b, b1p.reshape(ne, 1, de2), w2b, b2.reshape(ne, 1, D))


def _combine_kernel(meta_ref, nmeta_ref, eid_ref, gate_ref, x1_ref, g2_ref, fg_ref, yp_ref, o_ref,
                    ybuf_ref, gm_ref, sems):
    t = pl.program_id(0) * pl.num_programs(1) + pl.program_id(1)
    n_tiles = pl.num_programs(0) * pl.num_programs(1)
    slot = t % 2

    def run_copy(sl, l, g, n):
        return pltpu.make_async_copy(yp_ref.at[pl.ds(g, n), :], ybuf_ref.at[sl, pl.ds(l, n), :], sems.at[sl])

    @pl.when(t == 0)
    def _():
        ybuf_ref[...] = jnp.zeros_like(ybuf_ref)
        _for_each_run(meta_ref, lambda l, g, n: run_copy(0, l, g, n).start())

    @pl.when(t + 1 < n_tiles)
    def _():
        _for_each_run(nmeta_ref, lambda l, g, n: run_copy(1 - slot, l, g, n).start())

    run_copy(slot, 0, 0, _run_rows(meta_ref)).wait()

    _slot_matrix(_local_slots(eid_ref), [gate_ref[k:k + 1, :] for k in range(TOP_K)], gm_ref)
    y = _tdot(gm_ref[...], ybuf_ref[slot].astype(BF16))
    x2 = x1_ref[...] + g2_ref[...] * y
    o_ref[...] = _rmsnorm(x2, fg_ref[...])


def _combine(meta3, eid3, gate3, x1, mods3, final_g, y_pad):
    B, S, D = x1.shape
    n_tiles, _, dt = eid3.shape
    nt = S // dt
    nm = meta3.shape[2]
    tile3 = lambda b, i: (b * nt + i, 0, 0)
    next3 = lambda b, i: (jnp.minimum(b * nt + i + 1, n_tiles - 1), 0, 0)
    return pl.pallas_call(
        _combine_kernel,
        grid=(B, nt),
        in_specs=[pl.BlockSpec((None, 1, nm), tile3, memory_space=pltpu.SMEM),
                  pl.BlockSpec((None, 1, nm), next3, memory_space=pltpu.SMEM),
                  pl.BlockSpec((None, TOP_K, dt), tile3),
                  pl.BlockSpec((None, TOP_K, dt), tile3),
                  pl.BlockSpec((None, dt, D), lambda b, i: (b, i, 0)),
                  pl.BlockSpec((None, 1, D), lambda b, i: (b, 0, 5)),
                  pl.BlockSpec((1, D), lambda b, i: (0, 0)),
                  pl.BlockSpec(memory_space=pl.ANY)],
        out_specs=pl.BlockSpec((None, dt, D), lambda b, i: (b, i, 0)),
        out_shape=jax.ShapeDtypeStruct((B, S, D), F32),
        scratch_shapes=[pltpu.VMEM((2, _local_rows(dt), D), F32), pltpu.VMEM((_local_rows(dt), dt), BF16),
                        pltpu.SemaphoreType.DMA((2,))],
        compiler_params=_cparams(2),
        name="combine",
    )(meta3, meta3, eid3, gate3, x1, mods3, final_g.reshape(1, D), y_pad)


def _rope_tables(S, k_scale):
    n_freq = RET_DK // 4
    pos = jnp.arange(S, dtype=F32)
    rows = jnp.floor(pos / GRID_W)
    cols = pos - rows * GRID_W
    inv = ROPE_BASE ** (-jnp.arange(n_freq, dtype=F32) / n_freq)
    ang = jnp.concatenate([rows[:, None] * inv, cols[:, None] * inv], axis=-1)
    cos, sin = jnp.cos(ang), jnp.sin(ang)
    cos2 = jnp.concatenate([cos, cos], axis=-1)
    sin2 = jnp.concatenate([-sin, sin], axis=-1)
    return jnp.concatenate([cos2, sin2, cos2 * k_scale, sin2 * k_scale], axis=-1)


def _identity_tables(L, k_scale):
    one = jnp.ones((L, RET_DK), F32)
    zero = jnp.zeros((L, RET_DK), F32)
    return jnp.concatenate([one, zero, one * k_scale, zero], axis=-1)


def _lanes(t, width):
    return jnp.broadcast_to(t[:, :, None], t.shape + (width,))


def kernel(x, c, ctx, c_ctx, ada_w, ada_b, norm1_g, w_in, conv_w, conv_b, lru_wa, lru_ba, lru_wx, lru_bx,
           lru_lambda, w_rnn_proj, w_ret_proj, w_out, norm2_g, router_w, router_b, moe_w1, moe_b1, moe_w2,
           moe_b2, final_g):
    B, S, D = x.shape
    L = ctx.shape[1]
    N = B * S
    H = RET_HEADS
    lyr = 0
    d_in = w_in.shape[2]
    assert ada_w.shape[0] == 1 and d_in == N_CHUNKS * D and B == 8

    def pairs_apart(w):
        return jnp.swapaxes(w.reshape(D, H, RET_DK // 2, 2), 2, 3).reshape(D, H * RET_DK)

    w_in_b = w_in[lyr].astype(BF16)
    w_qk = jnp.concatenate([pairs_apart(w_in_b[:, 2 * D:3 * D]), pairs_apart(w_in_b[:, 3 * D:4 * D])], axis=1)
    wg = [(0.5 * jnp.concatenate([lru_wa[lyr, d], lru_wx[lyr, d]], axis=-1)).astype(BF16) for d in range(2)]
    de2 = moe_w1.shape[3]
    glu_lin = np.concatenate([np.arange(0, de2, 2), np.arange(1, de2, 2)])
    w1b = _regroup_glu_lin(moe_w1[lyr])
    b1p = moe_b1[lyr][:, glu_lin]
    rwt = router_w[lyr].T
    rwt_hi = rwt.astype(BF16)
    rwt_lo = (rwt - rwt_hi.astype(F32)).astype(BF16)

    k_scale = RET_DK ** -0.5
    tab_l = _rope_tables(S, k_scale)
    tab_c = _identity_tables(B * L, k_scale)
    log_g = jnp.log1p(-jnp.exp2(-5.0 - jnp.arange(H, dtype=F32)))
    C = min(RET_C, S)
    idx = jnp.arange(C, dtype=F32)
    dec = lambda e: jnp.exp(e[None, :] * log_g[:, None])
    intra = jnp.exp(jnp.abs(idx[:, None] - idx[None, :])[None] * log_g[:, None, None])
    qdec_f = _lanes(dec(idx + 1.0), RET_DK)
    qdec_b = _lanes(dec(C - idx), RET_DK)
    kdec_f = _lanes(dec(C - 1.0 - idx), RET_DK)
    kdec_b = _lanes(dec(idx), RET_DK)
    cdec = jnp.broadcast_to(jnp.exp(C * log_g)[:, None], (H, RET_DV))
    pos_c = jnp.arange(L, dtype=F32)
    cdec_f = _lanes(dec(L - 1.0 - pos_c), RET_DK)
    cdec_b = _lanes(dec(pos_c), RET_DK)

    cvec = jnp.zeros((16, D), F32).at[:B].set(c).at[B].set(c_ctx)
    mods3 = _ada(cvec, ada_w[lyr], ada_b[lyr]).reshape(16, 1, 6 * D)

    z_c, xr_c = _proj(ctx.reshape(1, B * L, D), mods3, lambda b: B, norm1_g[lyr], w_in_b, w_qk, tab_c,
                      min(PROJ_TM, B * L))
    xr_c = xr_c.reshape(B, L, D)
    z_l, xr_l = _proj(x, mods3, lambda b: b, norm1_g[lyr], w_in_b, w_qk, tab_l, min(PROJ_TM, S))

    zeros = jnp.zeros((B, D), F32)
    hs = []
    for d in range(2):
        args = (conv_w[lyr], conv_b[lyr], wg[d], 0.5 * lru_ba[lyr, d], 0.5 * lru_bx[lyr, d], lru_lambda[lyr, d])
        _, h0 = _rnn(xr_c, *args, zeros, reverse=(d == 1), tt=RNN_TT)
        h, _ = _rnn(xr_l, *args, h0, reverse=(d == 1), tt=RNN_TT)
        hs.append(h)

    s_fwd, s_bwd = _ret_ctx(z_c, B, cdec_f, cdec_b)
    sb = _ret_bwd(z_l, B, s_bwd, kdec_b, cdec)
    retg = _ret_fwd(z_l, B, sb, s_fwd, intra.astype(BF16), qdec_f.astype(BF16), qdec_b.astype(BF16),
                    kdec_f.astype(BF16), cdec)

    dt = min(DISP_TM, S)
    x1, hl2, eid3, gate3, cnt3 = _finish(
        x, hs[0], hs[1], z_l, retg, mods3, norm2_g[lyr], w_rnn_proj[lyr].astype(BF16),
        w_ret_proj[lyr].astype(BF16), w_out[lyr].astype(BF16), rwt_hi, rwt_lo, router_b[lyr], dt)

    cnt_t = cnt3[:, :, 0].astype(I32)
    cnt_t = (cnt_t + ROW_ALIGN - 1) // ROW_ALIGN * ROW_ALIGN
    cnt = jnp.sum(cnt_t, axis=0)
    padded = (cnt + MOE_BLOCK - 1) // MOE_BLOCK * MOE_BLOCK
    pad_end = jnp.cumsum(padded)
    pad_start = pad_end - padded
    gstart = pad_start[None, :] + jnp.cumsum(cnt_t, axis=0) - cnt_t
    loff = jnp.cumsum(cnt_t, axis=1) - cnt_t
    meta3 = jnp.concatenate([loff, cnt_t, gstart], axis=1).reshape(N // dt, 1, 3 * N_EXPERTS)
    zmeta = jnp.concatenate([jnp.zeros_like(cnt), padded - cnt, pad_start + cnt]).reshape(1, 3 * N_EXPERTS)
    n_blocks = -(-(N * TOP_K + (N // dt) * N_EXPERTS * (ROW_ALIGN - 1)) // MOE_BLOCK) + N_EXPERTS
    P = n_blocks * MOE_BLOCK
    blk_start = jnp.arange(n_blocks, dtype=I32) * MOE_BLOCK
    blk_expert = jnp.minimum(jnp.sum((pad_end[None, :] <= blk_start[:, None]).astype(I32), axis=1), N_EXPERTS - 1)
    n_used = (pad_end[-1:] // MOE_BLOCK).astype(I32)

    h_pad = _dispatch(hl2, eid3, meta3, zmeta, P)
    y_pad = _experts(blk_expert, n_used, h_pad, w1b, b1p, moe_w2[lyr], moe_b2[lyr])
    return _combine(meta3, eid3, gate3, x1, mods3, final_g, y_pad)
```

```python
import functools

import jax
import jax.numpy as jnp
import numpy as np
from jax import lax
from jax.experimental import pallas as pl
from jax.experimental.pallas import tpu as pltpu

F32 = jnp.float32
BF16 = jnp.bfloat16
I32 = jnp.int32

GRID_W = 64
RNN_BLOCKS = 8
CONV_W = 4
LRU_C = 8.0
RET_HEADS = 8
RET_DK = 128
RET_DV = 256
ROPE_BASE = 10000.0
N_EXPERTS = 32
TOP_K = 4
SWIGLU_ALPHA = 1.702
SWIGLU_LIMIT = 7.0
EPS = 1e-6
N_CHUNKS = 10

PROJ_TM = 512
PROJ_TN = 2048
RNN_TT = 256
RET_C = 256
RET_CPS = 4
FIN_TM = 512
MOE_BLOCK = 1024
DISP_TM = 512
ROW_ALIGN = 8
VMEM_LIMIT = 56 * 1024 * 1024


def _cparams(n_axes):
    return pltpu.CompilerParams(dimension_semantics=("arbitrary",) * n_axes,
                                vmem_limit_bytes=VMEM_LIMIT)


def _sigmoid(x):
    return 0.5 * (jnp.tanh(0.5 * x) + 1.0)


def _rmsnorm(x, g):
    return x * lax.rsqrt(jnp.mean(x * x, axis=-1, keepdims=True) + EPS) * g


def _tdot(a, b):
    return lax.dot_general(a, b, (((0,), (0,)), ((), ())), preferred_element_type=F32)


def _ada_kernel(c_ref, w_ref, b_ref, o_ref):
    c = c_ref[...]
    s = c * _sigmoid(c)
    o_ref[...] = jnp.dot(s, w_ref[...], preferred_element_type=F32,
                         precision=lax.Precision.HIGHEST) + b_ref[...]


def _ada(cvec, ada_w, ada_b):
    R, D = cvec.shape
    n = ada_w.shape[1] // D
    return pl.pallas_call(
        _ada_kernel,
        grid=(n,),
        in_specs=[pl.BlockSpec((R, D), lambda j: (0, 0)),
                  pl.BlockSpec((D, D), lambda j: (0, j)),
                  pl.BlockSpec((1, D), lambda j: (0, j))],
        out_specs=pl.BlockSpec((R, D), lambda j: (0, j)),
        out_shape=jax.ShapeDtypeStruct((R, n * D), F32),
        compiler_params=_cparams(1),
        name="ada",
    )(cvec, ada_w, ada_b.reshape(1, -1))


def _proj_kernel(x_ref, sh_ref, sc_ref, g_ref, w_ref, wqk_ref, tab_ref, o_ref, xr_ref):
    D = x_ref.shape[1]
    h = _rmsnorm(x_ref[...], g_ref[...])
    hb = (h * (1.0 + sc_ref[...]) + sh_ref[...]).astype(BF16)
    for j in range(w_ref.shape[1] // PROJ_TN):
        j0 = j * PROJ_TN
        wj = wqk_ref[...] if j == 1 else w_ref[:, j0:j0 + PROJ_TN]
        acc = jnp.dot(hb, wj, preferred_element_type=F32)
        if j == 0:
            xr_ref[...] = acc[:, :D].astype(xr_ref.dtype)
        if j != 1:
            o_ref[:, j0:j0 + PROJ_TN] = acc.astype(o_ref.dtype)
            continue
        for part in range(2):
            cos = tab_ref[:, (2 * part) * RET_DK:(2 * part + 1) * RET_DK]
            sin = tab_ref[:, (2 * part + 1) * RET_DK:(2 * part + 2) * RET_DK]
            for hh in range(RET_HEADS):
                c0 = part * RET_HEADS * RET_DK + hh * RET_DK
                t = acc[:, c0:c0 + RET_DK]
                o_ref[:, j0 + c0:j0 + c0 + RET_DK] = (
                    t * cos + pltpu.roll(t, RET_DK // 2, 1) * sin).astype(o_ref.dtype)


def _proj(x3, mods3, mod_row, norm_g, w_in_b, w_qk, tab, tm):
    B, S, D = x3.shape
    d_in = w_in_b.shape[1]
    return pl.pallas_call(
        _proj_kernel,
        grid=(B, S // tm),
        in_specs=[pl.BlockSpec((None, tm, D), lambda b, i: (b, i, 0)),
                  pl.BlockSpec((None, 1, D), lambda b, i: (mod_row(b), 0, 0)),
                  pl.BlockSpec((None, 1, D), lambda b, i: (mod_row(b), 0, 1)),
                  pl.BlockSpec((1, D), lambda b, i: (0, 0)),
                  pl.BlockSpec((D, d_in), lambda b, i: (0, 0), pipeline_mode=pl.Buffered(1)),
                  pl.BlockSpec((D, PROJ_TN), lambda b, i: (0, 0), pipeline_mode=pl.Buffered(1)),
                  pl.BlockSpec((tm, 4 * RET_DK), lambda b, i: (i, 0))],
        out_specs=[pl.BlockSpec((tm, d_in), lambda b, i: (i, b)),
                   pl.BlockSpec((None, tm, D), lambda b, i: (b, i, 0))],
        out_shape=[jax.ShapeDtypeStruct((S, B * d_in), BF16),
                   jax.ShapeDtypeStruct((B, S, D), BF16)],
        compiler_params=_cparams(2),
        name="proj",
    )(x3, mods3, mods3, norm_g.reshape(1, D), w_in_b, w_qk, tab)


def _rnn_kernel(xm_ref, xp_ref, xn_ref, cw_ref, cb_ref, wg_ref, ba_ref, bx_ref, lam_ref, h0_ref,
                h_ref, hfin_ref, xs_ref, a_ref, u_ref, hc_ref, *, reverse, n_tiles):
    i = pl.program_id(0)
    tile = (n_tiles - 1 - i) if reverse else i
    nb, tt, D = xm_ref.shape
    bw = D // RNN_BLOCKS
    HALO = xp_ref.shape[1]

    @pl.when(i == 0)
    def _():
        hc_ref[...] = h0_ref[...]

    GT = HALO
    gr = GT * nb
    ri = lax.broadcasted_iota(I32, (gr, gr), 0)
    ci = lax.broadcasted_iota(I32, (gr, gr), 1)
    perm = jnp.where((ri // nb == ci % GT) & (ri % nb == ci // GT), 1.0, 0.0).astype(BF16)

    def time_major(ref, t0):
        xg = jnp.concatenate([ref[b, t0:t0 + GT, :] for b in range(nb)], axis=0)
        return jnp.dot(perm, xg, preferred_element_type=F32)

    hr = HALO * nb
    R = tt * nb
    xs_ref[0:hr, :] = jnp.where(tile > 0, time_major(xp_ref, 0), 0.0)
    for g in range(tt // GT):
        xs_ref[hr + g * gr:hr + (g + 1) * gr, :] = time_major(xm_ref, g * GT)
    xs_ref[hr + R:2 * hr + R, :] = jnp.where(tile < n_tiles - 1, time_major(xn_ref, 0), 0.0)

    nl = -lam_ref[...]
    csp = (0.25 * LRU_C) * (jnp.maximum(nl, 0.0) + jnp.log1p(jnp.exp(-jnp.abs(nl))))
    SUB = 256

    def gates(s, carry):
        r0 = pl.multiple_of(s * SUB, SUB)
        base = hr - 2 * nb
        xc = cb_ref[...] + cw_ref[0:1, :] * xs_ref[pl.ds(pl.multiple_of(r0 + base, nb), SUB), :]
        for k in range(1, CONV_W):
            xc = xc + cw_ref[k:k + 1, :] * xs_ref[pl.ds(pl.multiple_of(r0 + base + k * nb, nb), SUB), :]
        xb = xc.astype(BF16)
        for n in range(RNN_BLOCKS):
            g = jnp.dot(xb[:, n * bw:(n + 1) * bw], wg_ref[n], preferred_element_type=F32)
            cs = slice(n * bw, (n + 1) * bw)
            tr = jnp.tanh(g[:, :bw] + ba_ref[:, cs])
            ti = jnp.tanh(g[:, bw:] + bx_ref[:, cs])
            t = jnp.tanh(csp[:, cs] * tr + csp[:, cs])
            rc = 1.0 / (1.0 + t)
            a_ref[pl.ds(r0, SUB), cs] = (1.0 - t) * rc
            root = t * lax.rsqrt(jnp.maximum(t, 1e-30))
            u_ref[pl.ds(r0, SUB), cs] = (root * rc) * ((ti + 1.0) * xc[:, cs])
        return carry

    lax.fori_loop(0, R // SUB, gates, 0)

    def step(t, h):
        ts = (tt - 1 - t) if reverse else t
        r0 = pl.multiple_of(ts * nb, nb)
        h = a_ref[pl.ds(r0, nb), :] * h + u_ref[pl.ds(r0, nb), :]
        u_ref[pl.ds(r0, nb), :] = h
        return h

    h = lax.fori_loop(0, tt, step, hc_ref[...], unroll=8)
    hc_ref[...] = h
    hfin_ref[...] = h
    for g in range(tt // GT):
        hg = _tdot(perm, u_ref[g * gr:(g + 1) * gr, :].astype(BF16))
        for b in range(nb):
            h_ref[b, g * GT:(g + 1) * GT, :] = hg[b * GT:(b + 1) * GT, :].astype(h_ref.dtype)


def _rnn(xr, conv_w, conv_b, wg, ba, bx, lam, h0, *, reverse, tt):
    nb, T, D = xr.shape
    tt = min(tt, T)
    n_tiles = T // tt
    HALO = 16
    hb = tt // HALO

    def tile_of(i):
        return (n_tiles - 1 - i) if reverse else i

    kern = functools.partial(_rnn_kernel, reverse=reverse, n_tiles=n_tiles)
    return pl.pallas_call(
        kern,
        grid=(n_tiles,),
        in_specs=[pl.BlockSpec((nb, tt, D), lambda i: (0, tile_of(i), 0)),
                  pl.BlockSpec((nb, HALO, D), lambda i: (0, jnp.maximum(tile_of(i) * hb - 1, 0), 0)),
                  pl.BlockSpec((nb, HALO, D), lambda i: (0, jnp.minimum((tile_of(i) + 1) * hb, n_tiles * hb - 1), 0)),
                  pl.BlockSpec((CONV_W, D), lambda i: (0, 0)),
                  pl.BlockSpec((1, D), lambda i: (0, 0)),
                  pl.BlockSpec((RNN_BLOCKS, D // RNN_BLOCKS, 2 * D // RNN_BLOCKS), lambda i: (0, 0, 0)),
                  pl.BlockSpec((1, D), lambda i: (0, 0)),
                  pl.BlockSpec((1, D), lambda i: (0, 0)),
                  pl.BlockSpec((1, D), lambda i: (0, 0)),
                  pl.BlockSpec((nb, D), lambda i: (0, 0))],
        out_specs=[pl.BlockSpec((nb, tt, D), lambda i: (0, tile_of(i), 0)),
                   pl.BlockSpec((nb, D), lambda i: (0, 0))],
        out_shape=[jax.ShapeDtypeStruct((nb, T, D), BF16),
                   jax.ShapeDtypeStruct((nb, D), F32)],
        scratch_shapes=[pltpu.VMEM(((tt + 2 * HALO) * nb, D), F32),
                        pltpu.VMEM((nb * tt, D), F32),
                        pltpu.VMEM((nb * tt, D), F32),
                        pltpu.VMEM((nb, D), F32)],
        compiler_params=_cparams(1),
        name="rnn_bwd" if reverse else "rnn_fwd",
    )(xr, xr, xr, conv_w, conv_b.reshape(1, D), wg, ba.reshape(1, D), bx.reshape(1, D),
      lam.reshape(1, D), h0)


def _ret_ctx_kernel(k_ref, v_ref, df_ref, db_ref, sf_ref, sb_ref):
    for hh in range(RET_HEADS):
        kh = k_ref[:, hh * RET_DK:(hh + 1) * RET_DK].astype(F32)
        vh = v_ref[:, hh * RET_DV:(hh + 1) * RET_DV]
        sf_ref[hh] = _tdot((kh * df_ref[hh]).astype(BF16), vh)
        sb_ref[hh] = _tdot((kh * db_ref[hh]).astype(BF16), vh)


def _ret_ctx(zc, B, dec_f, dec_b):
    L = zc.shape[0] // B
    H = RET_HEADS
    st = jax.ShapeDtypeStruct((B, H, RET_DK, RET_DV), F32)
    return pl.pallas_call(
        _ret_ctx_kernel,
        grid=(B,),
        in_specs=[pl.BlockSpec((L, H * RET_DK), lambda b: (b, 3)),
                  pl.BlockSpec((L, H * RET_DV), lambda b: (b, 2)),
                  pl.BlockSpec((H, L, RET_DK), lambda b: (0, 0, 0)),
                  pl.BlockSpec((H, L, RET_DK), lambda b: (0, 0, 0))],
        out_specs=[pl.BlockSpec((None, H, RET_DK, RET_DV), lambda b: (b, 0, 0, 0))] * 2,
        out_shape=[st, st],
        compiler_params=_cparams(1),
        name="ret_ctx",
    )(zc, zc, dec_f, dec_b)


def _ret_bwd_kernel(k_ref, v_ref, s0_ref, kd_ref, cd_ref, o_ref, st_ref):
    @pl.when(pl.program_id(1) == 0)
    def _():
        st_ref[...] = s0_ref[...]

    C = kd_ref.shape[1]
    for cc in reversed(range(k_ref.shape[0] // C)):
        rs = slice(cc * C, (cc + 1) * C)
        for hh in range(RET_HEADS):
            o_ref[cc, hh] = st_ref[hh].astype(o_ref.dtype)
            kh = k_ref[rs, hh * RET_DK:(hh + 1) * RET_DK].astype(F32)
            vh = v_ref[rs, hh * RET_DV:(hh + 1) * RET_DV]
            st_ref[hh] = st_ref[hh] * cd_ref[hh:hh + 1, :] + _tdot((kh * kd_ref[hh]).astype(BF16), vh)


def _ret_bwd(z, B, s_bwd, kdec_b, cdec):
    S = z.shape[0]
    H, C = RET_HEADS, min(RET_C, S)
    cps = min(RET_CPS, S // C)
    n = S // (C * cps)
    return pl.pallas_call(
        _ret_bwd_kernel,
        grid=(B, n),
        in_specs=[pl.BlockSpec((cps * C, H * RET_DK), lambda b, j: (n - 1 - j, b * N_CHUNKS + 3)),
                  pl.BlockSpec((cps * C, H * RET_DV), lambda b, j: (n - 1 - j, b * (N_CHUNKS // 2) + 2)),
                  pl.BlockSpec((None, H, RET_DK, RET_DV), lambda b, j: (b, 0, 0, 0)),
                  pl.BlockSpec((H, C, RET_DK), lambda b, j: (0, 0, 0)),
                  pl.BlockSpec((H, RET_DV), lambda b, j: (0, 0))],
        out_specs=pl.BlockSpec((None, cps, H, RET_DK, RET_DV), lambda b, j: (b, n - 1 - j, 0, 0, 0)),
        out_shape=jax.ShapeDtypeStruct((B, n * cps, H, RET_DK, RET_DV), BF16),
        scratch_shapes=[pltpu.VMEM((H, RET_DK, RET_DV), F32)],
        compiler_params=_cparams(2),
        name="ret_bwd",
    )(z, z, s_bwd, kdec_b, cdec)


def _ret_fwd_kernel(q_ref, k_ref, v_ref, gs_ref, sb_ref, s0_ref, intra_ref, qf_ref, qb_ref, kf_ref, cd_ref,
                    o_ref, st_ref):
    @pl.when(pl.program_id(1) == 0)
    def _():
        st_ref[...] = s0_ref[...]

    C = intra_ref.shape[1]
    for cc in range(q_ref.shape[0] // C):
        rs = slice(cc * C, (cc + 1) * C)
        for hh in range(RET_HEADS):
            qh = q_ref[rs, hh * RET_DK:(hh + 1) * RET_DK]
            kh = k_ref[rs, hh * RET_DK:(hh + 1) * RET_DK]
            vh = v_ref[rs, hh * RET_DV:(hh + 1) * RET_DV]
            s = lax.dot_general(qh, kh, (((1,), (1,)), ((), ())), preferred_element_type=F32)
            o = jnp.dot(s.astype(BF16) * intra_ref[hh], vh, preferred_element_type=F32)
            q2 = jnp.concatenate([qh * qf_ref[hh], qh * qb_ref[hh]], axis=1)
            s2 = jnp.concatenate([st_ref[hh].astype(BF16), sb_ref[cc, hh]], axis=0)
            o = o + jnp.dot(q2, s2, preferred_element_type=F32)
            st_ref[hh] = st_ref[hh] * cd_ref[hh:hh + 1, :] + _tdot(kh * kf_ref[hh], vh)
            mu = jnp.mean(o, axis=-1, keepdims=True)
            d = o - mu
            var = jnp.mean(d * d, axis=-1, keepdims=True)
            g = gs_ref[rs, hh * RET_DV:(hh + 1) * RET_DV]
            gate = g * _sigmoid(g)
            o_ref[rs, hh * RET_DV:(hh + 1) * RET_DV] = (d * lax.rsqrt(var + EPS)).astype(o_ref.dtype) * gate


def _ret_fwd(z, B, sb, s_fwd, intra, qdec_f, qdec_b, kdec_f, cdec):
    S = z.shape[0]
    H, C = RET_HEADS, min(RET_C, S)
    cps = min(RET_CPS, S // C)
    n = S // (C * cps)
    half = N_CHUNKS // 2
    return pl.pallas_call(
        _ret_fwd_kernel,
        grid=(B, n),
        in_specs=[pl.BlockSpec((cps * C, H * RET_DK), lambda b, j: (j, b * N_CHUNKS + 2)),
                  pl.BlockSpec((cps * C, H * RET_DK), lambda b, j: (j, b * N_CHUNKS + 3)),
                  pl.BlockSpec((cps * C, H * RET_DV), lambda b, j: (j, b * half + 2)),
                  pl.BlockSpec((cps * C, H * RET_DV), lambda b, j: (j, b * half + 3)),
                  pl.BlockSpec((None, cps, H, RET_DK, RET_DV), lambda b, j: (b, j, 0, 0, 0)),
                  pl.BlockSpec((None, H, RET_DK, RET_DV), lambda b, j: (b, 0, 0, 0)),
                  pl.BlockSpec((H, C, C), lambda b, j: (0, 0, 0)),
                  pl.BlockSpec((H, C, RET_DK), lambda b, j: (0, 0, 0)),
                  pl.BlockSpec((H, C, RET_DK), lambda b, j: (0, 0, 0)),
                  pl.BlockSpec((H, C, RET_DK), lambda b, j: (0, 0, 0)),
                  pl.BlockSpec((H, RET_DV), lambda b, j: (0, 0))],
        out_specs=pl.BlockSpec((None, cps * C, H * RET_DV), lambda b, j: (b, j, 0)),
        out_shape=jax.ShapeDtypeStruct((B, S, H * RET_DV), BF16),
        scratch_shapes=[pltpu.VMEM((H, RET_DK, RET_DV), F32)],
        compiler_params=_cparams(2),
        name="ret_fwd",
    )(z, z, z, z, sb, s_fwd, intra, qdec_f, qdec_b, kdec_f, cdec)


def _dot_t(a, b):
    return lax.dot_general(a, b, (((1,), (1,)), ((), ())), preferred_element_type=F32)


def _finish_kernel(x_ref, hf_ref, hb_ref, gr_ref, gab_ref, ret_ref, g1_ref, sh2_ref, sc2_ref, n2_ref,
                   wr_ref, wt_ref, wo_ref, rwh_ref, rwl_ref, rb_ref,
                   x1_ref, hl_ref, eid_ref, gate_ref, cnt_ref):
    D = x_ref.shape[1]
    tm = x_ref.shape[0]
    dt = eid_ref.shape[2]
    rnn = hf_ref[...].astype(F32) + hb_ref[...].astype(F32)
    y_rnn = jnp.dot((rnn * jax.nn.gelu(gr_ref[...].astype(F32))).astype(BF16), wr_ref[...],
                    preferred_element_type=F32)
    y_ret = jnp.dot(ret_ref[...], wt_ref[...], preferred_element_type=F32)
    ga = gab_ref[:, :D].astype(F32)
    gb = gab_ref[:, D:].astype(F32)
    merged = _sigmoid(ga) * y_rnn + _sigmoid(gb) * y_ret
    y = jnp.dot(merged.astype(BF16), wo_ref[...], preferred_element_type=F32)
    x1 = x_ref[...] + g1_ref[...] * y
    x1_ref[...] = x1
    hl = _rmsnorm(x1, n2_ref[...]) * (1.0 + sc2_ref[...]) + sh2_ref[...]
    hh = hl.astype(BF16)
    hl_ref[...] = hh

    hlo = (hl - hh.astype(F32)).astype(BF16)
    logits = (_dot_t(rwh_ref[...], hh) + _dot_t(rwh_ref[...], hlo) + _dot_t(rwl_ref[...], hh)) + rb_ref[...]

    ne = logits.shape[0]
    sub = lax.broadcasted_iota(I32, (ne, tm), 0)
    work = logits
    vals, idxs = [], []
    oh = jnp.zeros((ne, tm), F32)
    for _ in range(TOP_K):
        m = jnp.max(work, axis=0, keepdims=True)
        idx = jnp.min(jnp.where(work == m, sub, ne), axis=0, keepdims=True)
        hot = sub == idx
        vals.append(m)
        idxs.append(idx)
        oh = oh + jnp.where(hot, 1.0, 0.0)
        work = jnp.where(hot, -jnp.inf, work)
    es = [jnp.exp(v - vals[0]) for v in vals]
    inv = 1.0 / (es[0] + es[1] + es[2] + es[3])
    for part in range(tm // dt):
        ls = slice(part * dt, (part + 1) * dt)
        for k in range(TOP_K):
            eid_ref[part, k:k + 1, :] = idxs[k][:, ls]
            gate_ref[part, k:k + 1, :] = (es[k] * inv)[:, ls]
        cnt_ref[part] = jnp.sum(oh[:, ls], axis=1, keepdims=True)


def _finish(x3, hf, hb, z, retg, mods3, norm2_g, w_rnn_b, w_ret_b, w_out_b, rwt_hi, rwt_lo, router_b, dt):
    B, S, D = x3.shape
    tm = min(FIN_TM, S)
    nt = S // tm
    N = B * S
    ne = rwt_hi.shape[0]
    half = N_CHUNKS // 2
    per = tm // dt
    const2 = lambda b, i: (0, 0)
    tile3 = lambda b, i: (b * nt + i, 0, 0)
    return pl.pallas_call(
        _finish_kernel,
        grid=(B, nt),
        in_specs=[pl.BlockSpec((None, tm, D), lambda b, i: (b, i, 0)),
                  pl.BlockSpec((None, tm, D), lambda b, i: (b, i, 0)),
                  pl.BlockSpec((None, tm, D), lambda b, i: (b, i, 0)),
                  pl.BlockSpec((tm, D), lambda b, i: (i, b * N_CHUNKS + 1)),
                  pl.BlockSpec((tm, 2 * D), lambda b, i: (i, b * half + 4)),
                  pl.BlockSpec((None, tm, retg.shape[2]), lambda b, i: (b, i, 0)),
                  pl.BlockSpec((None, 1, D), lambda b, i: (b, 0, 2)),
                  pl.BlockSpec((None, 1, D), lambda b, i: (b, 0, 3)),
                  pl.BlockSpec((None, 1, D), lambda b, i: (b, 0, 4)),
                  pl.BlockSpec((1, D), const2),
                  pl.BlockSpec(w_rnn_b.shape, const2),
                  pl.BlockSpec(w_ret_b.shape, const2),
                  pl.BlockSpec(w_out_b.shape, const2),
                  pl.BlockSpec(rwt_hi.shape, const2),
                  pl.BlockSpec(rwt_lo.shape, const2),
                  pl.BlockSpec((ne, 1), const2)],
        out_specs=[pl.BlockSpec((None, tm, D), lambda b, i: (b, i, 0)),
                   pl.BlockSpec((tm, D), lambda b, i: (b * nt + i, 0)),
                   pl.BlockSpec((per, TOP_K, dt), tile3),
                   pl.BlockSpec((per, TOP_K, dt), tile3),
                   pl.BlockSpec((per, ne, 1), tile3)],
        out_shape=[jax.ShapeDtypeStruct((B, S, D), F32),
                   jax.ShapeDtypeStruct((N, D), BF16),
                   jax.ShapeDtypeStruct((N // dt, TOP_K, dt), I32),
                   jax.ShapeDtypeStruct((N // dt, TOP_K, dt), F32),
                   jax.ShapeDtypeStruct((N // dt, ne, 1), F32)],
        compiler_params=_cparams(2),
        name="finish",
    )(x3, hf, hb, z, z, retg, mods3, mods3, mods3, norm2_g.reshape(1, D),
      w_rnn_b, w_ret_b, w_out_b, rwt_hi, rwt_lo, router_b.reshape(ne, 1))


def _local_rows(dt):
    return TOP_K * dt + N_EXPERTS * ROW_ALIGN


def _local_slots(eid_ref):
    dt = eid_ref.shape[1]
    ne = N_EXPERTS
    sub = lax.broadcasted_iota(I32, (ne, dt), 0)
    hots = [sub == eid_ref[k:k + 1, :] for k in range(TOP_K)]
    oh = jnp.zeros((ne, dt), F32)
    for hot in hots:
        oh = oh + jnp.where(hot, 1.0, 0.0)
    earlier = jnp.where(lax.broadcasted_iota(I32, (dt, dt), 0) < lax.broadcasted_iota(I32, (dt, dt), 1), 1.0, 0.0)
    before = jnp.dot(oh.astype(BF16), earlier.astype(BF16), preferred_element_type=F32)
    cnt = jnp.broadcast_to(jnp.sum(oh, axis=1, keepdims=True), (ne, dt))
    cnt = jnp.ceil(cnt * (1.0 / ROW_ALIGN)) * ROW_ALIGN
    lower = jnp.where(lax.broadcasted_iota(I32, (ne, ne), 1) < lax.broadcasted_iota(I32, (ne, ne), 0), 1.0, 0.0)
    base = before + jnp.dot(lower.astype(BF16), cnt.astype(BF16), preferred_element_type=F32)
    return [jnp.sum(jnp.where(hot, base, 0.0), axis=0, keepdims=True).astype(I32) for hot in hots]


def _slot_matrix(slots, weights, out_ref):
    rows, dt = out_ref.shape
    ch = 64
    rel = lax.broadcasted_iota(I32, (ch, dt), 0).astype(F32).astype(BF16)
    slots_f = [s.astype(F32) for s in slots]
    weights_b = [jnp.asarray(w, F32).astype(BF16) for w in weights]
    zero = jnp.zeros((), BF16)
    for c in range(rows // ch):
        acc = None
        for s, w in zip(slots_f, weights_b):
            term = jnp.where(rel == (s - float(c * ch)).astype(BF16), w, zero)
            acc = term if acc is None else acc + term
        out_ref[c * ch:(c + 1) * ch, :] = acc


def _for_each_run(meta_ref, fn):
    def body(e, c):
        n = pl.multiple_of(meta_ref[0, N_EXPERTS + e], ROW_ALIGN)

        @pl.when(n > 0)
        def _():
            fn(pl.multiple_of(meta_ref[0, e], ROW_ALIGN), pl.multiple_of(meta_ref[0, 2 * N_EXPERTS + e], ROW_ALIGN), n)

        return c

    lax.fori_loop(0, N_EXPERTS, body, 0)


def _run_rows(meta_ref):
    last = N_EXPERTS - 1
    return pl.multiple_of(meta_ref[0, last] + meta_ref[0, N_EXPERTS + last], ROW_ALIGN)


def _dispatch_kernel(meta_ref, zmeta_ref, eid_ref, x_ref, hp_ref, sbuf_ref, zbuf_ref, pm_ref, rows_ref, sems, zsem):
    t = pl.program_id(0)
    slot = t % 2

    def zero_copy(_, g, n):
        return pltpu.make_async_copy(zbuf_ref.at[pl.ds(0, n), :], hp_ref.at[pl.ds(g, n), :], zsem)

    def run_copy(sl, l, g, n):
        return pltpu.make_async_copy(sbuf_ref.at[sl, pl.ds(l, n), :], hp_ref.at[pl.ds(g, n), :], sems.at[sl])

    @pl.when(t == 0)
    def _():
        zbuf_ref[...] = jnp.zeros_like(zbuf_ref)
        _for_each_run(zmeta_ref, lambda l, g, n: zero_copy(l, g, n).start())
        _for_each_run(zmeta_ref, lambda l, g, n: zero_copy(l, g, n).wait())

    _slot_matrix(_local_slots(eid_ref), [1.0] * TOP_K, pm_ref)
    sbuf_ref[slot] = jnp.dot(pm_ref[...], x_ref[...], preferred_element_type=F32)

    @pl.when(t > 0)
    def _():
        run_copy(1 - slot, 0, 0, pl.multiple_of(rows_ref[1 - slot], ROW_ALIGN)).wait()

    _for_each_run(meta_ref, lambda l, g, n: run_copy(slot, l, g, n).start())
    rows_ref[slot] = _run_rows(meta_ref)

    @pl.when(t == pl.num_programs(0) - 1)
    def _():
        run_copy(slot, 0, 0, _run_rows(meta_ref)).wait()


def _dispatch(hl2, eid3, meta3, zmeta, P):
    N, D = hl2.shape
    nt, _, dt = eid3.shape
    nm = meta3.shape[2]
    return pl.pallas_call(
        _dispatch_kernel,
        grid=(nt,),
        in_specs=[pl.BlockSpec((None, 1, nm), lambda i: (i, 0, 0), memory_space=pltpu.SMEM),
                  pl.BlockSpec((1, nm), lambda i: (0, 0), memory_space=pltpu.SMEM),
                  pl.BlockSpec((None, TOP_K, dt), lambda i: (i, 0, 0)),
                  pl.BlockSpec((dt, D), lambda i: (i, 0))],
        out_specs=pl.BlockSpec(memory_space=pl.ANY),
        out_shape=jax.ShapeDtypeStruct((P, D), F32),
        scratch_shapes=[pltpu.VMEM((2, _local_rows(dt), D), F32),
                        pltpu.VMEM((MOE_BLOCK, D), F32),
                        pltpu.VMEM((_local_rows(dt), dt), BF16),
                        pltpu.SMEM((2,), I32),
                        pltpu.SemaphoreType.DMA((2,)),
                        pltpu.SemaphoreType.DMA(())],
        compiler_params=_cparams(1),
        name="dispatch",
    )(meta3, zmeta, eid3, hl2)


def _regroup_kernel(w_ref, sel_ref, o_ref):
    half = w_ref.shape[1] // 2
    g = sel_ref.shape[0]
    wb = w_ref[...].astype(BF16)
    for j in range(w_ref.shape[1] // g):
        r = jnp.dot(wb[:, g * j:g * (j + 1)], sel_ref[...], preferred_element_type=F32)
        o_ref[:, (g // 2) * j:(g // 2) * (j + 1)] = r[:, :g // 2].astype(o_ref.dtype)
        o_ref[:, half + (g // 2) * j:half + (g // 2) * (j + 1)] = r[:, g // 2:].astype(o_ref.dtype)


def _regroup_glu_lin(w1):
    ne, D, de2 = w1.shape
    g = 256
    sel = np.zeros((g, g), np.float32)
    sel[np.arange(0, g, 2), np.arange(g // 2)] = 1.0
    sel[np.arange(1, g, 2), g // 2 + np.arange(g // 2)] = 1.0
    out = pl.pallas_call(
        _regroup_kernel,
        grid=(ne,),
        in_specs=[pl.BlockSpec((D, de2), lambda e: (e, 0)),
                  pl.BlockSpec((g, g), lambda e: (0, 0))],
        out_specs=pl.BlockSpec((D, de2), lambda e: (e, 0)),
        out_shape=jax.ShapeDtypeStruct((ne * D, de2), BF16),
        compiler_params=_cparams(1),
        name="regroup",
    )(w1.reshape(ne * D, de2), jnp.asarray(sel, BF16))
    return out.reshape(ne, D, de2)


def _expert_kernel(be_ref, nu_ref, x_ref, w1_ref, b1_ref, w2_ref, b2_ref, o_ref):
    del be_ref

    @pl.when(pl.program_id(0) < nu_ref[0])
    def _():
        de = w2_ref.shape[0]
        h = jnp.dot(x_ref[...].astype(BF16), w1_ref[...], preferred_element_type=F32) + b1_ref[...]
        glu = jnp.minimum(h[:, :de], SWIGLU_LIMIT)
        lin = jnp.clip(h[:, de:], -SWIGLU_LIMIT, SWIGLU_LIMIT)
        act = glu * _sigmoid(SWIGLU_ALPHA * glu) * (lin + 1.0)
        o_ref[...] = jnp.dot(act.astype(BF16), w2_ref[...].astype(BF16), preferred_element_type=F32) + b2_ref[...]


def _experts(blk_expert, n_used, h_pad, w1b, b1p, w2b, b2):
    P, D = h_pad.shape
    ne, _, de2 = w1b.shape
    de = de2 // 2
    nblk = P // MOE_BLOCK
    blk = lambda j, be, nu: (jnp.minimum(j, nu[0] - 1), 0)
    wsel = lambda j, be, nu: (be[jnp.minimum(j, nu[0] - 1)], 0, 0)
    return pl.pallas_call(
        _expert_kernel,
        grid_spec=pltpu.PrefetchScalarGridSpec(
            num_scalar_prefetch=2,
            grid=(nblk,),
            in_specs=[pl.BlockSpec((MOE_BLOCK, D), blk),
                      pl.BlockSpec((None, D, de2), wsel),
                      pl.BlockSpec((None, 1, de2), wsel),
                      pl.BlockSpec((None, de, D), wsel),
                      pl.BlockSpec((None, 1, D), wsel)],
            out_specs=pl.BlockSpec((MOE_BLOCK, D), blk)),
        out_shape=jax.ShapeDtypeStruct((P, D), F32),
        compiler_params=_cparams(1),
        name="experts",
    )(blk_expert, n_used, h_pad, w1b, b1p.reshape(ne, 1, de2), w2b, b2.reshape(ne, 1, D))


def _combine_kernel(meta_ref, nmeta_ref, eid_ref, gate_ref, x1_ref, g2_ref, fg_ref, yp_ref, o_ref,
                    ybuf_ref, gm_ref, sems):
    t = pl.program_id(0) * pl.num_programs(1) + pl.program_id(1)
    n_tiles = pl.num_programs(0) * pl.num_programs(1)
    slot = t % 2

    def run_copy(sl, l, g, n):
        return pltpu.make_async_copy(yp_ref.at[pl.ds(g, n), :], ybuf_ref.at[sl, pl.ds(l, n), :], sems.at[sl])

    @pl.when(t == 0)
    def _():
        ybuf_ref[...] = jnp.zeros_like(ybuf_ref)
        _for_each_run(meta_ref, lambda l, g, n: run_copy(0, l, g, n).start())

    @pl.when(t + 1 < n_tiles)
    def _():
        _for_each_run(nmeta_ref, lambda l, g, n: run_copy(1 - slot, l, g, n).start())

    run_copy(slot, 0, 0, _run_rows(meta_ref)).wait()

    _slot_matrix(_local_slots(eid_ref), [gate_ref[k:k + 1, :] for k in range(TOP_K)], gm_ref)
    y = _tdot(gm_ref[...], ybuf_ref[slot].astype(BF16))
    x2 = x1_ref[...] + g2_ref[...] * y
    o_ref[...] = _rmsnorm(x2, fg_ref[...])


def _combine(meta3, eid3, gate3, x1, mods3, final_g, y_pad):
    B, S, D = x1.shape
    n_tiles, _, dt = eid3.shape
    nt = S // dt
    nm = meta3.shape[2]
    tile3 = lambda b, i: (b * nt + i, 0, 0)
    next3 = lambda b, i: (jnp.minimum(b * nt + i + 1, n_tiles - 1), 0, 0)
    return pl.pallas_call(
        _combine_kernel,
        grid=(B, nt),
        in_specs=[pl.BlockSpec((None, 1, nm), tile3, memory_space=pltpu.SMEM),
                  pl.BlockSpec((None, 1, nm), next3, memory_space=pltpu.SMEM),
                  pl.BlockSpec((None, TOP_K, dt), tile3),
                  pl.BlockSpec((None, TOP_K, dt), tile3),
                  pl.BlockSpec((None, dt, D), lambda b, i: (b, i, 0)),
                  pl.BlockSpec((None, 1, D), lambda b, i: (b, 0, 5)),
                  pl.BlockSpec((1, D), lambda b, i: (0, 0)),
                  pl.BlockSpec(memory_space=pl.ANY)],
        out_specs=pl.BlockSpec((None, dt, D), lambda b, i: (b, i, 0)),
        out_shape=jax.ShapeDtypeStruct((B, S, D), F32),
        scratch_shapes=[pltpu.VMEM((2, _local_rows(dt), D), F32), pltpu.VMEM((_local_rows(dt), dt), BF16),
                        pltpu.SemaphoreType.DMA((2,))],
        compiler_params=_cparams(2),
        name="combine",
    )(meta3, meta3, eid3, gate3, x1, mods3, final_g.reshape(1, D), y_pad)


def _rope_tables(S, k_scale):
    n_freq = RET_DK // 4
    pos = jnp.arange(S, dtype=F32)
    rows = jnp.floor(pos / GRID_W)
    cols = pos - rows * GRID_W
    inv = ROPE_BASE ** (-jnp.arange(n_freq, dtype=F32) / n_freq)
    ang = jnp.concatenate([rows[:, None] * inv, cols[:, None] * inv], axis=-1)
    cos, sin = jnp.cos(ang), jnp.sin(ang)
    cos2 = jnp.concatenate([cos, cos], axis=-1)
    sin2 = jnp.concatenate([-sin, sin], axis=-1)
    return jnp.concatenate([cos2, sin2, cos2 * k_scale, sin2 * k_scale], axis=-1)


def _identity_tables(L, k_scale):
    one = jnp.ones((L, RET_DK), F32)
    zero = jnp.zeros((L, RET_DK), F32)
    return jnp.concatenate([one, zero, one * k_scale, zero], axis=-1)


def _lanes(t, width):
    return jnp.broadcast_to(t[:, :, None], t.shape + (width,))


def kernel(x, c, ctx, c_ctx, ada_w, ada_b, norm1_g, w_in, conv_w, conv_b, lru_wa, lru_ba, lru_wx, lru_bx,
           lru_lambda, w_rnn_proj, w_ret_proj, w_out, norm2_g, router_w, router_b, moe_w1, moe_b1, moe_w2,
           moe_b2, final_g):
    B, S, D = x.shape
    L = ctx.shape[1]
    N = B * S
    H = RET_HEADS
    lyr = 0
    d_in = w_in.shape[2]
    assert ada_w.shape[0] == 1 and d_in == N_CHUNKS * D and B == 8

    def pairs_apart(w):
        return jnp.swapaxes(w.reshape(D, H, RET_DK // 2, 2), 2, 3).reshape(D, H * RET_DK)

    w_in_b = w_in[lyr].astype(BF16)
    w_qk = jnp.concatenate([pairs_apart(w_in_b[:, 2 * D:3 * D]), pairs_apart(w_in_b[:, 3 * D:4 * D])], axis=1)
    wg = [(0.5 * jnp.concatenate([lru_wa[lyr, d], lru_wx[lyr, d]], axis=-1)).astype(BF16) for d in range(2)]
    de2 = moe_w1.shape[3]
    glu_lin = np.concatenate([np.arange(0, de2, 2), np.arange(1, de2, 2)])
    w1b = _regroup_glu_lin(moe_w1[lyr])
    b1p = moe_b1[lyr][:, glu_lin]
    rwt = router_w[lyr].T
    rwt_hi = rwt.astype(BF16)
    rwt_lo = (rwt - rwt_hi.astype(F32)).astype(BF16)

    k_scale = RET_DK ** -0.5
    tab_l = _rope_tables(S, k_scale)
    tab_c = _identity_tables(B * L, k_scale)
    log_g = jnp.log1p(-jnp.exp2(-5.0 - jnp.arange(H, dtype=F32)))
    C = min(RET_C, S)
    idx = jnp.arange(C, dtype=F32)
    dec = lambda e: jnp.exp(e[None, :] * log_g[:, None])
    intra = jnp.exp(jnp.abs(idx[:, None] - idx[None, :])[None] * log_g[:, None, None])
    qdec_f = _lanes(dec(idx + 1.0), RET_DK)
    qdec_b = _lanes(dec(C - idx), RET_DK)
    kdec_f = _lanes(dec(C - 1.0 - idx), RET_DK)
    kdec_b = _lanes(dec(idx), RET_DK)
    cdec = jnp.broadcast_to(jnp.exp(C * log_g)[:, None], (H, RET_DV))
    pos_c = jnp.arange(L, dtype=F32)
    cdec_f = _lanes(dec(L - 1.0 - pos_c), RET_DK)
    cdec_b = _lanes(dec(pos_c), RET_DK)

    cvec = jnp.zeros((16, D), F32).at[:B].set(c).at[B].set(c_ctx)
    mods3 = _ada(cvec, ada_w[lyr], ada_b[lyr]).reshape(16, 1, 6 * D)

    z_c, xr_c = _proj(ctx.reshape(1, B * L, D), mods3, lambda b: B, norm1_g[lyr], w_in_b, w_qk, tab_c,
                      min(PROJ_TM, B * L))
    xr_c = xr_c.reshape(B, L, D)
    z_l, xr_l = _proj(x, mods3, lambda b: b, norm1_g[lyr], w_in_b, w_qk, tab_l, min(PROJ_TM, S))

    zeros = jnp.zeros((B, D), F32)
    hs = []
    for d in range(2):
        args = (conv_w[lyr], conv_b[lyr], wg[d], 0.5 * lru_ba[lyr, d], 0.5 * lru_bx[lyr, d], lru_lambda[lyr, d])
        _, h0 = _rnn(xr_c, *args, zeros, reverse=(d == 1), tt=RNN_TT)
        h, _ = _rnn(xr_l, *args, h0, reverse=(d == 1), tt=RNN_TT)
        hs.append(h)

    s_fwd, s_bwd = _ret_ctx(z_c, B, cdec_f, cdec_b)
    sb = _ret_bwd(z_l, B, s_bwd, kdec_b, cdec)
    retg = _ret_fwd(z_l, B, sb, s_fwd, intra.astype(BF16), qdec_f.astype(BF16), qdec_b.astype(BF16),
                    kdec_f.astype(BF16), cdec)

    dt = min(DISP_TM, S)
    x1, hl2, eid3, gate3, cnt3 = _finish(
        x, hs[0], hs[1], z_l, retg, mods3, norm2_g[lyr], w_rnn_proj[lyr].astype(BF16),
        w_ret_proj[lyr].astype(BF16), w_out[lyr].astype(BF16), rwt_hi, rwt_lo, router_b[lyr], dt)

    cnt_t = cnt3[:, :, 0].astype(I32)
    cnt_t = (cnt_t + ROW_ALIGN - 1) // ROW_ALIGN * ROW_ALIGN
    cnt = jnp.sum(cnt_t, axis=0)
    padded = (cnt + MOE_BLOCK - 1) // MOE_BLOCK * MOE_BLOCK
    pad_end = jnp.cumsum(padded)
    pad_start = pad_end - padded
    gstart = pad_start[None, :] + jnp.cumsum(cnt_t, axis=0) - cnt_t
    loff = jnp.cumsum(cnt_t, axis=1) - cnt_t
    meta3 = jnp.concatenate([loff, cnt_t, gstart], axis=1).reshape(N // dt, 1, 3 * N_EXPERTS)
    zmeta = jnp.concatenate([jnp.zeros_like(cnt), padded - cnt, pad_start + cnt]).reshape(1, 3 * N_EXPERTS)
    n_blocks = -(-(N * TOP_K + (N // dt) * N_EXPERTS * (ROW_ALIGN - 1)) // MOE_BLOCK) + N_EXPERTS
    P = n_blocks * MOE_BLOCK
    blk_start = jnp.arange(n_blocks, dtype=I32) * MOE_BLOCK
    blk_expert = jnp.minimum(jnp.sum((pad_end[None, :] <= blk_start[:, None]).astype(I32), axis=1), N_EXPERTS - 1)
    n_used = (pad_end[-1:] // MOE_BLOCK).astype(I32)

    h_pad = _dispatch(hl2, eid3, meta3, zmeta, P)
    y_pad = _experts(blk_expert, n_used, h_pad, w1b, b1p, moe_w2[lyr], moe_b2[lyr])
    return _combine(meta3, eid3, gate3, x1, mods3, final_g, y_pad)
```

```python
import functools

import jax
import jax.numpy as jnp
import numpy as np
from jax import lax
from jax.experimental import pallas as pl
from jax.experimental.pallas import tpu as pltpu

F32 = jnp.float32
BF16 = jnp.bfloat16
I32 = jnp.int32

GRID_W = 64
RNN_BLOCKS = 8
CONV_W = 4
LRU_C = 8.0
RET_HEADS = 8
RET_DK = 128
RET_DV = 256
ROPE_BASE = 10000.0
N_EXPERTS = 32
TOP_K = 4
SWIGLU_ALPHA = 1.702
SWIGLU_LIMIT = 7.0
EPS = 1e-6
N_CHUNKS = 10

PROJ_TM = 512
PROJ_TN = 2048
RNN_TT = 256
RET_C = 256
RET_CPS = 4
FIN_TM = 512
MOE_BLOCK = 1024
DISP_TM = 512
ROW_ALIGN = 8
VMEM_LIMIT = 56 * 1024 * 1024


def _cparams(n_axes):
    return pltpu.CompilerParams(dimension_semantics=("arbitrary",) * n_axes,
                                vmem_limit_bytes=VMEM_LIMIT)


def _sigmoid(x):
    return 0.5 * (jnp.tanh(0.5 * x) + 1.0)


def _rmsnorm(x, g):
    return x * lax.rsqrt(jnp.mean(x * x, axis=-1, keepdims=True) + EPS) * g


def _tdot(a, b):
    return lax.dot_general(a, b, (((0,), (0,)), ((), ())), preferred_element_type=F32)


def _ada_kernel(c_ref, w_ref, b_ref, o_ref):
    c = c_ref[...]
    s = c * _sigmoid(c)
    o_ref[...] = jnp.dot(s, w_ref[...], preferred_element_type=F32,
                         precision=lax.Precision.HIGHEST) + b_ref[...]


def _ada(cvec, ada_w, ada_b):
    R, D = cvec.shape
    n = ada_w.shape[1] // D
    return pl.pallas_call(
        _ada_kernel,
        grid=(n,),
        in_specs=[pl.BlockSpec((R, D), lambda j: (0, 0)),
                  pl.BlockSpec((D, D), lambda j: (0, j)),
                  pl.BlockSpec((1, D), lambda j: (0, j))],
        out_specs=pl.BlockSpec((R, D), lambda j: (0, j)),
        out_shape=jax.ShapeDtypeStruct((R, n * D), F32),
        compiler_params=_cparams(1),
        name="ada",
    )(cvec, ada_w, ada_b.reshape(1, -1))


def _proj_kernel(x_ref, sh_ref, sc_ref, g_ref, w_ref, wqk_ref, tab_ref, o_ref, xr_ref):
    D = x_ref.shape[1]
    h = _rmsnorm(x_ref[...], g_ref[...])
    hb = (h * (1.0 + sc_ref[...]) + sh_ref[...]).astype(BF16)
    for j in range(w_ref.shape[1] // PROJ_TN):
        j0 = j * PROJ_TN
        wj = wqk_ref[...] if j == 1 else w_ref[:, j0:j0 + PROJ_TN]
        acc = jnp.dot(hb, wj, preferred_element_type=F32)
        if j == 0:
            xr_ref[...] = acc[:, :D].astype(xr_ref.dtype)
        if j != 1:
            o_ref[:, j0:j0 + PROJ_TN] = acc.astype(o_ref.dtype)
            continue
        for part in range(2):
            cos = tab_ref[:, (2 * part) * RET_DK:(2 * part + 1) * RET_DK]
            sin = tab_ref[:, (2 * part + 1) * RET_DK:(2 * part + 2) * RET_DK]
            for hh in range(RET_HEADS):
                c0 = part * RET_HEADS * RET_DK + hh * RET_DK
                t = acc[:, c0:c0 + RET_DK]
                o_ref[:, j0 + c0:j0 + c0 + RET_DK] = (
                    t * cos + pltpu.roll(t, RET_DK // 2, 1) * sin).astype(o_ref.dtype)


def _proj(x3, mods3, mod_row, norm_g, w_in_b, w_qk, tab, tm):
    B, S, D = x3.shape
    d_in = w_in_b.shape[1]
    return pl.pallas_call(
        _proj_kernel,
        grid=(B, S // tm),
        in_specs=[pl.BlockSpec((None, tm, D), lambda b, i: (b, i, 0)),
                  pl.BlockSpec((None, 1, D), lambda b, i: (mod_row(b), 0, 0)),
                  pl.BlockSpec((None, 1, D), lambda b, i: (mod_row(b), 0, 1)),
                  pl.BlockSpec((1, D), lambda b, i: (0, 0)),
                  pl.BlockSpec((D, d_in), lambda b, i: (0, 0), pipeline_mode=pl.Buffered(1)),
                  pl.BlockSpec((D, PROJ_TN), lambda b, i: (0, 0), pipeline_mode=pl.Buffered(1)),
                  pl.BlockSpec((tm, 4 * RET_DK), lambda b, i: (i, 0))],
        out_specs=[pl.BlockSpec((tm, d_in), lambda b, i: (i, b)),
                   pl.BlockSpec((None, tm, D), lambda b, i: (b, i, 0))],
        out_shape=[jax.ShapeDtypeStruct((S, B * d_in), BF16),
                   jax.ShapeDtypeStruct((B, S, D), BF16)],
        compiler_params=_cparams(2),
        name="proj",
    )(x3, mods3, mods3, norm_g.reshape(1, D), w_in_b, w_qk, tab)


def _rnn_kernel(xm_ref, xp_ref, xn_ref, cw_ref, cb_ref, wg_ref, ba_ref, bx_ref, lam_ref, h0_ref,
                h_ref, hfin_ref, xs_ref, a_ref, u_ref, hc_ref, *, reverse, n_tiles):
    i = pl.program_id(0)
    tile = (n_tiles - 1 - i) if reverse else i
    nb, tt, D = xm_ref.shape
    bw = D // RNN_BLOCKS
    HALO = xp_ref.shape[1]

    @pl.when(i == 0)
    def _():
        hc_ref[...] = h0_ref[...]

    GT = HALO
    gr = GT * nb
    ri = lax.broadcasted_iota(I32, (gr, gr), 0)
    ci = lax.broadcasted_iota(I32, (gr, gr), 1)
    perm = jnp.where((ri // nb == ci % GT) & (ri % nb == ci // GT), 1.0, 0.0).astype(BF16)

    def time_major(ref, t0):
        xg = jnp.concatenate([ref[b, t0:t0 + GT, :] for b in range(nb)], axis=0)
        return jnp.dot(perm, xg, preferred_element_type=F32)

    hr = HALO * nb
    R = tt * nb
    xs_ref[0:hr, :] = jnp.where(tile > 0, time_major(xp_ref, 0), 0.0)
    for g in range(tt // GT):
        xs_ref[hr + g * gr:hr + (g + 1) * gr, :] = time_major(xm_ref, g * GT)
    xs_ref[hr + R:2 * hr + R, :] = jnp.where(tile < n_tiles - 1, time_major(xn_ref, 0), 0.0)

    nl = -lam_ref[...]
    csp = (0.25 * LRU_C) * (jnp.maximum(nl, 0.0) + jnp.log1p(jnp.exp(-jnp.abs(nl))))
    SUB = 256

    def gates(s, carry):
        r0 = pl.multiple_of(s * SUB, SUB)
        base = hr - 2 * nb
        xc = cb_ref[...] + cw_ref[0:1, :] * xs_ref[pl.ds(pl.multiple_of(r0 + base, nb), SUB), :]
        for k in range(1, CONV_W):
            xc = xc + cw_ref[k:k + 1, :] * xs_ref[pl.ds(pl.multiple_of(r0 + base + k * nb, nb), SUB), :]
        xb = xc.astype(BF16)
        for n in range(RNN_BLOCKS):
            g = jnp.dot(xb[:, n * bw:(n + 1) * bw], wg_ref[n], preferred_element_type=F32)
            cs = slice(n * bw, (n + 1) * bw)
            tr = jnp.tanh(g[:, :bw] + ba_ref[:, cs])
            ti = jnp.tanh(g[:, bw:] + bx_ref[:, cs])
            t = jnp.tanh(csp[:, cs] * tr + csp[:, cs])
            rc = 1.0 / (1.0 + t)
            a_ref[pl.ds(r0, SUB), cs] = (1.0 - t) * rc
            root = t * lax.rsqrt(jnp.maximum(t, 1e-30))
            u_ref[pl.ds(r0, SUB), cs] = (root * rc) * ((ti + 1.0) * xc[:, cs])
        return carry

    lax.fori_loop(0, R // SUB, gates, 0)

    def step(t, h):
        ts = (tt - 1 - t) if reverse else t
        r0 = pl.multiple_of(ts * nb, nb)
        h = a_ref[pl.ds(r0, nb), :] * h + u_ref[pl.ds(r0, nb), :]
        u_ref[pl.ds(r0, nb), :] = h
        return h

    h = lax.fori_loop(0, tt, step, hc_ref[...], unroll=8)
    hc_ref[...] = h
    hfin_ref[...] = h
    for g in range(tt // GT):
        hg = _tdot(perm, u_ref[g * gr:(g + 1) * gr, :].astype(BF16))
        for b in range(nb):
            h_ref[b, g * GT:(g + 1) * GT, :] = hg[b * GT:(b + 1) * GT, :].astype(h_ref.dtype)


def _rnn(xr, conv_w, conv_b, wg, ba, bx, lam, h0, *, reverse, tt):
    nb, T, D = xr.shape
    tt = min(tt, T)
    n_tiles = T // tt
    HALO = 16
    hb = tt // HALO

    def tile_of(i):
        return (n_tiles - 1 - i) if reverse else i

    kern = functools.partial(_rnn_kernel, reverse=reverse, n_tiles=n_tiles)
    return pl.pallas_call(
        kern,
        grid=(n_tiles,),
        in_specs=[pl.BlockSpec((nb, tt, D), lambda i: (0, tile_of(i), 0)),
                  pl.BlockSpec((nb, HALO, D), lambda i: (0, jnp.maximum(tile_of(i) * hb - 1, 0), 0)),
                  pl.BlockSpec((nb, HALO, D), lambda i: (0, jnp.minimum((tile_of(i) + 1) * hb, n_tiles * hb - 1), 0)),
                  pl.BlockSpec((CONV_W, D), lambda i: (0, 0)),
                  pl.BlockSpec((1, D), lambda i: (0, 0)),
                  pl.BlockSpec((RNN_BLOCKS, D // RNN_BLOCKS, 2 * D // RNN_BLOCKS), lambda i: (0, 0, 0)),
                  pl.BlockSpec((1, D), lambda i: (0, 0)),
                  pl.BlockSpec((1, D), lambda i: (0, 0)),
                  pl.BlockSpec((1, D), lambda i: (0, 0)),
                  pl.BlockSpec((nb, D), lambda i: (0, 0))],
        out_specs=[pl.BlockSpec((nb, tt, D), lambda i: (0, tile_of(i), 0)),
                   pl.BlockSpec((nb, D), lambda i: (0, 0))],
        out_shape=[jax.ShapeDtypeStruct((nb, T, D), BF16),
                   jax.ShapeDtypeStruct((nb, D), F32)],
        scratch_shapes=[pltpu.VMEM(((tt + 2 * HALO) * nb, D), F32),
                        pltpu.VMEM((nb * tt, D), F32),
                        pltpu.VMEM((nb * tt, D), F32),
                        pltpu.VMEM((nb, D), F32)],
        compiler_params=_cparams(1),
        name="rnn_bwd" if reverse else "rnn_fwd",
    )(xr, xr, xr, conv_w, conv_b.reshape(1, D), wg, ba.reshape(1, D), bx.reshape(1, D),
      lam.reshape(1, D), h0)


def _ret_ctx_kernel(k_ref, v_ref, df_ref, db_ref, sf_ref, sb_ref):
    for hh in range(RET_HEADS):
        kh = k_ref[:, hh * RET_DK:(hh + 1) * RET_DK].astype(F32)
        vh = v_ref[:, hh * RET_DV:(hh + 1) * RET_DV]
        sf_ref[hh] = _tdot((kh * df_ref[hh]).astype(BF16), vh)
        sb_ref[hh] = _tdot((kh * db_ref[hh]).astype(BF16), vh)


def _ret_ctx(zc, B, dec_f, dec_b):
    L = zc.shape[0] // B
    H = RET_HEADS
    st = jax.ShapeDtypeStruct((B, H, RET_DK, RET_DV), F32)
    return pl.pallas_call(
        _ret_ctx_kernel,
        grid=(B,),
        in_specs=[pl.BlockSpec((L, H * RET_DK), lambda b: (b, 3)),
                  pl.BlockSpec((L, H * RET_DV), lambda b: (b, 2)),
                  pl.BlockSpec((H, L, RET_DK), lambda b: (0, 0, 0)),
                  pl.BlockSpec((H, L, RET_DK), lambda b: (0, 0, 0))],
        out_specs=[pl.BlockSpec((None, H, RET_DK, RET_DV), lambda b: (b, 0, 0, 0))] * 2,
        out_shape=[st, st],
        compiler_params=_cparams(1),
        name="ret_ctx",
    )(zc, zc, dec_f, dec_b)


def _ret_bwd_kernel(k_ref, v_ref, s0_ref, kd_ref, cd_ref, o_ref, st_ref):
    @pl.when(pl.program_id(1) == 0)
    def _():
        st_ref[...] = s0_ref[...]

    C = kd_ref.shape[1]
    for cc in reversed(range(k_ref.shape[0] // C)):
        rs = slice(cc * C, (cc + 1) * C)
        for hh in range(RET_HEADS):
            o_ref[cc, hh] = st_ref[hh].astype(o_ref.dtype)
            kh = k_ref[rs, hh * RET_DK:(hh + 1) * RET_DK].astype(F32)
            vh = v_ref[rs, hh * RET_DV:(hh + 1) * RET_DV]
            st_ref[hh] = st_ref[hh] * cd_ref[hh:hh + 1, :] + _tdot((kh * kd_ref[hh]).astype(BF16), vh)


def _ret_bwd(z, B, s_bwd, kdec_b, cdec):
    S = z.shape[0]
    H, C = RET_HEADS, min(RET_C, S)
    cps = min(RET_CPS, S // C)
    n = S // (C * cps)
    return pl.pallas_call(
        _ret_bwd_kernel,
        grid=(B, n),
        in_specs=[pl.BlockSpec((cps * C, H * RET_DK), lambda b, j: (n - 1 - j, b * N_CHUNKS + 3)),
                  pl.BlockSpec((cps * C, H * RET_DV), lambda b, j: (n - 1 - j, b * (N_CHUNKS // 2) + 2)),
                  pl.BlockSpec((None, H, RET_DK, RET_DV), lambda b, j: (b, 0, 0, 0)),
                  pl.BlockSpec((H, C, RET_DK), lambda b, j: (0, 0, 0)),
                  pl.BlockSpec((H, RET_DV), lambda b, j: (0, 0))],
        out_specs=pl.BlockSpec((None, cps, H, RET_DK, RET_DV), lambda b, j: (b, n - 1 - j, 0, 0, 0)),
        out_shape=jax.ShapeDtypeStruct((B, n * cps, H, RET_DK, RET_DV), BF16),
        scratch_shapes=[pltpu.VMEM((H, RET_DK, RET_DV), F32)],
        compiler_params=_cparams(2),
        name="ret_bwd",
    )(z, z, s_bwd, kdec_b, cdec)


def _ret_fwd_kernel(q_ref, k_ref, v_ref, gs_ref, sb_ref, s0_ref, intra_ref, qf_ref, qb_ref, kf_ref, cd_ref,
                    o_ref, st_ref):
    @pl.when(pl.program_id(1) == 0)
    def _():
        st_ref[...] = s0_ref[...]

    C = intra_ref.shape[1]
    for cc in range(q_ref.shape[0] // C):
        rs = slice(cc * C, (cc + 1) * C)
        for hh in range(RET_HEADS):
            qh = q_ref[rs, hh * RET_DK:(hh + 1) * RET_DK]
            kh = k_ref[rs, hh * RET_DK:(hh + 1) * RET_DK]
            vh = v_ref[rs, hh * RET_DV:(hh + 1) * RET_DV]
            s = lax.dot_general(qh, kh, (((1,), (1,)), ((), ())), preferred_element_type=F32)
            o = jnp.dot(s.astype(BF16) * intra_ref[hh], vh, preferred_element_type=F32)
            q2 = jnp.concatenate([qh * qf_ref[hh], qh * qb_ref[hh]], axis=1)
            s2 = jnp.concatenate([st_ref[hh].astype(BF16), sb_ref[cc, hh]], axis=0)
            o = o + jnp.dot(q2, s2, preferred_element_type=F32)
            st_ref[hh] = st_ref[hh] * cd_ref[hh:hh + 1, :] + _tdot(kh * kf_ref[hh], vh)
            mu = jnp.mean(o, axis=-1, keepdims=True)
            d = o - mu
            var = jnp.mean(d * d, axis=-1, keepdims=True)
            g = gs_ref[rs, hh * RET_DV:(hh + 1) * RET_DV]
            gate = g * _sigmoid(g)
            o_ref[rs, hh * RET_DV:(hh + 1) * RET_DV] = (d * lax.rsqrt(var + EPS)).astype(o_ref.dtype) * gate


def _ret_fwd(z, B, sb, s_fwd, intra, qdec_f, qdec_b, kdec_f, cdec):
    S = z.shape[0]
    H, C = RET_HEADS, min(RET_C, S)
    cps = min(RET_CPS, S // C)
    n = S // (C * cps)
    half = N_CHUNKS // 2
    return pl.pallas_call(
        _ret_fwd_kernel,
        grid=(B, n),
        in_specs=[pl.BlockSpec((cps * C, H * RET_DK), lambda b, j: (j, b * N_CHUNKS + 2)),
                  pl.BlockSpec((cps * C, H * RET_DK), lambda b, j: (j, b * N_CHUNKS + 3)),
                  pl.BlockSpec((cps * C, H * RET_DV), lambda b, j: (j, b * half + 2)),
                  pl.BlockSpec((cps * C, H * RET_DV), lambda b, j: (j, b * half + 3)),
                  pl.BlockSpec((None, cps, H, RET_DK, RET_DV), lambda b, j: (b, j, 0, 0, 0)),
                  pl.BlockSpec((None, H, RET_DK, RET_DV), lambda b, j: (b, 0, 0, 0)),
                  pl.BlockSpec((H, C, C), lambda b, j: (0, 0, 0)),
                  pl.BlockSpec((H, C, RET_DK), lambda b, j: (0, 0, 0)),
                  pl.BlockSpec((H, C, RET_DK), lambda b, j: (0, 0, 0)),
                  pl.BlockSpec((H, C, RET_DK), lambda b, j: (0, 0, 0)),
                  pl.BlockSpec((H, RET_DV), lambda b, j: (0, 0))],
        out_specs=pl.BlockSpec((None, cps * C, H * RET_DV), lambda b, j: (b, j, 0)),
        out_shape=jax.ShapeDtypeStruct((B, S, H * RET_DV), BF16),
        scratch_shapes=[pltpu.VMEM((H, RET_DK, RET_DV), F32)],
        compiler_params=_cparams(2),
        name="ret_fwd",
    )(z, z, z, z, sb, s_fwd, intra, qdec_f, qdec_b, kdec_f, cdec)


def _dot_t(a, b):
    return lax.dot_general(a, b, (((1,), (1,)), ((), ())), preferred_element_type=F32)


def _finish_kernel(x_ref, hf_ref, hb_ref, gr_ref, gab_ref, ret_ref, g1_ref, sh2_ref, sc2_ref, n2_ref,
                   wr_ref, wt_ref, wo_ref, rwh_ref, rwl_ref, rb_ref,
                   x1_ref, hl_ref, eid_ref, gate_ref, cnt_ref):
    D = x_ref.shape[1]
    tm = x_ref.shape[0]
    dt = eid_ref.shape[2]
    rnn = hf_ref[...].astype(F32) + hb_ref[...].astype(F32)
    y_rnn = jnp.dot((rnn * jax.nn.gelu(gr_ref[...].astype(F32))).astype(BF16), wr_ref[...],
                    preferred_element_type=F32)
    y_ret = jnp.dot(ret_ref[...], wt_ref[...], preferred_element_type=F32)
    ga = gab_ref[:, :D].astype(F32)
    gb = gab_ref[:, D:].astype(F32)
    merged = _sigmoid(ga) * y_rnn + _sigmoid(gb) * y_ret
    y = jnp.dot(merged.astype(BF16), wo_ref[...], preferred_element_type=F32)
    x1 = x_ref[...] + g1_ref[...] * y
    x1_ref[...] = x1
    hl = _rmsnorm(x1, n2_ref[...]) * (1.0 + sc2_ref[...]) + sh2_ref[...]
    hh = hl.astype(BF16)
    hl_ref[...] = hh

    hlo = (hl - hh.astype(F32)).astype(BF16)
    logits = (_dot_t(rwh_ref[...], hh) + _dot_t(rwh_ref[...], hlo) + _dot_t(rwl_ref[...], hh)) + rb_ref[...]

    ne = logits.shape[0]
    sub = lax.broadcasted_iota(I32, (ne, tm), 0)
    work = logits
    vals, idxs = [], []
    oh = jnp.zeros((ne, tm), F32)
    for _ in range(TOP_K):
        m = jnp.max(work, axis=0, keepdims=True)
        idx = jnp.min(jnp.where(work == m, sub, ne), axis=0, keepdims=True)
        hot = sub == idx
        vals.append(m)
        idxs.append(idx)
        oh = oh + jnp.where(hot, 1.0, 0.0)
        work = jnp.where(hot, -jnp.inf, work)
    es = [jnp.exp(v - vals[0]) for v in vals]
    inv = 1.0 / (es[0] + es[1] + es[2] + es[3])
    for part in range(tm // dt):
        ls = slice(part * dt, (part + 1) * dt)
        for k in range(TOP_K):
            eid_ref[part, k:k + 1, :] = idxs[k][:, ls]
            gate_ref[part, k:k + 1, :] = (es[k] * inv)[:, ls]
        cnt_ref[part] = jnp.sum(oh[:, ls], axis=1, keepdims=True)


def _finish(x3, hf, hb, z, retg, mods3, norm2_g, w_rnn_b, w_ret_b, w_out_b, rwt_hi, rwt_lo, router_b, dt):
    B, S, D = x3.shape
    tm = min(FIN_TM, S)
    nt = S // tm
    N = B * S
    ne = rwt_hi.shape[0]
    half = N_CHUNKS // 2
    per = tm // dt
    const2 = lambda b, i: (0, 0)
    tile3 = lambda b, i: (b * nt + i, 0, 0)
    return pl.pallas_call(
        _finish_kernel,
        grid=(B, nt),
        in_specs=[pl.BlockSpec((None, tm, D), lambda b, i: (b, i, 0)),
                  pl.BlockSpec((None, tm, D), lambda b, i: (b, i, 0)),
                  pl.BlockSpec((None, tm, D), lambda b, i: (b, i, 0)),
                  pl.BlockSpec((tm, D), lambda b, i: (i, b * N_CHUNKS + 1)),
                  pl.BlockSpec((tm, 2 * D), lambda b, i: (i, b * half + 4)),
                  pl.BlockSpec((None, tm, retg.shape[2]), lambda b, i: (b, i, 0)),
                  pl.BlockSpec((None, 1, D), lambda b, i: (b, 0, 2)),
                  pl.BlockSpec((None, 1, D), lambda b, i: (b, 0, 3)),
                  pl.BlockSpec((None, 1, D), lambda b, i: (b, 0, 4)),
                  pl.BlockSpec((1, D), const2),
                  pl.BlockSpec(w_rnn_b.shape, const2),
                  pl.BlockSpec(w_ret_b.shape, const2),
                  pl.BlockSpec(w_out_b.shape, const2),
                  pl.BlockSpec(rwt_hi.shape, const2),
                  pl.BlockSpec(rwt_lo.shape, const2),
                  pl.BlockSpec((ne, 1), const2)],
        out_specs=[pl.BlockSpec((None, tm, D), lambda b, i: (b, i, 0)),
                   pl.BlockSpec((tm, D), lambda b, i: (b * nt + i, 0)),
                   pl.BlockSpec((per, TOP_K, dt), tile3),
                   pl.BlockSpec((per, TOP_K, dt), tile3),
                   pl.BlockSpec((per, ne, 1), tile3)],
        out_shape=[jax.ShapeDtypeStruct((B, S, D), F32),
                   jax.ShapeDtypeStruct((N, D), BF16),
                   jax.ShapeDtypeStruct((N // dt, TOP_K, dt), I32),
                   jax.ShapeDtypeStruct((N // dt, TOP_K, dt), F32),
                   jax.ShapeDtypeStruct((N // dt, ne, 1), F32)],
        compiler_params=_cparams(2),
        name="finish",
    )(x3, hf, hb, z, z, retg, mods3, mods3, mods3, norm2_g.reshape(1, D),
      w_rnn_b, w_ret_b, w_out_b, rwt_hi, rwt_lo, router_b.reshape(ne, 1))


def _local_rows(dt):
    return TOP_K * dt + N_EXPERTS * ROW_ALIGN


def _local_slots(eid_ref):
    dt = eid_ref.shape[1]
    ne = N_EXPERTS
    sub = lax.broadcasted_iota(I32, (ne, dt), 0)
    hots = [sub == eid_ref[k:k + 1, :] for k in range(TOP_K)]
    oh = jnp.zeros((ne, dt), F32)
    for hot in hots:
        oh = oh + jnp.where(hot, 1.0, 0.0)
    earlier = jnp.where(lax.broadcasted_iota(I32, (dt, dt), 0) < lax.broadcasted_iota(I32, (dt, dt), 1), 1.0, 0.0)
    before = jnp.dot(oh.astype(BF16), earlier.astype(BF16), preferred_element_type=F32)
    cnt = jnp.broadcast_to(jnp.sum(oh, axis=1, keepdims=True), (ne, dt))
    cnt = jnp.ceil(cnt * (1.0 / ROW_ALIGN)) * ROW_ALIGN
    lower = jnp.where(lax.broadcasted_iota(I32, (ne, ne), 1) < lax.broadcasted_iota(I32, (ne, ne), 0), 1.0, 0.0)
    base = before + jnp.dot(lower.astype(BF16), cnt.astype(BF16), preferred_element_type=F32)
    return [jnp.sum(jnp.where(hot, base, 0.0), axis=0, keepdims=True).astype(I32) for hot in hots]


def _slot_matrix(slots, weights, out_ref):
    rows, dt = out_ref.shape
    ch = 64
    rel = lax.broadcasted_iota(I32, (ch, dt), 0).astype(F32).astype(BF16)
    slots_f = [s.astype(F32) for s in slots]
    weights_b = [jnp.asarray(w, F32).astype(BF16) for w in weights]
    zero = jnp.zeros((), BF16)
    for c in range(rows // ch):
        acc = None
        for s, w in zip(slots_f, weights_b):
            term = jnp.where(rel == (s - float(c * ch)).astype(BF16), w, zero)
            acc = term if acc is None else acc + term
        out_ref[c * ch:(c + 1) * ch, :] = acc


def _for_each_run(meta_ref, fn):
    def body(e, c):
        n = pl.multiple_of(meta_ref[0, N_EXPERTS + e], ROW_ALIGN)

        @pl.when(n > 0)
        def _():
            fn(pl.multiple_of(meta_ref[0, e], ROW_ALIGN), pl.multiple_of(meta_ref[0, 2 * N_EXPERTS + e], ROW_ALIGN), n)

        return c

    lax.fori_loop(0, N_EXPERTS, body, 0)


def _run_rows(meta_ref):
    last = N_EXPERTS - 1
    return pl.multiple_of(meta_ref[0, last] + meta_ref[0, N_EXPERTS + last], ROW_ALIGN)


def _dispatch_kernel(meta_ref, zmeta_ref, eid_ref, x_ref, hp_ref, sbuf_ref, zbuf_ref, pm_ref, rows_ref, sems, zsem):
    t = pl.program_id(0)
    slot = t % 2

    def zero_copy(_, g, n):
        return pltpu.make_async_copy(zbuf_ref.at[pl.ds(0, n), :], hp_ref.at[pl.ds(g, n), :], zsem)

    def run_copy(sl, l, g, n):
        return pltpu.make_async_copy(sbuf_ref.at[sl, pl.ds(l, n), :], hp_ref.at[pl.ds(g, n), :], sems.at[sl])

    @pl.when(t == 0)
    def _():
        zbuf_ref[...] = jnp.zeros_like(zbuf_ref)
        _for_each_run(zmeta_ref, lambda l, g, n: zero_copy(l, g, n).start())
        _for_each_run(zmeta_ref, lambda l, g, n: zero_copy(l, g, n).wait())

    _slot_matrix(_local_slots(eid_ref), [1.0] * TOP_K, pm_ref)
    sbuf_ref[slot] = jnp.dot(pm_ref[...], x_ref[...], preferred_element_type=F32)

    @pl.when(t > 0)
    def _():
        run_copy(1 - slot, 0, 0, pl.multiple_of(rows_ref[1 - slot], ROW_ALIGN)).wait()

    _for_each_run(meta_ref, lambda l, g, n: run_copy(slot, l, g, n).start())
    rows_ref[slot] = _run_rows(meta_ref)

    @pl.when(t == pl.num_programs(0) - 1)
    def _():
        run_copy(slot, 0, 0, _run_rows(meta_ref)).wait()


def _dispatch(hl2, eid3, meta3, zmeta, P):
    N, D = hl2.shape
    nt, _, dt = eid3.shape
    nm = meta3.shape[2]
    return pl.pallas_call(
        _dispatch_kernel,
        grid=(nt,),
        in_specs=[pl.BlockSpec((None, 1, nm), lambda i: (i, 0, 0), memory_space=pltpu.SMEM),
                  pl.BlockSpec((1, nm), lambda i: (0, 0), memory_space=pltpu.SMEM),
                  pl.BlockSpec((None, TOP_K, dt), lambda i: (i, 0, 0)),
                  pl.BlockSpec((dt, D), lambda i: (i, 0))],
        out_specs=pl.BlockSpec(memory_space=pl.ANY),
        out_shape=jax.ShapeDtypeStruct((P, D), F32),
        scratch_shapes=[pltpu.VMEM((2, _local_rows(dt), D), F32),
                        pltpu.VMEM((MOE_BLOCK, D), F32),
                        pltpu.VMEM((_local_rows(dt), dt), BF16),
                        pltpu.SMEM((2,), I32),
                        pltpu.SemaphoreType.DMA((2,)),
                        pltpu.SemaphoreType.DMA(())],
        compiler_params=_cparams(1),
        name="dispatch",
    )(meta3, zmeta, eid3, hl2)


def _regroup_kernel(w_ref, sel_ref, o_ref):
    half = w_ref.shape[1] // 2
    g = sel_ref.shape[0]
    wb = w_ref[...].astype(BF16)
    for j in range(w_ref.shape[1] // g):
        r = jnp.dot(wb[:, g * j:g * (j + 1)], sel_ref[...], preferred_element_type=F32)
        o_ref[:, (g // 2) * j:(g // 2) * (j + 1)] = r[:, :g // 2].astype(o_ref.dtype)
        o_ref[:, half + (g // 2) * j:half + (g // 2) * (j + 1)] = r[:, g // 2:].astype(o_ref.dtype)


def _regroup_glu_lin(w1):
    ne, D, de2 = w1.shape
    g = 256
    sel = np.zeros((g, g), np.float32)
    sel[np.arange(0, g, 2), np.arange(g // 2)] = 1.0
    sel[np.arange(1, g, 2), g // 2 + np.arange(g // 2)] = 1.0
    out = pl.pallas_call(
        _regroup_kernel,
        grid=(ne,),
        in_specs=[pl.BlockSpec((D, de2), lambda e: (e, 0)),
                  pl.BlockSpec((g, g), lambda e: (0, 0))],
        out_specs=pl.BlockSpec((D, de2), lambda e: (e, 0)),
        out_shape=jax.ShapeDtypeStruct((ne * D, de2), BF16),
        compiler_params=_cparams(1),
        name="regroup",
    )(w1.reshape(ne * D, de2), jnp.asarray(sel, BF16))
    return out.reshape(ne, D, de2)


def _expert_kernel(be_ref, nu_ref, x_ref, w1_ref, b1_ref, w2_ref, b2_ref, o_ref):
    del be_ref

    @pl.when(pl.program_id(0) < nu_ref[0])
    def _():
        de = w2_ref.shape[0]
        h = jnp.dot(x_ref[...].astype(BF16), w1_ref[...], preferred_element_type=F32) + b1_ref[...]
        glu = jnp.minimum(h[:, :de], SWIGLU_LIMIT)
        lin = jnp.clip(h[:, de:], -SWIGLU_LIMIT, SWIGLU_LIMIT)
        act = glu * _sigmoid(SWIGLU_ALPHA * glu) * (lin + 1.0)
        o_ref[...] = jnp.dot(act.astype(BF16), w2_ref[...].astype(BF16), preferred_element_type=F32) + b2_ref[...]


def _experts(blk_expert, n_used, h_pad, w1b, b1p, w2b, b2):
    P, D = h_pad.shape
    ne, _, de2 = w1b.shape
    de = de2 // 2
    nblk = P // MOE_BLOCK
    blk = lambda j, be, nu: (jnp.minimum(j, nu[0] - 1), 0)
    wsel = lambda j, be, nu: (be[jnp.minimum(j, nu[0] - 1)], 0, 0)
    return pl.pallas_call(
        _expert_kernel,
        grid_spec=pltpu.PrefetchScalarGridSpec(
            num_scalar_prefetch=2,
            grid=(nblk,),
            in_specs=[pl.BlockSpec((MOE_BLOCK, D), blk),
                      pl.BlockSpec((None, D, de2), wsel),
                      pl.BlockSpec((None, 1, de2), wsel),
                      pl.BlockSpec((None, de, D), wsel),
                      pl.BlockSpec((None, 1, D), wsel)],
            out_specs=pl.BlockSpec((MOE_BLOCK, D), blk)),
        out_shape=jax.ShapeDtypeStruct((P, D), F32),
        compiler_params=_cparams(1),
        name="experts",
    )(blk_expert, n_used, h_pad, w1b, b1p.reshape(ne, 1, de2), w2b, b2.reshape(ne, 1, D))


def _combine_kernel(meta_ref, nmeta_ref, eid_ref, gate_ref, x1_ref, g2_ref, fg_ref, yp_ref, o_ref,
                    ybuf_ref, gm_ref, sems):
    t = pl.program_id(0) * pl.num_programs(1) + pl.program_id(1)
    n_tiles = pl.num_programs(0) * pl.num_programs(1)
    slot = t % 2

    def run_copy(sl, l, g, n):
        return pltpu.make_async_copy(yp_ref.at[pl.ds(g, n), :], ybuf_ref.at[sl, pl.ds(l, n), :], sems.at[sl])

    @pl.when(t == 0)
    def _():
        ybuf_ref[...] = jnp.zeros_like(ybuf_ref)
        _for_each_run(meta_ref, lambda l, g, n: run_copy(0, l, g, n).start())

    @pl.when(t + 1 < n_tiles)
    def _():
        _for_each_run(nmeta_ref, lambda l, g, n: run_copy(1 - slot, l, g, n).start())

    run_copy(slot, 0, 0, _run_rows(meta_ref)).wait()

    _slot_matrix(_local_slots(eid_ref), [gate_ref[k:k + 1, :] for k in range(TOP_K)], gm_ref)
    y = _tdot(gm_ref[...], ybuf_ref[slot].astype(BF16))
    x2 = x1_ref[...] + g2_ref[...] * y
    o_ref[...] = _rmsnorm(x2, fg_ref[...])


def _combine(meta3, eid3, gate3, x1, mods3, final_g, y_pad):
    B, S, D = x1.shape
    n_tiles, _, dt = eid3.shape
    nt = S // dt
    nm = meta3.shape[2]
    tile3 = lambda b, i: (b * nt + i, 0, 0)
    next3 = lambda b, i: (jnp.minimum(b * nt + i + 1, n_tiles - 1), 0, 0)
    return pl.pallas_call(
        _combine_kernel,
        grid=(B, nt),
        in_specs=[pl.BlockSpec((None, 1, nm), tile3, memory_space=pltpu.SMEM),
                  pl.BlockSpec((None, 1, nm), next3, memory_space=pltpu.SMEM),
                  pl.BlockSpec((None, TOP_K, dt), tile3),
                  pl.BlockSpec((None, TOP_K, dt), tile3),
                  pl.BlockSpec((None, dt, D), lambda b, i: (b, i, 0)),
                  pl.BlockSpec((None, 1, D), lambda b, i: (b, 0, 5)),
                  pl.BlockSpec((1, D), lambda b, i: (0, 0)),
                  pl.BlockSpec(memory_space=pl.ANY)],
        out_specs=pl.BlockSpec((None, dt, D), lambda b, i: (b, i, 0)),
        out_shape=jax.ShapeDtypeStruct((B, S, D), F32),
        scratch_shapes=[pltpu.VMEM((2, _local_rows(dt), D), F32), pltpu.VMEM((_local_rows(dt), dt), BF16),
                        pltpu.SemaphoreType.DMA((2,))],
        compiler_params=_cparams(2),
        name="combine",
    )(meta3, meta3, eid3, gate3, x1, mods3, final_g.reshape(1, D), y_pad)


def _rope_tables(S, k_scale):
    n_freq = RET_DK // 4
    pos = np.arange(S)
    rows = (pos // GRID_W).astype(np.float32)
    cols = (pos % GRID_W).astype(np.float32)
    inv = (np.float32(ROPE_BASE) ** (-np.arange(n_freq, dtype=np.float32) / np.float32(n_freq))).astype(np.float32)
    ang = np.concatenate([rows[:, None] * inv, cols[:, None] * inv], axis=-1).astype(np.float32)
    cos, sin = np.cos(ang), np.sin(ang)
    cos2 = np.concatenate([cos, cos], axis=-1)
    sin2 = np.concatenate([-sin, sin], axis=-1)
    ks = np.float32(k_scale)
    return jnp.asarray(np.concatenate([cos2, sin2, cos2 * ks, sin2 * ks], axis=-1), F32)


def _identity_tables(L, k_scale):
    one = np.ones((L, RET_DK), np.float32)
    zero = np.zeros((L, RET_DK), np.float32)
    return jnp.asarray(np.concatenate([one, zero, one * np.float32(k_scale), zero], axis=-1), F32)


def _lanes(t, width):
    return np.ascontiguousarray(np.broadcast_to(t[:, :, None], t.shape + (width,)))


def kernel(x, c, ctx, c_ctx, ada_w, ada_b, norm1_g, w_in, conv_w, conv_b, lru_wa, lru_ba, lru_wx, lru_bx,
           lru_lambda, w_rnn_proj, w_ret_proj, w_out, norm2_g, router_w, router_b, moe_w1, moe_b1, moe_w2,
           moe_b2, final_g):
    B, S, D = x.shape
    L = ctx.shape[1]
    N = B * S
    H = RET_HEADS
    lyr = 0
    d_in = w_in.shape[2]
    assert ada_w.shape[0] == 1 and d_in == N_CHUNKS * D and B == 8

    def pairs_apart(w):
        return jnp.swapaxes(w.reshape(D, H, RET_DK // 2, 2), 2, 3).reshape(D, H * RET_DK)

    w_in_b = w_in[lyr].astype(BF16)
    w_qk = jnp.concatenate([pairs_apart(w_in_b[:, 2 * D:3 * D]), pairs_apart(w_in_b[:, 3 * D:4 * D])], axis=1)
    wg = [(0.5 * jnp.concatenate([lru_wa[lyr, d], lru_wx[lyr, d]], axis=-1)).astype(BF16) for d in range(2)]
    de2 = moe_w1.shape[3]
    glu_lin = np.concatenate([np.arange(0, de2, 2), np.arange(1, de2, 2)])
    w1b = _regroup_glu_lin(moe_w1[lyr])
    b1p = moe_b1[lyr][:, glu_lin]
    rwt = router_w[lyr].T
    rwt_hi = rwt.astype(BF16)
    rwt_lo = (rwt - rwt_hi.astype(F32)).astype(BF16)

    k_scale = RET_DK ** -0.5
    tab_l = _rope_tables(S, k_scale)
    tab_c = _identity_tables(B * L, k_scale)
    f4 = np.float32
    log_g = np.log1p(-np.exp2(-5.0 - np.arange(H, dtype=f4))).astype(f4)
    C = min(RET_C, S)
    idx = np.arange(C, dtype=f4)
    dec = lambda e: np.exp(e[None, :].astype(f4) * log_g[:, None]).astype(f4)
    intra = jnp.asarray(np.exp(np.abs(idx[:, None] - idx[None, :])[None] * log_g[:, None, None]).astype(f4))
    qdec_f = jnp.asarray(_lanes(dec(idx + 1.0), RET_DK))
    qdec_b = jnp.asarray(_lanes(dec(C - idx), RET_DK))
    kdec_f = jnp.asarray(_lanes(dec(C - 1.0 - idx), RET_DK))
    kdec_b = jnp.asarray(_lanes(dec(idx), RET_DK))
    cdec = jnp.asarray(np.broadcast_to(np.exp(C * log_g).astype(f4)[:, None], (H, RET_DV)))
    pos_c = np.arange(L, dtype=f4)
    cdec_f = jnp.asarray(_lanes(dec(L - 1.0 - pos_c), RET_DK))
    cdec_b = jnp.asarray(_lanes(dec(pos_c), RET_DK))

    cvec = jnp.zeros((16, D), F32).at[:B].set(c).at[B].set(c_ctx)
    mods3 = _ada(cvec, ada_w[lyr], ada_b[lyr]).reshape(16, 1, 6 * D)

    z_c, xr_c = _proj(ctx.reshape(1, B * L, D), mods3, lambda b: B, norm1_g[lyr], w_in_b, w_qk, tab_c,
                      min(PROJ_TM, B * L))
    xr_c = xr_c.reshape(B, L, D)
    z_l, xr_l = _proj(x, mods3, lambda b: b, norm1_g[lyr], w_in_b, w_qk, tab_l, min(PROJ_TM, S))

    zeros = jnp.zeros((B, D), F32)
    hs = []
    for d in range(2):
        args = (conv_w[lyr], conv_b[lyr], wg[d], 0.5 * lru_ba[lyr, d], 0.5 * lru_bx[lyr, d], lru_lambda[lyr, d])
        _, h0 = _rnn(xr_c, *args, zeros, reverse=(d == 1), tt=RNN_TT)
        h, _ = _rnn(xr_l, *args, h0, reverse=(d == 1), tt=RNN_TT)
        hs.append(h)

    s_fwd, s_bwd = _ret_ctx(z_c, B, cdec_f, cdec_b)
    sb = _ret_bwd(z_l, B, s_bwd, kdec_b, cdec)
    retg = _ret_fwd(z_l, B, sb, s_fwd, intra.astype(BF16), qdec_f.astype(BF16), qdec_b.astype(BF16),
                    kdec_f.astype(BF16), cdec)

    dt = min(DISP_TM, S)
    x1, hl2, eid3, gate3, cnt3 = _finish(
        x, hs[0], hs[1], z_l, retg, mods3, norm2_g[lyr], w_rnn_proj[lyr].astype(BF16),
        w_ret_proj[lyr].astype(BF16), w_out[lyr].astype(BF16), rwt_hi, rwt_lo, router_b[lyr], dt)

    cnt_t = cnt3[:, :, 0].astype(I32)
    cnt_t = (cnt_t + ROW_ALIGN - 1) // ROW_ALIGN * ROW_ALIGN
    cnt = jnp.sum(cnt_t, axis=0)
    padded = (cnt + MOE_BLOCK - 1) // MOE_BLOCK * MOE_BLOCK
    pad_end = jnp.cumsum(padded)
    pad_start = pad_end - padded
    gstart = pad_start[None, :] + jnp.cumsum(cnt_t, axis=0) - cnt_t
    loff = jnp.cumsum(cnt_t, axis=1) - cnt_t
    meta3 = jnp.concatenate([loff, cnt_t, gstart], axis=1).reshape(N // dt, 1, 3 * N_EXPERTS)
    zmeta = jnp.concatenate([jnp.zeros_like(cnt), padded - cnt, pad_start + cnt]).reshape(1, 3 * N_EXPERTS)
    n_blocks = -(-(N * TOP_K + (N // dt) * N_EXPERTS * (ROW_ALIGN - 1)) // MOE_BLOCK) + N_EXPERTS
    P = n_blocks * MOE_BLOCK
    blk_start = jnp.arange(n_blocks, dtype=I32) * MOE_BLOCK
    blk_expert = jnp.minimum(jnp.sum((pad_end[None, :] <= blk_start[:, None]).astype(I32), axis=1), N_EXPERTS - 1)
    n_used = (pad_end[-1:] // MOE_BLOCK).astype(I32)

    h_pad = _dispatch(hl2, eid3, meta3, zmeta, P)
    y_pad = _experts(blk_expert, n_used, h_pad, w1b, b1p, moe_w2[lyr], moe_b2[lyr])
    return _combine(meta3, eid3, gate3, x1, mods3, final_g, y_pad)
```

```python
import functools

import jax
import jax.numpy as jnp
import numpy as np
from jax import lax
from jax.experimental import pallas as pl
from jax.experimental.pallas import tpu as pltpu

F32 = jnp.float32
BF16 = jnp.bfloat16
I32 = jnp.int32

GRID_W = 64
RNN_BLOCKS = 8
CONV_W = 4
LRU_C = 8.0
RET_HEADS = 8
RET_DK = 128
RET_DV = 256
ROPE_BASE = 10000.0
N_EXPERTS = 32
TOP_K = 4
SWIGLU_ALPHA = 1.702
SWIGLU_LIMIT = 7.0
EPS = 1e-6
N_CHUNKS = 10

PROJ_TM = 512
PROJ_TN = 2048
RNN_TT = 256
RET_C = 256
RET_CPS = 4
FIN_TM = 512
MOE_BLOCK = 1024
DISP_TM = 512
ROW_ALIGN = 8
VMEM_LIMIT = 56 * 1024 * 1024


def _cparams(n_axes):
    return pltpu.CompilerParams(dimension_semantics=("arbitrary",) * n_axes,
                                vmem_limit_bytes=VMEM_LIMIT)


def _sigmoid(x):
    return 0.5 * (jnp.tanh(0.5 * x) + 1.0)


def _rmsnorm(x, g):
    return x * lax.rsqrt(jnp.mean(x * x, axis=-1, keepdims=True) + EPS) * g


def _tdot(a, b):
    return lax.dot_general(a, b, (((0,), (0,)), ((), ())), preferred_element_type=F32)


def _ada_kernel(c_ref, w_ref, b_ref, o_ref):
    c = c_ref[...]
    s = c * _sigmoid(c)
    o_ref[...] = jnp.dot(s, w_ref[...], preferred_element_type=F32,
                         precision=lax.Precision.HIGHEST) + b_ref[...]


def _ada(cvec, ada_w, ada_b):
    R, D = cvec.shape
    n = ada_w.shape[1] // D
    return pl.pallas_call(
        _ada_kernel,
        grid=(n,),
        in_specs=[pl.BlockSpec((R, D), lambda j: (0, 0)),
                  pl.BlockSpec((D, D), lambda j: (0, j)),
                  pl.BlockSpec((1, D), lambda j: (0, j))],
        out_specs=pl.BlockSpec((R, D), lambda j: (0, j)),
        out_shape=jax.ShapeDtypeStruct((R, n * D), F32),
        compiler_params=_cparams(1),
        name="ada",
    )(cvec, ada_w, ada_b.reshape(1, -1))


def _proj_kernel(x_ref, sh_ref, sc_ref, g_ref, w_ref, wqk_ref, tab_ref, o_ref, xr_ref):
    D = x_ref.shape[1]
    h = _rmsnorm(x_ref[...], g_ref[...])
    hb = (h * (1.0 + sc_ref[...]) + sh_ref[...]).astype(BF16)
    for j in range(w_ref.shape[1] // PROJ_TN):
        j0 = j * PROJ_TN
        wj = wqk_ref[...] if j == 1 else w_ref[:, j0:j0 + PROJ_TN]
        acc = jnp.dot(hb, wj, preferred_element_type=F32)
        if j == 0:
            xr_ref[...] = acc[:, :D].astype(xr_ref.dtype)
        if j != 1:
            o_ref[:, j0:j0 + PROJ_TN] = acc.astype(o_ref.dtype)
            continue
        for part in range(2):
            cos = tab_ref[:, (2 * part) * RET_DK:(2 * part + 1) * RET_DK]
            sin = tab_ref[:, (2 * part + 1) * RET_DK:(2 * part + 2) * RET_DK]
            for hh in range(RET_HEADS):
                c0 = part * RET_HEADS * RET_DK + hh * RET_DK
                t = acc[:, c0:c0 + RET_DK]
                o_ref[:, j0 + c0:j0 + c0 + RET_DK] = (
                    t * cos + pltpu.roll(t, RET_DK // 2, 1) * sin).astype(o_ref.dtype)


def _proj(x3, mods3, mod_row, norm_g, w_in_b, w_qk, tab, tm):
    B, S, D = x3.shape
    d_in = w_in_b.shape[1]
    return pl.pallas_call(
        _proj_kernel,
        grid=(B, S // tm),
        in_specs=[pl.BlockSpec((None, tm, D), lambda b, i: (b, i, 0)),
                  pl.BlockSpec((None, 1, D), lambda b, i: (mod_row(b), 0, 0)),
                  pl.BlockSpec((None, 1, D), lambda b, i: (mod_row(b), 0, 1)),
                  pl.BlockSpec((1, D), lambda b, i: (0, 0)),
                  pl.BlockSpec((D, d_in), lambda b, i: (0, 0), pipeline_mode=pl.Buffered(1)),
                  pl.BlockSpec((D, PROJ_TN), lambda b, i: (0, 0), pipeline_mode=pl.Buffered(1)),
                  pl.BlockSpec((tm, 4 * RET_DK), lambda b, i: (i, 0))],
        out_specs=[pl.BlockSpec((tm, d_in), lambda b, i: (i, b)),
                   pl.BlockSpec((None, tm, D), lambda b, i: (b, i, 0))],
        out_shape=[jax.ShapeDtypeStruct((S, B * d_in), BF16),
                   jax.ShapeDtypeStruct((B, S, D), BF16)],
        compiler_params=_cparams(2),
        name="proj",
    )(x3, mods3, mods3, norm_g.reshape(1, D), w_in_b, w_qk, tab)


def _rnn_kernel(xm_ref, xp_ref, xn_ref, cw_ref, cb_ref, wg_ref, ba_ref, bx_ref, lam_ref, h0_ref,
                h_ref, hfin_ref, xs_ref, a_ref, u_ref, hc_ref, *, reverse, n_tiles):
    i = pl.program_id(0)
    tile = (n_tiles - 1 - i) if reverse else i
    nb, tt, D = xm_ref.shape
    bw = D // RNN_BLOCKS
    HALO = xp_ref.shape[1]

    @pl.when(i == 0)
    def _():
        hc_ref[...] = h0_ref[...]

    GT = HALO
    gr = GT * nb
    ri = lax.broadcasted_iota(I32, (gr, gr), 0)
    ci = lax.broadcasted_iota(I32, (gr, gr), 1)
    perm = jnp.where((ri // nb == ci % GT) & (ri % nb == ci // GT), 1.0, 0.0).astype(BF16)

    def time_major(ref, t0):
        xg = jnp.concatenate([ref[b, t0:t0 + GT, :] for b in range(nb)], axis=0)
        return jnp.dot(perm, xg, preferred_element_type=F32)

    hr = HALO * nb
    R = tt * nb
    xs_ref[0:hr, :] = jnp.where(tile > 0, time_major(xp_ref, 0), 0.0)
    for g in range(tt // GT):
        xs_ref[hr + g * gr:hr + (g + 1) * gr, :] = time_major(xm_ref, g * GT)
    xs_ref[hr + R:2 * hr + R, :] = jnp.where(tile < n_tiles - 1, time_major(xn_ref, 0), 0.0)

    nl = -lam_ref[...]
    csp = (0.25 * LRU_C) * (jnp.maximum(nl, 0.0) + jnp.log1p(jnp.exp(-jnp.abs(nl))))
    SUB = 256

    def gates(s, carry):
        r0 = pl.multiple_of(s * SUB, SUB)
        base = hr - 2 * nb
        xc = cb_ref[...] + cw_ref[0:1, :] * xs_ref[pl.ds(pl.multiple_of(r0 + base, nb), SUB), :]
        for k in range(1, CONV_W):
            xc = xc + cw_ref[k:k + 1, :] * xs_ref[pl.ds(pl.multiple_of(r0 + base + k * nb, nb), SUB), :]
        xb = xc.astype(BF16)
        for n in range(RNN_BLOCKS):
            g = jnp.dot(xb[:, n * bw:(n + 1) * bw], wg_ref[n], preferred_element_type=F32)
            cs = slice(n * bw, (n + 1) * bw)
            tr = jnp.tanh(g[:, :bw] + ba_ref[:, cs])
            ti = jnp.tanh(g[:, bw:] + bx_ref[:, cs])
            t = jnp.tanh(csp[:, cs] * tr + csp[:, cs])
            rc = 1.0 / (1.0 + t)
            a_ref[pl.ds(r0, SUB), cs] = (1.0 - t) * rc
            root = t * lax.rsqrt(jnp.maximum(t, 1e-30))
            u_ref[pl.ds(r0, SUB), cs] = (root * rc) * ((ti + 1.0) * xc[:, cs])
        return carry

    lax.fori_loop(0, R // SUB, gates, 0)

    def step(t, h):
        ts = (tt - 1 - t) if reverse else t
        r0 = pl.multiple_of(ts * nb, nb)
        h = a_ref[pl.ds(r0, nb), :] * h + u_ref[pl.ds(r0, nb), :]
        u_ref[pl.ds(r0, nb), :] = h
        return h

    h = lax.fori_loop(0, tt, step, hc_ref[...], unroll=8)
    hc_ref[...] = h
    hfin_ref[...] = h
    for g in range(tt // GT):
        hg = _tdot(perm, u_ref[g * gr:(g + 1) * gr, :].astype(BF16))
        for b in range(nb):
            h_ref[b, g * GT:(g + 1) * GT, :] = hg[b * GT:(b + 1) * GT, :].astype(h_ref.dtype)


def _rnn(xr, conv_w, conv_b, wg, ba, bx, lam, h0, *, reverse, tt):
    nb, T, D = xr.shape
    tt = min(tt, T)
    n_tiles = T // tt
    HALO = 16
    hb = tt // HALO

    def tile_of(i):
        return (n_tiles - 1 - i) if reverse else i

    kern = functools.partial(_rnn_kernel, reverse=reverse, n_tiles=n_tiles)
    return pl.pallas_call(
        kern,
        grid=(n_tiles,),
        in_specs=[pl.BlockSpec((nb, tt, D), lambda i: (0, tile_of(i), 0)),
                  pl.BlockSpec((nb, HALO, D), lambda i: (0, jnp.maximum(tile_of(i) * hb - 1, 0), 0)),
                  pl.BlockSpec((nb, HALO, D), lambda i: (0, jnp.minimum((tile_of(i) + 1) * hb, n_tiles * hb - 1), 0)),
                  pl.BlockSpec((CONV_W, D), lambda i: (0, 0)),
                  pl.BlockSpec((1, D), lambda i: (0, 0)),
                  pl.BlockSpec((RNN_BLOCKS, D // RNN_BLOCKS, 2 * D // RNN_BLOCKS), lambda i: (0, 0, 0)),
                  pl.BlockSpec((1, D), lambda i: (0, 0)),
                  pl.BlockSpec((1, D), lambda i: (0, 0)),
                  pl.BlockSpec((1, D), lambda i: (0, 0)),
                  pl.BlockSpec((nb, D), lambda i: (0, 0))],
        out_specs=[pl.BlockSpec((nb, tt, D), lambda i: (0, tile_of(i), 0)),
                   pl.BlockSpec((nb, D), lambda i: (0, 0))],
        out_shape=[jax.ShapeDtypeStruct((nb, T, D), BF16),
                   jax.ShapeDtypeStruct((nb, D), F32)],
        scratch_shapes=[pltpu.VMEM(((tt + 2 * HALO) * nb, D), F32),
                        pltpu.VMEM((nb * tt, D), F32),
                        pltpu.VMEM((nb * tt, D), F32),
                        pltpu.VMEM((nb, D), F32)],
        compiler_params=_cparams(1),
        name="rnn_bwd" if reverse else "rnn_fwd",
    )(xr, xr, xr, conv_w, conv_b.reshape(1, D), wg, ba.reshape(1, D), bx.reshape(1, D),
      lam.reshape(1, D), h0)


def _ret_ctx_kernel(k_ref, v_ref, df_ref, db_ref, sf_ref, sb_ref):
    for hh in range(RET_HEADS):
        kh = k_ref[:, hh * RET_DK:(hh + 1) * RET_DK].astype(F32)
        vh = v_ref[:, hh * RET_DV:(hh + 1) * RET_DV]
        sf_ref[hh] = _tdot((kh * df_ref[hh]).astype(BF16), vh)
        sb_ref[hh] = _tdot((kh * db_ref[hh]).astype(BF16), vh)


def _ret_ctx(zc, B, dec_f, dec_b):
    L = zc.shape[0] // B
    H = RET_HEADS
    st = jax.ShapeDtypeStruct((B, H, RET_DK, RET_DV), F32)
    return pl.pallas_call(
        _ret_ctx_kernel,
        grid=(B,),
        in_specs=[pl.BlockSpec((L, H * RET_DK), lambda b: (b, 3)),
                  pl.BlockSpec((L, H * RET_DV), lambda b: (b, 2)),
                  pl.BlockSpec((H, L, RET_DK), lambda b: (0, 0, 0)),
                  pl.BlockSpec((H, L, RET_DK), lambda b: (0, 0, 0))],
        out_specs=[pl.BlockSpec((None, H, RET_DK, RET_DV), lambda b: (b, 0, 0, 0))] * 2,
        out_shape=[st, st],
        compiler_params=_cparams(1),
        name="ret_ctx",
    )(zc, zc, dec_f, dec_b)


def _ret_bwd_kernel(k_ref, v_ref, s0_ref, kd_ref, cd_ref, o_ref, st_ref):
    @pl.when(pl.program_id(1) == 0)
    def _():
        st_ref[...] = s0_ref[...]

    C = kd_ref.shape[1]
    for cc in reversed(range(k_ref.shape[0] // C)):
        rs = slice(cc * C, (cc + 1) * C)
        for hh in range(RET_HEADS):
            o_ref[cc, hh] = st_ref[hh].astype(o_ref.dtype)
            kh = k_ref[rs, hh * RET_DK:(hh + 1) * RET_DK].astype(F32)
            vh = v_ref[rs, hh * RET_DV:(hh + 1) * RET_DV]
            st_ref[hh] = st_ref[hh] * cd_ref[hh:hh + 1, :] + _tdot((kh * kd_ref[hh]).astype(BF16), vh)


def _ret_bwd(z, B, s_bwd, kdec_b, cdec):
    S = z.shape[0]
    H, C = RET_HEADS, min(RET_C, S)
    cps = min(RET_CPS, S // C)
    n = S // (C * cps)
    return pl.pallas_call(
        _ret_bwd_kernel,
        grid=(B, n),
        in_specs=[pl.BlockSpec((cps * C, H * RET_DK), lambda b, j: (n - 1 - j, b * N_CHUNKS + 3)),
                  pl.BlockSpec((cps * C, H * RET_DV), lambda b, j: (n - 1 - j, b * (N_CHUNKS // 2) + 2)),
                  pl.BlockSpec((None, H, RET_DK, RET_DV), lambda b, j: (b, 0, 0, 0)),
                  pl.BlockSpec((H, C, RET_DK), lambda b, j: (0, 0, 0)),
                  pl.BlockSpec((H, RET_DV), lambda b, j: (0, 0))],
        out_specs=pl.BlockSpec((None, cps, H, RET_DK, RET_DV), lambda b, j: (b, n - 1 - j, 0, 0, 0)),
        out_shape=jax.ShapeDtypeStruct((B, n * cps, H, RET_DK, RET_DV), BF16),
        scratch_shapes=[pltpu.VMEM((H, RET_DK, RET_DV), F32)],
        compiler_params=_cparams(2),
        name="ret_bwd",
    )(z, z, s_bwd, kdec_b, cdec)


def _ret_fwd_kernel(q_ref, k_ref, v_ref, gs_ref, sb_ref, s0_ref, intra_ref, qf_ref, qb_ref, kf_ref, cd_ref,
                    o_ref, st_ref):
    @pl.when(pl.program_id(1) == 0)
    def _():
        st_ref[...] = s0_ref[...]

    C = intra_ref.shape[1]
    for cc in range(q_ref.shape[0] // C):
        rs = slice(cc * C, (cc + 1) * C)
        for hh in range(RET_HEADS):
            qh = q_ref[rs, hh * RET_DK:(hh + 1) * RET_DK]
            kh = k_ref[rs, hh * RET_DK:(hh + 1) * RET_DK]
            vh = v_ref[rs, hh * RET_DV:(hh + 1) * RET_DV]
            s = lax.dot_general(qh, kh, (((1,), (1,)), ((), ())), preferred_element_type=F32)
            o = jnp.dot(s.astype(BF16) * intra_ref[hh], vh, preferred_element_type=F32)
            q2 = jnp.concatenate([qh * qf_ref[hh], qh * qb_ref[hh]], axis=1)
            s2 = jnp.concatenate([st_ref[hh].astype(BF16), sb_ref[cc, hh]], axis=0)
            o = o + jnp.dot(q2, s2, preferred_element_type=F32)
            st_ref[hh] = st_ref[hh] * cd_ref[hh:hh + 1, :] + _tdot(kh * kf_ref[hh], vh)
            mu = jnp.mean(o, axis=-1, keepdims=True)
            d = o - mu
            var = jnp.mean(d * d, axis=-1, keepdims=True)
            g = gs_ref[rs, hh * RET_DV:(hh + 1) * RET_DV]
            gate = g * _sigmoid(g)
            o_ref[rs, hh * RET_DV:(hh + 1) * RET_DV] = (d * lax.rsqrt(var + EPS)).astype(o_ref.dtype) * gate


def _ret_fwd(z, B, sb, s_fwd, intra, qdec_f, qdec_b, kdec_f, cdec):
    S = z.shape[0]
    H, C = RET_HEADS, min(RET_C, S)
    cps = min(RET_CPS, S // C)
    n = S // (C * cps)
    half = N_CHUNKS // 2
    return pl.pallas_call(
        _ret_fwd_kernel,
        grid=(B, n),
        in_specs=[pl.BlockSpec((cps * C, H * RET_DK), lambda b, j: (j, b * N_CHUNKS + 2)),
                  pl.BlockSpec((cps * C, H * RET_DK), lambda b, j: (j, b * N_CHUNKS + 3)),
                  pl.BlockSpec((cps * C, H * RET_DV), lambda b, j: (j, b * half + 2)),
                  pl.BlockSpec((cps * C, H * RET_DV), lambda b, j: (j, b * half + 3)),
                  pl.BlockSpec((None, cps, H, RET_DK, RET_DV), lambda b, j: (b, j, 0, 0, 0)),
                  pl.BlockSpec((None, H, RET_DK, RET_DV), lambda b, j: (b, 0, 0, 0)),
                  pl.BlockSpec((H, C, C), lambda b, j: (0, 0, 0)),
                  pl.BlockSpec((H, C, RET_DK), lambda b, j: (0, 0, 0)),
                  pl.BlockSpec((H, C, RET_DK), lambda b, j: (0, 0, 0)),
                  pl.BlockSpec((H, C, RET_DK), lambda b, j: (0, 0, 0)),
                  pl.BlockSpec((H, RET_DV), lambda b, j: (0, 0))],
        out_specs=pl.BlockSpec((None, cps * C, H * RET_DV), lambda b, j: (b, j, 0)),
        out_shape=jax.ShapeDtypeStruct((B, S, H * RET_DV), BF16),
        scratch_shapes=[pltpu.VMEM((H, RET_DK, RET_DV), F32)],
        compiler_params=_cparams(2),
        name="ret_fwd",
    )(z, z, z, z, sb, s_fwd, intra, qdec_f, qdec_b, kdec_f, cdec)


def _dot_t(a, b):
    return lax.dot_general(a, b, (((1,), (1,)), ((), ())), preferred_element_type=F32)


def _finish_kernel(x_ref, hf_ref, hb_ref, gr_ref, gab_ref, ret_ref, g1_ref, sh2_ref, sc2_ref, n2_ref,
                   wr_ref, wt_ref, wo_ref, rwh_ref, rwl_ref, rb_ref,
                   x1_ref, hl_ref, eid_ref, gate_ref, cnt_ref):
    D = x_ref.shape[1]
    tm = x_ref.shape[0]
    dt = eid_ref.shape[2]
    rnn = hf_ref[...].astype(F32) + hb_ref[...].astype(F32)
    y_rnn = jnp.dot((rnn * jax.nn.gelu(gr_ref[...].astype(F32))).astype(BF16), wr_ref[...],
                    preferred_element_type=F32)
    y_ret = jnp.dot(ret_ref[...], wt_ref[...], preferred_element_type=F32)
    ga = gab_ref[:, :D].astype(F32)
    gb = gab_ref[:, D:].astype(F32)
    merged = _sigmoid(ga) * y_rnn + _sigmoid(gb) * y_ret
    y = jnp.dot(merged.astype(BF16), wo_ref[...], preferred_element_type=F32)
    x1 = x_ref[...] + g1_ref[...] * y
    x1_ref[...] = x1
    hl = _rmsnorm(x1, n2_ref[...]) * (1.0 + sc2_ref[...]) + sh2_ref[...]
    hh = hl.astype(BF16)
    hl_ref[...] = hh

    hlo = (hl - hh.astype(F32)).astype(BF16)
    logits = (_dot_t(rwh_ref[...], hh) + _dot_t(rwh_ref[...], hlo) + _dot_t(rwl_ref[...], hh)) + rb_ref[...]

    ne = logits.shape[0]
    sub = lax.broadcasted_iota(I32, (ne, tm), 0)
    work = logits
    vals, idxs = [], []
    oh = jnp.zeros((ne, tm), F32)
    for _ in range(TOP_K):
        m = jnp.max(work, axis=0, keepdims=True)
        idx = jnp.min(jnp.where(work == m, sub, ne), axis=0, keepdims=True)
        hot = sub == idx
        vals.append(m)
        idxs.append(idx)
        oh = oh + jnp.where(hot, 1.0, 0.0)
        work = jnp.where(hot, -jnp.inf, work)
    es = [jnp.exp(v - vals[0]) for v in vals]
    inv = 1.0 / (es[0] + es[1] + es[2] + es[3])
    for part in range(tm // dt):
        ls = slice(part * dt, (part + 1) * dt)
        for k in range(TOP_K):
            eid_ref[part, k:k + 1, :] = idxs[k][:, ls]
            gate_ref[part, k:k + 1, :] = (es[k] * inv)[:, ls]
        cnt_ref[part] = jnp.sum(oh[:, ls], axis=1, keepdims=True)


def _finish(x3, hf, hb, z, retg, mods3, norm2_g, w_rnn_b, w_ret_b, w_out_b, rwt_hi, rwt_lo, router_b, dt):
    B, S, D = x3.shape
    tm = min(FIN_TM, S)
    nt = S // tm
    N = B * S
    ne = rwt_hi.shape[0]
    half = N_CHUNKS // 2
    per = tm // dt
    const2 = lambda b, i: (0, 0)
    tile3 = lambda b, i: (b * nt + i, 0, 0)
    return pl.pallas_call(
        _finish_kernel,
        grid=(B, nt),
        in_specs=[pl.BlockSpec((None, tm, D), lambda b, i: (b, i, 0)),
                  pl.BlockSpec((None, tm, D), lambda b, i: (b, i, 0)),
                  pl.BlockSpec((None, tm, D), lambda b, i: (b, i, 0)),
                  pl.BlockSpec((tm, D), lambda b, i: (i, b * N_CHUNKS + 1)),
                  pl.BlockSpec((tm, 2 * D), lambda b, i: (i, b * half + 4)),
                  pl.BlockSpec((None, tm, retg.shape[2]), lambda b, i: (b, i, 0)),
                  pl.BlockSpec((None, 1, D), lambda b, i: (b, 0, 2)),
                  pl.BlockSpec((None, 1, D), lambda b, i: (b, 0, 3)),
                  pl.BlockSpec((None, 1, D), lambda b, i: (b, 0, 4)),
                  pl.BlockSpec((1, D), const2),
                  pl.BlockSpec(w_rnn_b.shape, const2),
                  pl.BlockSpec(w_ret_b.shape, const2),
                  pl.BlockSpec(w_out_b.shape, const2),
                  pl.BlockSpec(rwt_hi.shape, const2),
                  pl.BlockSpec(rwt_lo.shape, const2),
                  pl.BlockSpec((ne, 1), const2)],
        out_specs=[pl.BlockSpec((None, tm, D), lambda b, i: (b, i, 0)),
                   pl.BlockSpec((tm, D), lambda b, i: (b * nt + i, 0)),
                   pl.BlockSpec((per, TOP_K, dt), tile3),
                   pl.BlockSpec((per, TOP_K, dt), tile3),
                   pl.BlockSpec((per, ne, 1), tile3)],
        out_shape=[jax.ShapeDtypeStruct((B, S, D), F32),
                   jax.ShapeDtypeStruct((N, D), BF16),
                   jax.ShapeDtypeStruct((N // dt, TOP_K, dt), I32),
                   jax.ShapeDtypeStruct((N // dt, TOP_K, dt), F32),
                   jax.ShapeDtypeStruct((N // dt, ne, 1), F32)],
        compiler_params=_cparams(2),
        name="finish",
    )(x3, hf, hb, z, z, retg, mods3, mods3, mods3, norm2_g.reshape(1, D),
      w_rnn_b, w_ret_b, w_out_b, rwt_hi, rwt_lo, router_b.reshape(ne, 1))


def _local_rows(dt):
    return TOP_K * dt + N_EXPERTS * ROW_ALIGN


def _local_slots(eid_ref):
    dt = eid_ref.shape[1]
    ne = N_EXPERTS
    sub = lax.broadcasted_iota(I32, (ne, dt), 0)
    hots = [sub == eid_ref[k:k + 1, :] for k in range(TOP_K)]
    oh = jnp.zeros((ne, dt), F32)
    for hot in hots:
        oh = oh + jnp.where(hot, 1.0, 0.0)
    earlier = jnp.where(lax.broadcasted_iota(I32, (dt, dt), 0) < lax.broadcasted_iota(I32, (dt, dt), 1), 1.0, 0.0)
    before = jnp.dot(oh.astype(BF16), earlier.astype(BF16), preferred_element_type=F32)
    cnt = jnp.broadcast_to(jnp.sum(oh, axis=1, keepdims=True), (ne, dt))
    cnt = jnp.ceil(cnt * (1.0 / ROW_ALIGN)) * ROW_ALIGN
    lower = jnp.where(lax.broadcasted_iota(I32, (ne, ne), 1) < lax.broadcasted_iota(I32, (ne, ne), 0), 1.0, 0.0)
    base = before + jnp.dot(lower.astype(BF16), cnt.astype(BF16), preferred_element_type=F32)
    return [jnp.sum(jnp.where(hot, base, 0.0), axis=0, keepdims=True).astype(I32) for hot in hots]


def _slot_matrix(slots, weights, out_ref):
    rows, dt = out_ref.shape
    ch = 64
    rel = lax.broadcasted_iota(I32, (ch, dt), 0).astype(F32).astype(BF16)
    slots_f = [s.astype(F32) for s in slots]
    weights_b = [jnp.asarray(w, F32).astype(BF16) for w in weights]
    zero = jnp.zeros((), BF16)
    for c in range(rows // ch):
        acc = None
        for s, w in zip(slots_f, weights_b):
            term = jnp.where(rel == (s - float(c * ch)).astype(BF16), w, zero)
            acc = term if acc is None else acc + term
        out_ref[c * ch:(c + 1) * ch, :] = acc


def _for_each_run(meta_ref, fn):
    def body(e, c):
        n = pl.multiple_of(meta_ref[0, N_EXPERTS + e], ROW_ALIGN)

        @pl.when(n > 0)
        def _():
            fn(pl.multiple_of(meta_ref[0, e], ROW_ALIGN), pl.multiple_of(meta_ref[0, 2 * N_EXPERTS + e], ROW_ALIGN), n)

        return c

    lax.fori_loop(0, N_EXPERTS, body, 0)


def _run_rows(meta_ref):
    last = N_EXPERTS - 1
    return pl.multiple_of(meta_ref[0, last] + meta_ref[0, N_EXPERTS + last], ROW_ALIGN)


def _dispatch_kernel(meta_ref, zmeta_ref, eid_ref, x_ref, hp_ref, slot_ref, sbuf_ref, zbuf_ref, pm_ref, rows_ref,
                     sems, zsem):
    t = pl.program_id(0)
    slot = t % 2

    def zero_copy(_, g, n):
        return pltpu.make_async_copy(zbuf_ref.at[pl.ds(0, n), :], hp_ref.at[pl.ds(g, n), :], zsem)

    def run_copy(sl, l, g, n):
        return pltpu.make_async_copy(sbuf_ref.at[sl, pl.ds(l, n), :], hp_ref.at[pl.ds(g, n), :], sems.at[sl])

    @pl.when(t == 0)
    def _():
        zbuf_ref[...] = jnp.zeros_like(zbuf_ref)
        _for_each_run(zmeta_ref, lambda l, g, n: zero_copy(l, g, n).start())
        _for_each_run(zmeta_ref, lambda l, g, n: zero_copy(l, g, n).wait())

    slots = _local_slots(eid_ref)
    for k in range(TOP_K):
        slot_ref[k:k + 1, :] = slots[k]
    _slot_matrix(slots, [1.0] * TOP_K, pm_ref)
    sbuf_ref[slot] = jnp.dot(pm_ref[...], x_ref[...], preferred_element_type=F32)

    @pl.when(t > 0)
    def _():
        run_copy(1 - slot, 0, 0, pl.multiple_of(rows_ref[1 - slot], ROW_ALIGN)).wait()

    _for_each_run(meta_ref, lambda l, g, n: run_copy(slot, l, g, n).start())
    rows_ref[slot] = _run_rows(meta_ref)

    @pl.when(t == pl.num_programs(0) - 1)
    def _():
        run_copy(slot, 0, 0, _run_rows(meta_ref)).wait()


def _dispatch(hl2, eid3, meta3, zmeta, P):
    N, D = hl2.shape
    nt, _, dt = eid3.shape
    nm = meta3.shape[2]
    return pl.pallas_call(
        _dispatch_kernel,
        grid=(nt,),
        in_specs=[pl.BlockSpec((None, 1, nm), lambda i: (i, 0, 0), memory_space=pltpu.SMEM),
                  pl.BlockSpec((1, nm), lambda i: (0, 0), memory_space=pltpu.SMEM),
                  pl.BlockSpec((None, TOP_K, dt), lambda i: (i, 0, 0)),
                  pl.BlockSpec((dt, D), lambda i: (i, 0))],
        out_specs=[pl.BlockSpec(memory_space=pl.ANY),
                   pl.BlockSpec((None, TOP_K, dt), lambda i: (i, 0, 0))],
        out_shape=[jax.ShapeDtypeStruct((P, D), F32),
                   jax.ShapeDtypeStruct((nt, TOP_K, dt), I32)],
        scratch_shapes=[pltpu.VMEM((2, _local_rows(dt), D), F32),
                        pltpu.VMEM((MOE_BLOCK, D), F32),
                        pltpu.VMEM((_local_rows(dt), dt), BF16),
                        pltpu.SMEM((2,), I32),
                        pltpu.SemaphoreType.DMA((2,)),
                        pltpu.SemaphoreType.DMA(())],
        compiler_params=_cparams(1),
        name="dispatch",
    )(meta3, zmeta, eid3, hl2)


def _regroup_kernel(w_ref, sel_ref, o_ref):
    half = w_ref.shape[1] // 2
    g = sel_ref.shape[0]
    wb = w_ref[...].astype(BF16)
    for j in range(w_ref.shape[1] // g):
        r = jnp.dot(wb[:, g * j:g * (j + 1)], sel_ref[...], preferred_element_type=F32)
        o_ref[:, (g // 2) * j:(g // 2) * (j + 1)] = r[:, :g // 2].astype(o_ref.dtype)
        o_ref[:, half + (g // 2) * j:half + (g // 2) * (j + 1)] = r[:, g // 2:].astype(o_ref.dtype)


def _regroup_glu_lin(w1):
    ne, D, de2 = w1.shape
    g = 256
    sel = np.zeros((g, g), np.float32)
    sel[np.arange(0, g, 2), np.arange(g // 2)] = 1.0
    sel[np.arange(1, g, 2), g // 2 + np.arange(g // 2)] = 1.0
    out = pl.pallas_call(
        _regroup_kernel,
        grid=(ne,),
        in_specs=[pl.BlockSpec((D, de2), lambda e: (e, 0)),
                  pl.BlockSpec((g, g), lambda e: (0, 0))],
        out_specs=pl.BlockSpec((D, de2), lambda e: (e, 0)),
        out_shape=jax.ShapeDtypeStruct((ne * D, de2), BF16),
        compiler_params=_cparams(1),
        name="regroup",
    )(w1.reshape(ne * D, de2), jnp.asarray(sel, BF16))
    return out.reshape(ne, D, de2)


def _expert_kernel(be_ref, nu_ref, x_ref, w1_ref, b1_ref, w2_ref, b2_ref, o_ref):
    del be_ref

    @pl.when(pl.program_id(0) < nu_ref[0])
    def _():
        de = w2_ref.shape[0]
        h = jnp.dot(x_ref[...].astype(BF16), w1_ref[...], preferred_element_type=F32) + b1_ref[...]
        glu = jnp.minimum(h[:, :de], SWIGLU_LIMIT)
        lin = jnp.clip(h[:, de:], -SWIGLU_LIMIT, SWIGLU_LIMIT)
        act = glu * _sigmoid(SWIGLU_ALPHA * glu) * (lin + 1.0)
        o_ref[...] = jnp.dot(act.astype(BF16), w2_ref[...].astype(BF16), preferred_element_type=F32) + b2_ref[...]


def _experts(blk_expert, n_used, h_pad, w1b, b1p, w2b, b2):
    P, D = h_pad.shape
    ne, _, de2 = w1b.shape
    de = de2 // 2
    nblk = P // MOE_BLOCK
    blk = lambda j, be, nu: (jnp.minimum(j, nu[0] - 1), 0)
    wsel = lambda j, be, nu: (be[jnp.minimum(j, nu[0] - 1)], 0, 0)
    return pl.pallas_call(
        _expert_kernel,
        grid_spec=pltpu.PrefetchScalarGridSpec(
            num_scalar_prefetch=2,
            grid=(nblk,),
            in_specs=[pl.BlockSpec((MOE_BLOCK, D), blk),
                      pl.BlockSpec((None, D, de2), wsel),
                      pl.BlockSpec((None, 1, de2), wsel),
                      pl.BlockSpec((None, de, D), wsel),
                      pl.BlockSpec((None, 1, D), wsel)],
            out_specs=pl.BlockSpec((MOE_BLOCK, D), blk)),
        out_shape=jax.ShapeDtypeStruct((P, D), F32),
        compiler_params=_cparams(1),
        name="experts",
    )(blk_expert, n_used, h_pad, w1b, b1p.reshape(ne, 1, de2), w2b, b2.reshape(ne, 1, D))


def _combine_kernel(meta_ref, nmeta_ref, slot_ref, gate_ref, x1_ref, g2_ref, fg_ref, yp_ref, o_ref,
                    ybuf_ref, gm_ref, sems):
    t = pl.program_id(0) * pl.num_programs(1) + pl.program_id(1)
    n_tiles = pl.num_programs(0) * pl.num_programs(1)
    slot = t % 2

    def run_copy(sl, l, g, n):
        return pltpu.make_async_copy(yp_ref.at[pl.ds(g, n), :], ybuf_ref.at[sl, pl.ds(l, n), :], sems.at[sl])

    @pl.when(t == 0)
    def _():
        ybuf_ref[...] = jnp.zeros_like(ybuf_ref)
        _for_each_run(meta_ref, lambda l, g, n: run_copy(0, l, g, n).start())

    @pl.when(t + 1 < n_tiles)
    def _():
        _for_each_run(nmeta_ref, lambda l, g, n: run_copy(1 - slot, l, g, n).start())

    run_copy(slot, 0, 0, _run_rows(meta_ref)).wait()

    _slot_matrix([slot_ref[k:k + 1, :] for k in range(TOP_K)], [gate_ref[k:k + 1, :] for k in range(TOP_K)], gm_ref)
    y = _tdot(gm_ref[...], ybuf_ref[slot].astype(BF16))
    x2 = x1_ref[...] + g2_ref[...] * y
    o_ref[...] = _rmsnorm(x2, fg_ref[...])


def _combine(meta3, slot3, gate3, x1, mods3, final_g, y_pad):
    B, S, D = x1.shape
    n_tiles, _, dt = slot3.shape
    nt = S // dt
    nm = meta3.shape[2]
    tile3 = lambda b, i: (b * nt + i, 0, 0)
    next3 = lambda b, i: (jnp.minimum(b * nt + i + 1, n_tiles - 1), 0, 0)
    return pl.pallas_call(
        _combine_kernel,
        grid=(B, nt),
        in_specs=[pl.BlockSpec((None, 1, nm), tile3, memory_space=pltpu.SMEM),
                  pl.BlockSpec((None, 1, nm), next3, memory_space=pltpu.SMEM),
                  pl.BlockSpec((None, TOP_K, dt), tile3),
                  pl.BlockSpec((None, TOP_K, dt), tile3),
                  pl.BlockSpec((None, dt, D), lambda b, i: (b, i, 0)),
                  pl.BlockSpec((None, 1, D), lambda b, i: (b, 0, 5)),
                  pl.BlockSpec((1, D), lambda b, i: (0, 0)),
                  pl.BlockSpec(memory_space=pl.ANY)],
        out_specs=pl.BlockSpec((None, dt, D), lambda b, i: (b, i, 0)),
        out_shape=jax.ShapeDtypeStruct((B, S, D), F32),
        scratch_shapes=[pltpu.VMEM((2, _local_rows(dt), D), F32), pltpu.VMEM((_local_rows(dt), dt), BF16),
                        pltpu.SemaphoreType.DMA((2,))],
        compiler_params=_cparams(2),
        name="combine",
    )(meta3, meta3, slot3, gate3, x1, mods3, final_g.reshape(1, D), y_pad)


def _rope_tables(S, k_scale):
    n_freq = RET_DK // 4
    pos = np.arange(S)
    rows = (pos // GRID_W).astype(np.float32)
    cols = (pos % GRID_W).astype(np.float32)
    inv = (np.float32(ROPE_BASE) ** (-np.arange(n_freq, dtype=np.float32) / np.float32(n_freq))).astype(np.float32)
    ang = np.concatenate([rows[:, None] * inv, cols[:, None] * inv], axis=-1).astype(np.float32)
    cos, sin = np.cos(ang), np.sin(ang)
    cos2 = np.concatenate([cos, cos], axis=-1)
    sin2 = np.concatenate([-sin, sin], axis=-1)
    ks = np.float32(k_scale)
    return jnp.asarray(np.concatenate([cos2, sin2, cos2 * ks, sin2 * ks], axis=-1), F32)


def _identity_tables(L, k_scale):
    one = np.ones((L, RET_DK), np.float32)
    zero = np.zeros((L, RET_DK), np.float32)
    return jnp.asarray(np.concatenate([one, zero, one * np.float32(k_scale), zero], axis=-1), F32)


def _lanes(t, width):
    return np.ascontiguousarray(np.broadcast_to(t[:, :, None], t.shape + (width,)))


def kernel(x, c, ctx, c_ctx, ada_w, ada_b, norm1_g, w_in, conv_w, conv_b, lru_wa, lru_ba, lru_wx, lru_bx,
           lru_lambda, w_rnn_proj, w_ret_proj, w_out, norm2_g, router_w, router_b, moe_w1, moe_b1, moe_w2,
           moe_b2, final_g):
    B, S, D = x.shape
    L = ctx.shape[1]
    N = B * S
    H = RET_HEADS
    lyr = 0
    d_in = w_in.shape[2]
    assert ada_w.shape[0] == 1 and d_in == N_CHUNKS * D and B == 8

    def pairs_apart(w):
        return jnp.swapaxes(w.reshape(D, H, RET_DK // 2, 2), 2, 3).reshape(D, H * RET_DK)

    w_in_b = w_in[lyr].astype(BF16)
    w_qk = jnp.concatenate([pairs_apart(w_in_b[:, 2 * D:3 * D]), pairs_apart(w_in_b[:, 3 * D:4 * D])], axis=1)
    wg = [(0.5 * jnp.concatenate([lru_wa[lyr, d], lru_wx[lyr, d]], axis=-1)).astype(BF16) for d in range(2)]
    de2 = moe_w1.shape[3]
    glu_lin = np.concatenate([np.arange(0, de2, 2), np.arange(1, de2, 2)])
    w1b = _regroup_glu_lin(moe_w1[lyr])
    b1p = moe_b1[lyr][:, glu_lin]
    rwt = router_w[lyr].T
    rwt_hi = rwt.astype(BF16)
    rwt_lo = (rwt - rwt_hi.astype(F32)).astype(BF16)

    k_scale = RET_DK ** -0.5
    tab_l = _rope_tables(S, k_scale)
    tab_c = _identity_tables(B * L, k_scale)
    f4 = np.float32
    log_g = np.log1p(-np.exp2(-5.0 - np.arange(H, dtype=f4))).astype(f4)
    C = min(RET_C, S)
    idx = np.arange(C, dtype=f4)
    dec = lambda e: np.exp(e[None, :].astype(f4) * log_g[:, None]).astype(f4)
    intra = jnp.asarray(np.exp(np.abs(idx[:, None] - idx[None, :])[None] * log_g[:, None, None]).astype(f4))
    qdec_f = jnp.asarray(_lanes(dec(idx + 1.0), RET_DK))
    qdec_b = jnp.asarray(_lanes(dec(C - idx), RET_DK))
    kdec_f = jnp.asarray(_lanes(dec(C - 1.0 - idx), RET_DK))
    kdec_b = jnp.asarray(_lanes(dec(idx), RET_DK))
    cdec = jnp.asarray(np.broadcast_to(np.exp(C * log_g).astype(f4)[:, None], (H, RET_DV)))
    pos_c = np.arange(L, dtype=f4)
    cdec_f = jnp.asarray(_lanes(dec(L - 1.0 - pos_c), RET_DK))
    cdec_b = jnp.asarray(_lanes(dec(pos_c), RET_DK))

    cvec = jnp.zeros((16, D), F32).at[:B].set(c).at[B].set(c_ctx)
    mods3 = _ada(cvec, ada_w[lyr], ada_b[lyr]).reshape(16, 1, 6 * D)

    z_c, xr_c = _proj(ctx.reshape(1, B * L, D), mods3, lambda b: B, norm1_g[lyr], w_in_b, w_qk, tab_c,
                      min(PROJ_TM, B * L))
    xr_c = xr_c.reshape(B, L, D)
    z_l, xr_l = _proj(x, mods3, lambda b: b, norm1_g[lyr], w_in_b, w_qk, tab_l, min(PROJ_TM, S))

    zeros = jnp.zeros((B, D), F32)
    hs = []
    for d in range(2):
        args = (conv_w[lyr], conv_b[lyr], wg[d], 0.5 * lru_ba[lyr, d], 0.5 * lru_bx[lyr, d], lru_lambda[lyr, d])
        _, h0 = _rnn(xr_c, *args, zeros, reverse=(d == 1), tt=RNN_TT)
        h, _ = _rnn(xr_l, *args, h0, reverse=(d == 1), tt=RNN_TT)
        hs.append(h)

    s_fwd, s_bwd = _ret_ctx(z_c, B, cdec_f, cdec_b)
    sb = _ret_bwd(z_l, B, s_bwd, kdec_b, cdec)
    retg = _ret_fwd(z_l, B, sb, s_fwd, intra.astype(BF16), qdec_f.astype(BF16), qdec_b.astype(BF16),
                    kdec_f.astype(BF16), cdec)

    dt = min(DISP_TM, S)
    x1, hl2, eid3, gate3, cnt3 = _finish(
        x, hs[0], hs[1], z_l, retg, mods3, norm2_g[lyr], w_rnn_proj[lyr].astype(BF16),
        w_ret_proj[lyr].astype(BF16), w_out[lyr].astype(BF16), rwt_hi, rwt_lo, router_b[lyr], dt)

    cnt_t = cnt3[:, :, 0].astype(I32)
    cnt_t = (cnt_t + ROW_ALIGN - 1) // ROW_ALIGN * ROW_ALIGN
    cnt = jnp.sum(cnt_t, axis=0)
    padded = (cnt + MOE_BLOCK - 1) // MOE_BLOCK * MOE_BLOCK
    pad_end = jnp.cumsum(padded)
    pad_start = pad_end - padded
    gstart = pad_start[None, :] + jnp.cumsum(cnt_t, axis=0) - cnt_t
    loff = jnp.cumsum(cnt_t, axis=1) - cnt_t
    meta3 = jnp.concatenate([loff, cnt_t, gstart], axis=1).reshape(N // dt, 1, 3 * N_EXPERTS)
    zmeta = jnp.concatenate([jnp.zeros_like(cnt), padded - cnt, pad_start + cnt]).reshape(1, 3 * N_EXPERTS)
    n_blocks = -(-(N * TOP_K + (N // dt) * N_EXPERTS * (ROW_ALIGN - 1)) // MOE_BLOCK) + N_EXPERTS
    P = n_blocks * MOE_BLOCK
    blk_start = jnp.arange(n_blocks, dtype=I32) * MOE_BLOCK
    blk_expert = jnp.minimum(jnp.sum((pad_end[None, :] <= blk_start[:, None]).astype(I32), axis=1), N_EXPERTS - 1)
    n_used = (pad_end[-1:] // MOE_BLOCK).astype(I32)

    h_pad, slot3 = _dispatch(hl2, eid3, meta3, zmeta, P)
    y_pad = _experts(blk_expert, n_used, h_pad, w1b, b1p, moe_w2[lyr], moe_b2[lyr])
    return _combine(meta3, slot3, gate3, x1, mods3, final_g, y_pad)
```

```python
import functools

import jax
import jax.numpy as jnp
import numpy as np
from jax import lax
from jax.experimental import pallas as pl
from jax.experimental.pallas import tpu as pltpu

F32 = jnp.float32
BF16 = jnp.bfloat16
I32 = jnp.int32

GRID_W = 64
RNN_BLOCKS = 8
CONV_W = 4
LRU_C = 8.0
RET_HEADS = 8
RET_DK = 128
RET_DV = 256
ROPE_BASE = 10000.0
N_EXPERTS = 32
TOP_K = 4
SWIGLU_ALPHA = 1.702
SWIGLU_LIMIT = 7.0
EPS = 1e-6
N_CHUNKS = 10

PROJ_TM = 512
PROJ_TN = 2048
RNN_TT = 256
RET_C = 256
RET_CPS = 4
FIN_TM = 512
MOE_BLOCK = 1024
DISP_TM = 512
ROW_ALIGN = 8
VMEM_LIMIT = 56 * 1024 * 1024


def _cparams(n_axes):
    return pltpu.CompilerParams(dimension_semantics=("arbitrary",) * n_axes,
                                vmem_limit_bytes=VMEM_LIMIT)


def _sigmoid(x):
    return 0.5 * (jnp.tanh(0.5 * x) + 1.0)


def _rmsnorm(x, g):
    return x * lax.rsqrt(jnp.mean(x * x, axis=-1, keepdims=True) + EPS) * g


U32 = jnp.uint32


def _pack_cols(v):
    w = v.shape[1] // 2
    lo = lax.shift_right_logical(lax.bitcast_convert_type(v[:, :w], U32), jnp.uint32(16))
    hi = lax.bitcast_convert_type(v[:, w:], U32) & jnp.uint32(0xFFFF0000)
    return lo | hi


def _unpack_cols(u):
    a = lax.bitcast_convert_type(lax.shift_left(u, jnp.uint32(16)), F32).astype(BF16)
    b = lax.bitcast_convert_type(u & jnp.uint32(0xFFFF0000), F32).astype(BF16)
    return jnp.concatenate([a, b], axis=1)


def _tdot(a, b):
    return lax.dot_general(a, b, (((0,), (0,)), ((), ())), preferred_element_type=F32)


def _ada_kernel(c_ref, w_ref, b_ref, o_ref):
    c = c_ref[...]
    s = c * _sigmoid(c)
    o_ref[...] = jnp.dot(s, w_ref[...], preferred_element_type=F32,
                         precision=lax.Precision.HIGHEST) + b_ref[...]


def _ada(cvec, ada_w, ada_b):
    R, D = cvec.shape
    n = ada_w.shape[1] // D
    return pl.pallas_call(
        _ada_kernel,
        grid=(n,),
        in_specs=[pl.BlockSpec((R, D), lambda j: (0, 0)),
                  pl.BlockSpec((D, D), lambda j: (0, j)),
                  pl.BlockSpec((1, D), lambda j: (0, j))],
        out_specs=pl.BlockSpec((R, D), lambda j: (0, j)),
        out_shape=jax.ShapeDtypeStruct((R, n * D), F32),
        compiler_params=_cparams(1),
        name="ada",
    )(cvec, ada_w, ada_b.reshape(1, -1))


def _proj_kernel(x_ref, sh_ref, sc_ref, g_ref, w_ref, wqk_ref, tab_ref, o_ref, xr_ref):
    D = x_ref.shape[1]
    h = _rmsnorm(x_ref[...], g_ref[...])
    hb = (h * (1.0 + sc_ref[...]) + sh_ref[...]).astype(BF16)
    for j in range(w_ref.shape[1] // PROJ_TN):
        j0 = j * PROJ_TN
        wj = wqk_ref[...] if j == 1 else w_ref[:, j0:j0 + PROJ_TN]
        acc = jnp.dot(hb, wj, preferred_element_type=F32)
        if j == 0:
            xr_ref[...] = acc[:, :D].astype(xr_ref.dtype)
        if j != 1:
            o_ref[:, j0:j0 + PROJ_TN] = acc.astype(o_ref.dtype)
            continue
        for part in range(2):
            cos = tab_ref[:, (2 * part) * RET_DK:(2 * part + 1) * RET_DK]
            sin = tab_ref[:, (2 * part + 1) * RET_DK:(2 * part + 2) * RET_DK]
            for hh in range(RET_HEADS):
                c0 = part * RET_HEADS * RET_DK + hh * RET_DK
                t = acc[:, c0:c0 + RET_DK]
                o_ref[:, j0 + c0:j0 + c0 + RET_DK] = (
                    t * cos + pltpu.roll(t, RET_DK // 2, 1) * sin).astype(o_ref.dtype)


def _proj(x3, mods3, mod_row, norm_g, w_in_b, w_qk, tab, tm):
    B, S, D = x3.shape
    d_in = w_in_b.shape[1]
    return pl.pallas_call(
        _proj_kernel,
        grid=(B, S // tm),
        in_specs=[pl.BlockSpec((None, tm, D), lambda b, i: (b, i, 0)),
                  pl.BlockSpec((None, 1, D), lambda b, i: (mod_row(b), 0, 0)),
                  pl.BlockSpec((None, 1, D), lambda b, i: (mod_row(b), 0, 1)),
                  pl.BlockSpec((1, D), lambda b, i: (0, 0)),
                  pl.BlockSpec((D, d_in), lambda b, i: (0, 0), pipeline_mode=pl.Buffered(1)),
                  pl.BlockSpec((D, PROJ_TN), lambda b, i: (0, 0), pipeline_mode=pl.Buffered(1)),
                  pl.BlockSpec((tm, 4 * RET_DK), lambda b, i: (i, 0))],
        out_specs=[pl.BlockSpec((tm, d_in), lambda b, i: (i, b)),
                   pl.BlockSpec((None, tm, D), lambda b, i: (b, i, 0))],
        out_shape=[jax.ShapeDtypeStruct((S, B * d_in), BF16),
                   jax.ShapeDtypeStruct((B, S, D), BF16)],
        compiler_params=_cparams(2),
        name="proj",
    )(x3, mods3, mods3, norm_g.reshape(1, D), w_in_b, w_qk, tab)


def _rnn_kernel(xm_ref, xp_ref, xn_ref, cw_ref, cb_ref, wg_ref, ba_ref, bx_ref, lam_ref, h0_ref,
                h_ref, hfin_ref, xs_ref, a_ref, u_ref, hc_ref, *, reverse, n_tiles):
    i = pl.program_id(0)
    tile = (n_tiles - 1 - i) if reverse else i
    nb, tt, D = xm_ref.shape
    bw = D // RNN_BLOCKS
    HALO = xp_ref.shape[1]

    @pl.when(i == 0)
    def _():
        hc_ref[...] = h0_ref[...]

    GT = HALO
    gr = GT * nb
    ri = lax.broadcasted_iota(I32, (gr, gr), 0)
    ci = lax.broadcasted_iota(I32, (gr, gr), 1)
    perm = jnp.where((ri // nb == ci % GT) & (ri % nb == ci // GT), 1.0, 0.0).astype(BF16)

    def time_major(ref, t0):
        xg = jnp.concatenate([ref[b, t0:t0 + GT, :] for b in range(nb)], axis=0)
        return jnp.dot(perm, xg, preferred_element_type=F32)

    hr = HALO * nb
    R = tt * nb
    xs_ref[0:hr, :] = jnp.where(tile > 0, time_major(xp_ref, 0), 0.0)
    for g in range(tt // GT):
        xs_ref[hr + g * gr:hr + (g + 1) * gr, :] = time_major(xm_ref, g * GT)
    xs_ref[hr + R:2 * hr + R, :] = jnp.where(tile < n_tiles - 1, time_major(xn_ref, 0), 0.0)

    nl = -lam_ref[...]
    csp = (0.25 * LRU_C) * (jnp.maximum(nl, 0.0) + jnp.log1p(jnp.exp(-jnp.abs(nl))))
    SUB = 256

    def gates(s, carry):
        r0 = pl.multiple_of(s * SUB, SUB)
        base = hr - 2 * nb
        xc = cb_ref[...] + cw_ref[0:1, :] * xs_ref[pl.ds(pl.multiple_of(r0 + base, nb), SUB), :]
        for k in range(1, CONV_W):
            xc = xc + cw_ref[k:k + 1, :] * xs_ref[pl.ds(pl.multiple_of(r0 + base + k * nb, nb), SUB), :]
        xb = xc.astype(BF16)
        for n in range(RNN_BLOCKS):
            g = jnp.dot(xb[:, n * bw:(n + 1) * bw], wg_ref[n], preferred_element_type=F32)
            cs = slice(n * bw, (n + 1) * bw)
            tr = jnp.tanh(g[:, :bw] + ba_ref[:, cs])
            ti = jnp.tanh(g[:, bw:] + bx_ref[:, cs])
            t = jnp.tanh(csp[:, cs] * tr + csp[:, cs])
            rc = 1.0 / (1.0 + t)
            a_ref[pl.ds(r0, SUB), cs] = (1.0 - t) * rc
            root = t * lax.rsqrt(jnp.maximum(t, 1e-30))
            u_ref[pl.ds(r0, SUB), cs] = (root * rc) * ((ti + 1.0) * xc[:, cs])
        return carry

    lax.fori_loop(0, R // SUB, gates, 0)

    def step(t, h):
        ts = (tt - 1 - t) if reverse else t
        r0 = pl.multiple_of(ts * nb, nb)
        h = a_ref[pl.ds(r0, nb), :] * h + u_ref[pl.ds(r0, nb), :]
        u_ref[pl.ds(r0, nb), :] = h
        return h

    h = lax.fori_loop(0, tt, step, hc_ref[...], unroll=8)
    hc_ref[...] = h
    hfin_ref[...] = h
    for g in range(tt // GT):
        hg = _tdot(perm, u_ref[g * gr:(g + 1) * gr, :].astype(BF16))
        for b in range(nb):
            h_ref[b, g * GT:(g + 1) * GT, :] = hg[b * GT:(b + 1) * GT, :].astype(h_ref.dtype)


def _rnn(xr, conv_w, conv_b, wg, ba, bx, lam, h0, *, reverse, tt):
    nb, T, D = xr.shape
    tt = min(tt, T)
    n_tiles = T // tt
    HALO = 16
    hb = tt // HALO

    def tile_of(i):
        return (n_tiles - 1 - i) if reverse else i

    kern = functools.partial(_rnn_kernel, reverse=reverse, n_tiles=n_tiles)
    return pl.pallas_call(
        kern,
        grid=(n_tiles,),
        in_specs=[pl.BlockSpec((nb, tt, D), lambda i: (0, tile_of(i), 0)),
                  pl.BlockSpec((nb, HALO, D), lambda i: (0, jnp.maximum(tile_of(i) * hb - 1, 0), 0)),
                  pl.BlockSpec((nb, HALO, D), lambda i: (0, jnp.minimum((tile_of(i) + 1) * hb, n_tiles * hb - 1), 0)),
                  pl.BlockSpec((CONV_W, D), lambda i: (0, 0)),
                  pl.BlockSpec((1, D), lambda i: (0, 0)),
                  pl.BlockSpec((RNN_BLOCKS, D // RNN_BLOCKS, 2 * D // RNN_BLOCKS), lambda i: (0, 0, 0)),
                  pl.BlockSpec((1, D), lambda i: (0, 0)),
                  pl.BlockSpec((1, D), lambda i: (0, 0)),
                  pl.BlockSpec((1, D), lambda i: (0, 0)),
                  pl.BlockSpec((nb, D), lambda i: (0, 0))],
        out_specs=[pl.BlockSpec((nb, tt, D), lambda i: (0, tile_of(i), 0)),
                   pl.BlockSpec((nb, D), lambda i: (0, 0))],
        out_shape=[jax.ShapeDtypeStruct((nb, T, D), BF16),
                   jax.ShapeDtypeStruct((nb, D), F32)],
        scratch_shapes=[pltpu.VMEM(((tt + 2 * HALO) * nb, D), F32),
                        pltpu.VMEM((nb * tt, D), F32),
                        pltpu.VMEM((nb * tt, D), F32),
                        pltpu.VMEM((nb, D), F32)],
        compiler_params=_cparams(1),
        name="rnn_bwd" if reverse else "rnn_fwd",
    )(xr, xr, xr, conv_w, conv_b.reshape(1, D), wg, ba.reshape(1, D), bx.reshape(1, D),
      lam.reshape(1, D), h0)


def _ret_ctx_kernel(k_ref, v_ref, df_ref, db_ref, sf_ref, sb_ref):
    for hh in range(RET_HEADS):
        kh = k_ref[:, hh * RET_DK:(hh + 1) * RET_DK].astype(F32)
        vh = v_ref[:, hh * RET_DV:(hh + 1) * RET_DV]
        sf_ref[hh] = _tdot((kh * df_ref[hh]).astype(BF16), vh)
        sb_ref[hh] = _tdot((kh * db_ref[hh]).astype(BF16), vh)


def _ret_ctx(zc, B, dec_f, dec_b):
    L = zc.shape[0] // B
    H = RET_HEADS
    st = jax.ShapeDtypeStruct((B, H, RET_DK, RET_DV), F32)
    return pl.pallas_call(
        _ret_ctx_kernel,
        grid=(B,),
        in_specs=[pl.BlockSpec((L, H * RET_DK), lambda b: (b, 3)),
                  pl.BlockSpec((L, H * RET_DV), lambda b: (b, 2)),
                  pl.BlockSpec((H, L, RET_DK), lambda b: (0, 0, 0)),
                  pl.BlockSpec((H, L, RET_DK), lambda b: (0, 0, 0))],
        out_specs=[pl.BlockSpec((None, H, RET_DK, RET_DV), lambda b: (b, 0, 0, 0))] * 2,
        out_shape=[st, st],
        compiler_params=_cparams(1),
        name="ret_ctx",
    )(zc, zc, dec_f, dec_b)


def _ret_bwd_kernel(k_ref, v_ref, s0_ref, kd_ref, cd_ref, o_ref, st_ref):
    @pl.when(pl.program_id(1) == 0)
    def _():
        st_ref[...] = s0_ref[...]

    C = kd_ref.shape[1]
    for cc in reversed(range(k_ref.shape[0] // C)):
        rs = slice(cc * C, (cc + 1) * C)
        for hh in range(RET_HEADS):
            o_ref[cc, hh] = st_ref[hh].astype(o_ref.dtype)
            kh = k_ref[rs, hh * RET_DK:(hh + 1) * RET_DK].astype(F32)
            vh = v_ref[rs, hh * RET_DV:(hh + 1) * RET_DV]
            st_ref[hh] = st_ref[hh] * cd_ref[hh:hh + 1, :] + _tdot((kh * kd_ref[hh]).astype(BF16), vh)


def _ret_bwd(z, B, s_bwd, kdec_b, cdec):
    S = z.shape[0]
    H, C = RET_HEADS, min(RET_C, S)
    cps = min(RET_CPS, S // C)
    n = S // (C * cps)
    return pl.pallas_call(
        _ret_bwd_kernel,
        grid=(B, n),
        in_specs=[pl.BlockSpec((cps * C, H * RET_DK), lambda b, j: (n - 1 - j, b * N_CHUNKS + 3)),
                  pl.BlockSpec((cps * C, H * RET_DV), lambda b, j: (n - 1 - j, b * (N_CHUNKS // 2) + 2)),
                  pl.BlockSpec((None, H, RET_DK, RET_DV), lambda b, j: (b, 0, 0, 0)),
                  pl.BlockSpec((H, C, RET_DK), lambda b, j: (0, 0, 0)),
                  pl.BlockSpec((H, RET_DV), lambda b, j: (0, 0))],
        out_specs=pl.BlockSpec((None, cps, H, RET_DK, RET_DV), lambda b, j: (b, n - 1 - j, 0, 0, 0)),
        out_shape=jax.ShapeDtypeStruct((B, n * cps, H, RET_DK, RET_DV), BF16),
        scratch_shapes=[pltpu.VMEM((H, RET_DK, RET_DV), F32)],
        compiler_params=_cparams(2),
        name="ret_bwd",
    )(z, z, s_bwd, kdec_b, cdec)


def _ret_fwd_kernel(q_ref, k_ref, v_ref, gs_ref, sb_ref, s0_ref, intra_ref, qf_ref, qb_ref, kf_ref, cd_ref,
                    o_ref, st_ref):
    @pl.when(pl.program_id(1) == 0)
    def _():
        st_ref[...] = s0_ref[...]

    C = intra_ref.shape[1]
    for cc in range(q_ref.shape[0] // C):
        rs = slice(cc * C, (cc + 1) * C)
        for hh in range(RET_HEADS):
            qh = q_ref[rs, hh * RET_DK:(hh + 1) * RET_DK]
            kh = k_ref[rs, hh * RET_DK:(hh + 1) * RET_DK]
            vh = v_ref[rs, hh * RET_DV:(hh + 1) * RET_DV]
            s = lax.dot_general(qh, kh, (((1,), (1,)), ((), ())), preferred_element_type=F32)
            o = jnp.dot(s.astype(BF16) * intra_ref[hh], vh, preferred_element_type=F32)
            q2 = jnp.concatenate([qh * qf_ref[hh], qh * qb_ref[hh]], axis=1)
            s2 = jnp.concatenate([st_ref[hh].astype(BF16), sb_ref[cc, hh]], axis=0)
            o = o + jnp.dot(q2, s2, preferred_element_type=F32)
            st_ref[hh] = st_ref[hh] * cd_ref[hh:hh + 1, :] + _tdot(kh * kf_ref[hh], vh)
            mu = jnp.mean(o, axis=-1, keepdims=True)
            d = o - mu
            var = jnp.mean(d * d, axis=-1, keepdims=True)
            g = gs_ref[rs, hh * RET_DV:(hh + 1) * RET_DV]
            gate = g * _sigmoid(g)
            o_ref[rs, hh * RET_DV:(hh + 1) * RET_DV] = (d * lax.rsqrt(var + EPS)).astype(o_ref.dtype) * gate


def _ret_fwd(z, B, sb, s_fwd, intra, qdec_f, qdec_b, kdec_f, cdec):
    S = z.shape[0]
    H, C = RET_HEADS, min(RET_C, S)
    cps = min(RET_CPS, S // C)
    n = S // (C * cps)
    half = N_CHUNKS // 2
    return pl.pallas_call(
        _ret_fwd_kernel,
        grid=(B, n),
        in_specs=[pl.BlockSpec((cps * C, H * RET_DK), lambda b, j: (j, b * N_CHUNKS + 2)),
                  pl.BlockSpec((cps * C, H * RET_DK), lambda b, j: (j, b * N_CHUNKS + 3)),
                  pl.BlockSpec((cps * C, H * RET_DV), lambda b, j: (j, b * half + 2)),
                  pl.BlockSpec((cps * C, H * RET_DV), lambda b, j: (j, b * half + 3)),
                  pl.BlockSpec((None, cps, H, RET_DK, RET_DV), lambda b, j: (b, j, 0, 0, 0)),
                  pl.BlockSpec((None, H, RET_DK, RET_DV), lambda b, j: (b, 0, 0, 0)),
                  pl.BlockSpec((H, C, C), lambda b, j: (0, 0, 0)),
                  pl.BlockSpec((H, C, RET_DK), lambda b, j: (0, 0, 0)),
                  pl.BlockSpec((H, C, RET_DK), lambda b, j: (0, 0, 0)),
                  pl.BlockSpec((H, C, RET_DK), lambda b, j: (0, 0, 0)),
                  pl.BlockSpec((H, RET_DV), lambda b, j: (0, 0))],
        out_specs=pl.BlockSpec((None, cps * C, H * RET_DV), lambda b, j: (b, j, 0)),
        out_shape=jax.ShapeDtypeStruct((B, S, H * RET_DV), BF16),
        scratch_shapes=[pltpu.VMEM((H, RET_DK, RET_DV), F32)],
        compiler_params=_cparams(2),
        name="ret_fwd",
    )(z, z, z, z, sb, s_fwd, intra, qdec_f, qdec_b, kdec_f, cdec)


def _dot_t(a, b):
    return lax.dot_general(a, b, (((1,), (1,)), ((), ())), preferred_element_type=F32)


def _finish_kernel(x_ref, hf_ref, hb_ref, gr_ref, gab_ref, ret_ref, g1_ref, sh2_ref, sc2_ref, n2_ref,
                   wr_ref, wt_ref, wo_ref, rwh_ref, rwl_ref, rb_ref,
                   x1_ref, hl_ref, eid_ref, gate_ref, cnt_ref):
    D = x_ref.shape[1]
    tm = x_ref.shape[0]
    dt = eid_ref.shape[2]
    rnn = hf_ref[...].astype(F32) + hb_ref[...].astype(F32)
    y_rnn = jnp.dot((rnn * jax.nn.gelu(gr_ref[...].astype(F32))).astype(BF16), wr_ref[...],
                    preferred_element_type=F32)
    y_ret = jnp.dot(ret_ref[...], wt_ref[...], preferred_element_type=F32)
    ga = gab_ref[:, :D].astype(F32)
    gb = gab_ref[:, D:].astype(F32)
    merged = _sigmoid(ga) * y_rnn + _sigmoid(gb) * y_ret
    y = jnp.dot(merged.astype(BF16), wo_ref[...], preferred_element_type=F32)
    x1 = x_ref[...] + g1_ref[...] * y
    x1_ref[...] = x1
    hl = _rmsnorm(x1, n2_ref[...]) * (1.0 + sc2_ref[...]) + sh2_ref[...]
    hh = hl.astype(BF16)
    hl_ref[...] = hh

    hlo = (hl - hh.astype(F32)).astype(BF16)
    logits = (_dot_t(rwh_ref[...], hh) + _dot_t(rwh_ref[...], hlo) + _dot_t(rwl_ref[...], hh)) + rb_ref[...]

    ne = logits.shape[0]
    sub = lax.broadcasted_iota(I32, (ne, tm), 0)
    work = logits
    vals, idxs = [], []
    oh = jnp.zeros((ne, tm), F32)
    for _ in range(TOP_K):
        m = jnp.max(work, axis=0, keepdims=True)
        idx = jnp.min(jnp.where(work == m, sub, ne), axis=0, keepdims=True)
        hot = sub == idx
        vals.append(m)
        idxs.append(idx)
        oh = oh + jnp.where(hot, 1.0, 0.0)
        work = jnp.where(hot, -jnp.inf, work)
    es = [jnp.exp(v - vals[0]) for v in vals]
    inv = 1.0 / (es[0] + es[1] + es[2] + es[3])
    for part in range(tm // dt):
        ls = slice(part * dt, (part + 1) * dt)
        for k in range(TOP_K):
            eid_ref[part, k:k + 1, :] = idxs[k][:, ls]
            gate_ref[part, k:k + 1, :] = (es[k] * inv)[:, ls]
        cnt_ref[part] = jnp.sum(oh[:, ls], axis=1, keepdims=True)


def _finish(x3, hf, hb, z, retg, mods3, norm2_g, w_rnn_b, w_ret_b, w_out_b, rwt_hi, rwt_lo, router_b, dt):
    B, S, D = x3.shape
    tm = min(FIN_TM, S)
    nt = S // tm
    N = B * S
    ne = rwt_hi.shape[0]
    half = N_CHUNKS // 2
    per = tm // dt
    const2 = lambda b, i: (0, 0)
    tile3 = lambda b, i: (b * nt + i, 0, 0)
    return pl.pallas_call(
        _finish_kernel,
        grid=(B, nt),
        in_specs=[pl.BlockSpec((None, tm, D), lambda b, i: (b, i, 0)),
                  pl.BlockSpec((None, tm, D), lambda b, i: (b, i, 0)),
                  pl.BlockSpec((None, tm, D), lambda b, i: (b, i, 0)),
                  pl.BlockSpec((tm, D), lambda b, i: (i, b * N_CHUNKS + 1)),
                  pl.BlockSpec((tm, 2 * D), lambda b, i: (i, b * half + 4)),
                  pl.BlockSpec((None, tm, retg.shape[2]), lambda b, i: (b, i, 0)),
                  pl.BlockSpec((None, 1, D), lambda b, i: (b, 0, 2)),
                  pl.BlockSpec((None, 1, D), lambda b, i: (b, 0, 3)),
                  pl.BlockSpec((None, 1, D), lambda b, i: (b, 0, 4)),
                  pl.BlockSpec((1, D), const2),
                  pl.BlockSpec(w_rnn_b.shape, const2),
                  pl.BlockSpec(w_ret_b.shape, const2),
                  pl.BlockSpec(w_out_b.shape, const2),
                  pl.BlockSpec(rwt_hi.shape, const2),
                  pl.BlockSpec(rwt_lo.shape, const2),
                  pl.BlockSpec((ne, 1), const2)],
        out_specs=[pl.BlockSpec((None, tm, D), lambda b, i: (b, i, 0)),
                   pl.BlockSpec((tm, D), lambda b, i: (b * nt + i, 0)),
                   pl.BlockSpec((per, TOP_K, dt), tile3),
                   pl.BlockSpec((per, TOP_K, dt), tile3),
                   pl.BlockSpec((per, ne, 1), tile3)],
        out_shape=[jax.ShapeDtypeStruct((B, S, D), F32),
                   jax.ShapeDtypeStruct((N, D), BF16),
                   jax.ShapeDtypeStruct((N // dt, TOP_K, dt), I32),
                   jax.ShapeDtypeStruct((N // dt, TOP_K, dt), F32),
                   jax.ShapeDtypeStruct((N // dt, ne, 1), F32)],
        compiler_params=_cparams(2),
        name="finish",
    )(x3, hf, hb, z, z, retg, mods3, mods3, mods3, norm2_g.reshape(1, D),
      w_rnn_b, w_ret_b, w_out_b, rwt_hi, rwt_lo, router_b.reshape(ne, 1))


def _local_rows(dt):
    return TOP_K * dt + N_EXPERTS * ROW_ALIGN


def _local_slots(eid_ref):
    dt = eid_ref.shape[1]
    ne = N_EXPERTS
    sub = lax.broadcasted_iota(I32, (ne, dt), 0)
    hots = [sub == eid_ref[k:k + 1, :] for k in range(TOP_K)]
    oh = jnp.zeros((ne, dt), F32)
    for hot in hots:
        oh = oh + jnp.where(hot, 1.0, 0.0)
    earlier = jnp.where(lax.broadcasted_iota(I32, (dt, dt), 0) < lax.broadcasted_iota(I32, (dt, dt), 1), 1.0, 0.0)
    before = jnp.dot(oh.astype(BF16), earlier.astype(BF16), preferred_element_type=F32)
    cnt = jnp.broadcast_to(jnp.sum(oh, axis=1, keepdims=True), (ne, dt))
    cnt = jnp.ceil(cnt * (1.0 / ROW_ALIGN)) * ROW_ALIGN
    lower = jnp.where(lax.broadcasted_iota(I32, (ne, ne), 1) < lax.broadcasted_iota(I32, (ne, ne), 0), 1.0, 0.0)
    base = before + jnp.dot(lower.astype(BF16), cnt.astype(BF16), preferred_element_type=F32)
    return [jnp.sum(jnp.where(hot, base, 0.0), axis=0, keepdims=True).astype(I32) for hot in hots]


def _slot_matrix(slots, weights, out_ref):
    rows, dt = out_ref.shape
    ch = 64
    rel = lax.broadcasted_iota(I32, (ch, dt), 0).astype(F32).astype(BF16)
    slots_f = [s.astype(F32) for s in slots]
    weights_b = [jnp.asarray(w, F32).astype(BF16) for w in weights]
    zero = jnp.zeros((), BF16)
    for c in range(rows // ch):
        acc = None
        for s, w in zip(slots_f, weights_b):
            term = jnp.where(rel == (s - float(c * ch)).astype(BF16), w, zero)
            acc = term if acc is None else acc + term
        out_ref[c * ch:(c + 1) * ch, :] = acc


def _for_each_run(meta_ref, fn):
    def body(e, c):
        n = pl.multiple_of(meta_ref[0, N_EXPERTS + e], ROW_ALIGN)

        @pl.when(n > 0)
        def _():
            fn(pl.multiple_of(meta_ref[0, e], ROW_ALIGN), pl.multiple_of(meta_ref[0, 2 * N_EXPERTS + e], ROW_ALIGN), n)

        return c

    lax.fori_loop(0, N_EXPERTS, body, 0)


def _run_rows(meta_ref):
    last = N_EXPERTS - 1
    return pl.multiple_of(meta_ref[0, last] + meta_ref[0, N_EXPERTS + last], ROW_ALIGN)


def _dispatch_kernel(meta_ref, zmeta_ref, eid_ref, x_ref, hp_ref, slot_ref, sbuf_ref, zbuf_ref, pm_ref, rows_ref,
                     sems, zsem):
    t = pl.program_id(0)
    slot = t % 2

    def zero_copy(_, g, n):
        return pltpu.make_async_copy(zbuf_ref.at[pl.ds(0, n), :], hp_ref.at[pl.ds(g, n), :], zsem)

    def run_copy(sl, l, g, n):
        return pltpu.make_async_copy(sbuf_ref.at[sl, pl.ds(l, n), :], hp_ref.at[pl.ds(g, n), :], sems.at[sl])

    @pl.when(t == 0)
    def _():
        zbuf_ref[...] = jnp.zeros_like(zbuf_ref)
        _for_each_run(zmeta_ref, lambda l, g, n: zero_copy(l, g, n).start())
        _for_each_run(zmeta_ref, lambda l, g, n: zero_copy(l, g, n).wait())

    slots = _local_slots(eid_ref)
    for k in range(TOP_K):
        slot_ref[k:k + 1, :] = slots[k]
    _slot_matrix(slots, [1.0] * TOP_K, pm_ref)
    sbuf_ref[slot] = _pack_cols(jnp.dot(pm_ref[...], x_ref[...], preferred_element_type=F32))

    @pl.when(t > 0)
    def _():
        run_copy(1 - slot, 0, 0, pl.multiple_of(rows_ref[1 - slot], ROW_ALIGN)).wait()

    _for_each_run(meta_ref, lambda l, g, n: run_copy(slot, l, g, n).start())
    rows_ref[slot] = _run_rows(meta_ref)

    @pl.when(t == pl.num_programs(0) - 1)
    def _():
        run_copy(slot, 0, 0, _run_rows(meta_ref)).wait()


def _dispatch(hl2, eid3, meta3, zmeta, P):
    N, D = hl2.shape
    nt, _, dt = eid3.shape
    nm = meta3.shape[2]
    return pl.pallas_call(
        _dispatch_kernel,
        grid=(nt,),
        in_specs=[pl.BlockSpec((None, 1, nm), lambda i: (i, 0, 0), memory_space=pltpu.SMEM),
                  pl.BlockSpec((1, nm), lambda i: (0, 0), memory_space=pltpu.SMEM),
                  pl.BlockSpec((None, TOP_K, dt), lambda i: (i, 0, 0)),
                  pl.BlockSpec((dt, D), lambda i: (i, 0))],
        out_specs=[pl.BlockSpec(memory_space=pl.ANY),
                   pl.BlockSpec((None, TOP_K, dt), lambda i: (i, 0, 0))],
        out_shape=[jax.ShapeDtypeStruct((P, D // 2), U32),
                   jax.ShapeDtypeStruct((nt, TOP_K, dt), I32)],
        scratch_shapes=[pltpu.VMEM((2, _local_rows(dt), D // 2), U32),
                        pltpu.VMEM((MOE_BLOCK, D // 2), U32),
                        pltpu.VMEM((_local_rows(dt), dt), BF16),
                        pltpu.SMEM((2,), I32),
                        pltpu.SemaphoreType.DMA((2,)),
                        pltpu.SemaphoreType.DMA(())],
        compiler_params=_cparams(1),
        name="dispatch",
    )(meta3, zmeta, eid3, hl2)


def _regroup_kernel(w_ref, sel_ref, o_ref):
    half = w_ref.shape[1] // 2
    g = sel_ref.shape[0]
    wb = w_ref[...].astype(BF16)
    for j in range(w_ref.shape[1] // g):
        r = jnp.dot(wb[:, g * j:g * (j + 1)], sel_ref[...], preferred_element_type=F32)
        o_ref[:, (g // 2) * j:(g // 2) * (j + 1)] = r[:, :g // 2].astype(o_ref.dtype)
        o_ref[:, half + (g // 2) * j:half + (g // 2) * (j + 1)] = r[:, g // 2:].astype(o_ref.dtype)


def _regroup_glu_lin(w1):
    ne, D, de2 = w1.shape
    g = 256
    sel = np.zeros((g, g), np.float32)
    sel[np.arange(0, g, 2), np.arange(g // 2)] = 1.0
    sel[np.arange(1, g, 2), g // 2 + np.arange(g // 2)] = 1.0
    out = pl.pallas_call(
        _regroup_kernel,
        grid=(ne,),
        in_specs=[pl.BlockSpec((D, de2), lambda e: (e, 0)),
                  pl.BlockSpec((g, g), lambda e: (0, 0))],
        out_specs=pl.BlockSpec((D, de2), lambda e: (e, 0)),
        out_shape=jax.ShapeDtypeStruct((ne * D, de2), BF16),
        compiler_params=_cparams(1),
        name="regroup",
    )(w1.reshape(ne * D, de2), jnp.asarray(sel, BF16))
    return out.reshape(ne, D, de2)


def _expert_kernel(be_ref, nu_ref, x_ref, w1_ref, b1_ref, w2_ref, b2_ref, o_ref):
    del be_ref

    @pl.when(pl.program_id(0) < nu_ref[0])
    def _():
        de = w2_ref.shape[0]
        h = jnp.dot(_unpack_cols(x_ref[...]), w1_ref[...], preferred_element_type=F32) + b1_ref[...]
        glu = jnp.minimum(h[:, :de], SWIGLU_LIMIT)
        lin = jnp.clip(h[:, de:], -SWIGLU_LIMIT, SWIGLU_LIMIT)
        act = glu * _sigmoid(SWIGLU_ALPHA * glu) * (lin + 1.0)
        y = jnp.dot(act.astype(BF16), w2_ref[...].astype(BF16), preferred_element_type=F32) + b2_ref[...]
        o_ref[...] = _pack_cols(y.astype(BF16).astype(F32))


def _experts(blk_expert, n_used, h_pad, w1b, b1p, w2b, b2):
    P = h_pad.shape[0]
    ne, D, de2 = w1b.shape
    de = de2 // 2
    nblk = P // MOE_BLOCK
    blk = lambda j, be, nu: (jnp.minimum(j, nu[0] - 1), 0)
    wsel = lambda j, be, nu: (be[jnp.minimum(j, nu[0] - 1)], 0, 0)
    return pl.pallas_call(
        _expert_kernel,
        grid_spec=pltpu.PrefetchScalarGridSpec(
            num_scalar_prefetch=2,
            grid=(nblk,),
            in_specs=[pl.BlockSpec((MOE_BLOCK, D // 2), blk),
                      pl.BlockSpec((None, D, de2), wsel),
                      pl.BlockSpec((None, 1, de2), wsel),
                      pl.BlockSpec((None, de, D), wsel),
                      pl.BlockSpec((None, 1, D), wsel)],
            out_specs=pl.BlockSpec((MOE_BLOCK, D // 2), blk)),
        out_shape=jax.ShapeDtypeStruct((P, D // 2), U32),
        compiler_params=_cparams(1),
        name="experts",
    )(blk_expert, n_used, h_pad, w1b, b1p.reshape(ne, 1, de2), w2b, b2.reshape(ne, 1, D))


def _combine_kernel(meta_ref, nmeta_ref, slot_ref, gate_ref, x1_ref, g2_ref, fg_ref, yp_ref, o_ref,
                    ybuf_ref, gm_ref, sems):
    t = pl.program_id(0) * pl.num_programs(1) + pl.program_id(1)
    n_tiles = pl.num_programs(0) * pl.num_programs(1)
    slot = t % 2

    def run_copy(sl, l, g, n):
        return pltpu.make_async_copy(yp_ref.at[pl.ds(g, n), :], ybuf_ref.at[sl, pl.ds(l, n), :], sems.at[sl])

    @pl.when(t == 0)
    def _():
        ybuf_ref[...] = jnp.zeros_like(ybuf_ref)
        _for_each_run(meta_ref, lambda l, g, n: run_copy(0, l, g, n).start())

    @pl.when(t + 1 < n_tiles)
    def _():
        _for_each_run(nmeta_ref, lambda l, g, n: run_copy(1 - slot, l, g, n).start())

    run_copy(slot, 0, 0, _run_rows(meta_ref)).wait()

    _slot_matrix([slot_ref[k:k + 1, :] for k in range(TOP_K)], [gate_ref[k:k + 1, :] for k in range(TOP_K)], gm_ref)
    y = _tdot(gm_ref[...], _unpack_cols(ybuf_ref[slot]))
    x2 = x1_ref[...] + g2_ref[...] * y
    o_ref[...] = _rmsnorm(x2, fg_ref[...])


def _combine(meta3, slot3, gate3, x1, mods3, final_g, y_pad):
    B, S, D = x1.shape
    n_tiles, _, dt = slot3.shape
    nt = S // dt
    nm = meta3.shape[2]
    tile3 = lambda b, i: (b * nt + i, 0, 0)
    next3 = lambda b, i: (jnp.minimum(b * nt + i + 1, n_tiles - 1), 0, 0)
    return pl.pallas_call(
        _combine_kernel,
        grid=(B, nt),
        in_specs=[pl.BlockSpec((None, 1, nm), tile3, memory_space=pltpu.SMEM),
                  pl.BlockSpec((None, 1, nm), next3, memory_space=pltpu.SMEM),
                  pl.BlockSpec((None, TOP_K, dt), tile3),
                  pl.BlockSpec((None, TOP_K, dt), tile3),
                  pl.BlockSpec((None, dt, D), lambda b, i: (b, i, 0)),
                  pl.BlockSpec((None, 1, D), lambda b, i: (b, 0, 5)),
                  pl.BlockSpec((1, D), lambda b, i: (0, 0)),
                  pl.BlockSpec(memory_space=pl.ANY)],
        out_specs=pl.BlockSpec((None, dt, D), lambda b, i: (b, i, 0)),
        out_shape=jax.ShapeDtypeStruct((B, S, D), F32),
        scratch_shapes=[pltpu.VMEM((2, _local_rows(dt), D // 2), U32), pltpu.VMEM((_local_rows(dt), dt), BF16),
                        pltpu.SemaphoreType.DMA((2,))],
        compiler_params=_cparams(2),
        name="combine",
    )(meta3, meta3, slot3, gate3, x1, mods3, final_g.reshape(1, D), y_pad)


def _rope_tables(S, k_scale):
    n_freq = RET_DK // 4
    pos = np.arange(S)
    rows = (pos // GRID_W).astype(np.float32)
    cols = (pos % GRID_W).astype(np.float32)
    inv = (np.float32(ROPE_BASE) ** (-np.arange(n_freq, dtype=np.float32) / np.float32(n_freq))).astype(np.float32)
    ang = np.concatenate([rows[:, None] * inv, cols[:, None] * inv], axis=-1).astype(np.float32)
    cos, sin = np.cos(ang), np.sin(ang)
    cos2 = np.concatenate([cos, cos], axis=-1)
    sin2 = np.concatenate([-sin, sin], axis=-1)
    ks = np.float32(k_scale)
    return jnp.asarray(np.concatenate([cos2, sin2, cos2 * ks, sin2 * ks], axis=-1), F32)


def _identity_tables(L, k_scale):
    one = np.ones((L, RET_DK), np.float32)
    zero = np.zeros((L, RET_DK), np.float32)
    return jnp.asarray(np.concatenate([one, zero, one * np.float32(k_scale), zero], axis=-1), F32)


def _lanes(t, width):
    return np.ascontiguousarray(np.broadcast_to(t[:, :, None], t.shape + (width,)))


def kernel(x, c, ctx, c_ctx, ada_w, ada_b, norm1_g, w_in, conv_w, conv_b, lru_wa, lru_ba, lru_wx, lru_bx,
           lru_lambda, w_rnn_proj, w_ret_proj, w_out, norm2_g, router_w, router_b, moe_w1, moe_b1, moe_w2,
           moe_b2, final_g):
    B, S, D = x.shape
    L = ctx.shape[1]
    N = B * S
    H = RET_HEADS
    lyr = 0
    d_in = w_in.shape[2]
    assert ada_w.shape[0] == 1 and d_in == N_CHUNKS * D and B == 8

    def pairs_apart(w):
        return jnp.swapaxes(w.reshape(D, H, RET_DK // 2, 2), 2, 3).reshape(D, H * RET_DK)

    w_in_b = w_in[lyr].astype(BF16)
    w_qk = jnp.concatenate([pairs_apart(w_in_b[:, 2 * D:3 * D]), pairs_apart(w_in_b[:, 3 * D:4 * D])], axis=1)
    wg = [(0.5 * jnp.concatenate([lru_wa[lyr, d], lru_wx[lyr, d]], axis=-1)).astype(BF16) for d in range(2)]
    de2 = moe_w1.shape[3]
    glu_lin = np.concatenate([np.arange(0, de2, 2), np.arange(1, de2, 2)])
    w1b = _regroup_glu_lin(moe_w1[lyr])
    b1p = moe_b1[lyr][:, glu_lin]
    rwt = router_w[lyr].T
    rwt_hi = rwt.astype(BF16)
    rwt_lo = (rwt - rwt_hi.astype(F32)).astype(BF16)

    k_scale = RET_DK ** -0.5
    tab_l = _rope_tables(S, k_scale)
    tab_c = _identity_tables(B * L, k_scale)
    f4 = np.float32
    log_g = np.log1p(-np.exp2(-5.0 - np.arange(H, dtype=f4))).astype(f4)
    C = min(RET_C, S)
    idx = np.arange(C, dtype=f4)
    dec = lambda e: np.exp(e[None, :].astype(f4) * log_g[:, None]).astype(f4)
    intra = jnp.asarray(np.exp(np.abs(idx[:, None] - idx[None, :])[None] * log_g[:, None, None]).astype(f4))
    qdec_f = jnp.asarray(_lanes(dec(idx + 1.0), RET_DK))
    qdec_b = jnp.asarray(_lanes(dec(C - idx), RET_DK))
    kdec_f = jnp.asarray(_lanes(dec(C - 1.0 - idx), RET_DK))
    kdec_b = jnp.asarray(_lanes(dec(idx), RET_DK))
    cdec = jnp.asarray(np.broadcast_to(np.exp(C * log_g).astype(f4)[:, None], (H, RET_DV)))
    pos_c = np.arange(L, dtype=f4)
    cdec_f = jnp.asarray(_lanes(dec(L - 1.0 - pos_c), RET_DK))
    cdec_b = jnp.asarray(_lanes(dec(pos_c), RET_DK))

    cvec = jnp.zeros((16, D), F32).at[:B].set(c).at[B].set(c_ctx)
    mods3 = _ada(cvec, ada_w[lyr], ada_b[lyr]).reshape(16, 1, 6 * D)

    z_c, xr_c = _proj(ctx.reshape(1, B * L, D), mods3, lambda b: B, norm1_g[lyr], w_in_b, w_qk, tab_c,
                      min(PROJ_TM, B * L))
    xr_c = xr_c.reshape(B, L, D)
    z_l, xr_l = _proj(x, mods3, lambda b: b, norm1_g[lyr], w_in_b, w_qk, tab_l, min(PROJ_TM, S))

    zeros = jnp.zeros((B, D), F32)
    hs = []
    for d in range(2):
        args = (conv_w[lyr], conv_b[lyr], wg[d], 0.5 * lru_ba[lyr, d], 0.5 * lru_bx[lyr, d], lru_lambda[lyr, d])
        _, h0 = _rnn(xr_c, *args, zeros, reverse=(d == 1), tt=RNN_TT)
        h, _ = _rnn(xr_l, *args, h0, reverse=(d == 1), tt=RNN_TT)
        hs.append(h)

    s_fwd, s_bwd = _ret_ctx(z_c, B, cdec_f, cdec_b)
    sb = _ret_bwd(z_l, B, s_bwd, kdec_b, cdec)
    retg = _ret_fwd(z_l, B, sb, s_fwd, intra.astype(BF16), qdec_f.astype(BF16), qdec_b.astype(BF16),
                    kdec_f.astype(BF16), cdec)

    dt = min(DISP_TM, S)
    x1, hl2, eid3, gate3, cnt3 = _finish(
        x, hs[0], hs[1], z_l, retg, mods3, norm2_g[lyr], w_rnn_proj[lyr].astype(BF16),
        w_ret_proj[lyr].astype(BF16), w_out[lyr].astype(BF16), rwt_hi, rwt_lo, router_b[lyr], dt)

    cnt_t = cnt3[:, :, 0].astype(I32)
    cnt_t = (cnt_t + ROW_ALIGN - 1) // ROW_ALIGN * ROW_ALIGN
    cnt = jnp.sum(cnt_t, axis=0)
    padded = (cnt + MOE_BLOCK - 1) // MOE_BLOCK * MOE_BLOCK
    pad_end = jnp.cumsum(padded)
    pad_start = pad_end - padded
    gstart = pad_start[None, :] + jnp.cumsum(cnt_t, axis=0) - cnt_t
    loff = jnp.cumsum(cnt_t, axis=1) - cnt_t
    meta3 = jnp.concatenate([loff, cnt_t, gstart], axis=1).reshape(N // dt, 1, 3 * N_EXPERTS)
    zmeta = jnp.concatenate([jnp.zeros_like(cnt), padded - cnt, pad_start + cnt]).reshape(1, 3 * N_EXPERTS)
    n_blocks = -(-(N * TOP_K + (N // dt) * N_EXPERTS * (ROW_ALIGN - 1)) // MOE_BLOCK) + N_EXPERTS
    P = n_blocks * MOE_BLOCK
    blk_start = jnp.arange(n_blocks, dtype=I32) * MOE_BLOCK
    blk_expert = jnp.minimum(jnp.sum((pad_end[None, :] <= blk_start[:, None]).astype(I32), axis=1), N_EXPERTS - 1)
    n_used = (pad_end[-1:] // MOE_BLOCK).astype(I32)

    h_pad, slot3 = _dispatch(hl2, eid3, meta3, zmeta, P)
    y_pad = _experts(blk_expert, n_used, h_pad, w1b, b1p, moe_w2[lyr], moe_b2[lyr])
    return _combine(meta3, slot3, gate3, x1, mods3, final_g, y_pad)
```

```python
import functools

import jax
import jax.numpy as jnp
import numpy as np
from jax import lax
from jax.experimental import pallas as pl
from jax.experimental.pallas import tpu as pltpu

F32 = jnp.float32
BF16 = jnp.bfloat16
I32 = jnp.int32

GRID_W = 64
RNN_BLOCKS = 8
CONV_W = 4
LRU_C = 8.0
RET_HEADS = 8
RET_DK = 128
RET_DV = 256
ROPE_BASE = 10000.0
N_EXPERTS = 32
TOP_K = 4
SWIGLU_ALPHA = 1.702
SWIGLU_LIMIT = 7.0
EPS = 1e-6
N_CHUNKS = 10

PROJ_TM = 512
PROJ_TN = 2048
RNN_TT = 256
RET_C = 256
RET_CPS = 4
FIN_TM = 512
MOE_BLOCK = 1024
DISP_TM = 512
ROW_ALIGN = 8
VMEM_LIMIT = 56 * 1024 * 1024


def _cparams(n_axes):
    return pltpu.CompilerParams(dimension_semantics=("arbitrary",) * n_axes,
                                vmem_limit_bytes=VMEM_LIMIT)


def _sigmoid(x):
    return 0.5 * (jnp.tanh(0.5 * x) + 1.0)


def _rmsnorm(x, g):
    return x * lax.rsqrt(jnp.mean(x * x, axis=-1, keepdims=True) + EPS) * g


U32 = jnp.uint32


def _pack_cols(v):
    w = v.shape[1] // 2
    lo = lax.shift_right_logical(lax.bitcast_convert_type(v[:, :w], U32), jnp.uint32(16))
    hi = lax.bitcast_convert_type(v[:, w:], U32) & jnp.uint32(0xFFFF0000)
    return lo | hi


def _unpack_cols(u):
    a = lax.bitcast_convert_type(lax.shift_left(u, jnp.uint32(16)), F32).astype(BF16)
    b = lax.bitcast_convert_type(u & jnp.uint32(0xFFFF0000), F32).astype(BF16)
    return jnp.concatenate([a, b], axis=1)


def _tdot(a, b):
    return lax.dot_general(a, b, (((0,), (0,)), ((), ())), preferred_element_type=F32)


def _ada_kernel(c_ref, w_ref, b_ref, o_ref):
    c = c_ref[...]
    s = c * _sigmoid(c)
    o_ref[...] = jnp.dot(s, w_ref[...], preferred_element_type=F32,
                         precision=lax.Precision.HIGHEST) + b_ref[...]


def _ada(cvec, ada_w, ada_b):
    R, D = cvec.shape
    n = ada_w.shape[1] // D
    return pl.pallas_call(
        _ada_kernel,
        grid=(n,),
        in_specs=[pl.BlockSpec((R, D), lambda j: (0, 0)),
                  pl.BlockSpec((D, D), lambda j: (0, j)),
                  pl.BlockSpec((1, D), lambda j: (0, j))],
        out_specs=pl.BlockSpec((R, D), lambda j: (0, j)),
        out_shape=jax.ShapeDtypeStruct((R, n * D), F32),
        compiler_params=_cparams(1),
        name="ada",
    )(cvec, ada_w, ada_b.reshape(1, -1))


def _proj_kernel(x_ref, sh_ref, sc_ref, g_ref, w_ref, wqk_ref, tab_ref, o_ref, xr_ref):
    D = x_ref.shape[1]
    h = _rmsnorm(x_ref[...], g_ref[...])
    hb = (h * (1.0 + sc_ref[...]) + sh_ref[...]).astype(BF16)
    for j in range(w_ref.shape[1] // PROJ_TN):
        j0 = j * PROJ_TN
        wj = wqk_ref[...] if j == 1 else w_ref[:, j0:j0 + PROJ_TN]
        acc = jnp.dot(hb, wj, preferred_element_type=F32)
        if j == 0:
            xr_ref[...] = acc[:, :D].astype(xr_ref.dtype)
        if j != 1:
            o_ref[:, j0:j0 + PROJ_TN] = acc.astype(o_ref.dtype)
            continue
        for part in range(2):
            cos = tab_ref[:, (2 * part) * RET_DK:(2 * part + 1) * RET_DK]
            sin = tab_ref[:, (2 * part + 1) * RET_DK:(2 * part + 2) * RET_DK]
            for hh in range(RET_HEADS):
                c0 = part * RET_HEADS * RET_DK + hh * RET_DK
                t = acc[:, c0:c0 + RET_DK]
                o_ref[:, j0 + c0:j0 + c0 + RET_DK] = (
                    t * cos + pltpu.roll(t, RET_DK // 2, 1) * sin).astype(o_ref.dtype)


def _proj(x3, mods3, mod_row, norm_g, w_in_b, w_qk, tab, tm):
    B, S, D = x3.shape
    d_in = w_in_b.shape[1]
    return pl.pallas_call(
        _proj_kernel,
        grid=(B, S // tm),
        in_specs=[pl.BlockSpec((None, tm, D), lambda b, i: (b, i, 0)),
                  pl.BlockSpec((None, 1, D), lambda b, i: (mod_row(b), 0, 0)),
                  pl.BlockSpec((None, 1, D), lambda b, i: (mod_row(b), 0, 1)),
                  pl.BlockSpec((1, D), lambda b, i: (0, 0)),
                  pl.BlockSpec((D, d_in), lambda b, i: (0, 0), pipeline_mode=pl.Buffered(1)),
                  pl.BlockSpec((D, PROJ_TN), lambda b, i: (0, 0), pipeline_mode=pl.Buffered(1)),
                  pl.BlockSpec((tm, 4 * RET_DK), lambda b, i: (i, 0))],
        out_specs=[pl.BlockSpec((tm, d_in), lambda b, i: (i, b)),
                   pl.BlockSpec((None, tm, D), lambda b, i: (b, i, 0))],
        out_shape=[jax.ShapeDtypeStruct((S, B * d_in), BF16),
                   jax.ShapeDtypeStruct((B, S, D), BF16)],
        compiler_params=_cparams(2),
        name="proj",
    )(x3, mods3, mods3, norm_g.reshape(1, D), w_in_b, w_qk, tab)


def _rnn_kernel(xm_ref, xp_ref, xn_ref, cw_ref, cb_ref, wg_ref, ba_ref, bx_ref, lam_ref, h0_ref,
                h_ref, hfin_ref, xs_ref, a_ref, u_ref, hc_ref, *, reverse, n_tiles):
    i = pl.program_id(0)
    tile = (n_tiles - 1 - i) if reverse else i
    nb, tt, D = xm_ref.shape
    bw = D // RNN_BLOCKS
    HALO = xp_ref.shape[1]

    @pl.when(i == 0)
    def _():
        hc_ref[...] = h0_ref[...]

    GT = HALO
    gr = GT * nb
    ri = lax.broadcasted_iota(I32, (gr, gr), 0)
    ci = lax.broadcasted_iota(I32, (gr, gr), 1)
    perm = jnp.where((ri // nb == ci % GT) & (ri % nb == ci // GT), 1.0, 0.0).astype(BF16)

    def time_major(ref, t0):
        xg = jnp.concatenate([ref[b, t0:t0 + GT, :] for b in range(nb)], axis=0)
        return jnp.dot(perm, xg, preferred_element_type=F32)

    hr = HALO * nb
    R = tt * nb
    xs_ref[0:hr, :] = jnp.where(tile > 0, time_major(xp_ref, 0), 0.0)
    for g in range(tt // GT):
        xs_ref[hr + g * gr:hr + (g + 1) * gr, :] = time_major(xm_ref, g * GT)
    xs_ref[hr + R:2 * hr + R, :] = jnp.where(tile < n_tiles - 1, time_major(xn_ref, 0), 0.0)

    nl = -lam_ref[...]
    csp = (0.25 * LRU_C) * (jnp.maximum(nl, 0.0) + jnp.log1p(jnp.exp(-jnp.abs(nl))))
    SUB = 256

    def gates(s, carry):
        r0 = pl.multiple_of(s * SUB, SUB)
        base = hr - 2 * nb
        xc = cb_ref[...] + cw_ref[0:1, :] * xs_ref[pl.ds(pl.multiple_of(r0 + base, nb), SUB), :]
        for k in range(1, CONV_W):
            xc = xc + cw_ref[k:k + 1, :] * xs_ref[pl.ds(pl.multiple_of(r0 + base + k * nb, nb), SUB), :]
        xb = xc.astype(BF16)
        for n in range(RNN_BLOCKS):
            g = jnp.dot(xb[:, n * bw:(n + 1) * bw], wg_ref[n], preferred_element_type=F32)
            cs = slice(n * bw, (n + 1) * bw)
            tr = jnp.tanh(g[:, :bw] + ba_ref[:, cs])
            ti = jnp.tanh(g[:, bw:] + bx_ref[:, cs])
            t = jnp.tanh(csp[:, cs] * tr + csp[:, cs])
            rc = 1.0 / (1.0 + t)
            a_ref[pl.ds(r0, SUB), cs] = (1.0 - t) * rc
            root = t * lax.rsqrt(jnp.maximum(t, 1e-30))
            u_ref[pl.ds(r0, SUB), cs] = (root * rc) * ((ti + 1.0) * xc[:, cs])
        return carry

    lax.fori_loop(0, R // SUB, gates, 0)

    def step(t, h):
        ts = (tt - 1 - t) if reverse else t
        r0 = pl.multiple_of(ts * nb, nb)
        h = a_ref[pl.ds(r0, nb), :] * h + u_ref[pl.ds(r0, nb), :]
        u_ref[pl.ds(r0, nb), :] = h
        return h

    h = lax.fori_loop(0, tt, step, hc_ref[...], unroll=8)
    hc_ref[...] = h
    hfin_ref[...] = h
    for g in range(tt // GT):
        hg = _tdot(perm, u_ref[g * gr:(g + 1) * gr, :].astype(BF16))
        for b in range(nb):
            h_ref[b, g * GT:(g + 1) * GT, :] = hg[b * GT:(b + 1) * GT, :].astype(h_ref.dtype)


def _rnn(xr, conv_w, conv_b, wg, ba, bx, lam, h0, *, reverse, tt):
    nb, T, D = xr.shape
    tt = min(tt, T)
    n_tiles = T // tt
    HALO = 16
    hb = tt // HALO

    def tile_of(i):
        return (n_tiles - 1 - i) if reverse else i

    kern = functools.partial(_rnn_kernel, reverse=reverse, n_tiles=n_tiles)
    return pl.pallas_call(
        kern,
        grid=(n_tiles,),
        in_specs=[pl.BlockSpec((nb, tt, D), lambda i: (0, tile_of(i), 0)),
                  pl.BlockSpec((nb, HALO, D), lambda i: (0, jnp.maximum(tile_of(i) * hb - 1, 0), 0)),
                  pl.BlockSpec((nb, HALO, D), lambda i: (0, jnp.minimum((tile_of(i) + 1) * hb, n_tiles * hb - 1), 0)),
                  pl.BlockSpec((CONV_W, D), lambda i: (0, 0)),
                  pl.BlockSpec((1, D), lambda i: (0, 0)),
                  pl.BlockSpec((RNN_BLOCKS, D // RNN_BLOCKS, 2 * D // RNN_BLOCKS), lambda i: (0, 0, 0)),
                  pl.BlockSpec((1, D), lambda i: (0, 0)),
                  pl.BlockSpec((1, D), lambda i: (0, 0)),
                  pl.BlockSpec((1, D), lambda i: (0, 0)),
                  pl.BlockSpec((nb, D), lambda i: (0, 0))],
        out_specs=[pl.BlockSpec((nb, tt, D), lambda i: (0, tile_of(i), 0)),
                   pl.BlockSpec((nb, D), lambda i: (0, 0))],
        out_shape=[jax.ShapeDtypeStruct((nb, T, D), BF16),
                   jax.ShapeDtypeStruct((nb, D), F32)],
        scratch_shapes=[pltpu.VMEM(((tt + 2 * HALO) * nb, D), F32),
                        pltpu.VMEM((nb * tt, D), F32),
                        pltpu.VMEM((nb * tt, D), F32),
                        pltpu.VMEM((nb, D), F32)],
        compiler_params=_cparams(1),
        name="rnn_bwd" if reverse else "rnn_fwd",
    )(xr, xr, xr, conv_w, conv_b.reshape(1, D), wg, ba.reshape(1, D), bx.reshape(1, D),
      lam.reshape(1, D), h0)


def _ret_ctx_kernel(k_ref, v_ref, df_ref, db_ref, sf_ref, sb_ref):
    for hh in range(RET_HEADS):
        kh = k_ref[:, hh * RET_DK:(hh + 1) * RET_DK].astype(F32)
        vh = v_ref[:, hh * RET_DV:(hh + 1) * RET_DV]
        sf_ref[hh] = _tdot((kh * df_ref[hh]).astype(BF16), vh)
        sb_ref[hh] = _tdot((kh * db_ref[hh]).astype(BF16), vh)


def _ret_ctx(zc, B, dec_f, dec_b):
    L = zc.shape[0] // B
    H = RET_HEADS
    st = jax.ShapeDtypeStruct((B, H, RET_DK, RET_DV), F32)
    return pl.pallas_call(
        _ret_ctx_kernel,
        grid=(B,),
        in_specs=[pl.BlockSpec((L, H * RET_DK), lambda b: (b, 3)),
                  pl.BlockSpec((L, H * RET_DV), lambda b: (b, 2)),
                  pl.BlockSpec((H, L, RET_DK), lambda b: (0, 0, 0)),
                  pl.BlockSpec((H, L, RET_DK), lambda b: (0, 0, 0))],
        out_specs=[pl.BlockSpec((None, H, RET_DK, RET_DV), lambda b: (b, 0, 0, 0))] * 2,
        out_shape=[st, st],
        compiler_params=_cparams(1),
        name="ret_ctx",
    )(zc, zc, dec_f, dec_b)


def _ret_bwd_kernel(k_ref, v_ref, s0_ref, kd_ref, cd_ref, o_ref, st_ref):
    @pl.when(pl.program_id(1) == 0)
    def _():
        st_ref[...] = s0_ref[...]

    C = kd_ref.shape[1]
    for cc in reversed(range(k_ref.shape[0] // C)):
        rs = slice(cc * C, (cc + 1) * C)
        for hh in range(RET_HEADS):
            o_ref[cc, hh] = st_ref[hh].astype(o_ref.dtype)
            kh = k_ref[rs, hh * RET_DK:(hh + 1) * RET_DK].astype(F32)
            vh = v_ref[rs, hh * RET_DV:(hh + 1) * RET_DV]
            st_ref[hh] = st_ref[hh] * cd_ref[hh:hh + 1, :] + _tdot((kh * kd_ref[hh]).astype(BF16), vh)


def _ret_bwd(z, B, s_bwd, kdec_b, cdec):
    S = z.shape[0]
    H, C = RET_HEADS, min(RET_C, S)
    cps = min(RET_CPS, S // C)
    n = S // (C * cps)
    return pl.pallas_call(
        _ret_bwd_kernel,
        grid=(B, n),
        in_specs=[pl.BlockSpec((cps * C, H * RET_DK), lambda b, j: (n - 1 - j, b * N_CHUNKS + 3)),
                  pl.BlockSpec((cps * C, H * RET_DV), lambda b, j: (n - 1 - j, b * (N_CHUNKS // 2) + 2)),
                  pl.BlockSpec((None, H, RET_DK, RET_DV), lambda b, j: (b, 0, 0, 0)),
                  pl.BlockSpec((H, C, RET_DK), lambda b, j: (0, 0, 0)),
                  pl.BlockSpec((H, RET_DV), lambda b, j: (0, 0))],
        out_specs=pl.BlockSpec((None, cps, H, RET_DK, RET_DV), lambda b, j: (b, n - 1 - j, 0, 0, 0)),
        out_shape=jax.ShapeDtypeStruct((B, n * cps, H, RET_DK, RET_DV), BF16),
        scratch_shapes=[pltpu.VMEM((H, RET_DK, RET_DV), F32)],
        compiler_params=_cparams(2),
        name="ret_bwd",
    )(z, z, s_bwd, kdec_b, cdec)


def _ret_fwd_kernel(q_ref, k_ref, v_ref, gs_ref, sb_ref, s0_ref, intra_ref, qf_ref, qb_ref, kf_ref, cd_ref,
                    o_ref, st_ref):
    @pl.when(pl.program_id(1) == 0)
    def _():
        st_ref[...] = s0_ref[...]

    C = intra_ref.shape[1]
    for cc in range(q_ref.shape[0] // C):
        rs = slice(cc * C, (cc + 1) * C)
        for hh in range(RET_HEADS):
            qh = q_ref[rs, hh * RET_DK:(hh + 1) * RET_DK]
            kh = k_ref[rs, hh * RET_DK:(hh + 1) * RET_DK]
            vh = v_ref[rs, hh * RET_DV:(hh + 1) * RET_DV]
            s = lax.dot_general(qh, kh, (((1,), (1,)), ((), ())), preferred_element_type=F32)
            o = jnp.dot(s.astype(BF16) * intra_ref[hh], vh, preferred_element_type=F32)
            q2 = jnp.concatenate([qh * qf_ref[hh], qh * qb_ref[hh]], axis=1)
            s2 = jnp.concatenate([st_ref[hh].astype(BF16), sb_ref[cc, hh]], axis=0)
            o = o + jnp.dot(q2, s2, preferred_element_type=F32)
            st_ref[hh] = st_ref[hh] * cd_ref[hh:hh + 1, :] + _tdot(kh * kf_ref[hh], vh)
            mu = jnp.mean(o, axis=-1, keepdims=True)
            d = o - mu
            var = jnp.mean(d * d, axis=-1, keepdims=True)
            g = gs_ref[rs, hh * RET_DV:(hh + 1) * RET_DV]
            gate = g * _sigmoid(g)
            o_ref[rs, hh * RET_DV:(hh + 1) * RET_DV] = (d * lax.rsqrt(var + EPS)).astype(o_ref.dtype) * gate


def _ret_fwd(z, B, sb, s_fwd, intra, qdec_f, qdec_b, kdec_f, cdec):
    S = z.shape[0]
    H, C = RET_HEADS, min(RET_C, S)
    cps = min(RET_CPS, S // C)
    n = S // (C * cps)
    half = N_CHUNKS // 2
    return pl.pallas_call(
        _ret_fwd_kernel,
        grid=(B, n),
        in_specs=[pl.BlockSpec((cps * C, H * RET_DK), lambda b, j: (j, b * N_CHUNKS + 2)),
                  pl.BlockSpec((cps * C, H * RET_DK), lambda b, j: (j, b * N_CHUNKS + 3)),
                  pl.BlockSpec((cps * C, H * RET_DV), lambda b, j: (j, b * half + 2)),
                  pl.BlockSpec((cps * C, H * RET_DV), lambda b, j: (j, b * half + 3)),
                  pl.BlockSpec((None, cps, H, RET_DK, RET_DV), lambda b, j: (b, j, 0, 0, 0)),
                  pl.BlockSpec((None, H, RET_DK, RET_DV), lambda b, j: (b, 0, 0, 0)),
                  pl.BlockSpec((H, C, C), lambda b, j: (0, 0, 0)),
                  pl.BlockSpec((H, C, RET_DK), lambda b, j: (0, 0, 0)),
                  pl.BlockSpec((H, C, RET_DK), lambda b, j: (0, 0, 0)),
                  pl.BlockSpec((H, C, RET_DK), lambda b, j: (0, 0, 0)),
                  pl.BlockSpec((H, RET_DV), lambda b, j: (0, 0))],
        out_specs=pl.BlockSpec((None, cps * C, H * RET_DV), lambda b, j: (b, j, 0)),
        out_shape=jax.ShapeDtypeStruct((B, S, H * RET_DV), BF16),
        scratch_shapes=[pltpu.VMEM((H, RET_DK, RET_DV), F32)],
        compiler_params=_cparams(2),
        name="ret_fwd",
    )(z, z, z, z, sb, s_fwd, intra, qdec_f, qdec_b, kdec_f, cdec)


def _dot_t(a, b):
    return lax.dot_general(a, b, (((1,), (1,)), ((), ())), preferred_element_type=F32)


def _finish_kernel(x_ref, hf_ref, hb_ref, gr_ref, gab_ref, ret_ref, g1_ref, sh2_ref, sc2_ref, n2_ref,
                   wr_ref, wt_ref, wo_ref, rwh_ref, rwl_ref, rb_ref,
                   x1_ref, hl_ref, eid_ref, gate_ref, cnt_ref):
    D = x_ref.shape[1]
    tm = x_ref.shape[0]
    dt = eid_ref.shape[2]
    rnn = hf_ref[...].astype(F32) + hb_ref[...].astype(F32)
    y_rnn = jnp.dot((rnn * jax.nn.gelu(gr_ref[...].astype(F32))).astype(BF16), wr_ref[...],
                    preferred_element_type=F32)
    y_ret = jnp.dot(ret_ref[...], wt_ref[...], preferred_element_type=F32)
    ga = gab_ref[:, :D].astype(F32)
    gb = gab_ref[:, D:].astype(F32)
    merged = _sigmoid(ga) * y_rnn + _sigmoid(gb) * y_ret
    y = jnp.dot(merged.astype(BF16), wo_ref[...], preferred_element_type=F32)
    x1 = x_ref[...] + g1_ref[...] * y
    x1_ref[...] = x1
    hl = _rmsnorm(x1, n2_ref[...]) * (1.0 + sc2_ref[...]) + sh2_ref[...]
    hh = hl.astype(BF16)
    hl_ref[...] = hh

    hlo = (hl - hh.astype(F32)).astype(BF16)
    logits = (_dot_t(rwh_ref[...], hh) + _dot_t(rwh_ref[...], hlo) + _dot_t(rwl_ref[...], hh)) + rb_ref[...]

    ne = logits.shape[0]
    sub = lax.broadcasted_iota(I32, (ne, tm), 0)
    work = logits
    vals, idxs = [], []
    oh = jnp.zeros((ne, tm), F32)
    for _ in range(TOP_K):
        m = jnp.max(work, axis=0, keepdims=True)
        idx = jnp.min(jnp.where(work == m, sub, ne), axis=0, keepdims=True)
        hot = sub == idx
        vals.append(m)
        idxs.append(idx)
        oh = oh + jnp.where(hot, 1.0, 0.0)
        work = jnp.where(hot, -jnp.inf, work)
    es = [jnp.exp(v - vals[0]) for v in vals]
    inv = 1.0 / (es[0] + es[1] + es[2] + es[3])
    for part in range(tm // dt):
        ls = slice(part * dt, (part + 1) * dt)
        for k in range(TOP_K):
            eid_ref[part, k:k + 1, :] = idxs[k][:, ls]
            gate_ref[part, k:k + 1, :] = (es[k] * inv)[:, ls]
        cnt_ref[part] = jnp.sum(oh[:, ls], axis=1, keepdims=True)


def _finish(x3, hf, hb, z, retg, mods3, norm2_g, w_rnn_b, w_ret_b, w_out_b, rwt_hi, rwt_lo, router_b, dt):
    B, S, D = x3.shape
    tm = min(FIN_TM, S)
    nt = S // tm
    N = B * S
    ne = rwt_hi.shape[0]
    half = N_CHUNKS // 2
    per = tm // dt
    const2 = lambda b, i: (0, 0)
    tile3 = lambda b, i: (b * nt + i, 0, 0)
    return pl.pallas_call(
        _finish_kernel,
        grid=(B, nt),
        in_specs=[pl.BlockSpec((None, tm, D), lambda b, i: (b, i, 0)),
                  pl.BlockSpec((None, tm, D), lambda b, i: (b, i, 0)),
                  pl.BlockSpec((None, tm, D), lambda b, i: (b, i, 0)),
                  pl.BlockSpec((tm, D), lambda b, i: (i, b * N_CHUNKS + 1)),
                  pl.BlockSpec((tm, 2 * D), lambda b, i: (i, b * half + 4)),
                  pl.BlockSpec((None, tm, retg.shape[2]), lambda b, i: (b, i, 0)),
                  pl.BlockSpec((None, 1, D), lambda b, i: (b, 0, 2)),
                  pl.BlockSpec((None, 1, D), lambda b, i: (b, 0, 3)),
                  pl.BlockSpec((None, 1, D), lambda b, i: (b, 0, 4)),
                  pl.BlockSpec((1, D), const2),
                  pl.BlockSpec(w_rnn_b.shape, const2),
                  pl.BlockSpec(w_ret_b.shape, const2),
                  pl.BlockSpec(w_out_b.shape, const2),
                  pl.BlockSpec(rwt_hi.shape, const2),
                  pl.BlockSpec(rwt_lo.shape, const2),
                  pl.BlockSpec((ne, 1), const2)],
        out_specs=[pl.BlockSpec((None, tm, D), lambda b, i: (b, i, 0)),
                   pl.BlockSpec((tm, D), lambda b, i: (b * nt + i, 0)),
                   pl.BlockSpec((per, TOP_K, dt), tile3),
                   pl.BlockSpec((per, TOP_K, dt), tile3),
                   pl.BlockSpec((per, ne, 1), tile3)],
        out_shape=[jax.ShapeDtypeStruct((B, S, D), F32),
                   jax.ShapeDtypeStruct((N, D), BF16),
                   jax.ShapeDtypeStruct((N // dt, TOP_K, dt), I32),
                   jax.ShapeDtypeStruct((N // dt, TOP_K, dt), F32),
                   jax.ShapeDtypeStruct((N // dt, ne, 1), F32)],
        compiler_params=_cparams(2),
        name="finish",
    )(x3, hf, hb, z, z, retg, mods3, mods3, mods3, norm2_g.reshape(1, D),
      w_rnn_b, w_ret_b, w_out_b, rwt_hi, rwt_lo, router_b.reshape(ne, 1))


def _local_rows(dt):
    return TOP_K * dt + N_EXPERTS * ROW_ALIGN


def _local_slots(eid_ref):
    dt = eid_ref.shape[1]
    ne = N_EXPERTS
    sub = lax.broadcasted_iota(I32, (ne, dt), 0)
    hots = [sub == eid_ref[k:k + 1, :] for k in range(TOP_K)]
    oh = jnp.zeros((ne, dt), F32)
    for hot in hots:
        oh = oh + jnp.where(hot, 1.0, 0.0)
    earlier = jnp.where(lax.broadcasted_iota(I32, (dt, dt), 0) < lax.broadcasted_iota(I32, (dt, dt), 1), 1.0, 0.0)
    before = jnp.dot(oh.astype(BF16), earlier.astype(BF16), preferred_element_type=F32)
    cnt = jnp.broadcast_to(jnp.sum(oh, axis=1, keepdims=True), (ne, dt))
    cnt = jnp.maximum(jnp.ceil(cnt * (1.0 / ROW_ALIGN)) * ROW_ALIGN, float(ROW_ALIGN))
    lower = jnp.where(lax.broadcasted_iota(I32, (ne, ne), 1) < lax.broadcasted_iota(I32, (ne, ne), 0), 1.0, 0.0)
    base = before + jnp.dot(lower.astype(BF16), cnt.astype(BF16), preferred_element_type=F32)
    return [jnp.sum(jnp.where(hot, base, 0.0), axis=0, keepdims=True).astype(I32) for hot in hots]


def _slot_matrix(slots, weights, out_ref):
    rows, dt = out_ref.shape
    ch = 64
    rel = lax.broadcasted_iota(I32, (ch, dt), 0).astype(F32).astype(BF16)
    slots_f = [s.astype(F32) for s in slots]
    weights_b = [jnp.asarray(w, F32).astype(BF16) for w in weights]
    zero = jnp.zeros((), BF16)
    for c in range(rows // ch):
        acc = None
        for s, w in zip(slots_f, weights_b):
            term = jnp.where(rel == (s - float(c * ch)).astype(BF16), w, zero)
            acc = term if acc is None else acc + term
        out_ref[c * ch:(c + 1) * ch, :] = acc


def _for_each_run(meta_ref, fn, maybe_empty=False):
    def one(e):
        n = pl.multiple_of(meta_ref[0, N_EXPERTS + e], ROW_ALIGN)
        args = (pl.multiple_of(meta_ref[0, e], ROW_ALIGN), pl.multiple_of(meta_ref[0, 2 * N_EXPERTS + e], ROW_ALIGN), n)
        if maybe_empty:
            pl.when(n > 0)(lambda: fn(*args))
        else:
            fn(*args)

    if maybe_empty:
        lax.fori_loop(0, N_EXPERTS, lambda e, c: (one(e), c)[1], 0)
    else:
        for e in range(N_EXPERTS):
            one(e)


def _run_rows(meta_ref):
    last = N_EXPERTS - 1
    return pl.multiple_of(meta_ref[0, last] + meta_ref[0, N_EXPERTS + last], ROW_ALIGN)


def _dispatch_kernel(meta_ref, zmeta_ref, eid_ref, x_ref, hp_ref, slot_ref, sbuf_ref, zbuf_ref, pm_ref, rows_ref,
                     sems, zsem):
    t = pl.program_id(0)
    slot = t % 2

    def zero_copy(_, g, n):
        return pltpu.make_async_copy(zbuf_ref.at[pl.ds(0, n), :], hp_ref.at[pl.ds(g, n), :], zsem)

    def run_copy(sl, l, g, n):
        return pltpu.make_async_copy(sbuf_ref.at[sl, pl.ds(l, n), :], hp_ref.at[pl.ds(g, n), :], sems.at[sl])

    @pl.when(t == 0)
    def _():
        zbuf_ref[...] = jnp.zeros_like(zbuf_ref)
        _for_each_run(zmeta_ref, lambda l, g, n: zero_copy(l, g, n).start(), maybe_empty=True)
        _for_each_run(zmeta_ref, lambda l, g, n: zero_copy(l, g, n).wait(), maybe_empty=True)

    slots = _local_slots(eid_ref)
    for k in range(TOP_K):
        slot_ref[k:k + 1, :] = slots[k]
    _slot_matrix(slots, [1.0] * TOP_K, pm_ref)
    sbuf_ref[slot] = _pack_cols(jnp.dot(pm_ref[...], x_ref[...], preferred_element_type=F32))

    @pl.when(t > 0)
    def _():
        run_copy(1 - slot, 0, 0, pl.multiple_of(rows_ref[1 - slot], ROW_ALIGN)).wait()

    _for_each_run(meta_ref, lambda l, g, n: run_copy(slot, l, g, n).start())
    rows_ref[slot] = _run_rows(meta_ref)

    @pl.when(t == pl.num_programs(0) - 1)
    def _():
        run_copy(slot, 0, 0, _run_rows(meta_ref)).wait()


def _dispatch(hl2, eid3, meta3, zmeta, P):
    N, D = hl2.shape
    nt, _, dt = eid3.shape
    nm = meta3.shape[2]
    return pl.pallas_call(
        _dispatch_kernel,
        grid=(nt,),
        in_specs=[pl.BlockSpec((None, 1, nm), lambda i: (i, 0, 0), memory_space=pltpu.SMEM),
                  pl.BlockSpec((1, nm), lambda i: (0, 0), memory_space=pltpu.SMEM),
                  pl.BlockSpec((None, TOP_K, dt), lambda i: (i, 0, 0)),
                  pl.BlockSpec((dt, D), lambda i: (i, 0))],
        out_specs=[pl.BlockSpec(memory_space=pl.ANY),
                   pl.BlockSpec((None, TOP_K, dt), lambda i: (i, 0, 0))],
        out_shape=[jax.ShapeDtypeStruct((P, D // 2), U32),
                   jax.ShapeDtypeStruct((nt, TOP_K, dt), I32)],
        scratch_shapes=[pltpu.VMEM((2, _local_rows(dt), D // 2), U32),
                        pltpu.VMEM((MOE_BLOCK, D // 2), U32),
                        pltpu.VMEM((_local_rows(dt), dt), BF16),
                        pltpu.SMEM((2,), I32),
                        pltpu.SemaphoreType.DMA((2,)),
                        pltpu.SemaphoreType.DMA(())],
        compiler_params=_cparams(1),
        name="dispatch",
    )(meta3, zmeta, eid3, hl2)


def _regroup_kernel(w_ref, sel_ref, o_ref):
    half = w_ref.shape[1] // 2
    g = sel_ref.shape[0]
    wb = w_ref[...].astype(BF16)
    for j in range(w_ref.shape[1] // g):
        r = jnp.dot(wb[:, g * j:g * (j + 1)], sel_ref[...], preferred_element_type=F32)
        o_ref[:, (g // 2) * j:(g // 2) * (j + 1)] = r[:, :g // 2].astype(o_ref.dtype)
        o_ref[:, half + (g // 2) * j:half + (g // 2) * (j + 1)] = r[:, g // 2:].astype(o_ref.dtype)


def _regroup_glu_lin(w1):
    ne, D, de2 = w1.shape
    g = 256
    sel = np.zeros((g, g), np.float32)
    sel[np.arange(0, g, 2), np.arange(g // 2)] = 1.0
    sel[np.arange(1, g, 2), g // 2 + np.arange(g // 2)] = 1.0
    out = pl.pallas_call(
        _regroup_kernel,
        grid=(ne,),
        in_specs=[pl.BlockSpec((D, de2), lambda e: (e, 0)),
                  pl.BlockSpec((g, g), lambda e: (0, 0))],
        out_specs=pl.BlockSpec((D, de2), lambda e: (e, 0)),
        out_shape=jax.ShapeDtypeStruct((ne * D, de2), BF16),
        compiler_params=_cparams(1),
        name="regroup",
    )(w1.reshape(ne * D, de2), jnp.asarray(sel, BF16))
    return out.reshape(ne, D, de2)


def _expert_kernel(be_ref, nu_ref, x_ref, w1_ref, b1_ref, w2_ref, b2_ref, o_ref):
    del be_ref

    @pl.when(pl.program_id(0) < nu_ref[0])
    def _():
        de = w2_ref.shape[0]
        h = jnp.dot(_unpack_cols(x_ref[...]), w1_ref[...], preferred_element_type=F32) + b1_ref[...]
        glu = jnp.minimum(h[:, :de], SWIGLU_LIMIT)
        lin = jnp.clip(h[:, de:], -SWIGLU_LIMIT, SWIGLU_LIMIT)
        act = glu * _sigmoid(SWIGLU_ALPHA * glu) * (lin + 1.0)
        y = jnp.dot(act.astype(BF16), w2_ref[...].astype(BF16), preferred_element_type=F32) + b2_ref[...]
        o_ref[...] = _pack_cols(y.astype(BF16).astype(F32))


def _experts(blk_expert, n_used, h_pad, w1b, b1p, w2b, b2):
    P = h_pad.shape[0]
    ne, D, de2 = w1b.shape
    de = de2 // 2
    nblk = P // MOE_BLOCK
    blk = lambda j, be, nu: (jnp.minimum(j, nu[0] - 1), 0)
    wsel = lambda j, be, nu: (be[jnp.minimum(j, nu[0] - 1)], 0, 0)
    return pl.pallas_call(
        _expert_kernel,
        grid_spec=pltpu.PrefetchScalarGridSpec(
            num_scalar_prefetch=2,
            grid=(nblk,),
            in_specs=[pl.BlockSpec((MOE_BLOCK, D // 2), blk),
                      pl.BlockSpec((None, D, de2), wsel),
                      pl.BlockSpec((None, 1, de2), wsel),
                      pl.BlockSpec((None, de, D), wsel),
                      pl.BlockSpec((None, 1, D), wsel)],
            out_specs=pl.BlockSpec((MOE_BLOCK, D // 2), blk)),
        out_shape=jax.ShapeDtypeStruct((P, D // 2), U32),
        compiler_params=_cparams(1),
        name="experts",
    )(blk_expert, n_used, h_pad, w1b, b1p.reshape(ne, 1, de2), w2b, b2.reshape(ne, 1, D))


def _combine_kernel(meta_ref, nmeta_ref, slot_ref, gate_ref, x1_ref, g2_ref, fg_ref, yp_ref, o_ref,
                    ybuf_ref, gm_ref, sems):
    t = pl.program_id(0) * pl.num_programs(1) + pl.program_id(1)
    n_tiles = pl.num_programs(0) * pl.num_programs(1)
    slot = t % 2

    def run_copy(sl, l, g, n):
        return pltpu.make_async_copy(yp_ref.at[pl.ds(g, n), :], ybuf_ref.at[sl, pl.ds(l, n), :], sems.at[sl])

    @pl.when(t == 0)
    def _():
        ybuf_ref[...] = jnp.zeros_like(ybuf_ref)
        _for_each_run(meta_ref, lambda l, g, n: run_copy(0, l, g, n).start())

    @pl.when(t + 1 < n_tiles)
    def _():
        _for_each_run(nmeta_ref, lambda l, g, n: run_copy(1 - slot, l, g, n).start())

    run_copy(slot, 0, 0, _run_rows(meta_ref)).wait()

    _slot_matrix([slot_ref[k:k + 1, :] for k in range(TOP_K)], [gate_ref[k:k + 1, :] for k in range(TOP_K)], gm_ref)
    y = _tdot(gm_ref[...], _unpack_cols(ybuf_ref[slot]))
    x2 = x1_ref[...] + g2_ref[...] * y
    o_ref[...] = _rmsnorm(x2, fg_ref[...])


def _combine(meta3, slot3, gate3, x1, mods3, final_g, y_pad):
    B, S, D = x1.shape
    n_tiles, _, dt = slot3.shape
    nt = S // dt
    nm = meta3.shape[2]
    tile3 = lambda b, i: (b * nt + i, 0, 0)
    next3 = lambda b, i: (jnp.minimum(b * nt + i + 1, n_tiles - 1), 0, 0)
    return pl.pallas_call(
        _combine_kernel,
        grid=(B, nt),
        in_specs=[pl.BlockSpec((None, 1, nm), tile3, memory_space=pltpu.SMEM),
                  pl.BlockSpec((None, 1, nm), next3, memory_space=pltpu.SMEM),
                  pl.BlockSpec((None, TOP_K, dt), tile3),
                  pl.BlockSpec((None, TOP_K, dt), tile3),
                  pl.BlockSpec((None, dt, D), lambda b, i: (b, i, 0)),
                  pl.BlockSpec((None, 1, D), lambda b, i: (b, 0, 5)),
                  pl.BlockSpec((1, D), lambda b, i: (0, 0)),
                  pl.BlockSpec(memory_space=pl.ANY)],
        out_specs=pl.BlockSpec((None, dt, D), lambda b, i: (b, i, 0)),
        out_shape=jax.ShapeDtypeStruct((B, S, D), F32),
        scratch_shapes=[pltpu.VMEM((2, _local_rows(dt), D // 2), U32), pltpu.VMEM((_local_rows(dt), dt), BF16),
                        pltpu.SemaphoreType.DMA((2,))],
        compiler_params=_cparams(2),
        name="combine",
    )(meta3, meta3, slot3, gate3, x1, mods3, final_g.reshape(1, D), y_pad)


def _rope_tables(S, k_scale):
    n_freq = RET_DK // 4
    pos = np.arange(S)
    rows = (pos // GRID_W).astype(np.float32)
    cols = (pos % GRID_W).astype(np.float32)
    inv = (np.float32(ROPE_BASE) ** (-np.arange(n_freq, dtype=np.float32) / np.float32(n_freq))).astype(np.float32)
    ang = np.concatenate([rows[:, None] * inv, cols[:, None] * inv], axis=-1).astype(np.float32)
    cos, sin = np.cos(ang), np.sin(ang)
    cos2 = np.concatenate([cos, cos], axis=-1)
    sin2 = np.concatenate([-sin, sin], axis=-1)
    ks = np.float32(k_scale)
    return jnp.asarray(np.concatenate([cos2, sin2, cos2 * ks, sin2 * ks], axis=-1), F32)


def _identity_tables(L, k_scale):
    one = np.ones((L, RET_DK), np.float32)
    zero = np.zeros((L, RET_DK), np.float32)
    return jnp.asarray(np.concatenate([one, zero, one * np.float32(k_scale), zero], axis=-1), F32)


def _lanes(t, width):
    return np.ascontiguousarray(np.broadcast_to(t[:, :, None], t.shape + (width,)))


def kernel(x, c, ctx, c_ctx, ada_w, ada_b, norm1_g, w_in, conv_w, conv_b, lru_wa, lru_ba, lru_wx, lru_bx,
           lru_lambda, w_rnn_proj, w_ret_proj, w_out, norm2_g, router_w, router_b, moe_w1, moe_b1, moe_w2,
           moe_b2, final_g):
    B, S, D = x.shape
    L = ctx.shape[1]
    N = B * S
    H = RET_HEADS
    lyr = 0
    d_in = w_in.shape[2]
    assert ada_w.shape[0] == 1 and d_in == N_CHUNKS * D and B == 8

    def pairs_apart(w):
        return jnp.swapaxes(w.reshape(D, H, RET_DK // 2, 2), 2, 3).reshape(D, H * RET_DK)

    w_in_b = w_in[lyr].astype(BF16)
    w_qk = jnp.concatenate([pairs_apart(w_in_b[:, 2 * D:3 * D]), pairs_apart(w_in_b[:, 3 * D:4 * D])], axis=1)
    wg = [(0.5 * jnp.concatenate([lru_wa[lyr, d], lru_wx[lyr, d]], axis=-1)).astype(BF16) for d in range(2)]
    de2 = moe_w1.shape[3]
    glu_lin = np.concatenate([np.arange(0, de2, 2), np.arange(1, de2, 2)])
    w1b = _regroup_glu_lin(moe_w1[lyr])
    b1p = moe_b1[lyr][:, glu_lin]
    rwt = router_w[lyr].T
    rwt_hi = rwt.astype(BF16)
    rwt_lo = (rwt - rwt_hi.astype(F32)).astype(BF16)

    k_scale = RET_DK ** -0.5
    tab_l = _rope_tables(S, k_scale)
    tab_c = _identity_tables(B * L, k_scale)
    f4 = np.float32
    log_g = np.log1p(-np.exp2(-5.0 - np.arange(H, dtype=f4))).astype(f4)
    C = min(RET_C, S)
    idx = np.arange(C, dtype=f4)
    dec = lambda e: np.exp(e[None, :].astype(f4) * log_g[:, None]).astype(f4)
    intra = jnp.asarray(np.exp(np.abs(idx[:, None] - idx[None, :])[None] * log_g[:, None, None]).astype(f4))
    qdec_f = jnp.asarray(_lanes(dec(idx + 1.0), RET_DK))
    qdec_b = jnp.asarray(_lanes(dec(C - idx), RET_DK))
    kdec_f = jnp.asarray(_lanes(dec(C - 1.0 - idx), RET_DK))
    kdec_b = jnp.asarray(_lanes(dec(idx), RET_DK))
    cdec = jnp.asarray(np.broadcast_to(np.exp(C * log_g).astype(f4)[:, None], (H, RET_DV)))
    pos_c = np.arange(L, dtype=f4)
    cdec_f = jnp.asarray(_lanes(dec(L - 1.0 - pos_c), RET_DK))
    cdec_b = jnp.asarray(_lanes(dec(pos_c), RET_DK))

    cvec = jnp.zeros((16, D), F32).at[:B].set(c).at[B].set(c_ctx)
    mods3 = _ada(cvec, ada_w[lyr], ada_b[lyr]).reshape(16, 1, 6 * D)

    z_c, xr_c = _proj(ctx.reshape(1, B * L, D), mods3, lambda b: B, norm1_g[lyr], w_in_b, w_qk, tab_c,
                      min(PROJ_TM, B * L))
    xr_c = xr_c.reshape(B, L, D)
    z_l, xr_l = _proj(x, mods3, lambda b: b, norm1_g[lyr], w_in_b, w_qk, tab_l, min(PROJ_TM, S))

    zeros = jnp.zeros((B, D), F32)
    hs = []
    for d in range(2):
        args = (conv_w[lyr], conv_b[lyr], wg[d], 0.5 * lru_ba[lyr, d], 0.5 * lru_bx[lyr, d], lru_lambda[lyr, d])
        _, h0 = _rnn(xr_c, *args, zeros, reverse=(d == 1), tt=RNN_TT)
        h, _ = _rnn(xr_l, *args, h0, reverse=(d == 1), tt=RNN_TT)
        hs.append(h)

    s_fwd, s_bwd = _ret_ctx(z_c, B, cdec_f, cdec_b)
    sb = _ret_bwd(z_l, B, s_bwd, kdec_b, cdec)
    retg = _ret_fwd(z_l, B, sb, s_fwd, intra.astype(BF16), qdec_f.astype(BF16), qdec_b.astype(BF16),
                    kdec_f.astype(BF16), cdec)

    dt = min(DISP_TM, S)
    x1, hl2, eid3, gate3, cnt3 = _finish(
        x, hs[0], hs[1], z_l, retg, mods3, norm2_g[lyr], w_rnn_proj[lyr].astype(BF16),
        w_ret_proj[lyr].astype(BF16), w_out[lyr].astype(BF16), rwt_hi, rwt_lo, router_b[lyr], dt)

    cnt_t = cnt3[:, :, 0].astype(I32)
    cnt_t = jnp.maximum((cnt_t + ROW_ALIGN - 1) // ROW_ALIGN * ROW_ALIGN, ROW_ALIGN)
    cnt = jnp.sum(cnt_t, axis=0)
    padded = (cnt + MOE_BLOCK - 1) // MOE_BLOCK * MOE_BLOCK
    pad_end = jnp.cumsum(padded)
    pad_start = pad_end - padded
    gstart = pad_start[None, :] + jnp.cumsum(cnt_t, axis=0) - cnt_t
    loff = jnp.cumsum(cnt_t, axis=1) - cnt_t
    meta3 = jnp.concatenate([loff, cnt_t, gstart], axis=1).reshape(N // dt, 1, 3 * N_EXPERTS)
    zmeta = jnp.concatenate([jnp.zeros_like(cnt), padded - cnt, pad_start + cnt]).reshape(1, 3 * N_EXPERTS)
    n_blocks = -(-(N * TOP_K + (N // dt) * N_EXPERTS * ROW_ALIGN) // MOE_BLOCK) + N_EXPERTS
    P = n_blocks * MOE_BLOCK
    blk_start = jnp.arange(n_blocks, dtype=I32) * MOE_BLOCK
    blk_expert = jnp.minimum(jnp.sum((pad_end[None, :] <= blk_start[:, None]).astype(I32), axis=1), N_EXPERTS - 1)
    n_used = (pad_end[-1:] // MOE_BLOCK).astype(I32)

    h_pad, slot3 = _dispatch(hl2, eid3, meta3, zmeta, P)
    y_pad = _experts(blk_expert, n_used, h_pad, w1b, b1p, moe_w2[lyr], moe_b2[lyr])
    return _combine(meta3, slot3, gate3, x1, mods3, final_g, y_pad)
```

```python
import functools

import jax
import jax.numpy as jnp
import numpy as np
from jax import lax
from jax.experimental import pallas as pl
from jax.experimental.pallas import tpu as pltpu

F32 = jnp.float32
BF16 = jnp.bfloat16
I32 = jnp.int32

GRID_W = 64
RNN_BLOCKS = 8
CONV_W = 4
LRU_C = 8.0
RET_HEADS = 8
RET_DK = 128
RET_DV = 256
ROPE_BASE = 10000.0
N_EXPERTS = 32
TOP_K = 4
SWIGLU_ALPHA = 1.702
SWIGLU_LIMIT = 7.0
EPS = 1e-6
N_CHUNKS = 10

PROJ_TM = 512
PROJ_TN = 2048
RNN_TT = 256
RET_C = 256
RET_CPS = 4
FIN_TM = 512
MOE_BLOCK = 1024
MOE_SUB = 256
DISP_TM = 512
ROW_ALIGN = 8
VMEM_LIMIT = 56 * 1024 * 1024


def _cparams(n_axes):
    return pltpu.CompilerParams(dimension_semantics=("arbitrary",) * n_axes,
                                vmem_limit_bytes=VMEM_LIMIT)


def _sigmoid(x):
    return 0.5 * (jnp.tanh(0.5 * x) + 1.0)


def _rmsnorm(x, g):
    return x * lax.rsqrt(jnp.mean(x * x, axis=-1, keepdims=True) + EPS) * g


U32 = jnp.uint32


def _pack_cols(v):
    w = v.shape[1] // 2
    lo = lax.shift_right_logical(lax.bitcast_convert_type(v[:, :w], U32), jnp.uint32(16))
    hi = lax.bitcast_convert_type(v[:, w:], U32) & jnp.uint32(0xFFFF0000)
    return lo | hi


def _unpack_cols(u):
    a = lax.bitcast_convert_type(lax.shift_left(u, jnp.uint32(16)), F32).astype(BF16)
    b = lax.bitcast_convert_type(u & jnp.uint32(0xFFFF0000), F32).astype(BF16)
    return jnp.concatenate([a, b], axis=1)


def _tdot(a, b):
    return lax.dot_general(a, b, (((0,), (0,)), ((), ())), preferred_element_type=F32)


def _ada_kernel(c_ref, w_ref, b_ref, o_ref):
    c = c_ref[...]
    s = c * _sigmoid(c)
    o_ref[...] = jnp.dot(s, w_ref[...], preferred_element_type=F32,
                         precision=lax.Precision.HIGHEST) + b_ref[...]


def _ada(cvec, ada_w, ada_b):
    R, D = cvec.shape
    n = ada_w.shape[1] // D
    return pl.pallas_call(
        _ada_kernel,
        grid=(n,),
        in_specs=[pl.BlockSpec((R, D), lambda j: (0, 0)),
                  pl.BlockSpec((D, D), lambda j: (0, j)),
                  pl.BlockSpec((1, D), lambda j: (0, j))],
        out_specs=pl.BlockSpec((R, D), lambda j: (0, j)),
        out_shape=jax.ShapeDtypeStruct((R, n * D), F32),
        compiler_params=_cparams(1),
        name="ada",
    )(cvec, ada_w, ada_b.reshape(1, -1))


def _proj_kernel(x_ref, sh_ref, sc_ref, g_ref, w_ref, wqk_ref, tab_ref, o_ref, xr_ref):
    D = x_ref.shape[1]
    h = _rmsnorm(x_ref[...], g_ref[...])
    hb = (h * (1.0 + sc_ref[...]) + sh_ref[...]).astype(BF16)
    for j in range(w_ref.shape[1] // PROJ_TN):
        j0 = j * PROJ_TN
        wj = wqk_ref[...] if j == 1 else w_ref[:, j0:j0 + PROJ_TN]
        acc = jnp.dot(hb, wj, preferred_element_type=F32)
        if j == 0:
            xr_ref[...] = acc[:, :D].astype(xr_ref.dtype)
        if j != 1:
            o_ref[:, j0:j0 + PROJ_TN] = acc.astype(o_ref.dtype)
            continue
        for part in range(2):
            cos = tab_ref[:, (2 * part) * RET_DK:(2 * part + 1) * RET_DK]
            sin = tab_ref[:, (2 * part + 1) * RET_DK:(2 * part + 2) * RET_DK]
            for hh in range(RET_HEADS):
                c0 = part * RET_HEADS * RET_DK + hh * RET_DK
                t = acc[:, c0:c0 + RET_DK]
                o_ref[:, j0 + c0:j0 + c0 + RET_DK] = (
                    t * cos + pltpu.roll(t, RET_DK // 2, 1) * sin).astype(o_ref.dtype)


def _proj(x3, mods3, mod_row, norm_g, w_in_b, w_qk, tab, tm):
    B, S, D = x3.shape
    d_in = w_in_b.shape[1]
    return pl.pallas_call(
        _proj_kernel,
        grid=(B, S // tm),
        in_specs=[pl.BlockSpec((None, tm, D), lambda b, i: (b, i, 0)),
                  pl.BlockSpec((None, 1, D), lambda b, i: (mod_row(b), 0, 0)),
                  pl.BlockSpec((None, 1, D), lambda b, i: (mod_row(b), 0, 1)),
                  pl.BlockSpec((1, D), lambda b, i: (0, 0)),
                  pl.BlockSpec((D, d_in), lambda b, i: (0, 0), pipeline_mode=pl.Buffered(1)),
                  pl.BlockSpec((D, PROJ_TN), lambda b, i: (0, 0), pipeline_mode=pl.Buffered(1)),
                  pl.BlockSpec((tm, 4 * RET_DK), lambda b, i: (i, 0))],
        out_specs=[pl.BlockSpec((tm, d_in), lambda b, i: (i, b)),
                   pl.BlockSpec((None, tm, D), lambda b, i: (b, i, 0))],
        out_shape=[jax.ShapeDtypeStruct((S, B * d_in), BF16),
                   jax.ShapeDtypeStruct((B, S, D), BF16)],
        compiler_params=_cparams(2),
        name="proj",
    )(x3, mods3, mods3, norm_g.reshape(1, D), w_in_b, w_qk, tab)


def _rnn_kernel(xm_ref, xp_ref, xn_ref, cw_ref, cb_ref, wg_ref, ba_ref, bx_ref, lam_ref, h0_ref,
                h_ref, hfin_ref, xs_ref, a_ref, u_ref, hc_ref, *, reverse, n_tiles):
    i = pl.program_id(0)
    tile = (n_tiles - 1 - i) if reverse else i
    nb, tt, D = xm_ref.shape
    bw = D // RNN_BLOCKS
    HALO = xp_ref.shape[1]

    @pl.when(i == 0)
    def _():
        hc_ref[...] = h0_ref[...]

    GT = HALO
    gr = GT * nb
    ri = lax.broadcasted_iota(I32, (gr, gr), 0)
    ci = lax.broadcasted_iota(I32, (gr, gr), 1)
    perm = jnp.where((ri // nb == ci % GT) & (ri % nb == ci // GT), 1.0, 0.0).astype(BF16)

    def time_major(ref, t0):
        xg = jnp.concatenate([ref[b, t0:t0 + GT, :] for b in range(nb)], axis=0)
        return jnp.dot(perm, xg, preferred_element_type=F32)

    hr = HALO * nb
    R = tt * nb
    xs_ref[0:hr, :] = jnp.where(tile > 0, time_major(xp_ref, 0), 0.0)
    for g in range(tt // GT):
        xs_ref[hr + g * gr:hr + (g + 1) * gr, :] = time_major(xm_ref, g * GT)
    xs_ref[hr + R:2 * hr + R, :] = jnp.where(tile < n_tiles - 1, time_major(xn_ref, 0), 0.0)

    nl = -lam_ref[...]
    csp = (0.25 * LRU_C) * (jnp.maximum(nl, 0.0) + jnp.log1p(jnp.exp(-jnp.abs(nl))))
    SUB = 256

    def gates(s, carry):
        r0 = pl.multiple_of(s * SUB, SUB)
        base = hr - 2 * nb
        xc = cb_ref[...] + cw_ref[0:1, :] * xs_ref[pl.ds(pl.multiple_of(r0 + base, nb), SUB), :]
        for k in range(1, CONV_W):
            xc = xc + cw_ref[k:k + 1, :] * xs_ref[pl.ds(pl.multiple_of(r0 + base + k * nb, nb), SUB), :]
        xb = xc.astype(BF16)
        for n in range(RNN_BLOCKS):
            g = jnp.dot(xb[:, n * bw:(n + 1) * bw], wg_ref[n], preferred_element_type=F32)
            cs = slice(n * bw, (n + 1) * bw)
            tr = jnp.tanh(g[:, :bw] + ba_ref[:, cs])
            ti = jnp.tanh(g[:, bw:] + bx_ref[:, cs])
            t = jnp.tanh(csp[:, cs] * tr + csp[:, cs])
            rc = 1.0 / (1.0 + t)
            a_ref[pl.ds(r0, SUB), cs] = (1.0 - t) * rc
            root = t * lax.rsqrt(jnp.maximum(t, 1e-30))
            u_ref[pl.ds(r0, SUB), cs] = (root * rc) * ((ti + 1.0) * xc[:, cs])
        return carry

    lax.fori_loop(0, R // SUB, gates, 0)

    def step(t, h):
        ts = (tt - 1 - t) if reverse else t
        r0 = pl.multiple_of(ts * nb, nb)
        h = a_ref[pl.ds(r0, nb), :] * h + u_ref[pl.ds(r0, nb), :]
        u_ref[pl.ds(r0, nb), :] = h
        return h

    h = lax.fori_loop(0, tt, step, hc_ref[...], unroll=8)
    hc_ref[...] = h
    hfin_ref[...] = h
    for g in range(tt // GT):
        hg = _tdot(perm, u_ref[g * gr:(g + 1) * gr, :].astype(BF16))
        for b in range(nb):
            h_ref[b, g * GT:(g + 1) * GT, :] = hg[b * GT:(b + 1) * GT, :].astype(h_ref.dtype)


def _rnn(xr, conv_w, conv_b, wg, ba, bx, lam, h0, *, reverse, tt):
    nb, T, D = xr.shape
    tt = min(tt, T)
    n_tiles = T // tt
    HALO = 16
    hb = tt // HALO

    def tile_of(i):
        return (n_tiles - 1 - i) if reverse else i

    kern = functools.partial(_rnn_kernel, reverse=reverse, n_tiles=n_tiles)
    return pl.pallas_call(
        kern,
        grid=(n_tiles,),
        in_specs=[pl.BlockSpec((nb, tt, D), lambda i: (0, tile_of(i), 0)),
                  pl.BlockSpec((nb, HALO, D), lambda i: (0, jnp.maximum(tile_of(i) * hb - 1, 0), 0)),
                  pl.BlockSpec((nb, HALO, D), lambda i: (0, jnp.minimum((tile_of(i) + 1) * hb, n_tiles * hb - 1), 0)),
                  pl.BlockSpec((CONV_W, D), lambda i: (0, 0)),
                  pl.BlockSpec((1, D), lambda i: (0, 0)),
                  pl.BlockSpec((RNN_BLOCKS, D // RNN_BLOCKS, 2 * D // RNN_BLOCKS), lambda i: (0, 0, 0)),
                  pl.BlockSpec((1, D), lambda i: (0, 0)),
                  pl.BlockSpec((1, D), lambda i: (0, 0)),
                  pl.BlockSpec((1, D), lambda i: (0, 0)),
                  pl.BlockSpec((nb, D), lambda i: (0, 0))],
        out_specs=[pl.BlockSpec((nb, tt, D), lambda i: (0, tile_of(i), 0)),
                   pl.BlockSpec((nb, D), lambda i: (0, 0))],
        out_shape=[jax.ShapeDtypeStruct((nb, T, D), BF16),
                   jax.ShapeDtypeStruct((nb, D), F32)],
        scratch_shapes=[pltpu.VMEM(((tt + 2 * HALO) * nb, D), F32),
                        pltpu.VMEM((nb * tt, D), F32),
                        pltpu.VMEM((nb * tt, D), F32),
                        pltpu.VMEM((nb, D), F32)],
        compiler_params=_cparams(1),
        name="rnn_bwd" if reverse else "rnn_fwd",
    )(xr, xr, xr, conv_w, conv_b.reshape(1, D), wg, ba.reshape(1, D), bx.reshape(1, D),
      lam.reshape(1, D), h0)


def _ret_ctx_kernel(k_ref, v_ref, df_ref, db_ref, sf_ref, sb_ref):
    for hh in range(RET_HEADS):
        kh = k_ref[:, hh * RET_DK:(hh + 1) * RET_DK].astype(F32)
        vh = v_ref[:, hh * RET_DV:(hh + 1) * RET_DV]
        sf_ref[hh] = _tdot((kh * df_ref[hh]).astype(BF16), vh)
        sb_ref[hh] = _tdot((kh * db_ref[hh]).astype(BF16), vh)


def _ret_ctx(zc, B, dec_f, dec_b):
    L = zc.shape[0] // B
    H = RET_HEADS
    st = jax.ShapeDtypeStruct((B, H, RET_DK, RET_DV), F32)
    return pl.pallas_call(
        _ret_ctx_kernel,
        grid=(B,),
        in_specs=[pl.BlockSpec((L, H * RET_DK), lambda b: (b, 3)),
                  pl.BlockSpec((L, H * RET_DV), lambda b: (b, 2)),
                  pl.BlockSpec((H, L, RET_DK), lambda b: (0, 0, 0)),
                  pl.BlockSpec((H, L, RET_DK), lambda b: (0, 0, 0))],
        out_specs=[pl.BlockSpec((None, H, RET_DK, RET_DV), lambda b: (b, 0, 0, 0))] * 2,
        out_shape=[st, st],
        compiler_params=_cparams(1),
        name="ret_ctx",
    )(zc, zc, dec_f, dec_b)


def _ret_bwd_kernel(k_ref, v_ref, s0_ref, kd_ref, cd_ref, o_ref, st_ref):
    @pl.when(pl.program_id(1) == 0)
    def _():
        st_ref[...] = s0_ref[...]

    C = kd_ref.shape[1]
    for cc in reversed(range(k_ref.shape[0] // C)):
        rs = slice(cc * C, (cc + 1) * C)
        for hh in range(RET_HEADS):
            o_ref[cc, hh] = st_ref[hh].astype(o_ref.dtype)
            kh = k_ref[rs, hh * RET_DK:(hh + 1) * RET_DK].astype(F32)
            vh = v_ref[rs, hh * RET_DV:(hh + 1) * RET_DV]
            st_ref[hh] = st_ref[hh] * cd_ref[hh:hh + 1, :] + _tdot((kh * kd_ref[hh]).astype(BF16), vh)


def _ret_bwd(z, B, s_bwd, kdec_b, cdec):
    S = z.shape[0]
    H, C = RET_HEADS, min(RET_C, S)
    cps = min(RET_CPS, S // C)
    n = S // (C * cps)
    return pl.pallas_call(
        _ret_bwd_kernel,
        grid=(B, n),
        in_specs=[pl.BlockSpec((cps * C, H * RET_DK), lambda b, j: (n - 1 - j, b * N_CHUNKS + 3)),
                  pl.BlockSpec((cps * C, H * RET_DV), lambda b, j: (n - 1 - j, b * (N_CHUNKS // 2) + 2)),
                  pl.BlockSpec((None, H, RET_DK, RET_DV), lambda b, j: (b, 0, 0, 0)),
                  pl.BlockSpec((H, C, RET_DK), lambda b, j: (0, 0, 0)),
                  pl.BlockSpec((H, RET_DV), lambda b, j: (0, 0))],
        out_specs=pl.BlockSpec((None, cps, H, RET_DK, RET_DV), lambda b, j: (b, n - 1 - j, 0, 0, 0)),
        out_shape=jax.ShapeDtypeStruct((B, n * cps, H, RET_DK, RET_DV), BF16),
        scratch_shapes=[pltpu.VMEM((H, RET_DK, RET_DV), F32)],
        compiler_params=_cparams(2),
        name="ret_bwd",
    )(z, z, s_bwd, kdec_b, cdec)


def _ret_fwd_kernel(q_ref, k_ref, v_ref, gs_ref, sb_ref, s0_ref, intra_ref, qf_ref, qb_ref, kf_ref, cd_ref,
                    o_ref, st_ref):
    @pl.when(pl.program_id(1) == 0)
    def _():
        st_ref[...] = s0_ref[...]

    C = intra_ref.shape[1]
    for cc in range(q_ref.shape[0] // C):
        rs = slice(cc * C, (cc + 1) * C)
        for hh in range(RET_HEADS):
            qh = q_ref[rs, hh * RET_DK:(hh + 1) * RET_DK]
            kh = k_ref[rs, hh * RET_DK:(hh + 1) * RET_DK]
            vh = v_ref[rs, hh * RET_DV:(hh + 1) * RET_DV]
            s = lax.dot_general(qh, kh, (((1,), (1,)), ((), ())), preferred_element_type=F32)
            o = jnp.dot(s.astype(BF16) * intra_ref[hh], vh, preferred_element_type=F32)
            q2 = jnp.concatenate([qh * qf_ref[hh], qh * qb_ref[hh]], axis=1)
            s2 = jnp.concatenate([st_ref[hh].astype(BF16), sb_ref[cc, hh]], axis=0)
            o = o + jnp.dot(q2, s2, preferred_element_type=F32)
            st_ref[hh] = st_ref[hh] * cd_ref[hh:hh + 1, :] + _tdot(kh * kf_ref[hh], vh)
            mu = jnp.mean(o, axis=-1, keepdims=True)
            d = o - mu
            var = jnp.mean(d * d, axis=-1, keepdims=True)
            g = gs_ref[rs, hh * RET_DV:(hh + 1) * RET_DV]
            gate = g * _sigmoid(g)
            o_ref[rs, hh * RET_DV:(hh + 1) * RET_DV] = (d * lax.rsqrt(var + EPS)).astype(o_ref.dtype) * gate


def _ret_fwd(z, B, sb, s_fwd, intra, qdec_f, qdec_b, kdec_f, cdec):
    S = z.shape[0]
    H, C = RET_HEADS, min(RET_C, S)
    cps = min(RET_CPS, S // C)
    n = S // (C * cps)
    half = N_CHUNKS // 2
    return pl.pallas_call(
        _ret_fwd_kernel,
        grid=(B, n),
        in_specs=[pl.BlockSpec((cps * C, H * RET_DK), lambda b, j: (j, b * N_CHUNKS + 2)),
                  pl.BlockSpec((cps * C, H * RET_DK), lambda b, j: (j, b * N_CHUNKS + 3)),
                  pl.BlockSpec((cps * C, H * RET_DV), lambda b, j: (j, b * half + 2)),
                  pl.BlockSpec((cps * C, H * RET_DV), lambda b, j: (j, b * half + 3)),
                  pl.BlockSpec((None, cps, H, RET_DK, RET_DV), lambda b, j: (b, j, 0, 0, 0)),
                  pl.BlockSpec((None, H, RET_DK, RET_DV), lambda b, j: (b, 0, 0, 0)),
                  pl.BlockSpec((H, C, C), lambda b, j: (0, 0, 0)),
                  pl.BlockSpec((H, C, RET_DK), lambda b, j: (0, 0, 0)),
                  pl.BlockSpec((H, C, RET_DK), lambda b, j: (0, 0, 0)),
                  pl.BlockSpec((H, C, RET_DK), lambda b, j: (0, 0, 0)),
                  pl.BlockSpec((H, RET_DV), lambda b, j: (0, 0))],
        out_specs=pl.BlockSpec((None, cps * C, H * RET_DV), lambda b, j: (b, j, 0)),
        out_shape=jax.ShapeDtypeStruct((B, S, H * RET_DV), BF16),
        scratch_shapes=[pltpu.VMEM((H, RET_DK, RET_DV), F32)],
        compiler_params=_cparams(2),
        name="ret_fwd",
    )(z, z, z, z, sb, s_fwd, intra, qdec_f, qdec_b, kdec_f, cdec)


def _dot_t(a, b):
    return lax.dot_general(a, b, (((1,), (1,)), ((), ())), preferred_element_type=F32)


def _finish_kernel(x_ref, hf_ref, hb_ref, gr_ref, gab_ref, ret_ref, g1_ref, sh2_ref, sc2_ref, n2_ref,
                   wr_ref, wt_ref, wo_ref, rwh_ref, rwl_ref, rb_ref,
                   x1_ref, hl_ref, eid_ref, gate_ref, cnt_ref):
    D = x_ref.shape[1]
    tm = x_ref.shape[0]
    dt = eid_ref.shape[2]
    rnn = hf_ref[...].astype(F32) + hb_ref[...].astype(F32)
    y_rnn = jnp.dot((rnn * jax.nn.gelu(gr_ref[...].astype(F32))).astype(BF16), wr_ref[...],
                    preferred_element_type=F32)
    y_ret = jnp.dot(ret_ref[...], wt_ref[...], preferred_element_type=F32)
    ga = gab_ref[:, :D].astype(F32)
    gb = gab_ref[:, D:].astype(F32)
    merged = _sigmoid(ga) * y_rnn + _sigmoid(gb) * y_ret
    y = jnp.dot(merged.astype(BF16), wo_ref[...], preferred_element_type=F32)
    x1 = x_ref[...] + g1_ref[...] * y
    x1_ref[...] = x1
    hl = _rmsnorm(x1, n2_ref[...]) * (1.0 + sc2_ref[...]) + sh2_ref[...]
    hh = hl.astype(BF16)
    hl_ref[...] = hh

    hlo = (hl - hh.astype(F32)).astype(BF16)
    logits = (_dot_t(rwh_ref[...], hh) + _dot_t(rwh_ref[...], hlo) + _dot_t(rwl_ref[...], hh)) + rb_ref[...]

    ne = logits.shape[0]
    sub = lax.broadcasted_iota(I32, (ne, tm), 0)
    work = logits
    vals, idxs = [], []
    oh = jnp.zeros((ne, tm), F32)
    for _ in range(TOP_K):
        m = jnp.max(work, axis=0, keepdims=True)
        idx = jnp.min(jnp.where(work == m, sub, ne), axis=0, keepdims=True)
        hot = sub == idx
        vals.append(m)
        idxs.append(idx)
        oh = oh + jnp.where(hot, 1.0, 0.0)
        work = jnp.where(hot, -jnp.inf, work)
    es = [jnp.exp(v - vals[0]) for v in vals]
    inv = 1.0 / (es[0] + es[1] + es[2] + es[3])
    for part in range(tm // dt):
        ls = slice(part * dt, (part + 1) * dt)
        for k in range(TOP_K):
            eid_ref[part, k:k + 1, :] = idxs[k][:, ls]
            gate_ref[part, k:k + 1, :] = (es[k] * inv)[:, ls]
        cnt_ref[part] = jnp.sum(oh[:, ls], axis=1, keepdims=True)


def _finish(x3, hf, hb, z, retg, mods3, norm2_g, w_rnn_b, w_ret_b, w_out_b, rwt_hi, rwt_lo, router_b, dt):
    B, S, D = x3.shape
    tm = min(FIN_TM, S)
    nt = S // tm
    N = B * S
    ne = rwt_hi.shape[0]
    half = N_CHUNKS // 2
    per = tm // dt
    const2 = lambda b, i: (0, 0)
    tile3 = lambda b, i: (b * nt + i, 0, 0)
    return pl.pallas_call(
        _finish_kernel,
        grid=(B, nt),
        in_specs=[pl.BlockSpec((None, tm, D), lambda b, i: (b, i, 0)),
                  pl.BlockSpec((None, tm, D), lambda b, i: (b, i, 0)),
                  pl.BlockSpec((None, tm, D), lambda b, i: (b, i, 0)),
                  pl.BlockSpec((tm, D), lambda b, i: (i, b * N_CHUNKS + 1)),
                  pl.BlockSpec((tm, 2 * D), lambda b, i: (i, b * half + 4)),
                  pl.BlockSpec((None, tm, retg.shape[2]), lambda b, i: (b, i, 0)),
                  pl.BlockSpec((None, 1, D), lambda b, i: (b, 0, 2)),
                  pl.BlockSpec((None, 1, D), lambda b, i: (b, 0, 3)),
                  pl.BlockSpec((None, 1, D), lambda b, i: (b, 0, 4)),
                  pl.BlockSpec((1, D), const2),
                  pl.BlockSpec(w_rnn_b.shape, const2),
                  pl.BlockSpec(w_ret_b.shape, const2),
                  pl.BlockSpec(w_out_b.shape, const2),
                  pl.BlockSpec(rwt_hi.shape, const2),
                  pl.BlockSpec(rwt_lo.shape, const2),
                  pl.BlockSpec((ne, 1), const2)],
        out_specs=[pl.BlockSpec((None, tm, D), lambda b, i: (b, i, 0)),
                   pl.BlockSpec((tm, D), lambda b, i: (b * nt + i, 0)),
                   pl.BlockSpec((per, TOP_K, dt), tile3),
                   pl.BlockSpec((per, TOP_K, dt), tile3),
                   pl.BlockSpec((per, ne, 1), tile3)],
        out_shape=[jax.ShapeDtypeStruct((B, S, D), F32),
                   jax.ShapeDtypeStruct((N, D), BF16),
                   jax.ShapeDtypeStruct((N // dt, TOP_K, dt), I32),
                   jax.ShapeDtypeStruct((N // dt, TOP_K, dt), F32),
                   jax.ShapeDtypeStruct((N // dt, ne, 1), F32)],
        compiler_params=_cparams(2),
        name="finish",
    )(x3, hf, hb, z, z, retg, mods3, mods3, mods3, norm2_g.reshape(1, D),
      w_rnn_b, w_ret_b, w_out_b, rwt_hi, rwt_lo, router_b.reshape(ne, 1))


def _local_rows(dt):
    return TOP_K * dt + N_EXPERTS * ROW_ALIGN


def _local_slots(eid_ref):
    dt = eid_ref.shape[1]
    ne = N_EXPERTS
    sub = lax.broadcasted_iota(I32, (ne, dt), 0)
    hots = [sub == eid_ref[k:k + 1, :] for k in range(TOP_K)]
    oh = jnp.zeros((ne, dt), F32)
    for hot in hots:
        oh = oh + jnp.where(hot, 1.0, 0.0)
    earlier = jnp.where(lax.broadcasted_iota(I32, (dt, dt), 0) < lax.broadcasted_iota(I32, (dt, dt), 1), 1.0, 0.0)
    before = jnp.dot(oh.astype(BF16), earlier.astype(BF16), preferred_element_type=F32)
    cnt = jnp.broadcast_to(jnp.sum(oh, axis=1, keepdims=True), (ne, dt))
    cnt = jnp.maximum(jnp.ceil(cnt * (1.0 / ROW_ALIGN)) * ROW_ALIGN, float(ROW_ALIGN))
    lower = jnp.where(lax.broadcasted_iota(I32, (ne, ne), 1) < lax.broadcasted_iota(I32, (ne, ne), 0), 1.0, 0.0)
    base = before + jnp.dot(lower.astype(BF16), cnt.astype(BF16), preferred_element_type=F32)
    return [jnp.sum(jnp.where(hot, base, 0.0), axis=0, keepdims=True).astype(I32) for hot in hots]


def _slot_matrix(slots, weights, out_ref):
    rows, dt = out_ref.shape
    ch = 64
    rel = lax.broadcasted_iota(I32, (ch, dt), 0).astype(F32).astype(BF16)
    slots_f = [s.astype(F32) for s in slots]
    weights_b = [jnp.asarray(w, F32).astype(BF16) for w in weights]
    zero = jnp.zeros((), BF16)
    for c in range(rows // ch):
        acc = None
        for s, w in zip(slots_f, weights_b):
            term = jnp.where(rel == (s - float(c * ch)).astype(BF16), w, zero)
            acc = term if acc is None else acc + term
        out_ref[c * ch:(c + 1) * ch, :] = acc


def _for_each_run(meta_ref, fn, maybe_empty=False):
    def one(e):
        n = pl.multiple_of(meta_ref[0, N_EXPERTS + e], ROW_ALIGN)
        args = (pl.multiple_of(meta_ref[0, e], ROW_ALIGN), pl.multiple_of(meta_ref[0, 2 * N_EXPERTS + e], ROW_ALIGN), n)
        if maybe_empty:
            pl.when(n > 0)(lambda: fn(*args))
        else:
            fn(*args)

    if maybe_empty:
        lax.fori_loop(0, N_EXPERTS, lambda e, c: (one(e), c)[1], 0)
    else:
        for e in range(N_EXPERTS):
            one(e)


def _run_rows(meta_ref):
    last = N_EXPERTS - 1
    return pl.multiple_of(meta_ref[0, last] + meta_ref[0, N_EXPERTS + last], ROW_ALIGN)


def _dispatch_kernel(meta_ref, zmeta_ref, eid_ref, x_ref, hp_ref, slot_ref, sbuf_ref, zbuf_ref, pm_ref, rows_ref,
                     sems, zsem):
    t = pl.program_id(0)
    slot = t % 2

    def zero_copy(_, g, n):
        return pltpu.make_async_copy(zbuf_ref.at[pl.ds(0, n), :], hp_ref.at[pl.ds(g, n), :], zsem)

    def run_copy(sl, l, g, n):
        return pltpu.make_async_copy(sbuf_ref.at[sl, pl.ds(l, n), :], hp_ref.at[pl.ds(g, n), :], sems.at[sl])

    @pl.when(t == 0)
    def _():
        zbuf_ref[...] = jnp.zeros_like(zbuf_ref)
        _for_each_run(zmeta_ref, lambda l, g, n: zero_copy(l, g, n).start(), maybe_empty=True)
        _for_each_run(zmeta_ref, lambda l, g, n: zero_copy(l, g, n).wait(), maybe_empty=True)

    slots = _local_slots(eid_ref)
    for k in range(TOP_K):
        slot_ref[k:k + 1, :] = slots[k]
    _slot_matrix(slots, [1.0] * TOP_K, pm_ref)
    sbuf_ref[slot] = _pack_cols(jnp.dot(pm_ref[...], x_ref[...], preferred_element_type=F32))

    @pl.when(t > 0)
    def _():
        run_copy(1 - slot, 0, 0, pl.multiple_of(rows_ref[1 - slot], ROW_ALIGN)).wait()

    _for_each_run(meta_ref, lambda l, g, n: run_copy(slot, l, g, n).start())
    rows_ref[slot] = _run_rows(meta_ref)

    @pl.when(t == pl.num_programs(0) - 1)
    def _():
        run_copy(slot, 0, 0, _run_rows(meta_ref)).wait()


def _dispatch(hl2, eid3, meta3, zmeta, P):
    N, D = hl2.shape
    nt, _, dt = eid3.shape
    nm = meta3.shape[2]
    return pl.pallas_call(
        _dispatch_kernel,
        grid=(nt,),
        in_specs=[pl.BlockSpec((None, 1, nm), lambda i: (i, 0, 0), memory_space=pltpu.SMEM),
                  pl.BlockSpec((1, nm), lambda i: (0, 0), memory_space=pltpu.SMEM),
                  pl.BlockSpec((None, TOP_K, dt), lambda i: (i, 0, 0)),
                  pl.BlockSpec((dt, D), lambda i: (i, 0))],
        out_specs=[pl.BlockSpec(memory_space=pl.ANY),
                   pl.BlockSpec((None, TOP_K, dt), lambda i: (i, 0, 0))],
        out_shape=[jax.ShapeDtypeStruct((P, D // 2), U32),
                   jax.ShapeDtypeStruct((nt, TOP_K, dt), I32)],
        scratch_shapes=[pltpu.VMEM((2, _local_rows(dt), D // 2), U32),
                        pltpu.VMEM((MOE_BLOCK, D // 2), U32),
                        pltpu.VMEM((_local_rows(dt), dt), BF16),
                        pltpu.SMEM((2,), I32),
                        pltpu.SemaphoreType.DMA((2,)),
                        pltpu.SemaphoreType.DMA(())],
        compiler_params=_cparams(1),
        name="dispatch",
    )(meta3, zmeta, eid3, hl2)


def _regroup_kernel(w_ref, sel_ref, o_ref):
    half = w_ref.shape[1] // 2
    g = sel_ref.shape[0]
    wb = w_ref[...].astype(BF16)
    for j in range(w_ref.shape[1] // g):
        r = jnp.dot(wb[:, g * j:g * (j + 1)], sel_ref[...], preferred_element_type=F32)
        o_ref[:, (g // 2) * j:(g // 2) * (j + 1)] = r[:, :g // 2].astype(o_ref.dtype)
        o_ref[:, half + (g // 2) * j:half + (g // 2) * (j + 1)] = r[:, g // 2:].astype(o_ref.dtype)


def _regroup_glu_lin(w1):
    ne, D, de2 = w1.shape
    g = 256
    sel = np.zeros((g, g), np.float32)
    sel[np.arange(0, g, 2), np.arange(g // 2)] = 1.0
    sel[np.arange(1, g, 2), g // 2 + np.arange(g // 2)] = 1.0
    out = pl.pallas_call(
        _regroup_kernel,
        grid=(ne,),
        in_specs=[pl.BlockSpec((D, de2), lambda e: (e, 0)),
                  pl.BlockSpec((g, g), lambda e: (0, 0))],
        out_specs=pl.BlockSpec((D, de2), lambda e: (e, 0)),
        out_shape=jax.ShapeDtypeStruct((ne * D, de2), BF16),
        compiler_params=_cparams(1),
        name="regroup",
    )(w1.reshape(ne * D, de2), jnp.asarray(sel, BF16))
    return out.reshape(ne, D, de2)


def _expert_kernel(be_ref, nu_ref, valid_ref, x_ref, w1_ref, b1_ref, w2_ref, b2_ref, o_ref):
    del be_ref, nu_ref
    de = w2_ref.shape[0]

    def rows(rs):
        h = jnp.dot(_unpack_cols(x_ref[rs, :]), w1_ref[...], preferred_element_type=F32) + b1_ref[...]
        glu = jnp.minimum(h[:, :de], SWIGLU_LIMIT)
        lin = jnp.clip(h[:, de:], -SWIGLU_LIMIT, SWIGLU_LIMIT)
        act = glu * _sigmoid(SWIGLU_ALPHA * glu) * (lin + 1.0)
        y = jnp.dot(act.astype(BF16), w2_ref[...].astype(BF16), preferred_element_type=F32) + b2_ref[...]
        o_ref[rs, :] = _pack_cols(y.astype(BF16).astype(F32))

    valid = valid_ref[pl.program_id(0)]

    @pl.when(valid == MOE_BLOCK)
    def _():
        rows(slice(0, MOE_BLOCK))

    @pl.when(valid < MOE_BLOCK)
    def _():
        for q in range(MOE_BLOCK // MOE_SUB):
            pl.when(valid > q * MOE_SUB)(lambda q=q: rows(slice(q * MOE_SUB, (q + 1) * MOE_SUB)))


def _experts(blk_expert, n_used, valid, h_pad, w1b, b1p, w2b, b2):
    P = h_pad.shape[0]
    ne, D, de2 = w1b.shape
    de = de2 // 2
    nblk = P // MOE_BLOCK
    blk = lambda j, be, nu, vl: (jnp.minimum(j, nu[0] - 1), 0)
    wsel = lambda j, be, nu, vl: (be[jnp.minimum(j, nu[0] - 1)], 0, 0)
    return pl.pallas_call(
        _expert_kernel,
        grid_spec=pltpu.PrefetchScalarGridSpec(
            num_scalar_prefetch=3,
            grid=(nblk,),
            in_specs=[pl.BlockSpec((MOE_BLOCK, D // 2), blk),
                      pl.BlockSpec((None, D, de2), wsel),
                      pl.BlockSpec((None, 1, de2), wsel),
                      pl.BlockSpec((None, de, D), wsel),
                      pl.BlockSpec((None, 1, D), wsel)],
            out_specs=pl.BlockSpec((MOE_BLOCK, D // 2), blk)),
        out_shape=jax.ShapeDtypeStruct((P, D // 2), U32),
        compiler_params=_cparams(1),
        name="experts",
    )(blk_expert, n_used, valid, h_pad, w1b, b1p.reshape(ne, 1, de2), w2b, b2.reshape(ne, 1, D))


def _combine_kernel(meta_ref, nmeta_ref, slot_ref, gate_ref, x1_ref, g2_ref, fg_ref, yp_ref, o_ref,
                    ybuf_ref, gm_ref, sems):
    t = pl.program_id(0) * pl.num_programs(1) + pl.program_id(1)
    n_tiles = pl.num_programs(0) * pl.num_programs(1)
    slot = t % 2

    def run_copy(sl, l, g, n):
        return pltpu.make_async_copy(yp_ref.at[pl.ds(g, n), :], ybuf_ref.at[sl, pl.ds(l, n), :], sems.at[sl])

    @pl.when(t == 0)
    def _():
        ybuf_ref[...] = jnp.zeros_like(ybuf_ref)
        _for_each_run(meta_ref, lambda l, g, n: run_copy(0, l, g, n).start())

    @pl.when(t + 1 < n_tiles)
    def _():
        _for_each_run(nmeta_ref, lambda l, g, n: run_copy(1 - slot, l, g, n).start())

    run_copy(slot, 0, 0, _run_rows(meta_ref)).wait()

    _slot_matrix([slot_ref[k:k + 1, :] for k in range(TOP_K)], [gate_ref[k:k + 1, :] for k in range(TOP_K)], gm_ref)
    y = _tdot(gm_ref[...], _unpack_cols(ybuf_ref[slot]))
    x2 = x1_ref[...] + g2_ref[...] * y
    o_ref[...] = _rmsnorm(x2, fg_ref[...])


def _combine(meta3, slot3, gate3, x1, mods3, final_g, y_pad):
    B, S, D = x1.shape
    n_tiles, _, dt = slot3.shape
    nt = S // dt
    nm = meta3.shape[2]
    tile3 = lambda b, i: (b * nt + i, 0, 0)
    next3 = lambda b, i: (jnp.minimum(b * nt + i + 1, n_tiles - 1), 0, 0)
    return pl.pallas_call(
        _combine_kernel,
        grid=(B, nt),
        in_specs=[pl.BlockSpec((None, 1, nm), tile3, memory_space=pltpu.SMEM),
                  pl.BlockSpec((None, 1, nm), next3, memory_space=pltpu.SMEM),
                  pl.BlockSpec((None, TOP_K, dt), tile3),
                  pl.BlockSpec((None, TOP_K, dt), tile3),
                  pl.BlockSpec((None, dt, D), lambda b, i: (b, i, 0)),
                  pl.BlockSpec((None, 1, D), lambda b, i: (b, 0, 5)),
                  pl.BlockSpec((1, D), lambda b, i: (0, 0)),
                  pl.BlockSpec(memory_space=pl.ANY)],
        out_specs=pl.BlockSpec((None, dt, D), lambda b, i: (b, i, 0)),
        out_shape=jax.ShapeDtypeStruct((B, S, D), F32),
        scratch_shapes=[pltpu.VMEM((2, _local_rows(dt), D // 2), U32), pltpu.VMEM((_local_rows(dt), dt), BF16),
                        pltpu.SemaphoreType.DMA((2,))],
        compiler_params=_cparams(2),
        name="combine",
    )(meta3, meta3, slot3, gate3, x1, mods3, final_g.reshape(1, D), y_pad)


def _rope_tables(S, k_scale):
    n_freq = RET_DK // 4
    pos = np.arange(S)
    rows = (pos // GRID_W).astype(np.float32)
    cols = (pos % GRID_W).astype(np.float32)
    inv = (np.float32(ROPE_BASE) ** (-np.arange(n_freq, dtype=np.float32) / np.float32(n_freq))).astype(np.float32)
    ang = np.concatenate([rows[:, None] * inv, cols[:, None] * inv], axis=-1).astype(np.float32)
    cos, sin = np.cos(ang), np.sin(ang)
    cos2 = np.concatenate([cos, cos], axis=-1)
    sin2 = np.concatenate([-sin, sin], axis=-1)
    ks = np.float32(k_scale)
    return jnp.asarray(np.concatenate([cos2, sin2, cos2 * ks, sin2 * ks], axis=-1), F32)


def _identity_tables(L, k_scale):
    one = np.ones((L, RET_DK), np.float32)
    zero = np.zeros((L, RET_DK), np.float32)
    return jnp.asarray(np.concatenate([one, zero, one * np.float32(k_scale), zero], axis=-1), F32)


def _lanes(t, width):
    return np.ascontiguousarray(np.broadcast_to(t[:, :, None], t.shape + (width,)))


def kernel(x, c, ctx, c_ctx, ada_w, ada_b, norm1_g, w_in, conv_w, conv_b, lru_wa, lru_ba, lru_wx, lru_bx,
           lru_lambda, w_rnn_proj, w_ret_proj, w_out, norm2_g, router_w, router_b, moe_w1, moe_b1, moe_w2,
           moe_b2, final_g):
    B, S, D = x.shape
    L = ctx.shape[1]
    N = B * S
    H = RET_HEADS
    lyr = 0
    d_in = w_in.shape[2]
    assert ada_w.shape[0] == 1 and d_in == N_CHUNKS * D and B == 8

    def pairs_apart(w):
        return jnp.swapaxes(w.reshape(D, H, RET_DK // 2, 2), 2, 3).reshape(D, H * RET_DK)

    w_in_b = w_in[lyr].astype(BF16)
    w_qk = jnp.concatenate([pairs_apart(w_in_b[:, 2 * D:3 * D]), pairs_apart(w_in_b[:, 3 * D:4 * D])], axis=1)
    wg = [(0.5 * jnp.concatenate([lru_wa[lyr, d], lru_wx[lyr, d]], axis=-1)).astype(BF16) for d in range(2)]
    de2 = moe_w1.shape[3]
    glu_lin = np.concatenate([np.arange(0, de2, 2), np.arange(1, de2, 2)])
    w1b = _regroup_glu_lin(moe_w1[lyr])
    b1p = moe_b1[lyr][:, glu_lin]
    rwt = router_w[lyr].T
    rwt_hi = rwt.astype(BF16)
    rwt_lo = (rwt - rwt_hi.astype(F32)).astype(BF16)

    k_scale = RET_DK ** -0.5
    tab_l = _rope_tables(S, k_scale)
    tab_c = _identity_tables(B * L, k_scale)
    f4 = np.float32
    log_g = np.log1p(-np.exp2(-5.0 - np.arange(H, dtype=f4))).astype(f4)
    C = min(RET_C, S)
    idx = np.arange(C, dtype=f4)
    dec = lambda e: np.exp(e[None, :].astype(f4) * log_g[:, None]).astype(f4)
    intra = jnp.asarray(np.exp(np.abs(idx[:, None] - idx[None, :])[None] * log_g[:, None, None]).astype(f4))
    qdec_f = jnp.asarray(_lanes(dec(idx + 1.0), RET_DK))
    qdec_b = jnp.asarray(_lanes(dec(C - idx), RET_DK))
    kdec_f = jnp.asarray(_lanes(dec(C - 1.0 - idx), RET_DK))
    kdec_b = jnp.asarray(_lanes(dec(idx), RET_DK))
    cdec = jnp.asarray(np.broadcast_to(np.exp(C * log_g).astype(f4)[:, None], (H, RET_DV)))
    pos_c = np.arange(L, dtype=f4)
    cdec_f = jnp.asarray(_lanes(dec(L - 1.0 - pos_c), RET_DK))
    cdec_b = jnp.asarray(_lanes(dec(pos_c), RET_DK))

    cvec = jnp.zeros((16, D), F32).at[:B].set(c).at[B].set(c_ctx)
    mods3 = _ada(cvec, ada_w[lyr], ada_b[lyr]).reshape(16, 1, 6 * D)

    z_c, xr_c = _proj(ctx.reshape(1, B * L, D), mods3, lambda b: B, norm1_g[lyr], w_in_b, w_qk, tab_c,
                      min(PROJ_TM, B * L))
    xr_c = xr_c.reshape(B, L, D)
    z_l, xr_l = _proj(x, mods3, lambda b: b, norm1_g[lyr], w_in_b, w_qk, tab_l, min(PROJ_TM, S))

    zeros = jnp.zeros((B, D), F32)
    hs = []
    for d in range(2):
        args = (conv_w[lyr], conv_b[lyr], wg[d], 0.5 * lru_ba[lyr, d], 0.5 * lru_bx[lyr, d], lru_lambda[lyr, d])
        _, h0 = _rnn(xr_c, *args, zeros, reverse=(d == 1), tt=RNN_TT)
        h, _ = _rnn(xr_l, *args, h0, reverse=(d == 1), tt=RNN_TT)
        hs.append(h)

    s_fwd, s_bwd = _ret_ctx(z_c, B, cdec_f, cdec_b)
    sb = _ret_bwd(z_l, B, s_bwd, kdec_b, cdec)
    retg = _ret_fwd(z_l, B, sb, s_fwd, intra.astype(BF16), qdec_f.astype(BF16), qdec_b.astype(BF16),
                    kdec_f.astype(BF16), cdec)

    dt = min(DISP_TM, S)
    x1, hl2, eid3, gate3, cnt3 = _finish(
        x, hs[0], hs[1], z_l, retg, mods3, norm2_g[lyr], w_rnn_proj[lyr].astype(BF16),
        w_ret_proj[lyr].astype(BF16), w_out[lyr].astype(BF16), rwt_hi, rwt_lo, router_b[lyr], dt)

    cnt_t = cnt3[:, :, 0].astype(I32)
    cnt_t = jnp.maximum((cnt_t + ROW_ALIGN - 1) // ROW_ALIGN * ROW_ALIGN, ROW_ALIGN)
    cnt = jnp.sum(cnt_t, axis=0)
    padded = (cnt + MOE_BLOCK - 1) // MOE_BLOCK * MOE_BLOCK
    pad_end = jnp.cumsum(padded)
    pad_start = pad_end - padded
    gstart = pad_start[None, :] + jnp.cumsum(cnt_t, axis=0) - cnt_t
    loff = jnp.cumsum(cnt_t, axis=1) - cnt_t
    meta3 = jnp.concatenate([loff, cnt_t, gstart], axis=1).reshape(N // dt, 1, 3 * N_EXPERTS)
    zmeta = jnp.concatenate([jnp.zeros_like(cnt), padded - cnt, pad_start + cnt]).reshape(1, 3 * N_EXPERTS)
    n_blocks = -(-(N * TOP_K + (N // dt) * N_EXPERTS * ROW_ALIGN) // MOE_BLOCK) + N_EXPERTS
    P = n_blocks * MOE_BLOCK
    blk_start = jnp.arange(n_blocks, dtype=I32) * MOE_BLOCK
    blk_expert = jnp.minimum(jnp.sum((pad_end[None, :] <= blk_start[:, None]).astype(I32), axis=1), N_EXPERTS - 1)
    n_used = (pad_end[-1:] // MOE_BLOCK).astype(I32)
    blk_valid = jnp.clip((pad_start + cnt)[blk_expert] - blk_start, 0, MOE_BLOCK)
    blk_valid = jnp.where(jnp.arange(n_blocks) < n_used[0], blk_valid, 0).astype(I32)

    h_pad, slot3 = _dispatch(hl2, eid3, meta3, zmeta, P)
    y_pad = _experts(blk_expert, n_used, blk_valid, h_pad, w1b, b1p, moe_w2[lyr], moe_b2[lyr])
    return _combine(meta3, slot3, gate3, x1, mods3, final_g, y_pad)
```

```python
import functools

import jax
import jax.numpy as jnp
import numpy as np
from jax import lax
from jax.experimental import pallas as pl
from jax.experimental.pallas import tpu as pltpu

F32 = jnp.float32
BF16 = jnp.bfloat16
I32 = jnp.int32

GRID_W = 64
RNN_BLOCKS = 8
CONV_W = 4
LRU_C = 8.0
RET_HEADS = 8
RET_DK = 128
RET_DV = 256
ROPE_BASE = 10000.0
N_EXPERTS = 32
TOP_K = 4
SWIGLU_ALPHA = 1.702
SWIGLU_LIMIT = 7.0
EPS = 1e-6
N_CHUNKS = 10

PROJ_TM = 512
PROJ_TN = 2048
RNN_TT = 256
RET_C = 256
RET_CPS = 4
FIN_TM = 512
MOE_BLOCK = 1024
MOE_SUB = 256
DISP_TM = 512
ROW_ALIGN = 8
VMEM_LIMIT = 56 * 1024 * 1024


def _cparams(n_axes):
    return pltpu.CompilerParams(dimension_semantics=("arbitrary",) * n_axes,
                                vmem_limit_bytes=VMEM_LIMIT)


def _sigmoid(x):
    return 0.5 * (jnp.tanh(0.5 * x) + 1.0)


def _rmsnorm(x, g):
    return x * lax.rsqrt(jnp.mean(x * x, axis=-1, keepdims=True) + EPS) * g


U32 = jnp.uint32


def _pack_cols(v):
    w = v.shape[1] // 2
    lo = lax.shift_right_logical(lax.bitcast_convert_type(v[:, :w], U32), jnp.uint32(16))
    hi = lax.bitcast_convert_type(v[:, w:], U32) & jnp.uint32(0xFFFF0000)
    return lo | hi


def _unpack_cols(u):
    a = lax.bitcast_convert_type(lax.shift_left(u, jnp.uint32(16)), F32).astype(BF16)
    b = lax.bitcast_convert_type(u & jnp.uint32(0xFFFF0000), F32).astype(BF16)
    return jnp.concatenate([a, b], axis=1)


def _tdot(a, b):
    return lax.dot_general(a, b, (((0,), (0,)), ((), ())), preferred_element_type=F32)


def _ada_kernel(c_ref, w_ref, b_ref, o_ref):
    c = c_ref[...]
    s = c * _sigmoid(c)
    o_ref[...] = jnp.dot(s, w_ref[...], preferred_element_type=F32,
                         precision=lax.Precision.HIGHEST) + b_ref[...]


def _ada(cvec, ada_w, ada_b):
    R, D = cvec.shape
    n = ada_w.shape[1] // D
    return pl.pallas_call(
        _ada_kernel,
        grid=(n,),
        in_specs=[pl.BlockSpec((R, D), lambda j: (0, 0)),
                  pl.BlockSpec((D, D), lambda j: (0, j)),
                  pl.BlockSpec((1, D), lambda j: (0, j))],
        out_specs=pl.BlockSpec((R, D), lambda j: (0, j)),
        out_shape=jax.ShapeDtypeStruct((R, n * D), F32),
        compiler_params=_cparams(1),
        name="ada",
    )(cvec, ada_w, ada_b.reshape(1, -1))


def _proj_kernel(x_ref, sh_ref, sc_ref, g_ref, w_ref, wqk_ref, tab_ref, o_ref, xr_ref):
    D = x_ref.shape[1]
    h = _rmsnorm(x_ref[...], g_ref[...])
    hb = (h * (1.0 + sc_ref[...]) + sh_ref[...]).astype(BF16)
    for j in range(w_ref.shape[1] // PROJ_TN):
        j0 = j * PROJ_TN
        wj = wqk_ref[...] if j == 1 else w_ref[:, j0:j0 + PROJ_TN]
        acc = jnp.dot(hb, wj, preferred_element_type=F32)
        if j == 0:
            xr_ref[...] = acc[:, :D].astype(xr_ref.dtype)
        if j != 1:
            o_ref[:, j0:j0 + PROJ_TN] = acc.astype(o_ref.dtype)
            continue
        for part in range(2):
            cos = tab_ref[:, (2 * part) * RET_DK:(2 * part + 1) * RET_DK]
            sin = tab_ref[:, (2 * part + 1) * RET_DK:(2 * part + 2) * RET_DK]
            for hh in range(RET_HEADS):
                c0 = part * RET_HEADS * RET_DK + hh * RET_DK
                t = acc[:, c0:c0 + RET_DK]
                o_ref[:, j0 + c0:j0 + c0 + RET_DK] = (
                    t * cos + pltpu.roll(t, RET_DK // 2, 1) * sin).astype(o_ref.dtype)


def _proj(x3, mods3, mod_row, norm_g, w_in_b, w_qk, tab, tm):
    B, S, D = x3.shape
    d_in = w_in_b.shape[1]
    return pl.pallas_call(
        _proj_kernel,
        grid=(B, S // tm),
        in_specs=[pl.BlockSpec((None, tm, D), lambda b, i: (b, i, 0)),
                  pl.BlockSpec((None, 1, D), lambda b, i: (mod_row(b), 0, 0)),
                  pl.BlockSpec((None, 1, D), lambda b, i: (mod_row(b), 0, 1)),
                  pl.BlockSpec((1, D), lambda b, i: (0, 0)),
                  pl.BlockSpec((D, d_in), lambda b, i: (0, 0), pipeline_mode=pl.Buffered(1)),
                  pl.BlockSpec((D, PROJ_TN), lambda b, i: (0, 0), pipeline_mode=pl.Buffered(1)),
                  pl.BlockSpec((tm, 4 * RET_DK), lambda b, i: (i, 0))],
        out_specs=[pl.BlockSpec((tm, d_in), lambda b, i: (i, b)),
                   pl.BlockSpec((None, tm, D), lambda b, i: (b, i, 0))],
        out_shape=[jax.ShapeDtypeStruct((S, B * d_in), BF16),
                   jax.ShapeDtypeStruct((B, S, D), BF16)],
        compiler_params=_cparams(2),
        name="proj",
    )(x3, mods3, mods3, norm_g.reshape(1, D), w_in_b, w_qk, tab)


def _rnn_kernel(xm_ref, xp_ref, xn_ref, cw_ref, cb_ref, wg_ref, ba_ref, bx_ref, lam_ref, h0_ref,
                h_ref, hfin_ref, xs_ref, a_ref, u_ref, hc_ref, *, reverse, n_tiles):
    i = pl.program_id(0)
    tile = (n_tiles - 1 - i) if reverse else i
    nb, tt, D = xm_ref.shape
    bw = D // RNN_BLOCKS
    HALO = xp_ref.shape[1]

    @pl.when(i == 0)
    def _():
        hc_ref[...] = h0_ref[...]

    GT = HALO
    gr = GT * nb
    ri = lax.broadcasted_iota(I32, (gr, gr), 0)
    ci = lax.broadcasted_iota(I32, (gr, gr), 1)
    perm = jnp.where((ri // nb == ci % GT) & (ri % nb == ci // GT), 1.0, 0.0).astype(BF16)

    def time_major(ref, t0):
        xg = jnp.concatenate([ref[b, t0:t0 + GT, :] for b in range(nb)], axis=0)
        return jnp.dot(perm, xg, preferred_element_type=F32)

    hr = HALO * nb
    R = tt * nb
    xs_ref[0:hr, :] = jnp.where(tile > 0, time_major(xp_ref, 0), 0.0)
    for g in range(tt // GT):
        xs_ref[hr + g * gr:hr + (g + 1) * gr, :] = time_major(xm_ref, g * GT)
    xs_ref[hr + R:2 * hr + R, :] = jnp.where(tile < n_tiles - 1, time_major(xn_ref, 0), 0.0)

    nl = -lam_ref[...]
    csp = (0.25 * LRU_C) * (jnp.maximum(nl, 0.0) + jnp.log1p(jnp.exp(-jnp.abs(nl))))
    SUB = 256

    def gates(s, carry):
        r0 = pl.multiple_of(s * SUB, SUB)
        base = hr - 2 * nb
        xc = cb_ref[...] + cw_ref[0:1, :] * xs_ref[pl.ds(pl.multiple_of(r0 + base, nb), SUB), :]
        for k in range(1, CONV_W):
            xc = xc + cw_ref[k:k + 1, :] * xs_ref[pl.ds(pl.multiple_of(r0 + base + k * nb, nb), SUB), :]
        xb = xc.astype(BF16)
        for n in range(RNN_BLOCKS):
            g = jnp.dot(xb[:, n * bw:(n + 1) * bw], wg_ref[n], preferred_element_type=F32)
            cs = slice(n * bw, (n + 1) * bw)
            tr = jnp.tanh(g[:, :bw] + ba_ref[:, cs])
            ti = jnp.tanh(g[:, bw:] + bx_ref[:, cs])
            t = jnp.tanh(csp[:, cs] * tr + csp[:, cs])
            rc = 1.0 / (1.0 + t)
            a_ref[pl.ds(r0, SUB), cs] = (1.0 - t) * rc
            root = t * lax.rsqrt(jnp.maximum(t, 1e-30))
            u_ref[pl.ds(r0, SUB), cs] = (root * rc) * ((ti + 1.0) * xc[:, cs])
        return carry

    lax.fori_loop(0, R // SUB, gates, 0)

    def step(t, h):
        ts = (tt - 1 - t) if reverse else t
        r0 = pl.multiple_of(ts * nb, nb)
        h = a_ref[pl.ds(r0, nb), :] * h + u_ref[pl.ds(r0, nb), :]
        u_ref[pl.ds(r0, nb), :] = h
        return h

    h = lax.fori_loop(0, tt, step, hc_ref[...], unroll=8)
    hc_ref[...] = h
    hfin_ref[...] = h
    for g in range(tt // GT):
        hg = _tdot(perm, u_ref[g * gr:(g + 1) * gr, :].astype(BF16))
        for b in range(nb):
            h_ref[b, g * GT:(g + 1) * GT, :] = hg[b * GT:(b + 1) * GT, :].astype(h_ref.dtype)


def _rnn(xr, conv_w, conv_b, wg, ba, bx, lam, h0, *, reverse, tt):
    nb, T, D = xr.shape
    tt = min(tt, T)
    n_tiles = T // tt
    HALO = 16
    hb = tt // HALO

    def tile_of(i):
        return (n_tiles - 1 - i) if reverse else i

    kern = functools.partial(_rnn_kernel, reverse=reverse, n_tiles=n_tiles)
    return pl.pallas_call(
        kern,
        grid=(n_tiles,),
        in_specs=[pl.BlockSpec((nb, tt, D), lambda i: (0, tile_of(i), 0)),
                  pl.BlockSpec((nb, HALO, D), lambda i: (0, jnp.maximum(tile_of(i) * hb - 1, 0), 0)),
                  pl.BlockSpec((nb, HALO, D), lambda i: (0, jnp.minimum((tile_of(i) + 1) * hb, n_tiles * hb - 1), 0)),
                  pl.BlockSpec((CONV_W, D), lambda i: (0, 0)),
                  pl.BlockSpec((1, D), lambda i: (0, 0)),
                  pl.BlockSpec((RNN_BLOCKS, D // RNN_BLOCKS, 2 * D // RNN_BLOCKS), lambda i: (0, 0, 0)),
                  pl.BlockSpec((1, D), lambda i: (0, 0)),
                  pl.BlockSpec((1, D), lambda i: (0, 0)),
                  pl.BlockSpec((1, D), lambda i: (0, 0)),
                  pl.BlockSpec((nb, D), lambda i: (0, 0))],
        out_specs=[pl.BlockSpec((nb, tt, D), lambda i: (0, tile_of(i), 0)),
                   pl.BlockSpec((nb, D), lambda i: (0, 0))],
        out_shape=[jax.ShapeDtypeStruct((nb, T, D), BF16),
                   jax.ShapeDtypeStruct((nb, D), F32)],
        scratch_shapes=[pltpu.VMEM(((tt + 2 * HALO) * nb, D), F32),
                        pltpu.VMEM((nb * tt, D), F32),
                        pltpu.VMEM((nb * tt, D), F32),
                        pltpu.VMEM((nb, D), F32)],
        compiler_params=_cparams(1),
        name="rnn_bwd" if reverse else "rnn_fwd",
    )(xr, xr, xr, conv_w, conv_b.reshape(1, D), wg, ba.reshape(1, D), bx.reshape(1, D),
      lam.reshape(1, D), h0)


def _ret_ctx_kernel(k_ref, v_ref, df_ref, db_ref, sf_ref, sb_ref):
    for hh in range(RET_HEADS):
        kh = k_ref[:, hh * RET_DK:(hh + 1) * RET_DK].astype(F32)
        vh = v_ref[:, hh * RET_DV:(hh + 1) * RET_DV]
        sf_ref[hh] = _tdot((kh * df_ref[hh]).astype(BF16), vh)
        sb_ref[hh] = _tdot((kh * db_ref[hh]).astype(BF16), vh)


def _ret_ctx(zc, B, dec_f, dec_b):
    L = zc.shape[0] // B
    H = RET_HEADS
    st = jax.ShapeDtypeStruct((B, H, RET_DK, RET_DV), F32)
    return pl.pallas_call(
        _ret_ctx_kernel,
        grid=(B,),
        in_specs=[pl.BlockSpec((L, H * RET_DK), lambda b: (b, 3)),
                  pl.BlockSpec((L, H * RET_DV), lambda b: (b, 2)),
                  pl.BlockSpec((H, L, RET_DK), lambda b: (0, 0, 0)),
                  pl.BlockSpec((H, L, RET_DK), lambda b: (0, 0, 0))],
        out_specs=[pl.BlockSpec((None, H, RET_DK, RET_DV), lambda b: (b, 0, 0, 0))] * 2,
        out_shape=[st, st],
        compiler_params=_cparams(1),
        name="ret_ctx",
    )(zc, zc, dec_f, dec_b)


def _ret_bwd_kernel(k_ref, v_ref, s0_ref, kd_ref, cd_ref, o_ref, st_ref):
    @pl.when(pl.program_id(1) == 0)
    def _():
        st_ref[...] = s0_ref[...]

    C = kd_ref.shape[1]
    for cc in reversed(range(k_ref.shape[0] // C)):
        rs = slice(cc * C, (cc + 1) * C)
        for hh in range(RET_HEADS):
            o_ref[cc, hh] = st_ref[hh].astype(o_ref.dtype)
            kh = k_ref[rs, hh * RET_DK:(hh + 1) * RET_DK].astype(F32)
            vh = v_ref[rs, hh * RET_DV:(hh + 1) * RET_DV]
            st_ref[hh] = st_ref[hh] * cd_ref[hh:hh + 1, :] + _tdot((kh * kd_ref[hh]).astype(BF16), vh)


def _ret_bwd(z, B, s_bwd, kdec_b, cdec):
    S = z.shape[0]
    H, C = RET_HEADS, min(RET_C, S)
    cps = min(RET_CPS, S // C)
    n = S // (C * cps)
    return pl.pallas_call(
        _ret_bwd_kernel,
        grid=(B, n),
        in_specs=[pl.BlockSpec((cps * C, H * RET_DK), lambda b, j: (n - 1 - j, b * N_CHUNKS + 3)),
                  pl.BlockSpec((cps * C, H * RET_DV), lambda b, j: (n - 1 - j, b * (N_CHUNKS // 2) + 2)),
                  pl.BlockSpec((None, H, RET_DK, RET_DV), lambda b, j: (b, 0, 0, 0)),
                  pl.BlockSpec((H, C, RET_DK), lambda b, j: (0, 0, 0)),
                  pl.BlockSpec((H, RET_DV), lambda b, j: (0, 0))],
        out_specs=pl.BlockSpec((None, cps, H, RET_DK, RET_DV), lambda b, j: (b, n - 1 - j, 0, 0, 0)),
        out_shape=jax.ShapeDtypeStruct((B, n * cps, H, RET_DK, RET_DV), BF16),
        scratch_shapes=[pltpu.VMEM((H, RET_DK, RET_DV), F32)],
        compiler_params=_cparams(2),
        name="ret_bwd",
    )(z, z, s_bwd, kdec_b, cdec)


def _ret_fwd_kernel(q_ref, k_ref, v_ref, gs_ref, sb_ref, s0_ref, intra_ref, qf_ref, qb_ref, kf_ref, cd_ref,
                    o_ref, st_ref):
    @pl.when(pl.program_id(1) == 0)
    def _():
        st_ref[...] = s0_ref[...]

    C = intra_ref.shape[1]
    for cc in range(q_ref.shape[0] // C):
        rs = slice(cc * C, (cc + 1) * C)
        for hh in range(RET_HEADS):
            qh = q_ref[rs, hh * RET_DK:(hh + 1) * RET_DK]
            kh = k_ref[rs, hh * RET_DK:(hh + 1) * RET_DK]
            vh = v_ref[rs, hh * RET_DV:(hh + 1) * RET_DV]
            s = lax.dot_general(qh, kh, (((1,), (1,)), ((), ())), preferred_element_type=F32)
            o = jnp.dot(s.astype(BF16) * intra_ref[hh], vh, preferred_element_type=F32)
            q2 = jnp.concatenate([qh * qf_ref[hh], qh * qb_ref[hh]], axis=1)
            s2 = jnp.concatenate([st_ref[hh].astype(BF16), sb_ref[cc, hh]], axis=0)
            o = o + jnp.dot(q2, s2, preferred_element_type=F32)
            st_ref[hh] = st_ref[hh] * cd_ref[hh:hh + 1, :] + _tdot(kh * kf_ref[hh], vh)
            mu = jnp.mean(o, axis=-1, keepdims=True)
            d = o - mu
            var = jnp.mean(d * d, axis=-1, keepdims=True)
            g = gs_ref[rs, hh * RET_DV:(hh + 1) * RET_DV]
            gate = g * _sigmoid(g)
            o_ref[rs, hh * RET_DV:(hh + 1) * RET_DV] = (d * lax.rsqrt(var + EPS)).astype(o_ref.dtype) * gate


def _ret_fwd(z, B, sb, s_fwd, intra, qdec_f, qdec_b, kdec_f, cdec):
    S = z.shape[0]
    H, C = RET_HEADS, min(RET_C, S)
    cps = min(RET_CPS, S // C)
    n = S // (C * cps)
    half = N_CHUNKS // 2
    return pl.pallas_call(
        _ret_fwd_kernel,
        grid=(B, n),
        in_specs=[pl.BlockSpec((cps * C, H * RET_DK), lambda b, j: (j, b * N_CHUNKS + 2)),
                  pl.BlockSpec((cps * C, H * RET_DK), lambda b, j: (j, b * N_CHUNKS + 3)),
                  pl.BlockSpec((cps * C, H * RET_DV), lambda b, j: (j, b * half + 2)),
                  pl.BlockSpec((cps * C, H * RET_DV), lambda b, j: (j, b * half + 3)),
                  pl.BlockSpec((None, cps, H, RET_DK, RET_DV), lambda b, j: (b, j, 0, 0, 0)),
                  pl.BlockSpec((None, H, RET_DK, RET_DV), lambda b, j: (b, 0, 0, 0)),
                  pl.BlockSpec((H, C, C), lambda b, j: (0, 0, 0)),
                  pl.BlockSpec((H, C, RET_DK), lambda b, j: (0, 0, 0)),
                  pl.BlockSpec((H, C, RET_DK), lambda b, j: (0, 0, 0)),
                  pl.BlockSpec((H, C, RET_DK), lambda b, j: (0, 0, 0)),
                  pl.BlockSpec((H, RET_DV), lambda b, j: (0, 0))],
        out_specs=pl.BlockSpec((None, cps * C, H * RET_DV), lambda b, j: (b, j, 0)),
        out_shape=jax.ShapeDtypeStruct((B, S, H * RET_DV), BF16),
        scratch_shapes=[pltpu.VMEM((H, RET_DK, RET_DV), F32)],
        compiler_params=_cparams(2),
        name="ret_fwd",
    )(z, z, z, z, sb, s_fwd, intra, qdec_f, qdec_b, kdec_f, cdec)


def _dot_t(a, b):
    return lax.dot_general(a, b, (((1,), (1,)), ((), ())), preferred_element_type=F32)


def _finish_kernel(x_ref, hf_ref, hb_ref, gr_ref, gab_ref, ret_ref, g1_ref, sh2_ref, sc2_ref, n2_ref,
                   wr_ref, wt_ref, wo_ref, rwh_ref, rwl_ref, rb_ref,
                   x1_ref, hl_ref, eid_ref, gate_ref, cnt_ref):
    D = x_ref.shape[1]
    tm = x_ref.shape[0]
    dt = eid_ref.shape[2]
    rnn = hf_ref[...].astype(F32) + hb_ref[...].astype(F32)
    y_rnn = jnp.dot((rnn * jax.nn.gelu(gr_ref[...].astype(F32))).astype(BF16), wr_ref[...],
                    preferred_element_type=F32)
    y_ret = jnp.dot(ret_ref[...], wt_ref[...], preferred_element_type=F32)
    ga = gab_ref[:, :D].astype(F32)
    gb = gab_ref[:, D:].astype(F32)
    merged = _sigmoid(ga) * y_rnn + _sigmoid(gb) * y_ret
    y = jnp.dot(merged.astype(BF16), wo_ref[...], preferred_element_type=F32)
    x1 = x_ref[...] + g1_ref[...] * y
    x1_ref[...] = x1
    hl = _rmsnorm(x1, n2_ref[...]) * (1.0 + sc2_ref[...]) + sh2_ref[...]
    hh = hl.astype(BF16)
    hl_ref[...] = hh

    hlo = (hl - hh.astype(F32)).astype(BF16)
    logits = (_dot_t(rwh_ref[...], hh) + _dot_t(rwh_ref[...], hlo) + _dot_t(rwl_ref[...], hh)) + rb_ref[...]

    ne = logits.shape[0]
    sub = lax.broadcasted_iota(I32, (ne, tm), 0)
    work = logits
    vals, idxs = [], []
    oh = jnp.zeros((ne, tm), F32)
    for _ in range(TOP_K):
        m = jnp.max(work, axis=0, keepdims=True)
        idx = jnp.min(jnp.where(work == m, sub, ne), axis=0, keepdims=True)
        hot = sub == idx
        vals.append(m)
        idxs.append(idx)
        oh = oh + jnp.where(hot, 1.0, 0.0)
        work = jnp.where(hot, -jnp.inf, work)
    es = [jnp.exp(v - vals[0]) for v in vals]
    inv = 1.0 / (es[0] + es[1] + es[2] + es[3])
    for part in range(tm // dt):
        ls = slice(part * dt, (part + 1) * dt)
        for k in range(TOP_K):
            eid_ref[part, k:k + 1, :] = idxs[k][:, ls]
            gate_ref[part, k:k + 1, :] = (es[k] * inv)[:, ls]
        cnt_ref[part] = jnp.sum(oh[:, ls], axis=1, keepdims=True)


def _finish(x3, hf, hb, z, retg, mods3, norm2_g, w_rnn_b, w_ret_b, w_out_b, rwt_hi, rwt_lo, router_b, dt):
    B, S, D = x3.shape
    tm = min(FIN_TM, S)
    nt = S // tm
    N = B * S
    ne = rwt_hi.shape[0]
    half = N_CHUNKS // 2
    per = tm // dt
    const2 = lambda b, i: (0, 0)
    tile3 = lambda b, i: (b * nt + i, 0, 0)
    return pl.pallas_call(
        _finish_kernel,
        grid=(B, nt),
        in_specs=[pl.BlockSpec((None, tm, D), lambda b, i: (b, i, 0)),
                  pl.BlockSpec((None, tm, D), lambda b, i: (b, i, 0)),
                  pl.BlockSpec((None, tm, D), lambda b, i: (b, i, 0)),
                  pl.BlockSpec((tm, D), lambda b, i: (i, b * N_CHUNKS + 1)),
                  pl.BlockSpec((tm, 2 * D), lambda b, i: (i, b * half + 4)),
                  pl.BlockSpec((None, tm, retg.shape[2]), lambda b, i: (b, i, 0)),
                  pl.BlockSpec((None, 1, D), lambda b, i: (b, 0, 2)),
                  pl.BlockSpec((None, 1, D), lambda b, i: (b, 0, 3)),
                  pl.BlockSpec((None, 1, D), lambda b, i: (b, 0, 4)),
                  pl.BlockSpec((1, D), const2),
                  pl.BlockSpec(w_rnn_b.shape, const2),
                  pl.BlockSpec(w_ret_b.shape, const2),
                  pl.BlockSpec(w_out_b.shape, const2),
                  pl.BlockSpec(rwt_hi.shape, const2),
                  pl.BlockSpec(rwt_lo.shape, const2),
                  pl.BlockSpec((ne, 1), const2)],
        out_specs=[pl.BlockSpec((None, tm, D), lambda b, i: (b, i, 0)),
                   pl.BlockSpec((tm, D), lambda b, i: (b * nt + i, 0)),
                   pl.BlockSpec((per, TOP_K, dt), tile3),
                   pl.BlockSpec((per, TOP_K, dt), tile3),
                   pl.BlockSpec((per, ne, 1), tile3)],
        out_shape=[jax.ShapeDtypeStruct((B, S, D), F32),
                   jax.ShapeDtypeStruct((N, D), BF16),
                   jax.ShapeDtypeStruct((N // dt, TOP_K, dt), I32),
                   jax.ShapeDtypeStruct((N // dt, TOP_K, dt), F32),
                   jax.ShapeDtypeStruct((N // dt, ne, 1), F32)],
        compiler_params=_cparams(2),
        name="finish",
    )(x3, hf, hb, z, z, retg, mods3, mods3, mods3, norm2_g.reshape(1, D),
      w_rnn_b, w_ret_b, w_out_b, rwt_hi, rwt_lo, router_b.reshape(ne, 1))


def _local_rows(dt):
    return TOP_K * dt + N_EXPERTS * ROW_ALIGN


def _local_slots(eid_ref):
    dt = eid_ref.shape[1]
    ne = N_EXPERTS
    sub = lax.broadcasted_iota(I32, (ne, dt), 0)
    hots = [sub == eid_ref[k:k + 1, :] for k in range(TOP_K)]
    oh = jnp.zeros((ne, dt), F32)
    for hot in hots:
        oh = oh + jnp.where(hot, 1.0, 0.0)
    earlier = jnp.where(lax.broadcasted_iota(I32, (dt, dt), 0) < lax.broadcasted_iota(I32, (dt, dt), 1), 1.0, 0.0)
    before = jnp.dot(oh.astype(BF16), earlier.astype(BF16), preferred_element_type=F32)
    cnt = jnp.broadcast_to(jnp.sum(oh, axis=1, keepdims=True), (ne, dt))
    cnt = jnp.maximum(jnp.ceil(cnt * (1.0 / ROW_ALIGN)) * ROW_ALIGN, float(ROW_ALIGN))
    lower = jnp.where(lax.broadcasted_iota(I32, (ne, ne), 1) < lax.broadcasted_iota(I32, (ne, ne), 0), 1.0, 0.0)
    base = before + jnp.dot(lower.astype(BF16), cnt.astype(BF16), preferred_element_type=F32)
    return [jnp.sum(jnp.where(hot, base, 0.0), axis=0, keepdims=True).astype(I32) for hot in hots]


def _slot_matrix(slots, weights, out_ref):
    rows, dt = out_ref.shape
    ch = 64
    rel = lax.broadcasted_iota(I32, (ch, dt), 0).astype(F32).astype(BF16)
    slots_f = [s.astype(F32) for s in slots]
    weights_b = [jnp.asarray(w, F32).astype(BF16) for w in weights]
    zero = jnp.zeros((), BF16)
    for c in range(rows // ch):
        acc = None
        for s, w in zip(slots_f, weights_b):
            term = jnp.where(rel == (s - float(c * ch)).astype(BF16), w, zero)
            acc = term if acc is None else acc + term
        out_ref[c * ch:(c + 1) * ch, :] = acc


def _for_each_run(meta_ref, fn, maybe_empty=False):
    def one(e):
        n = pl.multiple_of(meta_ref[0, N_EXPERTS + e], ROW_ALIGN)
        args = (pl.multiple_of(meta_ref[0, e], ROW_ALIGN), pl.multiple_of(meta_ref[0, 2 * N_EXPERTS + e], ROW_ALIGN), n)
        if maybe_empty:
            pl.when(n > 0)(lambda: fn(*args))
        else:
            fn(*args)

    if maybe_empty:
        lax.fori_loop(0, N_EXPERTS, lambda e, c: (one(e), c)[1], 0)
    else:
        for e in range(N_EXPERTS):
            one(e)


def _run_rows(meta_ref):
    last = N_EXPERTS - 1
    return pl.multiple_of(meta_ref[0, last] + meta_ref[0, N_EXPERTS + last], ROW_ALIGN)


def _dispatch_kernel(meta_ref, zmeta_ref, eid_ref, x_ref, hp_ref, slot_ref, sbuf_ref, zbuf_ref, pm_ref, rows_ref,
                     sems, zsem):
    t = pl.program_id(0)
    slot = t % 2

    def zero_copy(_, g, n):
        return pltpu.make_async_copy(zbuf_ref.at[pl.ds(0, n), :], hp_ref.at[pl.ds(g, n), :], zsem)

    def run_copy(sl, l, g, n):
        return pltpu.make_async_copy(sbuf_ref.at[sl, pl.ds(l, n), :], hp_ref.at[pl.ds(g, n), :], sems.at[sl])

    @pl.when(t == 0)
    def _():
        zbuf_ref[...] = jnp.zeros_like(zbuf_ref)
        _for_each_run(zmeta_ref, lambda l, g, n: zero_copy(l, g, n).start(), maybe_empty=True)
        _for_each_run(zmeta_ref, lambda l, g, n: zero_copy(l, g, n).wait(), maybe_empty=True)
        tail0 = zmeta_ref[0, 3 * N_EXPERTS]
        n_tail = zmeta_ref[0, 3 * N_EXPERTS + 1]

        def tail_copy(i):
            return zero_copy(0, pl.multiple_of(tail0 + i * MOE_BLOCK, MOE_BLOCK), MOE_BLOCK)

        lax.fori_loop(0, n_tail, lambda i, c: (tail_copy(i).start(), c)[1], 0)
        lax.fori_loop(0, n_tail, lambda i, c: (tail_copy(i).wait(), c)[1], 0)

    slots = _local_slots(eid_ref)
    for k in range(TOP_K):
        slot_ref[k:k + 1, :] = slots[k]
    _slot_matrix(slots, [1.0] * TOP_K, pm_ref)
    sbuf_ref[slot] = _pack_cols(jnp.dot(pm_ref[...], x_ref[...], preferred_element_type=F32))

    @pl.when(t > 0)
    def _():
        run_copy(1 - slot, 0, 0, pl.multiple_of(rows_ref[1 - slot], ROW_ALIGN)).wait()

    _for_each_run(meta_ref, lambda l, g, n: run_copy(slot, l, g, n).start())
    rows_ref[slot] = _run_rows(meta_ref)

    @pl.when(t == pl.num_programs(0) - 1)
    def _():
        run_copy(slot, 0, 0, _run_rows(meta_ref)).wait()


def _dispatch(hl2, eid3, meta3, zmeta, P):
    N, D = hl2.shape
    nt, _, dt = eid3.shape
    nm = meta3.shape[2]
    return pl.pallas_call(
        _dispatch_kernel,
        grid=(nt,),
        in_specs=[pl.BlockSpec((None, 1, nm), lambda i: (i, 0, 0), memory_space=pltpu.SMEM),
                  pl.BlockSpec(zmeta.shape, lambda i: (0, 0), memory_space=pltpu.SMEM),
                  pl.BlockSpec((None, TOP_K, dt), lambda i: (i, 0, 0)),
                  pl.BlockSpec((dt, D), lambda i: (i, 0))],
        out_specs=[pl.BlockSpec(memory_space=pl.ANY),
                   pl.BlockSpec((None, TOP_K, dt), lambda i: (i, 0, 0))],
        out_shape=[jax.ShapeDtypeStruct((P, D // 2), U32),
                   jax.ShapeDtypeStruct((nt, TOP_K, dt), I32)],
        scratch_shapes=[pltpu.VMEM((2, _local_rows(dt), D // 2), U32),
                        pltpu.VMEM((MOE_BLOCK, D // 2), U32),
                        pltpu.VMEM((_local_rows(dt), dt), BF16),
                        pltpu.SMEM((2,), I32),
                        pltpu.SemaphoreType.DMA((2,)),
                        pltpu.SemaphoreType.DMA(())],
        compiler_params=_cparams(1),
        name="dispatch",
    )(meta3, zmeta, eid3, hl2)


def _regroup_kernel(w_ref, sel_ref, o_ref):
    half = w_ref.shape[1] // 2
    g = sel_ref.shape[0]
    wb = w_ref[...].astype(BF16)
    for j in range(w_ref.shape[1] // g):
        r = jnp.dot(wb[:, g * j:g * (j + 1)], sel_ref[...], preferred_element_type=F32)
        o_ref[:, (g // 2) * j:(g // 2) * (j + 1)] = r[:, :g // 2].astype(o_ref.dtype)
        o_ref[:, half + (g // 2) * j:half + (g // 2) * (j + 1)] = r[:, g // 2:].astype(o_ref.dtype)


def _regroup_glu_lin(w1):
    ne, D, de2 = w1.shape
    g = 256
    sel = np.zeros((g, g), np.float32)
    sel[np.arange(0, g, 2), np.arange(g // 2)] = 1.0
    sel[np.arange(1, g, 2), g // 2 + np.arange(g // 2)] = 1.0
    out = pl.pallas_call(
        _regroup_kernel,
        grid=(ne,),
        in_specs=[pl.BlockSpec((D, de2), lambda e: (e, 0)),
                  pl.BlockSpec((g, g), lambda e: (0, 0))],
        out_specs=pl.BlockSpec((D, de2), lambda e: (e, 0)),
        out_shape=jax.ShapeDtypeStruct((ne * D, de2), BF16),
        compiler_params=_cparams(1),
        name="regroup",
    )(w1.reshape(ne * D, de2), jnp.asarray(sel, BF16))
    return out.reshape(ne, D, de2)


def _expert_kernel(be_ref, nu_ref, valid_ref, x_ref, w1_ref, b1_ref, w2_ref, b2_ref, o_ref):
    del be_ref, nu_ref
    de = w2_ref.shape[0]

    def rows(rs):
        h = jnp.dot(_unpack_cols(x_ref[rs, :]), w1_ref[...], preferred_element_type=F32) + b1_ref[...]
        glu = jnp.minimum(h[:, :de], SWIGLU_LIMIT)
        lin = jnp.clip(h[:, de:], -SWIGLU_LIMIT, SWIGLU_LIMIT)
        act = glu * _sigmoid(SWIGLU_ALPHA * glu) * (lin + 1.0)
        y = jnp.dot(act.astype(BF16), w2_ref[...].astype(BF16), preferred_element_type=F32) + b2_ref[...]
        o_ref[rs, :] = _pack_cols(y.astype(BF16).astype(F32))

    valid = valid_ref[pl.program_id(0)]

    @pl.when(valid == MOE_BLOCK)
    def _():
        rows(slice(0, MOE_BLOCK))

    @pl.when(valid < MOE_BLOCK)
    def _():
        for q in range(MOE_BLOCK // MOE_SUB):
            rs = slice(q * MOE_SUB, (q + 1) * MOE_SUB)

            @pl.when(valid > q * MOE_SUB)
            def _(rs=rs):
                rows(rs)

            @pl.when(valid <= q * MOE_SUB)
            def _(rs=rs):
                o_ref[rs, :] = jnp.zeros((MOE_SUB, o_ref.shape[1]), o_ref.dtype)


def _experts(blk_expert, n_used, valid, h_pad, w1b, b1p, w2b, b2):
    P = h_pad.shape[0]
    ne, D, de2 = w1b.shape
    de = de2 // 2
    nblk = P // MOE_BLOCK
    blk = lambda j, be, nu, vl: (jnp.minimum(j, nu[0] - 1), 0)
    wsel = lambda j, be, nu, vl: (be[jnp.minimum(j, nu[0] - 1)], 0, 0)
    return pl.pallas_call(
        _expert_kernel,
        grid_spec=pltpu.PrefetchScalarGridSpec(
            num_scalar_prefetch=3,
            grid=(nblk,),
            in_specs=[pl.BlockSpec((MOE_BLOCK, D // 2), blk),
                      pl.BlockSpec((None, D, de2), wsel),
                      pl.BlockSpec((None, 1, de2), wsel),
                      pl.BlockSpec((None, de, D), wsel),
                      pl.BlockSpec((None, 1, D), wsel)],
            out_specs=pl.BlockSpec((MOE_BLOCK, D // 2), lambda j, be, nu, vl: (j, 0))),
        out_shape=jax.ShapeDtypeStruct((P, D // 2), U32),
        compiler_params=_cparams(1),
        name="experts",
    )(blk_expert, n_used, valid, h_pad, w1b, b1p.reshape(ne, 1, de2), w2b, b2.reshape(ne, 1, D))


def _combine_kernel(meta_ref, nmeta_ref, slot_ref, gate_ref, x1_ref, g2_ref, fg_ref, yp_ref, o_ref,
                    ybuf_ref, gm_ref, sems):
    t = pl.program_id(0) * pl.num_programs(1) + pl.program_id(1)
    n_tiles = pl.num_programs(0) * pl.num_programs(1)
    slot = t % 2

    def run_copy(sl, l, g, n):
        return pltpu.make_async_copy(yp_ref.at[pl.ds(g, n), :], ybuf_ref.at[sl, pl.ds(l, n), :], sems.at[sl])

    @pl.when(t == 0)
    def _():
        ybuf_ref[...] = jnp.zeros_like(ybuf_ref)
        _for_each_run(meta_ref, lambda l, g, n: run_copy(0, l, g, n).start())

    @pl.when(t + 1 < n_tiles)
    def _():
        _for_each_run(nmeta_ref, lambda l, g, n: run_copy(1 - slot, l, g, n).start())

    run_copy(slot, 0, 0, _run_rows(meta_ref)).wait()

    _slot_matrix([slot_ref[k:k + 1, :] for k in range(TOP_K)], [gate_ref[k:k + 1, :] for k in range(TOP_K)], gm_ref)
    y = _tdot(gm_ref[...], _unpack_cols(ybuf_ref[slot]))
    x2 = x1_ref[...] + g2_ref[...] * y
    o_ref[...] = _rmsnorm(x2, fg_ref[...])


def _combine(meta3, slot3, gate3, x1, mods3, final_g, y_pad):
    B, S, D = x1.shape
    n_tiles, _, dt = slot3.shape
    nt = S // dt
    nm = meta3.shape[2]
    tile3 = lambda b, i: (b * nt + i, 0, 0)
    next3 = lambda b, i: (jnp.minimum(b * nt + i + 1, n_tiles - 1), 0, 0)
    return pl.pallas_call(
        _combine_kernel,
        grid=(B, nt),
        in_specs=[pl.BlockSpec((None, 1, nm), tile3, memory_space=pltpu.SMEM),
                  pl.BlockSpec((None, 1, nm), next3, memory_space=pltpu.SMEM),
                  pl.BlockSpec((None, TOP_K, dt), tile3),
                  pl.BlockSpec((None, TOP_K, dt), tile3),
                  pl.BlockSpec((None, dt, D), lambda b, i: (b, i, 0)),
                  pl.BlockSpec((None, 1, D), lambda b, i: (b, 0, 5)),
                  pl.BlockSpec((1, D), lambda b, i: (0, 0)),
                  pl.BlockSpec(memory_space=pl.ANY)],
        out_specs=pl.BlockSpec((None, dt, D), lambda b, i: (b, i, 0)),
        out_shape=jax.ShapeDtypeStruct((B, S, D), F32),
        scratch_shapes=[pltpu.VMEM((2, _local_rows(dt), D // 2), U32), pltpu.VMEM((_local_rows(dt), dt), BF16),
                        pltpu.SemaphoreType.DMA((2,))],
        compiler_params=_cparams(2),
        name="combine",
    )(meta3, meta3, slot3, gate3, x1, mods3, final_g.reshape(1, D), y_pad)


def _rope_tables(S, k_scale):
    n_freq = RET_DK // 4
    pos = np.arange(S)
    rows = (pos // GRID_W).astype(np.float32)
    cols = (pos % GRID_W).astype(np.float32)
    inv = (np.float32(ROPE_BASE) ** (-np.arange(n_freq, dtype=np.float32) / np.float32(n_freq))).astype(np.float32)
    ang = np.concatenate([rows[:, None] * inv, cols[:, None] * inv], axis=-1).astype(np.float32)
    cos, sin = np.cos(ang), np.sin(ang)
    cos2 = np.concatenate([cos, cos], axis=-1)
    sin2 = np.concatenate([-sin, sin], axis=-1)
    ks = np.float32(k_scale)
    return jnp.asarray(np.concatenate([cos2, sin2, cos2 * ks, sin2 * ks], axis=-1), F32)


def _identity_tables(L, k_scale):
    one = np.ones((L, RET_DK), np.float32)
    zero = np.zeros((L, RET_DK), np.float32)
    return jnp.asarray(np.concatenate([one, zero, one * np.float32(k_scale), zero], axis=-1), F32)


def _lanes(t, width):
    return np.ascontiguousarray(np.broadcast_to(t[:, :, None], t.shape + (width,)))


def kernel(x, c, ctx, c_ctx, ada_w, ada_b, norm1_g, w_in, conv_w, conv_b, lru_wa, lru_ba, lru_wx, lru_bx,
           lru_lambda, w_rnn_proj, w_ret_proj, w_out, norm2_g, router_w, router_b, moe_w1, moe_b1, moe_w2,
           moe_b2, final_g):
    B, S, D = x.shape
    L = ctx.shape[1]
    N = B * S
    H = RET_HEADS
    lyr = 0
    d_in = w_in.shape[2]
    assert ada_w.shape[0] == 1 and d_in == N_CHUNKS * D and B == 8

    def pairs_apart(w):
        return jnp.swapaxes(w.reshape(D, H, RET_DK // 2, 2), 2, 3).reshape(D, H * RET_DK)

    w_in_b = w_in[lyr].astype(BF16)
    w_qk = jnp.concatenate([pairs_apart(w_in_b[:, 2 * D:3 * D]), pairs_apart(w_in_b[:, 3 * D:4 * D])], axis=1)
    wg = [(0.5 * jnp.concatenate([lru_wa[lyr, d], lru_wx[lyr, d]], axis=-1)).astype(BF16) for d in range(2)]
    de2 = moe_w1.shape[3]
    glu_lin = np.concatenate([np.arange(0, de2, 2), np.arange(1, de2, 2)])
    w1b = _regroup_glu_lin(moe_w1[lyr])
    b1p = moe_b1[lyr][:, glu_lin]
    rwt = router_w[lyr].T
    rwt_hi = rwt.astype(BF16)
    rwt_lo = (rwt - rwt_hi.astype(F32)).astype(BF16)

    k_scale = RET_DK ** -0.5
    tab_l = _rope_tables(S, k_scale)
    tab_c = _identity_tables(B * L, k_scale)
    f4 = np.float32
    log_g = np.log1p(-np.exp2(-5.0 - np.arange(H, dtype=f4))).astype(f4)
    C = min(RET_C, S)
    idx = np.arange(C, dtype=f4)
    dec = lambda e: np.exp(e[None, :].astype(f4) * log_g[:, None]).astype(f4)
    intra = jnp.asarray(np.exp(np.abs(idx[:, None] - idx[None, :])[None] * log_g[:, None, None]).astype(f4))
    qdec_f = jnp.asarray(_lanes(dec(idx + 1.0), RET_DK))
    qdec_b = jnp.asarray(_lanes(dec(C - idx), RET_DK))
    kdec_f = jnp.asarray(_lanes(dec(C - 1.0 - idx), RET_DK))
    kdec_b = jnp.asarray(_lanes(dec(idx), RET_DK))
    cdec = jnp.asarray(np.broadcast_to(np.exp(C * log_g).astype(f4)[:, None], (H, RET_DV)))
    pos_c = np.arange(L, dtype=f4)
    cdec_f = jnp.asarray(_lanes(dec(L - 1.0 - pos_c), RET_DK))
    cdec_b = jnp.asarray(_lanes(dec(pos_c), RET_DK))

    cvec = jnp.zeros((16, D), F32).at[:B].set(c).at[B].set(c_ctx)
    mods3 = _ada(cvec, ada_w[lyr], ada_b[lyr]).reshape(16, 1, 6 * D)

    z_c, xr_c = _proj(ctx.reshape(1, B * L, D), mods3, lambda b: B, norm1_g[lyr], w_in_b, w_qk, tab_c,
                      min(PROJ_TM, B * L))
    xr_c = xr_c.reshape(B, L, D)
    z_l, xr_l = _proj(x, mods3, lambda b: b, norm1_g[lyr], w_in_b, w_qk, tab_l, min(PROJ_TM, S))

    zeros = jnp.zeros((B, D), F32)
    hs = []
    for d in range(2):
        args = (conv_w[lyr], conv_b[lyr], wg[d], 0.5 * lru_ba[lyr, d], 0.5 * lru_bx[lyr, d], lru_lambda[lyr, d])
        _, h0 = _rnn(xr_c, *args, zeros, reverse=(d == 1), tt=RNN_TT)
        h, _ = _rnn(xr_l, *args, h0, reverse=(d == 1), tt=RNN_TT)
        hs.append(h)

    s_fwd, s_bwd = _ret_ctx(z_c, B, cdec_f, cdec_b)
    sb = _ret_bwd(z_l, B, s_bwd, kdec_b, cdec)
    retg = _ret_fwd(z_l, B, sb, s_fwd, intra.astype(BF16), qdec_f.astype(BF16), qdec_b.astype(BF16),
                    kdec_f.astype(BF16), cdec)

    dt = min(DISP_TM, S)
    x1, hl2, eid3, gate3, cnt3 = _finish(
        x, hs[0], hs[1], z_l, retg, mods3, norm2_g[lyr], w_rnn_proj[lyr].astype(BF16),
        w_ret_proj[lyr].astype(BF16), w_out[lyr].astype(BF16), rwt_hi, rwt_lo, router_b[lyr], dt)

    cnt_t = cnt3[:, :, 0].astype(I32)
    cnt_t = jnp.maximum((cnt_t + ROW_ALIGN - 1) // ROW_ALIGN * ROW_ALIGN, ROW_ALIGN)
    cnt = jnp.sum(cnt_t, axis=0)
    padded = (cnt + MOE_BLOCK - 1) // MOE_BLOCK * MOE_BLOCK
    pad_end = jnp.cumsum(padded)
    pad_start = pad_end - padded
    gstart = pad_start[None, :] + jnp.cumsum(cnt_t, axis=0) - cnt_t
    loff = jnp.cumsum(cnt_t, axis=1) - cnt_t
    meta3 = jnp.concatenate([loff, cnt_t, gstart], axis=1).reshape(N // dt, 1, 3 * N_EXPERTS)
    n_blocks = -(-(N * TOP_K + (N // dt) * N_EXPERTS * ROW_ALIGN) // MOE_BLOCK) + N_EXPERTS
    P = n_blocks * MOE_BLOCK
    zmeta = jnp.concatenate([jnp.zeros_like(cnt), padded - cnt, pad_start + cnt,
                             pad_end[-1:], (P - pad_end[-1:]) // MOE_BLOCK]).reshape(1, 3 * N_EXPERTS + 2)
    blk_start = jnp.arange(n_blocks, dtype=I32) * MOE_BLOCK
    blk_expert = jnp.minimum(jnp.sum((pad_end[None, :] <= blk_start[:, None]).astype(I32), axis=1), N_EXPERTS - 1)
    n_used = (pad_end[-1:] // MOE_BLOCK).astype(I32)
    blk_valid = jnp.clip((pad_start + cnt)[blk_expert] - blk_start, 0, MOE_BLOCK)
    blk_valid = jnp.where(jnp.arange(n_blocks) < n_used[0], blk_valid, 0).astype(I32)

    h_pad, slot3 = _dispatch(hl2, eid3, meta3, zmeta, P)
    y_pad = _experts(blk_expert, n_used, blk_valid, h_pad, w1b, b1p, moe_w2[lyr], moe_b2[lyr])
    return _combine(meta3, slot3, gate3, x1, mods3, final_g, y_pad)
```

```python
import functools

import jax
import jax.numpy as jnp
import numpy as np
from jax import lax
from jax.experimental import pallas as pl
from jax.experimental.pallas import tpu as pltpu

F32 = jnp.float32
BF16 = jnp.bfloat16
I32 = jnp.int32

GRID_W = 64
RNN_BLOCKS = 8
CONV_W = 4
LRU_C = 8.0
RET_HEADS = 8
RET_DK = 128
RET_DV = 256
ROPE_BASE = 10000.0
N_EXPERTS = 32
TOP_K = 4
SWIGLU_ALPHA = 1.702
SWIGLU_LIMIT = 7.0
EPS = 1e-6
N_CHUNKS = 10

PROJ_TM = 512
PROJ_TN = 2048
RNN_TT = 256
RET_C = 256
RET_CPS = 4
FIN_TM = 512
MOE_BLOCK = 1024
MOE_SUB = 256
DISP_TM = 512
ROW_ALIGN = 8
VMEM_LIMIT = 56 * 1024 * 1024


def _cparams(n_axes):
    return pltpu.CompilerParams(dimension_semantics=("arbitrary",) * n_axes,
                                vmem_limit_bytes=VMEM_LIMIT)


def _sigmoid(x):
    return 0.5 * (jnp.tanh(0.5 * x) + 1.0)


def _rmsnorm(x, g):
    return x * lax.rsqrt(jnp.mean(x * x, axis=-1, keepdims=True) + EPS) * g


U32 = jnp.uint32


def _pack_cols(v):
    w = v.shape[1] // 2
    lo = lax.shift_right_logical(lax.bitcast_convert_type(v[:, :w], U32), jnp.uint32(16))
    hi = lax.bitcast_convert_type(v[:, w:], U32) & jnp.uint32(0xFFFF0000)
    return lo | hi


def _unpack_cols(u):
    a = lax.bitcast_convert_type(lax.shift_left(u, jnp.uint32(16)), F32).astype(BF16)
    b = lax.bitcast_convert_type(u & jnp.uint32(0xFFFF0000), F32).astype(BF16)
    return jnp.concatenate([a, b], axis=1)


def _tdot(a, b):
    return lax.dot_general(a, b, (((0,), (0,)), ((), ())), preferred_element_type=F32)


def _ada_kernel(c_ref, w_ref, b_ref, o_ref):
    c = c_ref[...]
    s = c * _sigmoid(c)
    o_ref[...] = jnp.dot(s, w_ref[...], preferred_element_type=F32,
                         precision=lax.Precision.HIGHEST) + b_ref[...]


def _ada(cvec, ada_w, ada_b):
    R, D = cvec.shape
    n = ada_w.shape[1] // D
    return pl.pallas_call(
        _ada_kernel,
        grid=(n,),
        in_specs=[pl.BlockSpec((R, D), lambda j: (0, 0)),
                  pl.BlockSpec((D, D), lambda j: (0, j)),
                  pl.BlockSpec((1, D), lambda j: (0, j))],
        out_specs=pl.BlockSpec((R, D), lambda j: (0, j)),
        out_shape=jax.ShapeDtypeStruct((R, n * D), F32),
        compiler_params=_cparams(1),
        name="ada",
    )(cvec, ada_w, ada_b.reshape(1, -1))


def _proj_kernel(x_ref, sh_ref, sc_ref, g_ref, w_ref, wqk_ref, tab_ref, o_ref, xr_ref):
    D = x_ref.shape[1]
    h = _rmsnorm(x_ref[...], g_ref[...])
    hb = (h * (1.0 + sc_ref[...]) + sh_ref[...]).astype(BF16)
    for j in range(w_ref.shape[1] // PROJ_TN):
        j0 = j * PROJ_TN
        wj = wqk_ref[...] if j == 1 else w_ref[:, j0:j0 + PROJ_TN]
        acc = jnp.dot(hb, wj, preferred_element_type=F32)
        if j == 0:
            xr_ref[...] = acc[:, :D].astype(xr_ref.dtype)
        if j != 1:
            o_ref[:, j0:j0 + PROJ_TN] = acc.astype(o_ref.dtype)
            continue
        for part in range(2):
            cos = tab_ref[:, (2 * part) * RET_DK:(2 * part + 1) * RET_DK]
            sin = tab_ref[:, (2 * part + 1) * RET_DK:(2 * part + 2) * RET_DK]
            for hh in range(RET_HEADS):
                c0 = part * RET_HEADS * RET_DK + hh * RET_DK
                t = acc[:, c0:c0 + RET_DK]
                o_ref[:, j0 + c0:j0 + c0 + RET_DK] = (
                    t * cos + pltpu.roll(t, RET_DK // 2, 1) * sin).astype(o_ref.dtype)


def _proj(x3, mods3, mod_row, norm_g, w_in_b, w_qk, tab, tm):
    B, S, D = x3.shape
    d_in = w_in_b.shape[1]
    return pl.pallas_call(
        _proj_kernel,
        grid=(B, S // tm),
        in_specs=[pl.BlockSpec((None, tm, D), lambda b, i: (b, i, 0)),
                  pl.BlockSpec((None, 1, D), lambda b, i: (mod_row(b), 0, 0)),
                  pl.BlockSpec((None, 1, D), lambda b, i: (mod_row(b), 0, 1)),
                  pl.BlockSpec((1, D), lambda b, i: (0, 0)),
                  pl.BlockSpec((D, d_in), lambda b, i: (0, 0), pipeline_mode=pl.Buffered(1)),
                  pl.BlockSpec((D, PROJ_TN), lambda b, i: (0, 0), pipeline_mode=pl.Buffered(1)),
                  pl.BlockSpec((tm, 4 * RET_DK), lambda b, i: (i, 0))],
        out_specs=[pl.BlockSpec((tm, d_in), lambda b, i: (i, b)),
                   pl.BlockSpec((None, tm, D), lambda b, i: (b, i, 0))],
        out_shape=[jax.ShapeDtypeStruct((S, B * d_in), BF16),
                   jax.ShapeDtypeStruct((B, S, D), BF16)],
        compiler_params=_cparams(2),
        name="proj",
    )(x3, mods3, mods3, norm_g.reshape(1, D), w_in_b, w_qk, tab)


def _rnn_kernel(xm_ref, xp_ref, xn_ref, cw_ref, cb_ref, wg_ref, ba_ref, bx_ref, lam_ref, h0_ref,
                h_ref, hfin_ref, xs_ref, a_ref, u_ref, hc_ref, *, reverse, n_tiles):
    i = pl.program_id(0)
    tile = (n_tiles - 1 - i) if reverse else i
    nb, tt, D = xm_ref.shape
    bw = D // RNN_BLOCKS
    HALO = xp_ref.shape[1]

    @pl.when(i == 0)
    def _():
        hc_ref[...] = h0_ref[...]

    GT = HALO
    gr = GT * nb
    ri = lax.broadcasted_iota(I32, (gr, gr), 0)
    ci = lax.broadcasted_iota(I32, (gr, gr), 1)
    perm = jnp.where((ri // nb == ci % GT) & (ri % nb == ci // GT), 1.0, 0.0).astype(BF16)

    def time_major(ref, t0):
        xg = jnp.concatenate([ref[b, t0:t0 + GT, :] for b in range(nb)], axis=0)
        return jnp.dot(perm, xg, preferred_element_type=F32)

    hr = HALO * nb
    R = tt * nb
    xs_ref[0:hr, :] = jnp.where(tile > 0, time_major(xp_ref, 0), 0.0)
    for g in range(tt // GT):
        xs_ref[hr + g * gr:hr + (g + 1) * gr, :] = time_major(xm_ref, g * GT)
    xs_ref[hr + R:2 * hr + R, :] = jnp.where(tile < n_tiles - 1, time_major(xn_ref, 0), 0.0)

    nl = -lam_ref[...]
    csp = (0.25 * LRU_C) * (jnp.maximum(nl, 0.0) + jnp.log1p(jnp.exp(-jnp.abs(nl))))
    SUB = 256

    def gates(s, carry):
        r0 = pl.multiple_of(s * SUB, SUB)
        base = hr - 2 * nb
        xc = cb_ref[...] + cw_ref[0:1, :] * xs_ref[pl.ds(pl.multiple_of(r0 + base, nb), SUB), :]
        for k in range(1, CONV_W):
            xc = xc + cw_ref[k:k + 1, :] * xs_ref[pl.ds(pl.multiple_of(r0 + base + k * nb, nb), SUB), :]
        xb = xc.astype(BF16)
        for n in range(RNN_BLOCKS):
            g = jnp.dot(xb[:, n * bw:(n + 1) * bw], wg_ref[n], preferred_element_type=F32)
            cs = slice(n * bw, (n + 1) * bw)
            tr = jnp.tanh(g[:, :bw] + ba_ref[:, cs])
            ti = jnp.tanh(g[:, bw:] + bx_ref[:, cs])
            t = jnp.tanh(csp[:, cs] * tr + csp[:, cs])
            rc = 1.0 / (1.0 + t)
            a_ref[pl.ds(r0, SUB), cs] = (1.0 - t) * rc
            root = t * lax.rsqrt(jnp.maximum(t, 1e-30))
            u_ref[pl.ds(r0, SUB), cs] = (root * rc) * ((ti + 1.0) * xc[:, cs])
        return carry

    lax.fori_loop(0, R // SUB, gates, 0)

    def step(t, h):
        ts = (tt - 1 - t) if reverse else t
        r0 = pl.multiple_of(ts * nb, nb)
        h = a_ref[pl.ds(r0, nb), :] * h + u_ref[pl.ds(r0, nb), :]
        u_ref[pl.ds(r0, nb), :] = h
        return h

    h = lax.fori_loop(0, tt, step, hc_ref[...], unroll=8)
    hc_ref[...] = h
    hfin_ref[...] = h
    for g in range(tt // GT):
        hg = _tdot(perm, u_ref[g * gr:(g + 1) * gr, :].astype(BF16))
        for b in range(nb):
            h_ref[b, g * GT:(g + 1) * GT, :] = hg[b * GT:(b + 1) * GT, :].astype(h_ref.dtype)


def _rnn(xr, conv_w, conv_b, wg, ba, bx, lam, h0, *, reverse, tt):
    nb, T, D = xr.shape
    tt = min(tt, T)
    n_tiles = T // tt
    HALO = 16
    hb = tt // HALO

    def tile_of(i):
        return (n_tiles - 1 - i) if reverse else i

    kern = functools.partial(_rnn_kernel, reverse=reverse, n_tiles=n_tiles)
    return pl.pallas_call(
        kern,
        grid=(n_tiles,),
        in_specs=[pl.BlockSpec((nb, tt, D), lambda i: (0, tile_of(i), 0)),
                  pl.BlockSpec((nb, HALO, D), lambda i: (0, jnp.maximum(tile_of(i) * hb - 1, 0), 0)),
                  pl.BlockSpec((nb, HALO, D), lambda i: (0, jnp.minimum((tile_of(i) + 1) * hb, n_tiles * hb - 1), 0)),
                  pl.BlockSpec((CONV_W, D), lambda i: (0, 0)),
                  pl.BlockSpec((1, D), lambda i: (0, 0)),
                  pl.BlockSpec((RNN_BLOCKS, D // RNN_BLOCKS, 2 * D // RNN_BLOCKS), lambda i: (0, 0, 0)),
                  pl.BlockSpec((1, D), lambda i: (0, 0)),
                  pl.BlockSpec((1, D), lambda i: (0, 0)),
                  pl.BlockSpec((1, D), lambda i: (0, 0)),
                  pl.BlockSpec((nb, D), lambda i: (0, 0))],
        out_specs=[pl.BlockSpec((nb, tt, D), lambda i: (0, tile_of(i), 0)),
                   pl.BlockSpec((nb, D), lambda i: (0, 0))],
        out_shape=[jax.ShapeDtypeStruct((nb, T, D), BF16),
                   jax.ShapeDtypeStruct((nb, D), F32)],
        scratch_shapes=[pltpu.VMEM(((tt + 2 * HALO) * nb, D), F32),
                        pltpu.VMEM((nb * tt, D), F32),
                        pltpu.VMEM((nb * tt, D), F32),
                        pltpu.VMEM((nb, D), F32)],
        compiler_params=_cparams(1),
        name="rnn_bwd" if reverse else "rnn_fwd",
    )(xr, xr, xr, conv_w, conv_b.reshape(1, D), wg, ba.reshape(1, D), bx.reshape(1, D),
      lam.reshape(1, D), h0)


def _ret_ctx_kernel(k_ref, v_ref, df_ref, db_ref, sf_ref, sb_ref):
    for hh in range(RET_HEADS):
        kh = k_ref[:, hh * RET_DK:(hh + 1) * RET_DK].astype(F32)
        vh = v_ref[:, hh * RET_DV:(hh + 1) * RET_DV]
        sf_ref[hh] = _tdot((kh * df_ref[hh]).astype(BF16), vh)
        sb_ref[hh] = _tdot((kh * db_ref[hh]).astype(BF16), vh)


def _ret_ctx(zc, B, dec_f, dec_b):
    L = zc.shape[0] // B
    H = RET_HEADS
    st = jax.ShapeDtypeStruct((B, H, RET_DK, RET_DV), F32)
    return pl.pallas_call(
        _ret_ctx_kernel,
        grid=(B,),
        in_specs=[pl.BlockSpec((L, H * RET_DK), lambda b: (b, 3)),
                  pl.BlockSpec((L, H * RET_DV), lambda b: (b, 2)),
                  pl.BlockSpec((H, L, RET_DK), lambda b: (0, 0, 0)),
                  pl.BlockSpec((H, L, RET_DK), lambda b: (0, 0, 0))],
        out_specs=[pl.BlockSpec((None, H, RET_DK, RET_DV), lambda b: (b, 0, 0, 0))] * 2,
        out_shape=[st, st],
        compiler_params=_cparams(1),
        name="ret_ctx",
    )(zc, zc, dec_f, dec_b)


def _ret_bwd_kernel(k_ref, v_ref, s0_ref, kd_ref, cd_ref, o_ref, st_ref):
    @pl.when(pl.program_id(1) == 0)
    def _():
        st_ref[...] = s0_ref[...]

    C = kd_ref.shape[1]
    for cc in reversed(range(k_ref.shape[0] // C)):
        rs = slice(cc * C, (cc + 1) * C)
        for hh in range(RET_HEADS):
            o_ref[cc, hh] = st_ref[hh].astype(o_ref.dtype)
            kh = k_ref[rs, hh * RET_DK:(hh + 1) * RET_DK].astype(F32)
            vh = v_ref[rs, hh * RET_DV:(hh + 1) * RET_DV]
            st_ref[hh] = st_ref[hh] * cd_ref[hh:hh + 1, :] + _tdot((kh * kd_ref[hh]).astype(BF16), vh)


def _ret_bwd(z, B, s_bwd, kdec_b, cdec):
    S = z.shape[0]
    H, C = RET_HEADS, min(RET_C, S)
    cps = min(RET_CPS, S // C)
    n = S // (C * cps)
    return pl.pallas_call(
        _ret_bwd_kernel,
        grid=(B, n),
        in_specs=[pl.BlockSpec((cps * C, H * RET_DK), lambda b, j: (n - 1 - j, b * N_CHUNKS + 3)),
                  pl.BlockSpec((cps * C, H * RET_DV), lambda b, j: (n - 1 - j, b * (N_CHUNKS // 2) + 2)),
                  pl.BlockSpec((None, H, RET_DK, RET_DV), lambda b, j: (b, 0, 0, 0)),
                  pl.BlockSpec((H, C, RET_DK), lambda b, j: (0, 0, 0)),
                  pl.BlockSpec((H, RET_DV), lambda b, j: (0, 0))],
        out_specs=pl.BlockSpec((None, cps, H, RET_DK, RET_DV), lambda b, j: (b, n - 1 - j, 0, 0, 0)),
        out_shape=jax.ShapeDtypeStruct((B, n * cps, H, RET_DK, RET_DV), BF16),
        scratch_shapes=[pltpu.VMEM((H, RET_DK, RET_DV), F32)],
        compiler_params=_cparams(2),
        name="ret_bwd",
    )(z, z, s_bwd, kdec_b, cdec)


def _ret_fwd_kernel(q_ref, k_ref, v_ref, gs_ref, sb_ref, s0_ref, intra_ref, qf_ref, qb_ref, kf_ref, cd_ref,
                    o_ref, st_ref):
    @pl.when(pl.program_id(1) == 0)
    def _():
        st_ref[...] = s0_ref[...]

    C = intra_ref.shape[1]
    for cc in range(q_ref.shape[0] // C):
        rs = slice(cc * C, (cc + 1) * C)
        for hh in range(RET_HEADS):
            qh = q_ref[rs, hh * RET_DK:(hh + 1) * RET_DK]
            kh = k_ref[rs, hh * RET_DK:(hh + 1) * RET_DK]
            vh = v_ref[rs, hh * RET_DV:(hh + 1) * RET_DV]
            s = lax.dot_general(qh, kh, (((1,), (1,)), ((), ())), preferred_element_type=F32)
            o = jnp.dot(s.astype(BF16) * intra_ref[hh], vh, preferred_element_type=F32)
            q2 = jnp.concatenate([qh * qf_ref[hh], qh * qb_ref[hh]], axis=1)
            s2 = jnp.concatenate([st_ref[hh].astype(BF16), sb_ref[cc, hh]], axis=0)
            o = o + jnp.dot(q2, s2, preferred_element_type=F32)
            st_ref[hh] = st_ref[hh] * cd_ref[hh:hh + 1, :] + _tdot(kh * kf_ref[hh], vh)
            mu = jnp.mean(o, axis=-1, keepdims=True)
            d = o - mu
            var = jnp.mean(d * d, axis=-1, keepdims=True)
            g = gs_ref[rs, hh * RET_DV:(hh + 1) * RET_DV]
            gate = g * _sigmoid(g)
            o_ref[rs, hh * RET_DV:(hh + 1) * RET_DV] = (d * lax.rsqrt(var + EPS)).astype(o_ref.dtype) * gate


def _ret_fwd(z, B, sb, s_fwd, intra, qdec_f, qdec_b, kdec_f, cdec):
    S = z.shape[0]
    H, C = RET_HEADS, min(RET_C, S)
    cps = min(RET_CPS, S // C)
    n = S // (C * cps)
    half = N_CHUNKS // 2
    return pl.pallas_call(
        _ret_fwd_kernel,
        grid=(B, n),
        in_specs=[pl.BlockSpec((cps * C, H * RET_DK), lambda b, j: (j, b * N_CHUNKS + 2)),
                  pl.BlockSpec((cps * C, H * RET_DK), lambda b, j: (j, b * N_CHUNKS + 3)),
                  pl.BlockSpec((cps * C, H * RET_DV), lambda b, j: (j, b * half + 2)),
                  pl.BlockSpec((cps * C, H * RET_DV), lambda b, j: (j, b * half + 3)),
                  pl.BlockSpec((None, cps, H, RET_DK, RET_DV), lambda b, j: (b, j, 0, 0, 0)),
                  pl.BlockSpec((None, H, RET_DK, RET_DV), lambda b, j: (b, 0, 0, 0)),
                  pl.BlockSpec((H, C, C), lambda b, j: (0, 0, 0)),
                  pl.BlockSpec((H, C, RET_DK), lambda b, j: (0, 0, 0)),
                  pl.BlockSpec((H, C, RET_DK), lambda b, j: (0, 0, 0)),
                  pl.BlockSpec((H, C, RET_DK), lambda b, j: (0, 0, 0)),
                  pl.BlockSpec((H, RET_DV), lambda b, j: (0, 0))],
        out_specs=pl.BlockSpec((None, cps * C, H * RET_DV), lambda b, j: (b, j, 0)),
        out_shape=jax.ShapeDtypeStruct((B, S, H * RET_DV), BF16),
        scratch_shapes=[pltpu.VMEM((H, RET_DK, RET_DV), F32)],
        compiler_params=_cparams(2),
        name="ret_fwd",
    )(z, z, z, z, sb, s_fwd, intra, qdec_f, qdec_b, kdec_f, cdec)


def _dot_t(a, b):
    return lax.dot_general(a, b, (((1,), (1,)), ((), ())), preferred_element_type=F32)


def _finish_kernel(x_ref, hf_ref, hb_ref, gr_ref, gab_ref, ret_ref, g1_ref, sh2_ref, sc2_ref, n2_ref,
                   wr_ref, wt_ref, wo_ref, rwh_ref, rwl_ref, rb_ref,
                   x1_ref, hl_ref, eid_ref, gate_ref, cnt_ref):
    D = x_ref.shape[1]
    tm = x_ref.shape[0]
    dt = eid_ref.shape[2]
    rnn = hf_ref[...].astype(F32) + hb_ref[...].astype(F32)
    y_rnn = jnp.dot((rnn * jax.nn.gelu(gr_ref[...].astype(F32))).astype(BF16), wr_ref[...],
                    preferred_element_type=F32)
    y_ret = jnp.dot(ret_ref[...], wt_ref[...], preferred_element_type=F32)
    ga = gab_ref[:, :D].astype(F32)
    gb = gab_ref[:, D:].astype(F32)
    merged = _sigmoid(ga) * y_rnn + _sigmoid(gb) * y_ret
    y = jnp.dot(merged.astype(BF16), wo_ref[...], preferred_element_type=F32)
    x1 = x_ref[...] + g1_ref[...] * y
    x1_ref[...] = x1
    hl = _rmsnorm(x1, n2_ref[...]) * (1.0 + sc2_ref[...]) + sh2_ref[...]
    hh = hl.astype(BF16)
    hl_ref[...] = hh

    hlo = (hl - hh.astype(F32)).astype(BF16)
    logits = (_dot_t(rwh_ref[...], hh) + _dot_t(rwh_ref[...], hlo) + _dot_t(rwl_ref[...], hh)) + rb_ref[...]

    ne = logits.shape[0]
    sub = lax.broadcasted_iota(I32, (ne, tm), 0)
    work = logits
    vals, idxs = [], []
    oh = jnp.zeros((ne, tm), F32)
    for _ in range(TOP_K):
        m = jnp.max(work, axis=0, keepdims=True)
        idx = jnp.min(jnp.where(work == m, sub, ne), axis=0, keepdims=True)
        hot = sub == idx
        vals.append(m)
        idxs.append(idx)
        oh = oh + jnp.where(hot, 1.0, 0.0)
        work = jnp.where(hot, -jnp.inf, work)
    es = [jnp.exp(v - vals[0]) for v in vals]
    inv = 1.0 / (es[0] + es[1] + es[2] + es[3])
    for part in range(tm // dt):
        ls = slice(part * dt, (part + 1) * dt)
        for k in range(TOP_K):
            eid_ref[part, k:k + 1, :] = idxs[k][:, ls]
            gate_ref[part, k:k + 1, :] = (es[k] * inv)[:, ls]
        cnt_ref[part] = jnp.sum(oh[:, ls], axis=1, keepdims=True)


def _finish(x3, hf, hb, z, retg, mods3, norm2_g, w_rnn_b, w_ret_b, w_out_b, rwt_hi, rwt_lo, router_b, dt):
    B, S, D = x3.shape
    tm = min(FIN_TM, S)
    nt = S // tm
    N = B * S
    ne = rwt_hi.shape[0]
    half = N_CHUNKS // 2
    per = tm // dt
    const2 = lambda b, i: (0, 0)
    tile3 = lambda b, i: (b * nt + i, 0, 0)
    return pl.pallas_call(
        _finish_kernel,
        grid=(B, nt),
        in_specs=[pl.BlockSpec((None, tm, D), lambda b, i: (b, i, 0)),
                  pl.BlockSpec((None, tm, D), lambda b, i: (b, i, 0)),
                  pl.BlockSpec((None, tm, D), lambda b, i: (b, i, 0)),
                  pl.BlockSpec((tm, D), lambda b, i: (i, b * N_CHUNKS + 1)),
                  pl.BlockSpec((tm, 2 * D), lambda b, i: (i, b * half + 4)),
                  pl.BlockSpec((None, tm, retg.shape[2]), lambda b, i: (b, i, 0)),
                  pl.BlockSpec((None, 1, D), lambda b, i: (b, 0, 2)),
                  pl.BlockSpec((None, 1, D), lambda b, i: (b, 0, 3)),
                  pl.BlockSpec((None, 1, D), lambda b, i: (b, 0, 4)),
                  pl.BlockSpec((1, D), const2),
                  pl.BlockSpec(w_rnn_b.shape, const2),
                  pl.BlockSpec(w_ret_b.shape, const2),
                  pl.BlockSpec(w_out_b.shape, const2),
                  pl.BlockSpec(rwt_hi.shape, const2),
                  pl.BlockSpec(rwt_lo.shape, const2),
                  pl.BlockSpec((ne, 1), const2)],
        out_specs=[pl.BlockSpec((None, tm, D), lambda b, i: (b, i, 0)),
                   pl.BlockSpec((tm, D), lambda b, i: (b * nt + i, 0)),
                   pl.BlockSpec((per, TOP_K, dt), tile3),
                   pl.BlockSpec((per, TOP_K, dt), tile3),
                   pl.BlockSpec((per, ne, 1), tile3)],
        out_shape=[jax.ShapeDtypeStruct((B, S, D), F32),
                   jax.ShapeDtypeStruct((N, D), BF16),
                   jax.ShapeDtypeStruct((N // dt, TOP_K, dt), I32),
                   jax.ShapeDtypeStruct((N // dt, TOP_K, dt), F32),
                   jax.ShapeDtypeStruct((N // dt, ne, 1), F32)],
        compiler_params=_cparams(2),
        name="finish",
    )(x3, hf, hb, z, z, retg, mods3, mods3, mods3, norm2_g.reshape(1, D),
      w_rnn_b, w_ret_b, w_out_b, rwt_hi, rwt_lo, router_b.reshape(ne, 1))


def _local_rows(dt):
    return TOP_K * dt + N_EXPERTS * ROW_ALIGN


def _local_slots(eid_ref):
    dt = eid_ref.shape[1]
    ne = N_EXPERTS
    sub = lax.broadcasted_iota(I32, (ne, dt), 0)
    hots = [sub == eid_ref[k:k + 1, :] for k in range(TOP_K)]
    oh = jnp.zeros((ne, dt), F32)
    for hot in hots:
        oh = oh + jnp.where(hot, 1.0, 0.0)
    earlier = jnp.where(lax.broadcasted_iota(I32, (dt, dt), 0) < lax.broadcasted_iota(I32, (dt, dt), 1), 1.0, 0.0)
    before = jnp.dot(oh.astype(BF16), earlier.astype(BF16), preferred_element_type=F32)
    cnt = jnp.broadcast_to(jnp.sum(oh, axis=1, keepdims=True), (ne, dt))
    cnt = jnp.maximum(jnp.ceil(cnt * (1.0 / ROW_ALIGN)) * ROW_ALIGN, float(ROW_ALIGN))
    lower = jnp.where(lax.broadcasted_iota(I32, (ne, ne), 1) < lax.broadcasted_iota(I32, (ne, ne), 0), 1.0, 0.0)
    base = before + jnp.dot(lower.astype(BF16), cnt.astype(BF16), preferred_element_type=F32)
    return [jnp.sum(jnp.where(hot, base, 0.0), axis=0, keepdims=True).astype(I32) for hot in hots]


def _slot_matrix(slots, weights, out_ref):
    rows, dt = out_ref.shape
    ch = 64
    rel = lax.broadcasted_iota(I32, (ch, dt), 0).astype(F32).astype(BF16)
    slots_f = [s.astype(F32) for s in slots]
    weights_b = [jnp.asarray(w, F32).astype(BF16) for w in weights]
    zero = jnp.zeros((), BF16)
    for c in range(rows // ch):
        acc = None
        for s, w in zip(slots_f, weights_b):
            term = jnp.where(rel == (s - float(c * ch)).astype(BF16), w, zero)
            acc = term if acc is None else acc + term
        out_ref[c * ch:(c + 1) * ch, :] = acc


def _for_each_run(meta_ref, fn, maybe_empty=False):
    def one(e):
        n = pl.multiple_of(meta_ref[0, N_EXPERTS + e], ROW_ALIGN)
        args = (pl.multiple_of(meta_ref[0, e], ROW_ALIGN), pl.multiple_of(meta_ref[0, 2 * N_EXPERTS + e], ROW_ALIGN), n)
        if maybe_empty:
            pl.when(n > 0)(lambda: fn(*args))
        else:
            fn(*args, e % 2)

    if maybe_empty:
        lax.fori_loop(0, N_EXPERTS, lambda e, c: (one(e), c)[1], 0)
    else:
        for e in range(N_EXPERTS):
            one(e)


def _run_rows(meta_ref):
    last = N_EXPERTS - 1
    return pl.multiple_of(meta_ref[0, last] + meta_ref[0, N_EXPERTS + last], ROW_ALIGN)


def _dispatch_kernel(meta_ref, zmeta_ref, eid_ref, x_ref, hp_ref, slot_ref, sbuf_ref, zbuf_ref, pm_ref, rows_ref,
                     sems, zsem):
    t = pl.program_id(0)
    slot = t % 2

    def zero_copy(_, g, n):
        return pltpu.make_async_copy(zbuf_ref.at[pl.ds(0, n), :], hp_ref.at[pl.ds(g, n), :], zsem)

    def run_copy(sl, l, g, n):
        return pltpu.make_async_copy(sbuf_ref.at[sl, pl.ds(l, n), :], hp_ref.at[pl.ds(g, n), :], sems.at[sl])

    @pl.when(t == 0)
    def _():
        zbuf_ref[...] = jnp.zeros_like(zbuf_ref)
        _for_each_run(zmeta_ref, lambda l, g, n: zero_copy(l, g, n).start(), maybe_empty=True)
        _for_each_run(zmeta_ref, lambda l, g, n: zero_copy(l, g, n).wait(), maybe_empty=True)
        tail0 = zmeta_ref[0, 3 * N_EXPERTS]
        n_tail = zmeta_ref[0, 3 * N_EXPERTS + 1]

        def tail_copy(i):
            return zero_copy(0, pl.multiple_of(tail0 + i * MOE_BLOCK, MOE_BLOCK), MOE_BLOCK)

        lax.fori_loop(0, n_tail, lambda i, c: (tail_copy(i).start(), c)[1], 0)
        lax.fori_loop(0, n_tail, lambda i, c: (tail_copy(i).wait(), c)[1], 0)

    slots = _local_slots(eid_ref)
    for k in range(TOP_K):
        slot_ref[k:k + 1, :] = slots[k]
    _slot_matrix(slots, [1.0] * TOP_K, pm_ref)
    sbuf_ref[slot] = _pack_cols(jnp.dot(pm_ref[...], x_ref[...], preferred_element_type=F32))

    @pl.when(t > 0)
    def _():
        run_copy(1 - slot, 0, 0, pl.multiple_of(rows_ref[1 - slot], ROW_ALIGN)).wait()

    _for_each_run(meta_ref, lambda l, g, n, pr: run_copy(slot, l, g, n).start(priority=pr))
    rows_ref[slot] = _run_rows(meta_ref)

    @pl.when(t == pl.num_programs(0) - 1)
    def _():
        run_copy(slot, 0, 0, _run_rows(meta_ref)).wait()


def _dispatch(hl2, eid3, meta3, zmeta, P):
    N, D = hl2.shape
    nt, _, dt = eid3.shape
    nm = meta3.shape[2]
    return pl.pallas_call(
        _dispatch_kernel,
        grid=(nt,),
        in_specs=[pl.BlockSpec((None, 1, nm), lambda i: (i, 0, 0), memory_space=pltpu.SMEM),
                  pl.BlockSpec(zmeta.shape, lambda i: (0, 0), memory_space=pltpu.SMEM),
                  pl.BlockSpec((None, TOP_K, dt), lambda i: (i, 0, 0)),
                  pl.BlockSpec((dt, D), lambda i: (i, 0))],
        out_specs=[pl.BlockSpec(memory_space=pl.ANY),
                   pl.BlockSpec((None, TOP_K, dt), lambda i: (i, 0, 0))],
        out_shape=[jax.ShapeDtypeStruct((P, D // 2), U32),
                   jax.ShapeDtypeStruct((nt, TOP_K, dt), I32)],
        scratch_shapes=[pltpu.VMEM((2, _local_rows(dt), D // 2), U32),
                        pltpu.VMEM((MOE_BLOCK, D // 2), U32),
                        pltpu.VMEM((_local_rows(dt), dt), BF16),
                        pltpu.SMEM((2,), I32),
                        pltpu.SemaphoreType.DMA((2,)),
                        pltpu.SemaphoreType.DMA(())],
        compiler_params=_cparams(1),
        name="dispatch",
    )(meta3, zmeta, eid3, hl2)


def _regroup_kernel(w_ref, sel_ref, o_ref):
    half = w_ref.shape[1] // 2
    g = sel_ref.shape[0]
    wb = w_ref[...].astype(BF16)
    for j in range(w_ref.shape[1] // g):
        r = jnp.dot(wb[:, g * j:g * (j + 1)], sel_ref[...], preferred_element_type=F32)
        o_ref[:, (g // 2) * j:(g // 2) * (j + 1)] = r[:, :g // 2].astype(o_ref.dtype)
        o_ref[:, half + (g // 2) * j:half + (g // 2) * (j + 1)] = r[:, g // 2:].astype(o_ref.dtype)


def _regroup_glu_lin(w1):
    ne, D, de2 = w1.shape
    g = 256
    sel = np.zeros((g, g), np.float32)
    sel[np.arange(0, g, 2), np.arange(g // 2)] = 1.0
    sel[np.arange(1, g, 2), g // 2 + np.arange(g // 2)] = 1.0
    out = pl.pallas_call(
        _regroup_kernel,
        grid=(ne,),
        in_specs=[pl.BlockSpec((D, de2), lambda e: (e, 0)),
                  pl.BlockSpec((g, g), lambda e: (0, 0))],
        out_specs=pl.BlockSpec((D, de2), lambda e: (e, 0)),
        out_shape=jax.ShapeDtypeStruct((ne * D, de2), BF16),
        compiler_params=_cparams(1),
        name="regroup",
    )(w1.reshape(ne * D, de2), jnp.asarray(sel, BF16))
    return out.reshape(ne, D, de2)


def _expert_kernel(be_ref, nu_ref, valid_ref, x_ref, w1_ref, b1_ref, w2_ref, b2_ref, o_ref):
    del be_ref, nu_ref
    de = w2_ref.shape[0]

    def rows(rs):
        h = jnp.dot(_unpack_cols(x_ref[rs, :]), w1_ref[...], preferred_element_type=F32) + b1_ref[...]
        glu = jnp.minimum(h[:, :de], SWIGLU_LIMIT)
        lin = jnp.clip(h[:, de:], -SWIGLU_LIMIT, SWIGLU_LIMIT)
        act = glu * _sigmoid(SWIGLU_ALPHA * glu) * (lin + 1.0)
        y = jnp.dot(act.astype(BF16), w2_ref[...].astype(BF16), preferred_element_type=F32) + b2_ref[...]
        o_ref[rs, :] = _pack_cols(y.astype(BF16).astype(F32))

    valid = valid_ref[pl.program_id(0)]

    @pl.when(valid == MOE_BLOCK)
    def _():
        rows(slice(0, MOE_BLOCK))

    @pl.when(valid < MOE_BLOCK)
    def _():
        for q in range(MOE_BLOCK // MOE_SUB):
            rs = slice(q * MOE_SUB, (q + 1) * MOE_SUB)

            @pl.when(valid > q * MOE_SUB)
            def _(rs=rs):
                rows(rs)

            @pl.when(valid <= q * MOE_SUB)
            def _(rs=rs):
                o_ref[rs, :] = jnp.zeros((MOE_SUB, o_ref.shape[1]), o_ref.dtype)


def _experts(blk_expert, n_used, valid, h_pad, w1b, b1p, w2b, b2):
    P = h_pad.shape[0]
    ne, D, de2 = w1b.shape
    de = de2 // 2
    nblk = P // MOE_BLOCK
    blk = lambda j, be, nu, vl: (jnp.minimum(j, nu[0] - 1), 0)
    wsel = lambda j, be, nu, vl: (be[jnp.minimum(j, nu[0] - 1)], 0, 0)
    return pl.pallas_call(
        _expert_kernel,
        grid_spec=pltpu.PrefetchScalarGridSpec(
            num_scalar_prefetch=3,
            grid=(nblk,),
            in_specs=[pl.BlockSpec((MOE_BLOCK, D // 2), blk),
                      pl.BlockSpec((None, D, de2), wsel),
                      pl.BlockSpec((None, 1, de2), wsel),
                      pl.BlockSpec((None, de, D), wsel),
                      pl.BlockSpec((None, 1, D), wsel)],
            out_specs=pl.BlockSpec((MOE_BLOCK, D // 2), lambda j, be, nu, vl: (j, 0))),
        out_shape=jax.ShapeDtypeStruct((P, D // 2), U32),
        compiler_params=_cparams(1),
        name="experts",
    )(blk_expert, n_used, valid, h_pad, w1b, b1p.reshape(ne, 1, de2), w2b, b2.reshape(ne, 1, D))


def _combine_kernel(meta_ref, nmeta_ref, slot_ref, gate_ref, x1_ref, g2_ref, fg_ref, yp_ref, o_ref,
                    ybuf_ref, gm_ref, sems):
    t = pl.program_id(0) * pl.num_programs(1) + pl.program_id(1)
    n_tiles = pl.num_programs(0) * pl.num_programs(1)
    slot = t % 2

    def run_copy(sl, l, g, n):
        return pltpu.make_async_copy(yp_ref.at[pl.ds(g, n), :], ybuf_ref.at[sl, pl.ds(l, n), :], sems.at[sl])

    @pl.when(t == 0)
    def _():
        ybuf_ref[...] = jnp.zeros_like(ybuf_ref)
        _for_each_run(meta_ref, lambda l, g, n, pr: run_copy(0, l, g, n).start(priority=pr))

    @pl.when(t + 1 < n_tiles)
    def _():
        _for_each_run(nmeta_ref, lambda l, g, n, pr: run_copy(1 - slot, l, g, n).start(priority=pr))

    run_copy(slot, 0, 0, _run_rows(meta_ref)).wait()

    _slot_matrix([slot_ref[k:k + 1, :] for k in range(TOP_K)], [gate_ref[k:k + 1, :] for k in range(TOP_K)], gm_ref)
    y = _tdot(gm_ref[...], _unpack_cols(ybuf_ref[slot]))
    x2 = x1_ref[...] + g2_ref[...] * y
    o_ref[...] = _rmsnorm(x2, fg_ref[...])


def _combine(meta3, slot3, gate3, x1, mods3, final_g, y_pad):
    B, S, D = x1.shape
    n_tiles, _, dt = slot3.shape
    nt = S // dt
    nm = meta3.shape[2]
    tile3 = lambda b, i: (b * nt + i, 0, 0)
    next3 = lambda b, i: (jnp.minimum(b * nt + i + 1, n_tiles - 1), 0, 0)
    return pl.pallas_call(
        _combine_kernel,
        grid=(B, nt),
        in_specs=[pl.BlockSpec((None, 1, nm), tile3, memory_space=pltpu.SMEM),
                  pl.BlockSpec((None, 1, nm), next3, memory_space=pltpu.SMEM),
                  pl.BlockSpec((None, TOP_K, dt), tile3),
                  pl.BlockSpec((None, TOP_K, dt), tile3),
                  pl.BlockSpec((None, dt, D), lambda b, i: (b, i, 0)),
                  pl.BlockSpec((None, 1, D), lambda b, i: (b, 0, 5)),
                  pl.BlockSpec((1, D), lambda b, i: (0, 0)),
                  pl.BlockSpec(memory_space=pl.ANY)],
        out_specs=pl.BlockSpec((None, dt, D), lambda b, i: (b, i, 0)),
        out_shape=jax.ShapeDtypeStruct((B, S, D), F32),
        scratch_shapes=[pltpu.VMEM((2, _local_rows(dt), D // 2), U32), pltpu.VMEM((_local_rows(dt), dt), BF16),
                        pltpu.SemaphoreType.DMA((2,))],
        compiler_params=_cparams(2),
        name="combine",
    )(meta3, meta3, slot3, gate3, x1, mods3, final_g.reshape(1, D), y_pad)


def _rope_tables(S, k_scale):
    n_freq = RET_DK // 4
    pos = np.arange(S)
    rows = (pos // GRID_W).astype(np.float32)
    cols = (pos % GRID_W).astype(np.float32)
    inv = (np.float32(ROPE_BASE) ** (-np.arange(n_freq, dtype=np.float32) / np.float32(n_freq))).astype(np.float32)
    ang = np.concatenate([rows[:, None] * inv, cols[:, None] * inv], axis=-1).astype(np.float32)
    cos, sin = np.cos(ang), np.sin(ang)
    cos2 = np.concatenate([cos, cos], axis=-1)
    sin2 = np.concatenate([-sin, sin], axis=-1)
    ks = np.float32(k_scale)
    return jnp.asarray(np.concatenate([cos2, sin2, cos2 * ks, sin2 * ks], axis=-1), F32)


def _identity_tables(L, k_scale):
    one = np.ones((L, RET_DK), np.float32)
    zero = np.zeros((L, RET_DK), np.float32)
    return jnp.asarray(np.concatenate([one, zero, one * np.float32(k_scale), zero], axis=-1), F32)


def _lanes(t, width):
    return np.ascontiguousarray(np.broadcast_to(t[:, :, None], t.shape + (width,)))


def kernel(x, c, ctx, c_ctx, ada_w, ada_b, norm1_g, w_in, conv_w, conv_b, lru_wa, lru_ba, lru_wx, lru_bx,
           lru_lambda, w_rnn_proj, w_ret_proj, w_out, norm2_g, router_w, router_b, moe_w1, moe_b1, moe_w2,
           moe_b2, final_g):
    B, S, D = x.shape
    L = ctx.shape[1]
    N = B * S
    H = RET_HEADS
    lyr = 0
    d_in = w_in.shape[2]
    assert ada_w.shape[0] == 1 and d_in == N_CHUNKS * D and B == 8

    def pairs_apart(w):
        return jnp.swapaxes(w.reshape(D, H, RET_DK // 2, 2), 2, 3).reshape(D, H * RET_DK)

    w_in_b = w_in[lyr].astype(BF16)
    w_qk = jnp.concatenate([pairs_apart(w_in_b[:, 2 * D:3 * D]), pairs_apart(w_in_b[:, 3 * D:4 * D])], axis=1)
    wg = [(0.5 * jnp.concatenate([lru_wa[lyr, d], lru_wx[lyr, d]], axis=-1)).astype(BF16) for d in range(2)]
    de2 = moe_w1.shape[3]
    glu_lin = np.concatenate([np.arange(0, de2, 2), np.arange(1, de2, 2)])
    w1b = _regroup_glu_lin(moe_w1[lyr])
    b1p = moe_b1[lyr][:, glu_lin]
    rwt = router_w[lyr].T
    rwt_hi = rwt.astype(BF16)
    rwt_lo = (rwt - rwt_hi.astype(F32)).astype(BF16)

    k_scale = RET_DK ** -0.5
    tab_l = _rope_tables(S, k_scale)
    tab_c = _identity_tables(B * L, k_scale)
    f4 = np.float32
    log_g = np.log1p(-np.exp2(-5.0 - np.arange(H, dtype=f4))).astype(f4)
    C = min(RET_C, S)
    idx = np.arange(C, dtype=f4)
    dec = lambda e: np.exp(e[None, :].astype(f4) * log_g[:, None]).astype(f4)
    intra = jnp.asarray(np.exp(np.abs(idx[:, None] - idx[None, :])[None] * log_g[:, None, None]).astype(f4))
    qdec_f = jnp.asarray(_lanes(dec(idx + 1.0), RET_DK))
    qdec_b = jnp.asarray(_lanes(dec(C - idx), RET_DK))
    kdec_f = jnp.asarray(_lanes(dec(C - 1.0 - idx), RET_DK))
    kdec_b = jnp.asarray(_lanes(dec(idx), RET_DK))
    cdec = jnp.asarray(np.broadcast_to(np.exp(C * log_g).astype(f4)[:, None], (H, RET_DV)))
    pos_c = np.arange(L, dtype=f4)
    cdec_f = jnp.asarray(_lanes(dec(L - 1.0 - pos_c), RET_DK))
    cdec_b = jnp.asarray(_lanes(dec(pos_c), RET_DK))

    cvec = jnp.zeros((16, D), F32).at[:B].set(c).at[B].set(c_ctx)
    mods3 = _ada(cvec, ada_w[lyr], ada_b[lyr]).reshape(16, 1, 6 * D)

    z_c, xr_c = _proj(ctx.reshape(1, B * L, D), mods3, lambda b: B, norm1_g[lyr], w_in_b, w_qk, tab_c,
                      min(PROJ_TM, B * L))
    xr_c = xr_c.reshape(B, L, D)
    z_l, xr_l = _proj(x, mods3, lambda b: b, norm1_g[lyr], w_in_b, w_qk, tab_l, min(PROJ_TM, S))

    zeros = jnp.zeros((B, D), F32)
    hs = []
    for d in range(2):
        args = (conv_w[lyr], conv_b[lyr], wg[d], 0.5 * lru_ba[lyr, d], 0.5 * lru_bx[lyr, d], lru_lambda[lyr, d])
        _, h0 = _rnn(xr_c, *args, zeros, reverse=(d == 1), tt=RNN_TT)
        h, _ = _rnn(xr_l, *args, h0, reverse=(d == 1), tt=RNN_TT)
        hs.append(h)

    s_fwd, s_bwd = _ret_ctx(z_c, B, cdec_f, cdec_b)
    sb = _ret_bwd(z_l, B, s_bwd, kdec_b, cdec)
    retg = _ret_fwd(z_l, B, sb, s_fwd, intra.astype(BF16), qdec_f.astype(BF16), qdec_b.astype(BF16),
                    kdec_f.astype(BF16), cdec)

    dt = min(DISP_TM, S)
    x1, hl2, eid3, gate3, cnt3 = _finish(
        x, hs[0], hs[1], z_l, retg, mods3, norm2_g[lyr], w_rnn_proj[lyr].astype(BF16),
        w_ret_proj[lyr].astype(BF16), w_out[lyr].astype(BF16), rwt_hi, rwt_lo, router_b[lyr], dt)

    cnt_t = cnt3[:, :, 0].astype(I32)
    cnt_t = jnp.maximum((cnt_t + ROW_ALIGN - 1) // ROW_ALIGN * ROW_ALIGN, ROW_ALIGN)
    cnt = jnp.sum(cnt_t, axis=0)
    padded = (cnt + MOE_BLOCK - 1) // MOE_BLOCK * MOE_BLOCK
    pad_end = jnp.cumsum(padded)
    pad_start = pad_end - padded
    gstart = pad_start[None, :] + jnp.cumsum(cnt_t, axis=0) - cnt_t
    loff = jnp.cumsum(cnt_t, axis=1) - cnt_t
    meta3 = jnp.concatenate([loff, cnt_t, gstart], axis=1).reshape(N // dt, 1, 3 * N_EXPERTS)
    n_blocks = -(-(N * TOP_K + (N // dt) * N_EXPERTS * ROW_ALIGN) // MOE_BLOCK) + N_EXPERTS
    P = n_blocks * MOE_BLOCK
    zmeta = jnp.concatenate([jnp.zeros_like(cnt), padded - cnt, pad_start + cnt,
                             pad_end[-1:], (P - pad_end[-1:]) // MOE_BLOCK]).reshape(1, 3 * N_EXPERTS + 2)
    blk_start = jnp.arange(n_blocks, dtype=I32) * MOE_BLOCK
    blk_expert = jnp.minimum(jnp.sum((pad_end[None, :] <= blk_start[:, None]).astype(I32), axis=1), N_EXPERTS - 1)
    n_used = (pad_end[-1:] // MOE_BLOCK).astype(I32)
    blk_valid = jnp.clip((pad_start + cnt)[blk_expert] - blk_start, 0, MOE_BLOCK)
    blk_valid = jnp.where(jnp.arange(n_blocks) < n_used[0], blk_valid, 0).astype(I32)

    h_pad, slot3 = _dispatch(hl2, eid3, meta3, zmeta, P)
    y_pad = _experts(blk_expert, n_used, blk_valid, h_pad, w1b, b1p, moe_w2[lyr], moe_b2[lyr])
    return _combine(meta3, slot3, gate3, x1, mods3, final_g, y_pad)
```
